```python
import jax, jax.numpy as jnp
from jax import lax
import numpy as np

D_MODEL = 1024
BATCH = 2
SEQ = 8192
DEPTH = 2

GRID_W = 64
CTX_LEN = 256
EPS = 1e-6
POS_BASE = 10000.0
ML_WIDTH = D_MODEL // 2
N_ML_HEADS = 4
ML_HEAD_DIM = ML_WIDTH // N_ML_HEADS
ML_CHUNK = 64
CONV_W = 3
N_ML_GATES = 4 * N_ML_HEADS
GLA_V_WIDTH = D_MODEL - ML_WIDTH
GLA_K_WIDTH = GLA_V_WIDTH // 2
N_GLA_HEADS = 4
GLA_KEY_DIM = GLA_K_WIDTH // N_GLA_HEADS
GLA_VAL_DIM = GLA_V_WIDTH // N_GLA_HEADS
GLA_RANK = 16
GLA_GATE_TAU = 16.0
GLA_CHUNK = 16
N_EXPERTS = 16
EC_FACTOR = 2
D_EXPERT = D_MODEL
OFF_ML_Q = 0
OFF_ML_K = OFF_ML_Q + ML_WIDTH
OFF_ML_V = OFF_ML_K + ML_WIDTH
OFF_ML_O = OFF_ML_V + ML_WIDTH
OFF_ML_G = OFF_ML_O + ML_WIDTH
OFF_GLA_Q = OFF_ML_G + N_ML_GATES
OFF_GLA_K = OFF_GLA_Q + GLA_K_WIDTH
OFF_GLA_V = OFF_GLA_K + GLA_K_WIDTH
OFF_GLA_R = OFF_GLA_V + GLA_V_WIDTH
OFF_GLA_A = OFF_GLA_R + GLA_V_WIDTH
PROJ_WIDTH = OFF_GLA_A + 2 * GLA_RANK

kernel_name = "hybrid_mlstm_gla_ec_moe_prefix_dit"


def rms_norm(x, g):
    x32 = x.astype(jnp.float32)
    y = x32 * lax.rsqrt(jnp.mean(x32 * x32, axis=-1, keepdims=True) + EPS)
    return y.astype(x.dtype) * g


def head_rms_norm(h, g, n_heads):
    b, t, w = h.shape
    return rms_norm(h.reshape(b, t, n_heads, w // n_heads), g.reshape(n_heads, w // n_heads)).reshape(b, t, w)


def modulate(h, shift, scale):
    return h * (1 + scale) + shift


def dw_conv(x, w):
    pad = CONV_W // 2
    t = x.shape[1]
    xp = jnp.pad(x, ((0, 0), (pad, pad), (0, 0)))
    return sum(xp[:, j:j + t] * w[j] for j in range(CONV_W))


def to_heads(x, n_heads):
    b, t, w = x.shape
    return x.reshape(b, t, n_heads, w // n_heads).transpose(0, 2, 1, 3).astype(jnp.float32)


def from_heads(x, dtype):
    b, h, t, d = x.shape
    return x.transpose(0, 2, 1, 3).reshape(b, t, h * d).astype(dtype)


def chunk(x, size):
    return x.reshape(x.shape[:2] + (x.shape[2] // size, size) + x.shape[3:])


def flip_time(xs):
    return tuple(None if t is None else jnp.flip(t, axis=2) for t in xs)


def mlstm_states(k, v, log_i, log_f, state):
    kc, vc = chunk(k, ML_CHUNK), chunk(v, ML_CHUNK)
    b = jnp.cumsum(chunk(log_f, ML_CHUNK), axis=-1)
    b_end = b[..., -1]
    w_end = b_end[..., None] - b + chunk(log_i, ML_CHUNK)
    m_kv = jnp.max(w_end, axis=-1)
    e = jnp.exp(w_end - m_kv[..., None])
    c_kv = jnp.einsum('bhnl,bhnlv,bhnlk->bhnvk', e, vc, kc)
    n_kv = jnp.einsum('bhnl,bhnlk->bhnk', e, kc)

    def step(carry, inp):
        c_st, n_st, m_st = carry
        be, mk, ck, nk = inp
        m_new = jnp.maximum(be + m_st, mk)
        a_old = jnp.exp(be + m_st - m_new)
        a_new = jnp.exp(mk - m_new)
        c_new = a_old[..., None, None] * c_st + a_new[..., None, None] * ck
        n_new = a_old[..., None] * n_st + a_new[..., None] * nk
        return (c_new, n_new, m_new), carry

    xs = tuple(jnp.moveaxis(t, 2, 0) for t in (b_end, m_kv, c_kv, n_kv))
    final, pre = lax.scan(step, state, xs)
    pre = tuple(jnp.moveaxis(t, 0, 2) for t in pre)
    return pre, final


def mlstm_outputs(q, k, v, log_i, log_f, pre):
    c_prev, n_prev, m_prev = pre
    bsz, nh, t, dv = v.shape
    qc = chunk(q, ML_CHUNK) * (q.shape[-1] ** -0.5)
    kc, vc = chunk(k, ML_CHUNK), chunk(v, ML_CHUNK)
    b = jnp.cumsum(chunk(log_f, ML_CHUNK), axis=-1)
    li = chunk(log_i, ML_CHUNK)
    causal = jnp.tril(jnp.ones((ML_CHUNK, ML_CHUNK), dtype=bool))
    d = jnp.where(causal, b[..., :, None] - b[..., None, :] + li[..., None, :], -jnp.inf)
    m_intra = jnp.max(d, axis=-1)
    m_inter = b + m_prev[..., None]
    m = jnp.maximum(m_inter, m_intra)
    s = jnp.einsum('bhntk,bhnsk->bhnts', qc, kc) * jnp.exp(d - m[..., None])
    a = jnp.exp(m_inter - m)
    num = jnp.einsum('bhnts,bhnsv->bhntv', s, vc) + a[..., None] * jnp.einsum('bhntk,bhnvk->bhntv', qc, c_prev)
    den = jnp.sum(s, axis=-1) + a * jnp.einsum('bhntk,bhnk->bhnt', qc, n_prev)
    h = num / jnp.maximum(jnp.abs(den), jnp.exp(-m))[..., None]
    return h.reshape(bsz, nh, t, dv)


def gla_states(k, v, log_a, state):
    kc, vc = chunk(k, GLA_CHUNK), chunk(v, GLA_CHUNK)
    b = jnp.cumsum(chunk(log_a, GLA_CHUNK), axis=3)
    b_end = b[:, :, :, -1]
    kv = jnp.einsum('bhnsk,bhnsv->bhnkv', kc * jnp.exp(b_end[:, :, :, None] - b), vc)

    def step(s_st, inp):
        be, kvc = inp
        return jnp.exp(be)[..., None] * s_st + kvc, s_st

    final, pre = lax.scan(step, state, (jnp.moveaxis(b_end, 2, 0), jnp.moveaxis(kv, 2, 0)))
    return jnp.moveaxis(pre, 0, 2), final


def gla_outputs(q, k, v, log_a, pre):
    bsz, nh, t, dv = v.shape
    qc = chunk(q, GLA_CHUNK) * (q.shape[-1] ** -0.5)
    kc, vc = chunk(k, GLA_CHUNK), chunk(v, GLA_CHUNK)
    b = jnp.cumsum(chunk(log_a, GLA_CHUNK), axis=3)
    inter = jnp.einsum('bhntk,bhnkv->bhntv', qc * jnp.exp(b), pre)
    causal = jnp.tril(jnp.ones((GLA_CHUNK, GLA_CHUNK), dtype=bool))
    diff = b[:, :, :, :, None, :] - b[:, :, :, None, :, :]
    decay = jnp.exp(jnp.where(causal[:, :, None], diff, -jnp.inf))
    att = jnp.einsum('bhntk,bhnsk,bhntsk->bhnts', qc, kc, decay)
    o = jnp.einsum('bhnts,bhnsv->bhntv', att, vc) + inter
    return o.reshape(bsz, nh, t, dv)


def bidirectional(state_fn, out_fn, ctx_f, ctx_b, lat_f, lat_b, init, ctx_out):
    cb, lb = flip_time(ctx_b), flip_time(lat_b)
    pre_cf, s_f = state_fn(*ctx_f[1:], init)
    pre_cb, s_b = state_fn(*cb[1:], init)
    pre_lf, _ = state_fn(*lat_f[1:], s_f)
    pre_lb, _ = state_fn(*lb[1:], s_b)
    y_lat = out_fn(*lat_f, pre_lf) + jnp.flip(out_fn(*lb, pre_lb), axis=2)
    y_ctx = (out_fn(*ctx_f, pre_cf) + jnp.flip(out_fn(*cb, pre_cb), axis=2)) if ctx_out else None
    return y_ctx, y_lat


def mlstm_prep(p, conv_w, b_gates, with_q):
    bsz, t, _ = p.shape
    k = to_heads(jax.nn.silu(dw_conv(p[..., OFF_ML_K:OFF_ML_V], conv_w[:, ML_WIDTH:])), N_ML_HEADS)
    q = to_heads(jax.nn.silu(dw_conv(p[..., OFF_ML_Q:OFF_ML_K], conv_w[:, :ML_WIDTH])), N_ML_HEADS) if with_q else None
    v = to_heads(p[..., OFF_ML_V:OFF_ML_O], N_ML_HEADS)
    g = (p[..., OFF_ML_G:OFF_GLA_Q] + b_gates).astype(jnp.float32)
    g = g.reshape(bsz, t, 4, N_ML_HEADS).transpose(2, 0, 3, 1)
    fwd = (q, k, v, g[0], jax.nn.log_sigmoid(g[1]))
    bwd = (q, k, v, g[2], jax.nn.log_sigmoid(g[3]))
    return fwd, bwd


def gla_prep(p, w_a2, b_a, with_q):
    q = to_heads(p[..., OFF_GLA_Q:OFF_GLA_K], N_GLA_HEADS) if with_q else None
    k = to_heads(p[..., OFF_GLA_K:OFF_GLA_V], N_GLA_HEADS)
    v = to_heads(p[..., OFF_GLA_V:OFF_GLA_R], N_GLA_HEADS)
    a_lr = p[..., OFF_GLA_A:PROJ_WIDTH]
    la_f = to_heads(jax.nn.log_sigmoid((a_lr[..., :GLA_RANK] @ w_a2[0] + b_a[0]).astype(jnp.float32)) / GLA_GATE_TAU, N_GLA_HEADS)
    la_b = to_heads(jax.nn.log_sigmoid((a_lr[..., GLA_RANK:] @ w_a2[1] + b_a[1]).astype(jnp.float32)) / GLA_GATE_TAU, N_GLA_HEADS)
    return (q, k, v, la_f), (q, k, v, la_b)


def merge_groups(h_ml, o_gla, p, g_ml, g_gla):
    y_ml = head_rms_norm(from_heads(h_ml, p.dtype), g_ml, N_ML_HEADS) * jax.nn.sigmoid(p[..., OFF_ML_O:OFF_ML_G])
    y_gla = head_rms_norm(from_heads(o_gla, p.dtype), g_gla, N_GLA_HEADS) * jax.nn.silu(p[..., OFF_GLA_R:OFF_GLA_A])
    return jnp.concatenate([y_ml, y_gla], axis=-1)


def token_mixers(hc, hl, w_in, conv_w, b_gates, w_a2, b_a, g_ml, g_gla, w_out, ctx_out):
    bsz = hl.shape[0]
    pc = hc @ w_in
    pl = hl @ w_in
    f32 = jnp.float32
    ml_init = (jnp.zeros((bsz, N_ML_HEADS, ML_HEAD_DIM, ML_HEAD_DIM), f32),
               jnp.zeros((bsz, N_ML_HEADS, ML_HEAD_DIM), f32),
               jnp.zeros((bsz, N_ML_HEADS), f32))
    gla_init = jnp.zeros((bsz, N_GLA_HEADS, GLA_KEY_DIM, GLA_VAL_DIM), f32)
    mc_f, mc_b = mlstm_prep(pc, conv_w, b_gates, ctx_out)
    ml_f, ml_b = mlstm_prep(pl, conv_w, b_gates, True)
    hm_c, hm_l = bidirectional(mlstm_states, mlstm_outputs, mc_f, mc_b, ml_f, ml_b, ml_init, ctx_out)
    gc_f, gc_b = gla_prep(pc, w_a2, b_a, ctx_out)
    gl_f, gl_b = gla_prep(pl, w_a2, b_a, True)
    go_c, go_l = bidirectional(gla_states, gla_outputs, gc_f, gc_b, gl_f, gl_b, gla_init, ctx_out)
    y_l = merge_groups(hm_l, go_l, pl, g_ml, g_gla) @ w_out
    y_c = (merge_groups(hm_c, go_c, pc, g_ml, g_gla) @ w_out) if ctx_out else None
    return y_c, y_l


def expert_choice_ffn(h, w_router, w_gate, w_up, w_down):
    n, d = h.shape[1], h.shape[2]
    cap = EC_FACTOR * n // N_EXPERTS
    aff = jax.nn.softmax(jnp.einsum('bnd,de->bne', h, w_router).astype(jnp.float32), axis=-1)
    g, idx = lax.top_k(jnp.swapaxes(aff, 1, 2), cap)
    xin = jax.vmap(lambda hb, ib: hb[ib])(h, idx)
    hid = jax.nn.silu(jnp.einsum('becd,edf->becf', xin, w_gate)) * jnp.einsum('becd,edf->becf', xin, w_up)
    y = jnp.einsum('becf,efd->becd', hid, w_down) * g[..., None].astype(h.dtype)
    return jax.vmap(lambda yb, ib: jnp.zeros((n, d), h.dtype).at[ib.reshape(-1)].add(yb.reshape(-1, d)))(y, idx)


def grid_pos_embedding(rows, d, dtype):
    r = jnp.broadcast_to(jnp.arange(rows, dtype=jnp.float32)[:, None], (rows, GRID_W)).reshape(-1)
    col = jnp.broadcast_to(jnp.arange(GRID_W, dtype=jnp.float32)[None, :], (rows, GRID_W)).reshape(-1)
    quarter = d // 4
    freq = jnp.power(POS_BASE, -jnp.arange(quarter, dtype=jnp.float32) / quarter)
    ar, ac = r[:, None] * freq, col[:, None] * freq
    return jnp.concatenate([jnp.sin(ar), jnp.cos(ar), jnp.sin(ac), jnp.cos(ac)], axis=-1).astype(dtype)


def setup_inputs(seed: int = 0) -> dict:
    key = jax.random.key(seed)
    ks = jax.random.split(key, 24)
    D, E, F = D_MODEL, N_EXPERTS, D_EXPERT

    def nrm(k, shape, scale):
        return scale * jax.random.normal(k, shape, jnp.float32)

    gate_base = jnp.repeat(jnp.array([0.0, 3.0, 0.0, 3.0], jnp.float32), N_ML_HEADS)
    return {
        "x": nrm(ks[0], (BATCH, SEQ, D), 1.0),
        "c": nrm(ks[1], (BATCH, D), 1.0),
        "ctx": nrm(ks[2], (BATCH, CTX_LEN, D), 1.0),
        "c_ctx": nrm(ks[3], (D,), 1.0),
        "w_ada": nrm(ks[4], (DEPTH, D, 6 * D), 0.5 * D ** -0.5),
        "b_ada": nrm(ks[5], (DEPTH, 6 * D), 0.02),
        "g_mix_pre": 1.0 + nrm(ks[6], (DEPTH, D), 0.05),
        "g_mix_post": 1.0 + nrm(ks[7], (DEPTH, D), 0.05),
        "g_ffn_pre": 1.0 + nrm(ks[8], (DEPTH, D), 0.05),
        "g_ffn_post": 1.0 + nrm(ks[9], (DEPTH, D), 0.05),
        "w_in": nrm(ks[10], (DEPTH, D, PROJ_WIDTH), D ** -0.5),
        "conv_qk": nrm(ks[11], (DEPTH, CONV_W, 2 * ML_WIDTH), 0.5),
        "b_ml_gates": gate_base + nrm(ks[12], (DEPTH, N_ML_GATES), 0.1),
        "w_gla_a2": nrm(ks[13], (DEPTH, 2, GLA_RANK, GLA_K_WIDTH), GLA_RANK ** -0.5),
        "b_gla_a": nrm(ks[14], (DEPTH, 2, GLA_K_WIDTH), 0.1),
        "g_ml_norm": 1.0 + nrm(ks[15], (DEPTH, ML_WIDTH), 0.05),
        "g_gla_norm": 1.0 + nrm(ks[16], (DEPTH, GLA_V_WIDTH), 0.05),
        "w_out": nrm(ks[17], (DEPTH, D, D), D ** -0.5),
        "w_router": nrm(ks[18], (DEPTH, D, E), D ** -0.5),
        "w_e_gate": nrm(ks[19], (DEPTH, E, D, F), D ** -0.5),
        "w_e_up": nrm(ks[20], (DEPTH, E, D, F), D ** -0.5),
        "w_e_down": nrm(ks[21], (DEPTH, E, F, D), F ** -0.5),
    }


def reference(x, c, ctx, c_ctx, w_ada, b_ada, g_mix_pre, g_mix_post, g_ffn_pre, g_ffn_post,
              w_in, conv_qk, b_ml_gates, w_gla_a2, b_gla_a, g_ml_norm, g_gla_norm, w_out,
              w_router, w_e_gate, w_e_up, w_e_down):
    n_tok = x.shape[1]
    rows = n_tok // GRID_W
    x = x + grid_pos_embedding(rows, x.shape[-1], x.dtype)[None]
    xc = ctx
    for l in range(DEPTH):
        last = l == DEPTH - 1
        mod = jax.nn.silu(c) @ w_ada[l] + b_ada[l]
        mod_c = jax.nn.silu(c_ctx) @ w_ada[l] + b_ada[l]
        sh1, sc1, ga1, sh2, sc2, ga2 = jnp.split(mod[:, None, :], 6, axis=-1)
        csh1, csc1, cga1, csh2, csc2, cga2 = jnp.split(mod_c, 6)
        hl = modulate(rms_norm(x, g_mix_pre[l]), sh1, sc1)
        hc = modulate(rms_norm(xc, g_mix_pre[l]), csh1, csc1)
        y_c, y_l = token_mixers(hc, hl, w_in[l], conv_qk[l], b_ml_gates[l], w_gla_a2[l], b_gla_a[l],
                                g_ml_norm[l], g_gla_norm[l], w_out[l], not last)
        x = x + ga1 * rms_norm(y_l, g_mix_post[l])
        hl2 = modulate(rms_norm(x, g_ffn_pre[l]), sh2, sc2)
        x = x + ga2 * rms_norm(expert_choice_ffn(hl2, w_router[l], w_e_gate[l], w_e_up[l], w_e_down[l]), g_ffn_post[l])
        if not last:
            xc = xc + cga1 * rms_norm(y_c, g_mix_post[l])
            hc2 = modulate(rms_norm(xc, g_ffn_pre[l]), csh2, csc2)
            xc = xc + cga2 * rms_norm(expert_choice_ffn(hc2, w_router[l], w_e_gate[l], w_e_up[l], w_e_down[l]), g_ffn_post[l])
    return x
```

```python
import functools

import jax
import jax.numpy as jnp
from jax import lax
from jax.experimental import pallas as pl
from jax.experimental.pallas import tpu as pltpu

F32 = jnp.float32
BF16 = jnp.bfloat16
I32 = jnp.int32

EPS = 1e-6
GRID_W = 64
POS_BASE = 10000.0
N_HEADS = 4
ML_DH = 128
GLA_DK = 64
GLA_DV = 128
GLA_GATE_TAU = 16.0
N_EXPERTS = 16
EC_FACTOR = 2

LANES = 128
TB = 256
ML_L = 128
GLA_L = 64
WIN = 80
ROW_ALIGN = 16
UNSEL = 2047.0
VMEM_LIMIT = 56 * 1024 * 1024


def _cparams(sem):
    return pltpu.CompilerParams(dimension_semantics=sem, vmem_limit_bytes=VMEM_LIMIT)


def _split2(a):
    hi = a.astype(BF16)
    lo = (a - hi.astype(F32)).astype(BF16)
    return hi, lo


def _split3(a):
    hi = a.astype(BF16)
    r = a - hi.astype(F32)
    mid = r.astype(BF16)
    lo = (r - mid.astype(F32)).astype(BF16)
    return hi, mid, lo


_NN = (((1,), (0,)), ((), ()))
_NT = (((1,), (1,)), ((), ()))
_TN = (((0,), (0,)), ((), ()))


def _mm(a, b, dims=_NN):
    return lax.dot_general(a, b, dims, preferred_element_type=F32)


def _dot3(a, b, dims=_NN):
    ah, al = _split2(a)
    bh, bl = _split2(b)
    return _mm(ah, bh, dims) + (_mm(ah, bl, dims) + _mm(al, bh, dims))


def _dot_exact_l(m_bf16, x, dims=_NN):
    hi, mid, lo = _split3(x)
    return _mm(m_bf16, hi, dims) + (_mm(m_bf16, mid, dims) + _mm(m_bf16, lo, dims))


def _dot_exact_r(x, m_bf16, dims=_NN):
    hi, mid, lo = _split3(x)
    return _mm(hi, m_bf16, dims) + (_mm(mid, m_bf16, dims) + _mm(lo, m_bf16, dims))


def _rms(x, g):
    return x * lax.rsqrt(jnp.mean(x * x, axis=-1, keepdims=True) + EPS) * g


def _log_sigmoid(x):
    return jnp.minimum(x, 0.0) - jnp.log1p(jnp.exp(-jnp.abs(x)))


def _sigmoid(x):
    return 1.0 / (1.0 + jnp.exp(-x))


def _silu(x):
    return x * _sigmoid(x)


def _iota(shape, dim):
    return lax.broadcasted_iota(I32, shape, dim)


def _rev_block(i, ncb, nblk):
    return jnp.where(i < ncb, ncb - 1 - i, nblk - 1 - (i - ncb))


def _ada_kernel(c_ref, w_ref, b_ref, o_ref):
    a = _silu(c_ref[...])
    o_ref[0] = _dot3(a, w_ref[0]) + b_ref[0]


def _ada(cc, w_ada, b_ada):
    depth, d, n6 = w_ada.shape
    tn = 1536
    return pl.pallas_call(
        _ada_kernel,
        grid=(depth, n6 // tn),
        in_specs=[pl.BlockSpec((8, d), lambda l, j: (0, 0)),
                  pl.BlockSpec((1, d, tn), lambda l, j: (l, 0, j)),
                  pl.BlockSpec((1, 1, tn), lambda l, j: (l, 0, j))],
        out_specs=pl.BlockSpec((1, 8, tn), lambda l, j: (l, 0, j)),
        out_shape=jax.ShapeDtypeStruct((depth, 8, n6), F32),
        compiler_params=_cparams(("parallel", "parallel")),
        name="ada",
    )(cc, w_ada, b_ada.reshape(depth, 1, n6))


def _in_kernel(x_ref, mod_ref, g_ref, wb_ref, ws_ref, wst_ref, bs_ref, bst_ref,
               pb_ref, ps_ref, pst_ref):
    h = _rms(x_ref[0], g_ref[...]) * (1.0 + mod_ref[0, 0, 1:2, :]) + mod_ref[0, 0, 0:1, :]
    pb_ref[0] = _mm(h.astype(BF16), wb_ref[...])
    ps_ref[0] = _dot3(h, ws_ref[...]) + bs_ref[...]
    pst_ref[0] = _dot3(wst_ref[...], h, _NT) + bst_ref[...]


def _in_proj(xa, mod, g, wb, ws, wst, bs, bst, ncb):
    b, t, d = xa.shape
    nb = wb.shape[1]
    return pl.pallas_call(
        _in_kernel,
        grid=(b, t // TB),
        in_specs=[pl.BlockSpec((1, TB, d), lambda bi, i: (bi, i, 0)),
                  pl.BlockSpec((1, 1, 8, d), lambda bi, i: (bi, jnp.where(i < ncb, 0, 1), 0, 0)),
                  pl.BlockSpec((1, d), lambda bi, i: (0, 0)),
                  pl.BlockSpec((d, nb), lambda bi, i: (0, 0)),
                  pl.BlockSpec((d, LANES), lambda bi, i: (0, 0)),
                  pl.BlockSpec((LANES, d), lambda bi, i: (0, 0)),
                  pl.BlockSpec((1, LANES), lambda bi, i: (0, 0)),
                  pl.BlockSpec((LANES, 1), lambda bi, i: (0, 0))],
        out_specs=[pl.BlockSpec((1, TB, nb), lambda bi, i: (bi, i, 0)),
                   pl.BlockSpec((1, TB, LANES), lambda bi, i: (bi, i, 0)),
                   pl.BlockSpec((1, LANES, TB), lambda bi, i: (bi, 0, i))],
        out_shape=[jax.ShapeDtypeStruct((b, t, nb), F32),
                   jax.ShapeDtypeStruct((b, t, LANES), F32),
                   jax.ShapeDtypeStruct((b, LANES, t), F32)],
        compiler_params=_cparams(("parallel", "parallel")),
        name="in_proj",
    )(xa, mod, g, wb, ws, wst, bs, bst)


def _conv3(x, hl, hr, w):
    rows = _iota(x.shape, 0)
    prev = jnp.where(rows == 0, hl, pltpu.roll(x, 1, axis=0))
    nxt = jnp.where(rows == x.shape[0] - 1, hr, pltpu.roll(x, x.shape[0] - 1, axis=0))
    return prev * w[0:1] + x * w[1:2] + nxt * w[2:3]


def _ml_dir(d, j, ncb, nblk, qk_ref, hl_ref, hr_ref, v_ref, g_ref, gt_ref, cw_ref, out_ref,
            c_s, n_s, m_s):
    ll = ML_L
    lvalid = jnp.logical_and(j != 0, j != ncb)
    rvalid = jnp.logical_and(j != ncb - 1, j != nblk - 1)
    hl = jnp.where(lvalid, hl_ref[0, 7:8, :], 0.0)
    hr = jnp.where(rvalid, hr_ref[0, 0:1, :], 0.0)
    qk = _silu(_conv3(qk_ref[0], hl, hr, cw_ref[...]))
    dq = N_HEADS * ML_DH
    v = v_ref[0]
    ls = _log_sigmoid(g_ref[0])
    lst = _log_sigmoid(gt_ref[0])
    gi = g_ref[0]
    git = gt_ref[0]
    rows = _iota((ll, ll), 0)
    cols = _iota((ll, ll), 1)
    causal = (cols >= rows) if d else (cols <= rows)
    tri = jnp.where(causal, 1.0, 0.0).astype(BF16)
    trit = jnp.where((rows >= cols) if d else (rows <= cols), 1.0, 0.0).astype(BF16)
    order = range(TB // ll - 1, -1, -1) if d else range(TB // ll)
    for c in order:
        r0 = c * ll
        bc = _dot_exact_l(tri, ls[r0:r0 + ll, :])
        br = _dot_exact_r(lst[:, r0:r0 + ll], trit)
        for h in range(N_HEADS):
            ci = 8 * d + h
            cf = 8 * d + 4 + h
            sidx = d * N_HEADS + h
            li_col = gi[r0:r0 + ll, ci:ci + 1]
            li_row = git[ci:ci + 1, r0:r0 + ll]
            bcol = bc[:, cf:cf + 1]
            brow = br[cf:cf + 1, :]
            bend = bcol[0:1, :] if d else bcol[ll - 1:ll, :]
            m_prev = m_s[sidx][:, 0:1]
            q_h = qk[r0:r0 + ll, h * ML_DH:(h + 1) * ML_DH] * (ML_DH ** -0.5)
            k_h = qk[r0:r0 + ll, dq + h * ML_DH:dq + (h + 1) * ML_DH]
            v_h = v[r0:r0 + ll, h * ML_DH:(h + 1) * ML_DH]
            qb, kb, vb = q_h.astype(BF16), k_h.astype(BF16), v_h.astype(BF16)
            dmat = jnp.where(causal, bcol - brow + li_row, -jnp.inf)
            m_intra = jnp.max(dmat, axis=1, keepdims=True)
            m_inter = bcol + m_prev
            m_t = jnp.maximum(m_inter, m_intra)
            s = _mm(qb, kb, _NT) * jnp.exp(dmat - m_t)
            a = jnp.exp(m_inter - m_t)
            c_prev = c_s[sidx]
            n_prev = n_s[sidx]
            num = _mm(s.astype(BF16), vb) + a * _mm(qb, c_prev.astype(BF16), _NT)
            den = jnp.sum(s, axis=1, keepdims=True) + a * jnp.sum(q_h * n_prev, axis=1, keepdims=True)
            hout = num / jnp.maximum(jnp.abs(den), jnp.exp(-m_t))
            out_ref[0, r0:r0 + ll, h * ML_DH:(h + 1) * ML_DH] = hout
            w_end = bend - bcol + li_col
            m_kv = jnp.max(w_end, axis=0, keepdims=True)
            e = jnp.exp(w_end - m_kv)
            c_kv = _mm((e * v_h).astype(BF16), kb, _TN)
            n_kv = jnp.sum(e * k_h, axis=0, keepdims=True)
            m_new = jnp.maximum(bend + m_prev, m_kv)
            a_old = jnp.exp(bend + m_prev - m_new)
            a_new = jnp.exp(m_kv - m_new)
            c_s[sidx] = a_old * c_prev + a_new * c_kv
            n_s[sidx] = a_old * n_prev + a_new * n_kv
            m_s[sidx] = jnp.broadcast_to(m_new, (1, LANES))


def _ml_kernel(ncb, nblk, qkf, hlf, hrf, vf, gf, gtf, qkb, hlb, hrb, vb, gb, gtb, cw,
               of_ref, ob_ref, c_s, n_s, m_s):
    i = pl.program_id(1)

    @pl.when(i == 0)
    def _():
        c_s[...] = jnp.zeros_like(c_s)
        n_s[...] = jnp.zeros_like(n_s)
        m_s[...] = jnp.zeros_like(m_s)

    _ml_dir(0, i, ncb, nblk, qkf, hlf, hrf, vf, gf, gtf, cw, of_ref, c_s, n_s, m_s)
    _ml_dir(1, _rev_block(i, ncb, nblk), ncb, nblk, qkb, hlb, hrb, vb, gb, gtb, cw, ob_ref, c_s, n_s, m_s)


def _dir_specs(blk, t, qk_w, qk_cb, v_cb):
    r8 = TB // 8
    last8 = t // 8 - 1
    return [
        pl.BlockSpec((1, TB, qk_w), lambda b, i: (b, blk(i), qk_cb)),
        pl.BlockSpec((1, 8, qk_w), lambda b, i: (b, jnp.maximum(blk(i) * r8 - 1, 0), qk_cb)),
        pl.BlockSpec((1, 8, qk_w), lambda b, i: (b, jnp.minimum((blk(i) + 1) * r8, last8), qk_cb)),
        pl.BlockSpec((1, TB, 512), lambda b, i: (b, blk(i), v_cb)),
        pl.BlockSpec((1, TB, LANES), lambda b, i: (b, blk(i), 0)),
        pl.BlockSpec((1, LANES, TB), lambda b, i: (b, 0, blk(i))),
    ]


def _mlstm(pb, ps, pst, cw, ncb):
    b, t, _ = pb.shape
    nblk = t // TB
    fwd = lambda i: i
    bwd = lambda i: _rev_block(i, ncb, nblk)
    specs = _dir_specs(fwd, t, 1024, 0, 2) + _dir_specs(bwd, t, 1024, 0, 2)
    specs.append(pl.BlockSpec((3, 1024), lambda b_, i: (0, 0)))
    args = [pb, pb, pb, pb, ps, pst] * 2 + [cw]
    ns = 2 * N_HEADS
    return pl.pallas_call(
        functools.partial(_ml_kernel, ncb, nblk),
        grid=(b, nblk),
        in_specs=specs,
        out_specs=[pl.BlockSpec((1, TB, 512), lambda b_, i: (b_, i, 0)),
                   pl.BlockSpec((1, TB, 512), lambda b_, i: (b_, bwd(i), 0))],
        out_shape=[jax.ShapeDtypeStruct((b, t, 512), F32)] * 2,
        scratch_shapes=[pltpu.VMEM((ns, ML_DH, ML_DH), F32),
                        pltpu.VMEM((ns, 1, ML_DH), F32),
                        pltpu.VMEM((ns, 1, LANES), F32)],
        compiler_params=_cparams(("parallel", "arbitrary")),
        name="mlstm",
    )(*args)


def _gla_dir(d, qk_ref, v_ref, ps_ref, w2_ref, ba_ref, out_ref, s_s):
    ll = GLA_L
    dkw = N_HEADS * GLA_DK
    qk = qk_ref[0]
    v = v_ref[0]
    la = _log_sigmoid(_dot3(ps_ref[0], w2_ref[d]) + ba_ref[d]) * (1.0 / GLA_GATE_TAU)
    rows = _iota((ll, ll), 0)
    cols = _iota((ll, ll), 1)
    causal = (cols >= rows) if d else (cols <= rows)
    tri = jnp.where(causal, 1.0, 0.0).astype(BF16)
    order = range(TB // ll - 1, -1, -1) if d else range(TB // ll)
    for c in order:
        r0 = c * ll
        bcum = _dot_exact_l(tri, la[r0:r0 + ll, :])
        ref = bcum[ll // 2:ll // 2 + 1, :]
        bend = bcum[0:1, :] if d else bcum[ll - 1:ll, :]
        qt = qk[r0:r0 + ll, 0:dkw] * (GLA_DK ** -0.5) * jnp.exp(bcum - ref)
        kt = qk[r0:r0 + ll, dkw:2 * dkw] * jnp.exp(ref - bcum)
        e_ref = jnp.exp(ref)
        e_end = jnp.exp(bend)
        e_er = jnp.exp(bend - ref)
        for h in range(N_HEADS):
            sl = slice(h * GLA_DK, (h + 1) * GLA_DK)
            sidx = d * N_HEADS + h
            qb = qt[:, sl].astype(BF16)
            kb = kt[:, sl].astype(BF16)
            vb = v[r0:r0 + ll, h * GLA_DV:(h + 1) * GLA_DV].astype(BF16)
            att = jnp.where(causal, _mm(qb, kb, _NT), 0.0)
            st = s_s[sidx]
            o = _mm(att.astype(BF16), vb) + _mm(qb, (st * e_ref[:, sl]).astype(BF16), _NT)
            out_ref[0, r0:r0 + ll, h * GLA_DV:(h + 1) * GLA_DV] = o
            kdec = (kt[:, sl] * e_er[:, sl]).astype(BF16)
            s_s[sidx] = st * e_end[:, sl] + _mm(vb, kdec, _TN)


def _gla_kernel(ncb, nblk, qkf, vf, psf, qkb, vb, psb, w2, ba, of_ref, ob_ref, s_s):
    i = pl.program_id(1)

    @pl.when(i == 0)
    def _():
        s_s[...] = jnp.zeros_like(s_s)

    _gla_dir(0, qkf, vf, psf, w2, ba, of_ref, s_s)
    _gla_dir(1, qkb, vb, psb, w2, ba, ob_ref, s_s)


def _gla(pb, ps, w2e, ba, ncb):
    b, t, _ = pb.shape
    nblk = t // TB
    fwd = lambda i: i
    bwd = lambda i: _rev_block(i, ncb, nblk)

    def dspecs(blk):
        return [pl.BlockSpec((1, TB, 512), lambda b_, i: (b_, blk(i), 4)),
                pl.BlockSpec((1, TB, 512), lambda b_, i: (b_, blk(i), 5)),
                pl.BlockSpec((1, TB, LANES), lambda b_, i: (b_, blk(i), 0))]

    specs = dspecs(fwd) + dspecs(bwd) + [
        pl.BlockSpec((2, LANES, 256), lambda b_, i: (0, 0, 0)),
        pl.BlockSpec((2, 1, 256), lambda b_, i: (0, 0, 0))]
    return pl.pallas_call(
        functools.partial(_gla_kernel, ncb, nblk),
        grid=(b, nblk),
        in_specs=specs,
        out_specs=[pl.BlockSpec((1, TB, 512), lambda b_, i: (b_, i, 0)),
                   pl.BlockSpec((1, TB, 512), lambda b_, i: (b_, bwd(i), 0))],
        out_shape=[jax.ShapeDtypeStruct((b, t, 512), F32)] * 2,
        scratch_shapes=[pltpu.VMEM((2 * N_HEADS, GLA_DV, GLA_DK), F32)],
        compiler_params=_cparams(("parallel", "arbitrary")),
        name="gla",
    )(pb, pb, ps, pb, pb, ps, w2e, ba)


def _head_norm(x, g):
    outs = []
    for h in range(N_HEADS):
        seg = x[:, h * 128:(h + 1) * 128]
        outs.append(seg * lax.rsqrt(jnp.mean(seg * seg, axis=-1, keepdims=True) + EPS))
    return jnp.concatenate(outs, axis=-1) * g


def _out_kernel(hf, hb, of, ob, og, rg, x_ref, mod_ref, gml, ggla, wo, gpost, gpre, wr, wrt,
                x1_ref, h2e_ref, aff_ref):
    y_ml = _head_norm(hf[0] + hb[0], gml[...]) * _sigmoid(og[0])
    y_gla = _head_norm(of[0] + ob[0], ggla[...]) * _silu(rg[0])
    y = jnp.concatenate([y_ml, y_gla], axis=-1).astype(BF16)
    y2 = _mm(y, wo[...])
    mod = mod_ref[0, 0]
    x1 = x_ref[0] + mod[2:3, :] * _rms(y2, gpost[...])
    x1_ref[0] = x1
    h2 = _rms(x1, gpre[...]) * (1.0 + mod[4:5, :]) + mod[3:4, :]
    d = h2.shape[1]
    lt = _dot3(wrt[...], h2, _NT)
    ext = jnp.exp(lt - jnp.max(lt, axis=0, keepdims=True))
    aff_ref[0] = ext / jnp.sum(ext, axis=0, keepdims=True)
    ln = _dot3(h2, wr[...])
    ln = jnp.where(_iota(ln.shape, 1) < N_EXPERTS, ln, -jnp.inf)
    exn = jnp.exp(ln - jnp.max(ln, axis=1, keepdims=True))
    affn = exn / jnp.sum(exn, axis=1, keepdims=True)
    a_hi, a_mid, a_lo = _split3(affn)
    h2e_ref[0, :, 0:d] = h2.astype(BF16)
    h2e_ref[0, :, d:d + LANES] = a_hi
    h2e_ref[0, :, d + LANES:d + 2 * LANES] = a_mid
    h2e_ref[0, :, d + 2 * LANES:d + 3 * LANES] = a_lo


def _out_proj(hf, hb, of, ob, pb, xa, mod, gml, ggla, wo, gpost, gpre, wr, wrt, ncb):
    b, t, d = xa.shape
    de = d + 3 * LANES
    tile = lambda cb: pl.BlockSpec((1, TB, 512), lambda bi, i: (bi, i, cb))
    full = lambda shp: pl.BlockSpec(shp, lambda bi, i: tuple(0 for _ in shp))
    return pl.pallas_call(
        _out_kernel,
        grid=(b, t // TB),
        in_specs=[tile(0), tile(0), tile(0), tile(0), tile(3), tile(6),
                  pl.BlockSpec((1, TB, d), lambda bi, i: (bi, i, 0)),
                  pl.BlockSpec((1, 1, 8, d), lambda bi, i: (bi, jnp.where(i < ncb, 0, 1), 0, 0)),
                  full((1, 512)), full((1, 512)), full((d, d)), full((1, d)), full((1, d)),
                  full((d, LANES)), full((N_EXPERTS, d))],
        out_specs=[pl.BlockSpec((1, TB, d), lambda bi, i: (bi, i, 0)),
                   pl.BlockSpec((1, TB, de), lambda bi, i: (bi, i, 0)),
                   pl.BlockSpec((1, N_EXPERTS, TB), lambda bi, i: (bi, 0, i))],
        out_shape=[jax.ShapeDtypeStruct((b, t, d), F32),
                   jax.ShapeDtypeStruct((b, t, de), BF16),
                   jax.ShapeDtypeStruct((b, N_EXPERTS, t), F32)],
        compiler_params=_cparams(("parallel", "parallel")),
        name="out_proj",
    )(hf, hb, of, ob, pb, pb, xa, mod, gml, ggla, wo, gpost, gpre, wr, wrt)


def _cumsum_blocks(x, r):
    n = x.shape[0]
    xb = x.astype(BF16)
    li = _iota((LANES, LANES), 0)
    lj = _iota((LANES, LANES), 1)
    upper = jnp.where(li <= lj, 1.0, 0.0).astype(BF16)
    ones = jnp.ones((LANES, LANES), BF16)
    inrow = _mm(xb, upper)
    tot = _mm(xb, ones)
    ri = _iota((n, n), 0)
    rj = _iota((n, n), 1)
    same = (ri // r) == (rj // r)
    strict = jnp.where(jnp.logical_and(same, rj < ri), 1.0, 0.0).astype(BF16)
    off = _mm(strict, tot.astype(BF16))
    return inrow + off, off


def _select(aff, r, cap, base_slot):
    ne = N_EXPERTS
    n = ne * r
    aff3 = aff.reshape(ne, r, LANES)
    capf = jnp.float32(cap)

    def body(k, prefix):
        cand = prefix | (jnp.int32(1) << (30 - k))
        candf = lax.bitcast_convert_type(cand, F32)
        cnt = jnp.sum(jnp.where(aff3 >= candf, 1.0, 0.0), axis=(1, 2), keepdims=True)
        return jnp.where(cnt >= capf, cand, prefix)

    thr = lax.bitcast_convert_type(lax.fori_loop(0, 31, body, jnp.zeros((ne, 1, 1), I32)), F32)
    gt = jnp.where(aff3 > thr, 1.0, 0.0)
    eq = jnp.where(aff3 == thr, 1.0, 0.0)
    need = capf - jnp.sum(gt, axis=(1, 2), keepdims=True)
    eq2 = eq.reshape(n, LANES)
    cs_eq, _ = _cumsum_blocks(eq2, r)
    eq_rank = (cs_eq - eq2).reshape(ne, r, LANES)
    sel = (gt + eq * jnp.where(eq_rank < need, 1.0, 0.0)).reshape(n, LANES)
    cs, off = _cumsum_blocks(sel, r)
    slot = jnp.where(sel > 0.5, cs - 1.0 + base_slot, UNSEL)
    return slot, off


def _sel_kernel(rc, rl, cap_c, cap_l, *refs):
    if rc:
        affc, affl, slc, offc, sll, offl = refs
        slc[0], offc[0] = _select(affc[0], rc, cap_c, float(cap_l))
    else:
        affl, sll, offl = refs
    sll[0], offl[0] = _select(affl[0], rl, cap_l, 0.0)


def _route(aff_c, aff_l, cap_c, cap_l):
    b = aff_l.shape[0]
    ne = N_EXPERTS
    rl = aff_l.shape[1] // ne
    rc = aff_c.shape[1] // ne if aff_c is not None else 0
    args = ([aff_c] if rc else []) + [aff_l]
    in_specs, out_shape, out_specs = [], [], []
    for a in args:
        spec = pl.BlockSpec((1,) + a.shape[1:], lambda bi: (bi, 0, 0))
        in_specs.append(spec)
        out_shape += [jax.ShapeDtypeStruct(a.shape, F32)] * 2
        out_specs += [spec, spec]
    return pl.pallas_call(
        functools.partial(_sel_kernel, rc, rl, cap_c, cap_l),
        grid=(b,),
        in_specs=in_specs,
        out_specs=out_specs,
        out_shape=out_shape,
        compiler_params=_cparams(("parallel",)),
        name="route",
    )(*args)


def _window(lo_ref, base, e, m_rows):
    lo_e = lo_ref[base + e]
    hi_e = lo_ref[base + N_EXPERTS + e]
    a_e = jnp.minimum((lo_e // ROW_ALIGN) * ROW_ALIGN, m_rows - WIN)
    return a_e, hi_e


def _disp_kernel(m_rows, ng, lo_ref, h_ref, slot_ref, x_ref):
    ne = N_EXPERTS
    gi = pl.program_id(1)
    i = pl.program_id(2)
    base = (pl.program_id(0) * pl.num_programs(2) + i) * (2 * ne)

    @pl.when(i == 0)
    def _():
        x_ref[...] = jnp.zeros_like(x_ref)

    h = h_ref[0]
    sl = slot_ref[0]
    sub = _iota((WIN, TB), 0).astype(F32)
    wins = [_window(lo_ref, base, gi * ng + k, m_rows) for k in range(ng)]
    rounds = jnp.int32(1)
    for a_e, hi_e in wins:
        rounds = jnp.maximum(rounds, (hi_e - a_e + WIN - 1) // WIN)

    def one_round(rd, carry):
        wts, starts = [], []
        for k, (a_e, _) in enumerate(wins):
            first = a_e + rd * WIN
            a_r = jnp.minimum(first, m_rows - WIN)
            srow = sl[k:k + 1, :]
            hit = jnp.logical_and(srow - a_r.astype(F32) == sub, srow >= first.astype(F32))
            wts.append(jnp.where(hit, 1.0, 0.0).astype(BF16))
            starts.append(a_r)
        g = _mm(jnp.concatenate(wts, axis=0), h).astype(BF16)
        for k, a_r in enumerate(starts):
            rows = pl.ds(pl.multiple_of(a_r, ROW_ALIGN), WIN)
            x_ref[0, k, rows, :] = x_ref[0, k, rows, :] + g[k * WIN:(k + 1) * WIN, :]
        return carry

    lax.fori_loop(0, rounds, one_round, 0)


def _dispatch(lohi, h2e, slots, m_rows, t0, nt):
    b, t, de = h2e.shape
    ne = N_EXPERTS
    ng = 4
    grid_spec = pltpu.PrefetchScalarGridSpec(
        num_scalar_prefetch=1,
        grid=(b, ne // ng, nt),
        in_specs=[pl.BlockSpec((1, TB, de), lambda bi, gi, i, *_: (bi, i + t0, 0)),
                  pl.BlockSpec((1, ng, TB), lambda bi, gi, i, *_: (bi * (ne // ng) + gi, 0, i + t0))],
        out_specs=pl.BlockSpec((1, ng, m_rows, de), lambda bi, gi, i, *_: (bi, gi, 0, 0)))
    return pl.pallas_call(
        functools.partial(_disp_kernel, m_rows, ng),
        grid_spec=grid_spec,
        out_shape=jax.ShapeDtypeStruct((b, ne, m_rows, de), BF16),
        compiler_params=_cparams(("parallel", "parallel", "arbitrary")),
        name="dispatch",
    )(lohi, h2e, slots.reshape(b * (ne // ng), ng, t))


def _row_chunks(m_rows, cap_l):
    step = min(256, cap_l)
    chunks = [(s, step) for s in range(0, cap_l, step)]
    if m_rows > cap_l:
        chunks.append((cap_l, m_rows - cap_l))
    return chunks


def _exp_kernel(m_rows, cap_l, x_ref, wg_ref, wu_ref, wd_ref, y_ref):
    ei = pl.program_id(1)
    d = wg_ref.shape[1]
    wg = wg_ref[0]
    wu = wu_ref[0]
    wd = wd_ref[0]
    for r0, mc in _row_chunks(m_rows, cap_l):
        xs = x_ref[0, 0, r0:r0 + mc, 0:d]
        hid = _silu(_mm(xs, wg)) * _mm(xs, wu)
        y = _mm(hid.astype(BF16), wd)
        pieces = x_ref[0, 0, r0:r0 + mc, d:d + 3 * LANES].astype(F32)
        mine = (_iota((mc, 3 * LANES), 1) % LANES) == ei
        gate = jnp.sum(jnp.where(mine, pieces, 0.0), axis=1, keepdims=True)
        y_ref[0, 0, r0:r0 + mc, :] = (y * gate).astype(BF16)


def _experts(xin, wg, wu, wd, cap_l):
    b, ne, m_rows, de = xin.shape
    _, d, f = wg.shape
    return pl.pallas_call(
        functools.partial(_exp_kernel, m_rows, cap_l),
        grid=(b, ne),
        in_specs=[pl.BlockSpec((1, 1, m_rows, de), lambda bi, ei: (bi, ei, 0, 0)),
                  pl.BlockSpec((1, d, f), lambda bi, ei: (ei, 0, 0)),
                  pl.BlockSpec((1, d, f), lambda bi, ei: (ei, 0, 0)),
                  pl.BlockSpec((1, f, d), lambda bi, ei: (ei, 0, 0))],
        out_specs=pl.BlockSpec((1, 1, m_rows, d), lambda bi, ei: (bi, ei, 0, 0)),
        out_shape=jax.ShapeDtypeStruct((b, ne, m_rows, d), BF16),
        compiler_params=_cparams(("parallel", "arbitrary")),
        name="experts",
    )(xin, wg, wu, wd)


def _comb_kernel(m_rows, lo_ref, slot_ref, x1_ref, mod_ref, gpost, y_ref, out_ref):
    ne = N_EXPERTS
    i = pl.program_id(1)
    base = (pl.program_id(0) * pl.num_programs(1) + i) * (2 * ne)
    sl = slot_ref[0]
    hi = jnp.floor(sl * (1.0 / 32.0))
    lo = sl - hi * 32.0
    eye = jnp.where(_iota((TB, TB), 0) == _iota((TB, TB), 1), 1.0, 0.0).astype(BF16)
    slt = _mm(eye, hi.astype(BF16), _NT) * 32.0 + _mm(eye, lo.astype(BF16), _NT)
    lane = _iota((TB, WIN), 1).astype(F32)
    wins = [_window(lo_ref, base, e, m_rows) for e in range(ne)]
    rounds = jnp.int32(1)
    for a_e, hi_e in wins:
        rounds = jnp.maximum(rounds, (hi_e - a_e + WIN - 1) // WIN)

    def one_round(rd, acc):
        ws, ys = [], []
        for e, (a_e, _) in enumerate(wins):
            first = a_e + rd * WIN
            a_r = jnp.minimum(first, m_rows - WIN)
            scol = slt[:, e:e + 1]
            hit = jnp.logical_and(scol - a_r.astype(F32) == lane, scol >= first.astype(F32))
            ws.append(jnp.where(hit, 1.0, 0.0).astype(BF16))
            ys.append(y_ref[0, e, pl.ds(pl.multiple_of(a_r, ROW_ALIGN), WIN), :])
        return acc + _mm(jnp.concatenate(ws, axis=1), jnp.concatenate(ys, axis=0))

    ffn = lax.fori_loop(0, rounds, one_round, jnp.zeros(out_ref.shape[1:], F32))
    mod = mod_ref[0, 0]
    out_ref[0] = x1_ref[0] + mod[5:6, :] * _rms(ffn, gpost[...])


def _combine(lohi, slots, x1, mod, gpost, y, t0, nt, ncb):
    b, t, d = x1.shape
    ne = N_EXPERTS
    m_rows = y.shape[2]
    grid_spec = pltpu.PrefetchScalarGridSpec(
        num_scalar_prefetch=1,
        grid=(b, nt),
        in_specs=[pl.BlockSpec((1, ne, TB), lambda bi, i, *_: (bi, 0, i + t0)),
                  pl.BlockSpec((1, TB, d), lambda bi, i, *_: (bi, i + t0, 0)),
                  pl.BlockSpec((1, 1, 8, d), lambda bi, i, *_: (bi, jnp.where(i + t0 < ncb, 0, 1), 0, 0)),
                  pl.BlockSpec((1, d), lambda bi, i, *_: (0, 0)),
                  pl.BlockSpec((1, ne, m_rows, d), lambda bi, i, *_: (bi, 0, 0, 0),
                               pipeline_mode=pl.Buffered(1))],
        out_specs=pl.BlockSpec((1, TB, d), lambda bi, i, *_: (bi, i, 0)))
    return pl.pallas_call(
        functools.partial(_comb_kernel, m_rows),
        grid_spec=grid_spec,
        out_shape=jax.ShapeDtypeStruct((b, nt * TB, d), F32),
        compiler_params=_cparams(("parallel", "arbitrary")),
        name="combine",
    )(lohi, slots, x1, mod, gpost, y)


def _pos_embedding(rows, d):
    r = jnp.broadcast_to(jnp.arange(rows, dtype=F32)[:, None], (rows, GRID_W)).reshape(-1)
    col = jnp.broadcast_to(jnp.arange(GRID_W, dtype=F32)[None, :], (rows, GRID_W)).reshape(-1)
    quarter = d // 4
    freq = jnp.power(POS_BASE, -jnp.arange(quarter, dtype=F32) / quarter)
    ar, ac = r[:, None] * freq, col[:, None] * freq
    return jnp.concatenate([jnp.sin(ar), jnp.cos(ar), jnp.sin(ac), jnp.cos(ac)], axis=-1)


def _tile_bounds(off, r, ntile, cap, base):
    b = off.shape[0]
    o = off.reshape(b, N_EXPERTS, r, LANES)[:, :, :, 0]
    lo = o[:, :, ::TB // LANES][:, :, :ntile] + base
    hi = jnp.concatenate([lo[:, :, 1:], jnp.full((b, N_EXPERTS, 1), cap + base, F32)], axis=2)
    return lo, hi


def kernel(x, c, ctx, c_ctx, w_ada, b_ada, g_mix_pre, g_mix_post, g_ffn_pre, g_ffn_post,
           w_in, conv_qk, b_ml_gates, w_gla_a2, b_gla_a, g_ml_norm, g_gla_norm, w_out,
           w_router, w_e_gate, w_e_up, w_e_down):
    bsz, n_tok, d = x.shape
    lc = ctx.shape[1]
    depth = w_in.shape[0]
    ne = N_EXPERTS
    t = lc + n_tok
    ncb = lc // TB
    nblk = t // TB
    assert lc % TB == 0 and n_tok % TB == 0 and d == 1024
    cap_l = EC_FACTOR * n_tok // ne
    cap_c = EC_FACTOR * lc // ne

    pos = _pos_embedding(n_tok // GRID_W, d)
    xa = jnp.concatenate([ctx, x + pos[None]], axis=1)

    cc = jnp.zeros((8, d), F32).at[:bsz].set(c).at[bsz].set(c_ctx)
    mods = _ada(cc, w_ada, b_ada)

    wide = jnp.concatenate([w_in[:, :, 0:2048], w_in[:, :, 2064:3600]], axis=2).astype(BF16)
    narrow = jnp.concatenate([w_in[:, :, 2048:2064], w_in[:, :, 3600:3632]], axis=2)
    narrow = jnp.pad(narrow, ((0, 0), (0, 0), (0, LANES - narrow.shape[2])))
    bias_s = jnp.pad(b_ml_gates, ((0, 0), (0, LANES - b_ml_gates.shape[1])))
    w2e = jnp.zeros((depth, 2, LANES, 256), F32)
    w2e = w2e.at[:, 0, 16:32].set(w_gla_a2[:, 0]).at[:, 1, 32:48].set(w_gla_a2[:, 1])
    w_r = jnp.pad(w_router, ((0, 0), (0, 0), (0, LANES - ne)))

    for l in range(depth):
        last = l == depth - 1
        ml = mods[l, :bsz].reshape(bsz, 6, d)
        mc_ = jnp.broadcast_to(mods[l, bsz].reshape(1, 6, d), (bsz, 6, d))
        mod = jnp.pad(jnp.stack([mc_, ml], axis=1), ((0, 0), (0, 0), (0, 2), (0, 0)))

        pb, ps, pst = _in_proj(xa, mod, g_mix_pre[l][None], wide[l], narrow[l], narrow[l].T,
                               bias_s[l][None], bias_s[l][:, None], ncb)
        hf, hb = _mlstm(pb, ps, pst, conv_qk[l], ncb)
        of, ob = _gla(pb, ps, w2e[l], b_gla_a[l][:, None, :], ncb)
        x1, h2e, aff = _out_proj(hf, hb, of, ob, pb, xa, mod, g_ml_norm[l][None], g_gla_norm[l][None],
                                 w_out[l].astype(BF16), g_mix_post[l][None], g_ffn_pre[l][None],
                                 w_r[l], w_router[l].T, ncb)

        rl = n_tok // LANES
        aff_l = aff[:, :, lc:].reshape(bsz, ne * rl, LANES)
        if last:
            sll, offl = _route(None, aff_l, 0, cap_l)
            slots = jnp.pad(sll.reshape(bsz, ne, n_tok), ((0, 0), (0, 0), (lc, 0)), constant_values=UNSEL)
            lo, hi = _tile_bounds(offl, rl, nblk - ncb, cap_l, 0)
            t0, nt, m_rows = ncb, nblk - ncb, cap_l
        else:
            rc = max(lc // LANES, 8)
            aff_c = aff[:, :, :lc].reshape(bsz, ne, lc // LANES, LANES)
            aff_c = jnp.pad(aff_c, ((0, 0), (0, 0), (0, rc - lc // LANES), (0, 0)), constant_values=-1.0)
            slc, offc, sll, offl = _route(aff_c.reshape(bsz, ne * rc, LANES), aff_l, cap_c, cap_l)
            slots = jnp.concatenate([slc.reshape(bsz, ne, rc * LANES)[:, :, :lc],
                                     sll.reshape(bsz, ne, n_tok)], axis=2)
            lo_l, hi_l = _tile_bounds(offl, rl, nblk - ncb, cap_l, 0)
            lo_c, hi_c = _tile_bounds(offc, rc, ncb, cap_c, cap_l)
            lo = jnp.concatenate([lo_c, lo_l], axis=2)
            hi = jnp.concatenate([hi_c, hi_l], axis=2)
            t0, nt, m_rows = 0, nblk, cap_l + cap_c
        lohi = jnp.concatenate([lo, hi], axis=1).transpose(0, 2, 1).astype(I32).reshape(-1)
        xin = _dispatch(lohi, h2e, slots, m_rows, t0, nt)
        y = _experts(xin, w_e_gate[l].astype(BF16), w_e_up[l].astype(BF16), w_e_down[l].astype(BF16), cap_l)
        xa = _combine(lohi, slots, x1, mod, g_ffn_post[l][None], y, t0, nt, ncb)
    return xa
```

```python
import functools

import jax
import jax.numpy as jnp
from jax import lax
from jax.experimental import pallas as pl
from jax.experimental.pallas import tpu as pltpu

F32 = jnp.float32
BF16 = jnp.bfloat16
I32 = jnp.int32

EPS = 1e-6
GRID_W = 64
POS_BASE = 10000.0
N_HEADS = 4
ML_DH = 128
GLA_DK = 64
GLA_DV = 128
GLA_GATE_TAU = 16.0
N_EXPERTS = 16
EC_FACTOR = 2

LANES = 128
TB = 256
ML_L = 128
GLA_L = 64
WIN = 80
ROW_ALIGN = 16
UNSEL = 2047.0
VMEM_LIMIT = 56 * 1024 * 1024


def _cparams(sem):
    return pltpu.CompilerParams(dimension_semantics=sem, vmem_limit_bytes=VMEM_LIMIT)


def _split2(a):
    hi = a.astype(BF16)
    lo = (a - hi.astype(F32)).astype(BF16)
    return hi, lo


def _split3(a):
    hi = a.astype(BF16)
    r = a - hi.astype(F32)
    mid = r.astype(BF16)
    lo = (r - mid.astype(F32)).astype(BF16)
    return hi, mid, lo


_NN = (((1,), (0,)), ((), ()))
_NT = (((1,), (1,)), ((), ()))
_TN = (((0,), (0,)), ((), ()))


def _mm(a, b, dims=_NN):
    return lax.dot_general(a, b, dims, preferred_element_type=F32)


def _dot3(a, b, dims=_NN):
    ah, al = _split2(a)
    bh, bl = _split2(b)
    return _mm(ah, bh, dims) + (_mm(ah, bl, dims) + _mm(al, bh, dims))


def _dot_exact_l(m_bf16, x, dims=_NN):
    hi, mid, lo = _split3(x)
    return _mm(m_bf16, hi, dims) + (_mm(m_bf16, mid, dims) + _mm(m_bf16, lo, dims))


def _dot_exact_r(x, m_bf16, dims=_NN):
    hi, mid, lo = _split3(x)
    return _mm(hi, m_bf16, dims) + (_mm(mid, m_bf16, dims) + _mm(lo, m_bf16, dims))


def _rms(x, g):
    return x * lax.rsqrt(jnp.mean(x * x, axis=-1, keepdims=True) + EPS) * g


def _log_sigmoid(x):
    return jnp.minimum(x, 0.0) - jnp.log1p(jnp.exp(-jnp.abs(x)))


def _sigmoid(x):
    return 1.0 / (1.0 + jnp.exp(-x))


def _silu(x):
    return x * _sigmoid(x)


def _iota(shape, dim):
    return lax.broadcasted_iota(I32, shape, dim)


def _rev_block(i, ncb, nblk):
    return jnp.where(i < ncb, ncb - 1 - i, nblk - 1 - (i - ncb))


def _ada_kernel(c_ref, w_ref, b_ref, o_ref):
    a = _silu(c_ref[...])
    o_ref[0] = _dot3(a, w_ref[0]) + b_ref[0]


def _ada(cc, w_ada, b_ada):
    depth, d, n6 = w_ada.shape
    tn = 1536
    return pl.pallas_call(
        _ada_kernel,
        grid=(depth, n6 // tn),
        in_specs=[pl.BlockSpec((8, d), lambda l, j: (0, 0)),
                  pl.BlockSpec((1, d, tn), lambda l, j: (l, 0, j)),
                  pl.BlockSpec((1, 1, tn), lambda l, j: (l, 0, j))],
        out_specs=pl.BlockSpec((1, 8, tn), lambda l, j: (l, 0, j)),
        out_shape=jax.ShapeDtypeStruct((depth, 8, n6), F32),
        compiler_params=_cparams(("parallel", "parallel")),
        name="ada",
    )(cc, w_ada, b_ada.reshape(depth, 1, n6))


def _project(x, mod_ref, g_ref, wb_ref, ws_ref, bs_ref, pb_ref, ps_ref, pst_ref):
    h = _rms(x, g_ref[...]) * (1.0 + mod_ref[0, 0, 1:2, :]) + mod_ref[0, 0, 0:1, :]
    hh, hl = _split2(h)
    pb_ref[0] = _mm(hh, wb_ref[...])
    pr = _mm(jnp.concatenate([hh, hl], axis=0), ws_ref[...])
    n = x.shape[0]
    ps = (pr[:n, :LANES] + pr[:n, LANES:]) + (pr[n:, :LANES] + pr[n:, LANES:]) + bs_ref[...]
    ps_ref[0] = ps
    pst_ref[0] = ps.T


def _in_kernel(x_ref, mod_ref, g_ref, wb_ref, ws_ref, bs_ref, pb_ref, ps_ref, pst_ref):
    _project(x_ref[0], mod_ref, g_ref, wb_ref, ws_ref, bs_ref, pb_ref, ps_ref, pst_ref)


def _in0_kernel(ncb, x_ref, c_ref, pr_ref, pc_ref, mod_ref, g_ref, wb_ref, ws_ref, bs_ref,
                xa_ref, pb_ref, ps_ref, pst_ref):
    i = pl.program_id(1)
    half = pr_ref.shape[2]
    reps = TB // GRID_W
    prow = jnp.concatenate([jnp.broadcast_to(pr_ref[0, k:k + 1, :], (GRID_W, half)) for k in range(reps)], axis=0)
    pcol = jnp.concatenate([pc_ref[...]] * reps, axis=0)
    lat = x_ref[0] + jnp.concatenate([prow, pcol], axis=1)
    xa = jnp.where(i < ncb, c_ref[0], lat)
    xa_ref[0] = xa
    _project(xa, mod_ref, g_ref, wb_ref, ws_ref, bs_ref, pb_ref, ps_ref, pst_ref)


def _in_proj(xa, mod, g, wb, ws, bs, ncb, first=None):
    if first is None:
        b, t, d = xa.shape
    else:
        b, t, d = first[0].shape[0], first[0].shape[1] + first[1].shape[1], first[0].shape[2]
    nb = wb.shape[1]
    const = lambda shp: pl.BlockSpec(shp, lambda bi, i: tuple(0 for _ in shp))
    common = [pl.BlockSpec((1, 1, 8, d), lambda bi, i: (bi, jnp.where(i < ncb, 0, 1), 0, 0)),
              const((1, d)), const((d, nb)), const((d, 2 * LANES)), const((1, LANES))]
    out_specs = [pl.BlockSpec((1, TB, nb), lambda bi, i: (bi, i, 0)),
                 pl.BlockSpec((1, TB, LANES), lambda bi, i: (bi, i, 0)),
                 pl.BlockSpec((1, LANES, TB), lambda bi, i: (bi, 0, i))]
    out_shape = [jax.ShapeDtypeStruct((b, t, nb), F32),
                 jax.ShapeDtypeStruct((b, t, LANES), F32),
                 jax.ShapeDtypeStruct((b, LANES, t), F32)]
    if first is None:
        body = _in_kernel
        in_specs = [pl.BlockSpec((1, TB, d), lambda bi, i: (bi, i, 0))] + common
        args = (xa, mod, g, wb, ws, bs)
    else:
        x, ctx, pos_r, pos_c = first
        reps = TB // GRID_W
        body = functools.partial(_in0_kernel, ncb)
        in_specs = [pl.BlockSpec((1, TB, d), lambda bi, i: (bi, jnp.maximum(i - ncb, 0), 0)),
                    pl.BlockSpec((1, TB, d), lambda bi, i: (bi, jnp.minimum(i, ncb - 1), 0)),
                    pl.BlockSpec((1, reps, d // 2), lambda bi, i: (jnp.maximum(i - ncb, 0), 0, 0)),
                    const((GRID_W, d // 2))] + common
        out_specs = [pl.BlockSpec((1, TB, d), lambda bi, i: (bi, i, 0))] + out_specs
        out_shape = [jax.ShapeDtypeStruct((b, t, d), F32)] + out_shape
        args = (x, ctx, pos_r.reshape(-1, reps, d // 2), pos_c, mod, g, wb, ws, bs)
    return pl.pallas_call(
        body,
        grid=(b, t // TB),
        in_specs=in_specs,
        out_specs=out_specs,
        out_shape=out_shape,
        compiler_params=_cparams(("parallel", "parallel")),
        name="in_proj",
    )(*args)


def _conv3(x, hl, hr, w):
    rows = _iota(x.shape, 0)
    prev = jnp.where(rows == 0, hl, pltpu.roll(x, 1, axis=0))
    nxt = jnp.where(rows == x.shape[0] - 1, hr, pltpu.roll(x, x.shape[0] - 1, axis=0))
    return prev * w[0:1] + x * w[1:2] + nxt * w[2:3]


def _ml_dir(d, j, ncb, nblk, qk_ref, hl_ref, hr_ref, v_ref, g_ref, gt_ref, cw_ref, out_ref,
            c_s, n_s, m_s):
    ll = ML_L
    lvalid = jnp.logical_and(j != 0, j != ncb)
    rvalid = jnp.logical_and(j != ncb - 1, j != nblk - 1)
    hl = jnp.where(lvalid, hl_ref[0, 7:8, :], 0.0)
    hr = jnp.where(rvalid, hr_ref[0, 0:1, :], 0.0)
    qk = _silu(_conv3(qk_ref[0], hl, hr, cw_ref[...]))
    dq = N_HEADS * ML_DH
    v = v_ref[0]
    ls = _log_sigmoid(g_ref[0])
    lst = _log_sigmoid(gt_ref[0])
    gi = g_ref[0]
    git = gt_ref[0]
    rows = _iota((ll, ll), 0)
    cols = _iota((ll, ll), 1)
    causal = (cols >= rows) if d else (cols <= rows)
    tri = jnp.where(causal, 1.0, 0.0).astype(BF16)
    trit = jnp.where((rows >= cols) if d else (rows <= cols), 1.0, 0.0).astype(BF16)
    order = range(TB // ll - 1, -1, -1) if d else range(TB // ll)
    for c in order:
        r0 = c * ll
        bc = _dot_exact_l(tri, ls[r0:r0 + ll, :])
        br = _dot_exact_r(lst[:, r0:r0 + ll], trit)
        for h in range(N_HEADS):
            ci = 8 * d + h
            cf = 8 * d + 4 + h
            sidx = d * N_HEADS + h
            li_col = gi[r0:r0 + ll, ci:ci + 1]
            li_row = git[ci:ci + 1, r0:r0 + ll]
            bcol = bc[:, cf:cf + 1]
            brow = br[cf:cf + 1, :]
            bend = bcol[0:1, :] if d else bcol[ll - 1:ll, :]
            m_prev = m_s[sidx][:, 0:1]
            q_h = qk[r0:r0 + ll, h * ML_DH:(h + 1) * ML_DH] * (ML_DH ** -0.5)
            k_h = qk[r0:r0 + ll, dq + h * ML_DH:dq + (h + 1) * ML_DH]
            v_h = v[r0:r0 + ll, h * ML_DH:(h + 1) * ML_DH]
            qb, kb, vb = q_h.astype(BF16), k_h.astype(BF16), v_h.astype(BF16)
            dmat = jnp.where(causal, bcol - brow + li_row, -jnp.inf)
            m_intra = jnp.max(dmat, axis=1, keepdims=True)
            m_inter = bcol + m_prev
            m_t = jnp.maximum(m_inter, m_intra)
            s = _mm(qb, kb, _NT) * jnp.exp(dmat - m_t)
            a = jnp.exp(m_inter - m_t)
            c_prev = c_s[sidx]
            n_prev = n_s[sidx]
            num = _mm(s.astype(BF16), vb) + a * _mm(qb, c_prev.astype(BF16), _NT)
            den = jnp.sum(s, axis=1, keepdims=True) + a * jnp.sum(q_h * n_prev, axis=1, keepdims=True)
            hout = num / jnp.maximum(jnp.abs(den), jnp.exp(-m_t))
            out_ref[0, r0:r0 + ll, h * ML_DH:(h + 1) * ML_DH] = hout
            w_end = bend - bcol + li_col
            m_kv = jnp.max(w_end, axis=0, keepdims=True)
            e = jnp.exp(w_end - m_kv)
            c_kv = _mm((e * v_h).astype(BF16), kb, _TN)
            n_kv = jnp.sum(e * k_h, axis=0, keepdims=True)
            m_new = jnp.maximum(bend + m_prev, m_kv)
            a_old = jnp.exp(bend + m_prev - m_new)
            a_new = jnp.exp(m_kv - m_new)
            c_s[sidx] = a_old * c_prev + a_new * c_kv
            n_s[sidx] = a_old * n_prev + a_new * n_kv
            m_s[sidx] = jnp.broadcast_to(m_new, (1, LANES))


def _ml_kernel(ncb, nblk, qkf, hlf, hrf, vf, gf, gtf, qkb, hlb, hrb, vb, gb, gtb, cw,
               of_ref, ob_ref, c_s, n_s, m_s):
    i = pl.program_id(1)

    @pl.when(i == 0)
    def _():
        c_s[...] = jnp.zeros_like(c_s)
        n_s[...] = jnp.zeros_like(n_s)
        m_s[...] = jnp.zeros_like(m_s)

    _ml_dir(0, i, ncb, nblk, qkf, hlf, hrf, vf, gf, gtf, cw, of_ref, c_s, n_s, m_s)
    _ml_dir(1, _rev_block(i, ncb, nblk), ncb, nblk, qkb, hlb, hrb, vb, gb, gtb, cw, ob_ref, c_s, n_s, m_s)


def _dir_specs(blk, t, qk_w, qk_cb, v_cb):
    r8 = TB // 8
    last8 = t // 8 - 1
    return [
        pl.BlockSpec((1, TB, qk_w), lambda b, i: (b, blk(i), qk_cb)),
        pl.BlockSpec((1, 8, qk_w), lambda b, i: (b, jnp.maximum(blk(i) * r8 - 1, 0), qk_cb)),
        pl.BlockSpec((1, 8, qk_w), lambda b, i: (b, jnp.minimum((blk(i) + 1) * r8, last8), qk_cb)),
        pl.BlockSpec((1, TB, 512), lambda b, i: (b, blk(i), v_cb)),
        pl.BlockSpec((1, TB, LANES), lambda b, i: (b, blk(i), 0)),
        pl.BlockSpec((1, LANES, TB), lambda b, i: (b, 0, blk(i))),
    ]


def _mlstm(pb, ps, pst, cw, ncb):
    b, t, _ = pb.shape
    nblk = t // TB
    fwd = lambda i: i
    bwd = lambda i: _rev_block(i, ncb, nblk)
    specs = _dir_specs(fwd, t, 1024, 0, 2) + _dir_specs(bwd, t, 1024, 0, 2)
    specs.append(pl.BlockSpec((3, 1024), lambda b_, i: (0, 0)))
    args = [pb, pb, pb, pb, ps, pst] * 2 + [cw]
    ns = 2 * N_HEADS
    return pl.pallas_call(
        functools.partial(_ml_kernel, ncb, nblk),
        grid=(b, nblk),
        in_specs=specs,
        out_specs=[pl.BlockSpec((1, TB, 512), lambda b_, i: (b_, i, 0)),
                   pl.BlockSpec((1, TB, 512), lambda b_, i: (b_, bwd(i), 0))],
        out_shape=[jax.ShapeDtypeStruct((b, t, 512), F32)] * 2,
        scratch_shapes=[pltpu.VMEM((ns, ML_DH, ML_DH), F32),
                        pltpu.VMEM((ns, 1, ML_DH), F32),
                        pltpu.VMEM((ns, 1, LANES), F32)],
        compiler_params=_cparams(("parallel", "arbitrary")),
        name="mlstm",
    )(*args)


def _gla_dir(d, qk_ref, v_ref, ps_ref, w2_ref, ba_ref, out_ref, s_s):
    ll = GLA_L
    dkw = N_HEADS * GLA_DK
    qk = qk_ref[0]
    v = v_ref[0]
    la = _log_sigmoid(_dot3(ps_ref[0], w2_ref[d]) + ba_ref[d]) * (1.0 / GLA_GATE_TAU)
    rows = _iota((ll, ll), 0)
    cols = _iota((ll, ll), 1)
    causal = (cols >= rows) if d else (cols <= rows)
    tri = jnp.where(causal, 1.0, 0.0).astype(BF16)
    order = range(TB // ll - 1, -1, -1) if d else range(TB // ll)
    for c in order:
        r0 = c * ll
        bcum = _dot_exact_l(tri, la[r0:r0 + ll, :])
        ref = bcum[ll // 2:ll // 2 + 1, :]
        bend = bcum[0:1, :] if d else bcum[ll - 1:ll, :]
        qt = qk[r0:r0 + ll, 0:dkw] * (GLA_DK ** -0.5) * jnp.exp(bcum - ref)
        kt = qk[r0:r0 + ll, dkw:2 * dkw] * jnp.exp(ref - bcum)
        e_ref = jnp.exp(ref)
        e_end = jnp.exp(bend)
        e_er = jnp.exp(bend - ref)
        for h in range(N_HEADS):
            sl = slice(h * GLA_DK, (h + 1) * GLA_DK)
            sidx = d * N_HEADS + h
            qb = qt[:, sl].astype(BF16)
            kb = kt[:, sl].astype(BF16)
            vb = v[r0:r0 + ll, h * GLA_DV:(h + 1) * GLA_DV].astype(BF16)
            att = jnp.where(causal, _mm(qb, kb, _NT), 0.0)
            st = s_s[sidx]
            o = _mm(att.astype(BF16), vb) + _mm(qb, (st * e_ref[:, sl]).astype(BF16), _NT)
            out_ref[0, r0:r0 + ll, h * GLA_DV:(h + 1) * GLA_DV] = o
            kdec = (kt[:, sl] * e_er[:, sl]).astype(BF16)
            s_s[sidx] = st * e_end[:, sl] + _mm(vb, kdec, _TN)


def _gla_kernel(ncb, nblk, qkf, vf, psf, qkb, vb, psb, w2, ba, of_ref, ob_ref, s_s):
    i = pl.program_id(1)

    @pl.when(i == 0)
    def _():
        s_s[...] = jnp.zeros_like(s_s)

    _gla_dir(0, qkf, vf, psf, w2, ba, of_ref, s_s)
    _gla_dir(1, qkb, vb, psb, w2, ba, ob_ref, s_s)


def _gla(pb, ps, w2e, ba, ncb):
    b, t, _ = pb.shape
    nblk = t // TB
    fwd = lambda i: i
    bwd = lambda i: _rev_block(i, ncb, nblk)

    def dspecs(blk):
        return [pl.BlockSpec((1, TB, 512), lambda b_, i: (b_, blk(i), 4)),
                pl.BlockSpec((1, TB, 512), lambda b_, i: (b_, blk(i), 5)),
                pl.BlockSpec((1, TB, LANES), lambda b_, i: (b_, blk(i), 0))]

    specs = dspecs(fwd) + dspecs(bwd) + [
        pl.BlockSpec((2, LANES, 256), lambda b_, i: (0, 0, 0)),
        pl.BlockSpec((2, 1, 256), lambda b_, i: (0, 0, 0))]
    return pl.pallas_call(
        functools.partial(_gla_kernel, ncb, nblk),
        grid=(b, nblk),
        in_specs=specs,
        out_specs=[pl.BlockSpec((1, TB, 512), lambda b_, i: (b_, i, 0)),
                   pl.BlockSpec((1, TB, 512), lambda b_, i: (b_, bwd(i), 0))],
        out_shape=[jax.ShapeDtypeStruct((b, t, 512), F32)] * 2,
        scratch_shapes=[pltpu.VMEM((2 * N_HEADS, GLA_DV, GLA_DK), F32)],
        compiler_params=_cparams(("parallel", "arbitrary")),
        name="gla",
    )(pb, pb, ps, pb, pb, ps, w2e, ba)


def _head_norm(x, g):
    outs = []
    for h in range(N_HEADS):
        seg = x[:, h * 128:(h + 1) * 128]
        outs.append(seg * lax.rsqrt(jnp.mean(seg * seg, axis=-1, keepdims=True) + EPS))
    return jnp.concatenate(outs, axis=-1) * g


def _out_kernel(hf, hb, of, ob, og, rg, x_ref, mod_ref, gml, ggla, wo, gpost, gpre, wrt,
                x1_ref, h2e_ref, aff_ref):
    y_ml = _head_norm(hf[0] + hb[0], gml[...]) * _sigmoid(og[0])
    y_gla = _head_norm(of[0] + ob[0], ggla[...]) * _silu(rg[0])
    y = jnp.concatenate([y_ml, y_gla], axis=-1).astype(BF16)
    y2 = _mm(y, wo[...])
    mod = mod_ref[0, 0]
    x1 = x_ref[0] + mod[2:3, :] * _rms(y2, gpost[...])
    x1_ref[0] = x1
    h2 = _rms(x1, gpre[...]) * (1.0 + mod[4:5, :]) + mod[3:4, :]
    d = h2.shape[1]
    lt = _dot3(wrt[...], h2, _NT)
    ext = jnp.exp(lt - jnp.max(lt, axis=0, keepdims=True))
    aff = ext / jnp.sum(ext, axis=0, keepdims=True)
    aff_ref[0] = aff
    ne = N_EXPERTS
    afft = jnp.concatenate([aff, jnp.zeros((LANES - ne, aff.shape[1]), F32)], axis=0).T
    a_hi, a_mid, a_lo = _split3(afft)
    pieces = a_hi.astype(F32) + pltpu.roll(a_mid.astype(F32), ne, axis=1) + pltpu.roll(a_lo.astype(F32), 2 * ne, axis=1)
    h2e_ref[0, :, 0:d] = h2.astype(BF16)
    h2e_ref[0, :, d:d + LANES] = pieces.astype(BF16)


def _out_proj(hf, hb, of, ob, pb, xa, mod, gml, ggla, wo, gpost, gpre, wrt, ncb):
    b, t, d = xa.shape
    de = d + LANES
    tile = lambda cb: pl.BlockSpec((1, TB, 512), lambda bi, i: (bi, i, cb))
    full = lambda shp: pl.BlockSpec(shp, lambda bi, i: tuple(0 for _ in shp))
    return pl.pallas_call(
        _out_kernel,
        grid=(b, t // TB),
        in_specs=[tile(0), tile(0), tile(0), tile(0), tile(3), tile(6),
                  pl.BlockSpec((1, TB, d), lambda bi, i: (bi, i, 0)),
                  pl.BlockSpec((1, 1, 8, d), lambda bi, i: (bi, jnp.where(i < ncb, 0, 1), 0, 0)),
                  full((1, 512)), full((1, 512)), full((d, d)), full((1, d)), full((1, d)),
                  full((N_EXPERTS, d))],
        out_specs=[pl.BlockSpec((1, TB, d), lambda bi, i: (bi, i, 0)),
                   pl.BlockSpec((1, TB, de), lambda bi, i: (bi, i, 0)),
                   pl.BlockSpec((1, N_EXPERTS, TB), lambda bi, i: (bi, 0, i))],
        out_shape=[jax.ShapeDtypeStruct((b, t, d), F32),
                   jax.ShapeDtypeStruct((b, t, de), BF16),
                   jax.ShapeDtypeStruct((b, N_EXPERTS, t), F32)],
        compiler_params=_cparams(("parallel", "parallel")),
        name="out_proj",
    )(hf, hb, of, ob, pb, pb, xa, mod, gml, ggla, wo, gpost, gpre, wrt)


def _cumsum_blocks(x, r):
    n = x.shape[0]
    xb = x.astype(BF16)
    li = _iota((LANES, LANES), 0)
    lj = _iota((LANES, LANES), 1)
    upper = jnp.where(li <= lj, 1.0, 0.0).astype(BF16)
    ones = jnp.ones((LANES, LANES), BF16)
    inrow = _mm(xb, upper)
    tot = _mm(xb, ones)
    ri = _iota((n, n), 0)
    rj = _iota((n, n), 1)
    same = (ri // r) == (rj // r)
    strict = jnp.where(jnp.logical_and(same, rj < ri), 1.0, 0.0).astype(BF16)
    off = _mm(strict, tot.astype(BF16))
    return inrow + off, off


def _select(aff, r, cap, base_slot):
    ne = N_EXPERTS
    n = ne * r
    aff3 = aff.reshape(ne, r, LANES)
    capf = jnp.float32(cap)

    def body(k, prefix):
        cand = prefix | (jnp.int32(1) << (30 - k))
        candf = lax.bitcast_convert_type(cand, F32)
        cnt = jnp.sum(jnp.where(aff3 >= candf, 1.0, 0.0), axis=(1, 2), keepdims=True)
        return jnp.where(cnt >= capf, cand, prefix)

    thr = lax.bitcast_convert_type(lax.fori_loop(0, 31, body, jnp.zeros((ne, 1, 1), I32)), F32)
    gt = jnp.where(aff3 > thr, 1.0, 0.0)
    eq = jnp.where(aff3 == thr, 1.0, 0.0)
    need = capf - jnp.sum(gt, axis=(1, 2), keepdims=True)
    eq2 = eq.reshape(n, LANES)
    cs_eq, _ = _cumsum_blocks(eq2, r)
    eq_rank = (cs_eq - eq2).reshape(ne, r, LANES)
    sel = (gt + eq * jnp.where(eq_rank < need, 1.0, 0.0)).reshape(n, LANES)
    cs, off = _cumsum_blocks(sel, r)
    slot = jnp.where(sel > 0.5, cs - 1.0 + base_slot, UNSEL)
    return slot, off


def _sel_kernel(rc, rl, cap_c, cap_l, *refs):
    if rc:
        affc, affl, slc, offc, sll, offl = refs
        slc[0], offc[0] = _select(affc[0], rc, cap_c, float(cap_l))
    else:
        affl, sll, offl = refs
    sll[0], offl[0] = _select(affl[0], rl, cap_l, 0.0)


def _route(aff_c, aff_l, cap_c, cap_l):
    b = aff_l.shape[0]
    ne = N_EXPERTS
    rl = aff_l.shape[1] // ne
    rc = aff_c.shape[1] // ne if aff_c is not None else 0
    args = ([aff_c] if rc else []) + [aff_l]
    in_specs, out_shape, out_specs = [], [], []
    for a in args:
        spec = pl.BlockSpec((1,) + a.shape[1:], lambda bi: (bi, 0, 0))
        in_specs.append(spec)
        out_shape += [jax.ShapeDtypeStruct(a.shape, F32)] * 2
        out_specs += [spec, spec]
    return pl.pallas_call(
        functools.partial(_sel_kernel, rc, rl, cap_c, cap_l),
        grid=(b,),
        in_specs=in_specs,
        out_specs=out_specs,
        out_shape=out_shape,
        compiler_params=_cparams(("parallel",)),
        name="route",
    )(*args)


def _window(lo_ref, base, e, m_rows):
    lo_e = lo_ref[base + e]
    hi_e = lo_ref[base + N_EXPERTS + e]
    a_e = jnp.minimum((lo_e // ROW_ALIGN) * ROW_ALIGN, m_rows - WIN)
    return a_e, hi_e


def _disp_kernel(m_rows, ng, lo_ref, h_ref, slot_ref, x_ref):
    ne = N_EXPERTS
    gi = pl.program_id(1)
    i = pl.program_id(2)
    base = (pl.program_id(0) * pl.num_programs(2) + i) * (2 * ne)

    @pl.when(i == 0)
    def _():
        x_ref[...] = jnp.zeros_like(x_ref)

    h = h_ref[0]
    sl = slot_ref[0]
    sub = _iota((WIN, TB), 0).astype(F32)
    wins = [_window(lo_ref, base, gi * ng + k, m_rows) for k in range(ng)]
    rounds = jnp.int32(1)
    for a_e, hi_e in wins:
        rounds = jnp.maximum(rounds, (hi_e - a_e + WIN - 1) // WIN)

    def one_round(rd, carry):
        wts, starts = [], []
        for k, (a_e, _) in enumerate(wins):
            first = a_e + rd * WIN
            a_r = jnp.minimum(first, m_rows - WIN)
            srow = sl[k:k + 1, :]
            hit = jnp.logical_and(srow - a_r.astype(F32) == sub, srow >= first.astype(F32))
            wts.append(jnp.where(hit, 1.0, 0.0).astype(BF16))
            starts.append(a_r)
        g = _mm(jnp.concatenate(wts, axis=0), h).astype(BF16)
        for k, a_r in enumerate(starts):
            rows = pl.ds(pl.multiple_of(a_r, ROW_ALIGN), WIN)
            x_ref[0, k, rows, :] = x_ref[0, k, rows, :] + g[k * WIN:(k + 1) * WIN, :]
        return carry

    lax.fori_loop(0, rounds, one_round, 0)


def _dispatch(lohi, h2e, slots, m_rows, t0, nt):
    b, t, de = h2e.shape
    ne = N_EXPERTS
    ng = 4
    grid_spec = pltpu.PrefetchScalarGridSpec(
        num_scalar_prefetch=1,
        grid=(b, ne // ng, nt),
        in_specs=[pl.BlockSpec((1, TB, de), lambda bi, gi, i, *_: (bi, i + t0, 0)),
                  pl.BlockSpec((1, ng, TB), lambda bi, gi, i, *_: (bi * (ne // ng) + gi, 0, i + t0))],
        out_specs=pl.BlockSpec((1, ng, m_rows, de), lambda bi, gi, i, *_: (bi, gi, 0, 0)))
    return pl.pallas_call(
        functools.partial(_disp_kernel, m_rows, ng),
        grid_spec=grid_spec,
        out_shape=jax.ShapeDtypeStruct((b, ne, m_rows, de), BF16),
        compiler_params=_cparams(("parallel", "parallel", "arbitrary")),
        name="dispatch",
    )(lohi, h2e, slots.reshape(b * (ne // ng), ng, t))


def _row_chunks(m_rows, cap_l):
    step = min(256, cap_l)
    chunks = [(s, step) for s in range(0, cap_l, step)]
    if m_rows > cap_l:
        chunks.append((cap_l, m_rows - cap_l))
    return chunks


def _exp_kernel(m_rows, cap_l, x_ref, wg_ref, wu_ref, wd_ref, y_ref, wg_s, wu_s, wd_s):
    ei = pl.program_id(0)
    d = wg_ref.shape[2]

    @pl.when(pl.program_id(1) == 0)
    def _():
        wg_s[...] = wg_ref[0, 0].astype(BF16)
        wu_s[...] = wu_ref[0, 0].astype(BF16)
        wd_s[...] = wd_ref[0, 0].astype(BF16)

    for r0, mc in _row_chunks(m_rows, cap_l):
        xs = x_ref[0, 0, r0:r0 + mc, 0:d]
        hid = _silu(_mm(xs, wg_s[...])) * _mm(xs, wu_s[...])
        y = _mm(hid.astype(BF16), wd_s[...])
        pieces = x_ref[0, 0, r0:r0 + mc, d:d + LANES].astype(F32)
        lane = _iota((mc, LANES), 1)
        mine = jnp.logical_and(lane % N_EXPERTS == ei, lane < 3 * N_EXPERTS)
        gate = jnp.sum(jnp.where(mine, pieces, 0.0), axis=1, keepdims=True)
        y_ref[0, 0, r0:r0 + mc, :] = (y * gate).astype(BF16)


def _experts(xin, wg, wu, wd, layer, cap_l):
    b, ne, m_rows, de = xin.shape
    _, _, d, f = wg.shape
    return pl.pallas_call(
        functools.partial(_exp_kernel, m_rows, cap_l),
        grid=(ne, b),
        in_specs=[pl.BlockSpec((1, 1, m_rows, de), lambda ei, bi: (bi, ei, 0, 0)),
                  pl.BlockSpec((1, 1, d, f), lambda ei, bi: (layer, ei, 0, 0)),
                  pl.BlockSpec((1, 1, d, f), lambda ei, bi: (layer, ei, 0, 0)),
                  pl.BlockSpec((1, 1, f, d), lambda ei, bi: (layer, ei, 0, 0))],
        out_specs=pl.BlockSpec((1, 1, m_rows, d), lambda ei, bi: (bi, ei, 0, 0)),
        out_shape=jax.ShapeDtypeStruct((b, ne, m_rows, d), BF16),
        scratch_shapes=[pltpu.VMEM((d, f), BF16), pltpu.VMEM((d, f), BF16), pltpu.VMEM((f, d), BF16)],
        compiler_params=_cparams(("arbitrary", "arbitrary")),
        name="experts",
    )(xin, wg, wu, wd)


def _comb_kernel(m_rows, lo_ref, slot_ref, x1_ref, mod_ref, gpost, y_ref, out_ref):
    ne = N_EXPERTS
    i = pl.program_id(1)
    base = (pl.program_id(0) * pl.num_programs(1) + i) * (2 * ne)
    sl = slot_ref[0]
    hi = jnp.floor(sl * (1.0 / 32.0))
    lo = sl - hi * 32.0
    eye = jnp.where(_iota((TB, TB), 0) == _iota((TB, TB), 1), 1.0, 0.0).astype(BF16)
    slt = _mm(eye, hi.astype(BF16), _NT) * 32.0 + _mm(eye, lo.astype(BF16), _NT)
    lane = _iota((TB, WIN), 1).astype(F32)
    wins = [_window(lo_ref, base, e, m_rows) for e in range(ne)]
    rounds = jnp.int32(1)
    for a_e, hi_e in wins:
        rounds = jnp.maximum(rounds, (hi_e - a_e + WIN - 1) // WIN)

    def one_round(rd, acc):
        ws, ys = [], []
        for e, (a_e, _) in enumerate(wins):
            first = a_e + rd * WIN
            a_r = jnp.minimum(first, m_rows - WIN)
            scol = slt[:, e:e + 1]
            hit = jnp.logical_and(scol - a_r.astype(F32) == lane, scol >= first.astype(F32))
            ws.append(jnp.where(hit, 1.0, 0.0).astype(BF16))
            ys.append(y_ref[0, e, pl.ds(pl.multiple_of(a_r, ROW_ALIGN), WIN), :])
        return acc + _mm(jnp.concatenate(ws, axis=1), jnp.concatenate(ys, axis=0))

    ffn = lax.fori_loop(0, rounds, one_round, jnp.zeros(out_ref.shape[1:], F32))
    mod = mod_ref[0, 0]
    out_ref[0] = x1_ref[0] + mod[5:6, :] * _rms(ffn, gpost[...])


def _combine(lohi, slots, x1, mod, gpost, y, t0, nt, ncb):
    b, t, d = x1.shape
    ne = N_EXPERTS
    m_rows = y.shape[2]
    grid_spec = pltpu.PrefetchScalarGridSpec(
        num_scalar_prefetch=1,
        grid=(b, nt),
        in_specs=[pl.BlockSpec((1, ne, TB), lambda bi, i, *_: (bi, 0, i + t0)),
                  pl.BlockSpec((1, TB, d), lambda bi, i, *_: (bi, i + t0, 0)),
                  pl.BlockSpec((1, 1, 8, d), lambda bi, i, *_: (bi, jnp.where(i + t0 < ncb, 0, 1), 0, 0)),
                  pl.BlockSpec((1, d), lambda bi, i, *_: (0, 0)),
                  pl.BlockSpec((1, ne, m_rows, d), lambda bi, i, *_: (bi, 0, 0, 0),
                               pipeline_mode=pl.Buffered(1))],
        out_specs=pl.BlockSpec((1, TB, d), lambda bi, i, *_: (bi, i, 0)))
    return pl.pallas_call(
        functools.partial(_comb_kernel, m_rows),
        grid_spec=grid_spec,
        out_shape=jax.ShapeDtypeStruct((b, nt * TB, d), F32),
        compiler_params=_cparams(("parallel", "arbitrary")),
        name="combine",
    )(lohi, slots, x1, mod, gpost, y)


def _pos_tables(rows, d):
    quarter = d // 4
    freq = jnp.power(POS_BASE, -jnp.arange(quarter, dtype=F32) / quarter)
    ar = jnp.arange(rows, dtype=F32)[:, None] * freq
    ac = jnp.arange(GRID_W, dtype=F32)[:, None] * freq
    return (jnp.concatenate([jnp.sin(ar), jnp.cos(ar)], axis=-1),
            jnp.concatenate([jnp.sin(ac), jnp.cos(ac)], axis=-1))


def _tile_bounds(off, r, ntile, cap, base):
    b = off.shape[0]
    o = off.reshape(b, N_EXPERTS, r, LANES)[:, :, :, 0]
    lo = o[:, :, ::TB // LANES][:, :, :ntile] + base
    hi = jnp.concatenate([lo[:, :, 1:], jnp.full((b, N_EXPERTS, 1), cap + base, F32)], axis=2)
    return lo, hi


def kernel(x, c, ctx, c_ctx, w_ada, b_ada, g_mix_pre, g_mix_post, g_ffn_pre, g_ffn_post,
           w_in, conv_qk, b_ml_gates, w_gla_a2, b_gla_a, g_ml_norm, g_gla_norm, w_out,
           w_router, w_e_gate, w_e_up, w_e_down):
    bsz, n_tok, d = x.shape
    lc = ctx.shape[1]
    depth = w_in.shape[0]
    ne = N_EXPERTS
    t = lc + n_tok
    ncb = lc // TB
    nblk = t // TB
    assert lc % TB == 0 and n_tok % TB == 0 and d == 1024
    cap_l = EC_FACTOR * n_tok // ne
    cap_c = EC_FACTOR * lc // ne

    assert TB % GRID_W == 0
    pos_r, pos_c = _pos_tables(n_tok // GRID_W, d)
    xa = None

    cc = jnp.zeros((8, d), F32).at[:bsz].set(c).at[bsz].set(c_ctx)
    mods = _ada(cc, w_ada, b_ada)

    wide = jnp.concatenate([w_in[:, :, 0:2048], w_in[:, :, 2064:3600]], axis=2).astype(BF16)
    narrow = jnp.concatenate([w_in[:, :, 2048:2064], w_in[:, :, 3600:3632]], axis=2)
    narrow = jnp.pad(narrow, ((0, 0), (0, 0), (0, LANES - narrow.shape[2])))
    narrow = jnp.concatenate(_split2(narrow), axis=2)
    bias_s = jnp.pad(b_ml_gates, ((0, 0), (0, LANES - b_ml_gates.shape[1])))
    w2e = jnp.zeros((depth, 2, LANES, 256), F32)
    w2e = w2e.at[:, 0, 16:32].set(w_gla_a2[:, 0]).at[:, 1, 32:48].set(w_gla_a2[:, 1])

    for l in range(depth):
        last = l == depth - 1
        ml = mods[l, :bsz].reshape(bsz, 6, d)
        mc_ = jnp.broadcast_to(mods[l, bsz].reshape(1, 6, d), (bsz, 6, d))
        mod = jnp.pad(jnp.stack([mc_, ml], axis=1), ((0, 0), (0, 0), (0, 2), (0, 0)))

        if l == 0:
            xa, pb, ps, pst = _in_proj(None, mod, g_mix_pre[l][None], wide[l], narrow[l], bias_s[l][None], ncb,
                                       first=(x, ctx, pos_r, pos_c))
        else:
            pb, ps, pst = _in_proj(xa, mod, g_mix_pre[l][None], wide[l], narrow[l], bias_s[l][None], ncb)
        hf, hb = _mlstm(pb, ps, pst, conv_qk[l], ncb)
        of, ob = _gla(pb, ps, w2e[l], b_gla_a[l][:, None, :], ncb)
        x1, h2e, aff = _out_proj(hf, hb, of, ob, pb, xa, mod, g_ml_norm[l][None], g_gla_norm[l][None],
                                 w_out[l].astype(BF16), g_mix_post[l][None], g_ffn_pre[l][None],
                                 w_router[l].T, ncb)

        rl = n_tok // LANES
        aff_l = aff[:, :, lc:].reshape(bsz, ne * rl, LANES)
        if last:
            sll, offl = _route(None, aff_l, 0, cap_l)
            slots = jnp.pad(sll.reshape(bsz, ne, n_tok), ((0, 0), (0, 0), (lc, 0)), constant_values=UNSEL)
            lo, hi = _tile_bounds(offl, rl, nblk - ncb, cap_l, 0)
            t0, nt, m_rows = ncb, nblk - ncb, cap_l
        else:
            rc = max(lc // LANES, 8)
            aff_c = aff[:, :, :lc].reshape(bsz, ne, lc // LANES, LANES)
            aff_c = jnp.pad(aff_c, ((0, 0), (0, 0), (0, rc - lc // LANES), (0, 0)), constant_values=-1.0)
            slc, offc, sll, offl = _route(aff_c.reshape(bsz, ne * rc, LANES), aff_l, cap_c, cap_l)
            slots = jnp.concatenate([slc.reshape(bsz, ne, rc * LANES)[:, :, :lc],
                                     sll.reshape(bsz, ne, n_tok)], axis=2)
            lo_l, hi_l = _tile_bounds(offl, rl, nblk - ncb, cap_l, 0)
            lo_c, hi_c = _tile_bounds(offc, rc, ncb, cap_c, cap_l)
            lo = jnp.concatenate([lo_c, lo_l], axis=2)
            hi = jnp.concatenate([hi_c, hi_l], axis=2)
            t0, nt, m_rows = 0, nblk, cap_l + cap_c
        lohi = jnp.concatenate([lo, hi], axis=1).transpose(0, 2, 1).astype(I32).reshape(-1)
        xin = _dispatch(lohi, h2e, slots, m_rows, t0, nt)
        y = _experts(xin, w_e_gate, w_e_up, w_e_down, l, cap_l)
        xa = _combine(lohi, slots, x1, mod, g_ffn_post[l][None], y, t0, nt, ncb)
    return xa
```

```python
import functools

import jax
import jax.numpy as jnp
from jax import lax
from jax.experimental import pallas as pl
from jax.experimental.pallas import tpu as pltpu

F32 = jnp.float32
BF16 = jnp.bfloat16
I32 = jnp.int32

EPS = 1e-6
GRID_W = 64
POS_BASE = 10000.0
N_HEADS = 4
ML_DH = 128
GLA_DK = 64
GLA_DV = 128
GLA_GATE_TAU = 16.0
N_EXPERTS = 16
EC_FACTOR = 2

LANES = 128
TB = 256
ML_L = 128
GLA_L = 128
WIN = 80
ROW_ALIGN = 16
UNSEL = 2047.0
VMEM_LIMIT = 56 * 1024 * 1024


def _cparams(sem):
    return pltpu.CompilerParams(dimension_semantics=sem, vmem_limit_bytes=VMEM_LIMIT)


def _split2(a):
    hi = a.astype(BF16)
    lo = (a - hi.astype(F32)).astype(BF16)
    return hi, lo


def _split3(a):
    hi = a.astype(BF16)
    r = a - hi.astype(F32)
    mid = r.astype(BF16)
    lo = (r - mid.astype(F32)).astype(BF16)
    return hi, mid, lo


_NN = (((1,), (0,)), ((), ()))
_NT = (((1,), (1,)), ((), ()))
_TN = (((0,), (0,)), ((), ()))


def _mm(a, b, dims=_NN):
    return lax.dot_general(a, b, dims, preferred_element_type=F32)


def _dot3(a, b, dims=_NN):
    ah, al = _split2(a)
    bh, bl = _split2(b)
    return _mm(ah, bh, dims) + (_mm(ah, bl, dims) + _mm(al, bh, dims))


def _dot_exact_l(m_bf16, x, dims=_NN):
    hi, mid, lo = _split3(x)
    return _mm(m_bf16, hi, dims) + (_mm(m_bf16, mid, dims) + _mm(m_bf16, lo, dims))


def _dot_exact_r(x, m_bf16, dims=_NN):
    hi, mid, lo = _split3(x)
    return _mm(hi, m_bf16, dims) + (_mm(mid, m_bf16, dims) + _mm(lo, m_bf16, dims))


def _rms(x, g):
    return x * lax.rsqrt(jnp.mean(x * x, axis=-1, keepdims=True) + EPS) * g


def _log_sigmoid(x):
    return jnp.minimum(x, 0.0) - jnp.log1p(jnp.exp(-jnp.abs(x)))


def _sigmoid(x):
    return 1.0 / (1.0 + jnp.exp(-x))


def _silu(x):
    return x * _sigmoid(x)


def _iota(shape, dim):
    return lax.broadcasted_iota(I32, shape, dim)


def _rev_block(i, ncb, nblk):
    return jnp.where(i < ncb, ncb - 1 - i, nblk - 1 - (i - ncb))


def _ada_kernel(c_ref, w_ref, b_ref, o_ref):
    a = _silu(c_ref[...])
    o_ref[0] = _dot3(a, w_ref[0]) + b_ref[0]


def _ada(cc, w_ada, b_ada):
    depth, d, n6 = w_ada.shape
    tn = 1536
    return pl.pallas_call(
        _ada_kernel,
        grid=(depth, n6 // tn),
        in_specs=[pl.BlockSpec((8, d), lambda l, j: (0, 0)),
                  pl.BlockSpec((1, d, tn), lambda l, j: (l, 0, j)),
                  pl.BlockSpec((1, 1, tn), lambda l, j: (l, 0, j))],
        out_specs=pl.BlockSpec((1, 8, tn), lambda l, j: (l, 0, j)),
        out_shape=jax.ShapeDtypeStruct((depth, 8, n6), F32),
        compiler_params=_cparams(("parallel", "parallel")),
        name="ada",
    )(cc, w_ada, b_ada.reshape(depth, 1, n6))


def _project(x, mod_ref, g_ref, wb_ref, ws_ref, bs_ref, pb_ref, ps_ref):
    h = _rms(x, g_ref[...]) * (1.0 + mod_ref[0, 0, 1:2, :]) + mod_ref[0, 0, 0:1, :]
    hh, hl = _split2(h)
    pb_ref[0] = _mm(hh, wb_ref[...])
    pr = _mm(jnp.concatenate([hh, hl], axis=0), ws_ref[...])
    n = x.shape[0]
    ps_ref[0] = (pr[:n, :LANES] + pr[:n, LANES:]) + (pr[n:, :LANES] + pr[n:, LANES:]) + bs_ref[...]


def _in_kernel(x_ref, mod_ref, g_ref, wb_ref, ws_ref, bs_ref, pb_ref, ps_ref):
    _project(x_ref[0], mod_ref, g_ref, wb_ref, ws_ref, bs_ref, pb_ref, ps_ref)


def _in0_kernel(ncb, x_ref, c_ref, pr_ref, pc_ref, mod_ref, g_ref, wb_ref, ws_ref, bs_ref,
                xa_ref, pb_ref, ps_ref):
    i = pl.program_id(1)
    half = pr_ref.shape[2]
    reps = TB // GRID_W
    prow = jnp.concatenate([jnp.broadcast_to(pr_ref[0, k:k + 1, :], (GRID_W, half)) for k in range(reps)], axis=0)
    pcol = jnp.concatenate([pc_ref[...]] * reps, axis=0)
    lat = x_ref[0] + jnp.concatenate([prow, pcol], axis=1)
    xa = jnp.where(i < ncb, c_ref[0], lat)
    xa_ref[0] = xa
    _project(xa, mod_ref, g_ref, wb_ref, ws_ref, bs_ref, pb_ref, ps_ref)


def _in_proj(xa, mod, g, wb, ws, bs, ncb, first=None):
    if first is None:
        b, t, d = xa.shape
    else:
        b, t, d = first[0].shape[0], first[0].shape[1] + first[1].shape[1], first[0].shape[2]
    nb = wb.shape[1]
    const = lambda shp: pl.BlockSpec(shp, lambda bi, i: tuple(0 for _ in shp))
    common = [pl.BlockSpec((1, 1, 8, d), lambda bi, i: (bi, jnp.where(i < ncb, 0, 1), 0, 0)),
              const((1, d)), const((d, nb)), const((d, 2 * LANES)), const((1, LANES))]
    out_specs = [pl.BlockSpec((1, TB, nb), lambda bi, i: (bi, i, 0)),
                 pl.BlockSpec((1, TB, LANES), lambda bi, i: (bi, i, 0))]
    out_shape = [jax.ShapeDtypeStruct((b, t, nb), F32),
                 jax.ShapeDtypeStruct((b, t, LANES), F32)]
    if first is None:
        body = _in_kernel
        in_specs = [pl.BlockSpec((1, TB, d), lambda bi, i: (bi, i, 0))] + common
        args = (xa, mod, g, wb, ws, bs)
    else:
        x, ctx, pos_r, pos_c = first
        reps = TB // GRID_W
        body = functools.partial(_in0_kernel, ncb)
        in_specs = [pl.BlockSpec((1, TB, d), lambda bi, i: (bi, jnp.maximum(i - ncb, 0), 0)),
                    pl.BlockSpec((1, TB, d), lambda bi, i: (bi, jnp.minimum(i, ncb - 1), 0)),
                    pl.BlockSpec((1, reps, d // 2), lambda bi, i: (jnp.maximum(i - ncb, 0), 0, 0)),
                    const((GRID_W, d // 2))] + common
        out_specs = [pl.BlockSpec((1, TB, d), lambda bi, i: (bi, i, 0))] + out_specs
        out_shape = [jax.ShapeDtypeStruct((b, t, d), F32)] + out_shape
        args = (x, ctx, pos_r.reshape(-1, reps, d // 2), pos_c, mod, g, wb, ws, bs)
    return pl.pallas_call(
        body,
        grid=(b, t // TB),
        in_specs=in_specs,
        out_specs=out_specs,
        out_shape=out_shape,
        compiler_params=_cparams(("parallel", "parallel")),
        name="in_proj",
    )(*args)


def _conv3(x, hl, hr, w):
    rows = _iota(x.shape, 0)
    prev = jnp.where(rows == 0, hl, pltpu.roll(x, 1, axis=0))
    nxt = jnp.where(rows == x.shape[0] - 1, hr, pltpu.roll(x, x.shape[0] - 1, axis=0))
    return prev * w[0:1] + x * w[1:2] + nxt * w[2:3]


def _qkconv_kernel(ncb, nblk, qk_ref, hl_ref, hr_ref, cw_ref, o_ref):
    j = pl.program_id(1)
    lvalid = jnp.logical_and(j != 0, j != ncb)
    rvalid = jnp.logical_and(j != ncb - 1, j != nblk - 1)
    hl = jnp.where(lvalid, hl_ref[0, 7:8, :], 0.0)
    hr = jnp.where(rvalid, hr_ref[0, 0:1, :], 0.0)
    y = _silu(_conv3(qk_ref[0], hl, hr, cw_ref[...]))
    dq = N_HEADS * ML_DH
    o_ref[0, :, 0:dq] = (y[:, 0:dq] * (ML_DH ** -0.5)).astype(BF16)
    o_ref[0, :, dq:] = y[:, dq:].astype(BF16)


def _qkconv(pb, cw, ncb):
    b, t, _ = pb.shape
    nblk = t // TB
    w = 2 * N_HEADS * ML_DH
    specs = _dir_specs(lambda i: i, t, w, 0, 2)[:3] + [pl.BlockSpec((3, w), lambda b_, i: (0, 0))]
    return pl.pallas_call(
        functools.partial(_qkconv_kernel, ncb, nblk),
        grid=(b, nblk),
        in_specs=specs,
        out_specs=pl.BlockSpec((1, TB, w), lambda b_, i: (b_, i, 0)),
        out_shape=jax.ShapeDtypeStruct((b, t, w), BF16),
        compiler_params=_cparams(("parallel", "parallel")),
        name="qkconv",
    )(pb, pb, pb, cw)


def _cummax_rows(x, reverse):
    n = x.shape[0]
    rows = _iota(x.shape, 0)
    s = 1
    while s < n:
        if reverse:
            sh = jnp.where(rows < n - s, pltpu.roll(x, n - s, axis=0), -jnp.inf)
        else:
            sh = jnp.where(rows >= s, pltpu.roll(x, s, axis=0), -jnp.inf)
        x = jnp.maximum(x, sh)
        s *= 2
    return x


def _ml_pair(fwd, bwd, c_s, m_s):
    ll = ML_L
    nch = TB // ll
    dq = N_HEADS * ML_DH
    rows = _iota((ll, ll), 0)
    cols = _iota((ll, ll), 1)
    causal = [cols <= rows, cols >= rows]
    tri = [jnp.where(m, 1.0, 0.0).astype(BF16) for m in causal]
    ones = jnp.ones((ll, ML_DH), BF16)
    qk = [fwd[0][0], bwd[0][0]]
    v = [fwd[1][0], bwd[1][0]]
    g = [fwd[2][0], bwd[2][0]]
    outs = [fwd[3], bwd[3]]
    units = [(d, h) for d in range(2) for h in range(N_HEADS)]
    cx = {u: c_s[u[0] * N_HEADS + u[1]] for u in units}
    m_row = [m_s[0], m_s[1]]
    for step in range(nch):
        r0 = [step * ll, (nch - 1 - step) * ll]
        alpha, a_in, em, e_w, ut, a_old, a_new = [], [], [], [], [], [], []
        for d in range(2):
            gc = g[d][r0[d]:r0[d] + ll, :]
            bc = _dot_exact_l(tri[d], _log_sigmoid(gc))
            u = pltpu.roll(gc, 4, axis=1) - bc
            cm = _cummax_rows(u, bool(d))
            neg_alpha = jnp.maximum(m_row[d], cm)
            alpha.append(-neg_alpha)
            a_in.append(jnp.exp(m_row[d] - neg_alpha))
            em.append(jnp.exp(-neg_alpha - bc))
            last = slice(0, 1) if d else slice(ll - 1, ll)
            cm_end = cm[last, :]
            bend = bc[last, :]
            e_w.append(jnp.exp(u - cm_end))
            ut.append(u.T)
            m_kv = bend + cm_end
            m_new = jnp.maximum(bend + m_row[d], m_kv)
            a_old.append(jnp.exp(bend + m_row[d] - m_new))
            a_new.append(jnp.exp(m_kv - m_new))
            m_row[d] = m_new
        lane = lambda d, h: 8 * d + 4 + h
        qb = {(d, h): qk[d][r0[d]:r0[d] + ll, h * ML_DH:(h + 1) * ML_DH] for d, h in units}
        kb = {(d, h): qk[d][r0[d]:r0[d] + ll, dq + h * ML_DH:dq + (h + 1) * ML_DH] for d, h in units}
        vh = {(d, h): v[d][r0[d]:r0[d] + ll, h * ML_DH:(h + 1) * ML_DH] for d, h in units}
        sc = {u: _mm(qb[u], kb[u], _NT) for u in units}
        inter = {u: _mm(qb[u], cx[u].astype(BF16), _NT) for u in units}
        sb = {}
        for d, h in units:
            c = lane(d, h)
            arg = jnp.where(causal[d], alpha[d][:, c:c + 1] + ut[d][c:c + 1, :], -jnp.inf)
            sb[(d, h)] = (sc[(d, h)] * jnp.exp(arg)).astype(BF16)
        ckv = {}
        for d, h in units:
            c = lane(d, h)
            ew = e_w[d][:, c:c + 1]
            ev = jnp.concatenate([(ew * vh[(d, h)]).astype(BF16), jnp.broadcast_to(ew, (ll, ML_DH)).astype(BF16)], axis=1)
            ckv[(d, h)] = _mm(ev, kb[(d, h)], _TN)
        for d, h in units:
            c = lane(d, h)
            nd = (_mm(sb[(d, h)], jnp.concatenate([vh[(d, h)].astype(BF16), ones], axis=1))
                  + a_in[d][:, c:c + 1] * inter[(d, h)])
            den = jnp.maximum(jnp.abs(nd[:, ML_DH:]), em[d][:, c:c + 1])
            outs[d][0, r0[d]:r0[d] + ll, h * ML_DH:(h + 1) * ML_DH] = nd[:, :ML_DH] / den
        cx = {(d, h): a_old[d][:, lane(d, h):lane(d, h) + 1] * cx[(d, h)]
              + a_new[d][:, lane(d, h):lane(d, h) + 1] * ckv[(d, h)] for d, h in units}
    for d, h in units:
        c_s[d * N_HEADS + h] = cx[(d, h)]
    m_s[0] = m_row[0]
    m_s[1] = m_row[1]


def _ml2_kernel(qkf, vf, gf, qkb, vb, gb, of_ref, ob_ref, c_s, m_s):
    @pl.when(pl.program_id(1) == 0)
    def _():
        c_s[...] = jnp.zeros_like(c_s)
        m_s[...] = jnp.zeros_like(m_s)

    _ml_pair((qkf, vf, gf, of_ref), (qkb, vb, gb, ob_ref), c_s, m_s)


def _mlstm2(qc, pb, ps, ncb):
    b, t, _ = pb.shape
    nblk = t // TB
    fwd = lambda i: i
    bwd = lambda i: _rev_block(i, ncb, nblk)

    def dspecs(blk):
        return [pl.BlockSpec((1, TB, qc.shape[2]), lambda b_, i: (b_, blk(i), 0)),
                pl.BlockSpec((1, TB, 512), lambda b_, i: (b_, blk(i), 2)),
                pl.BlockSpec((1, TB, LANES), lambda b_, i: (b_, blk(i), 0))]

    specs = dspecs(fwd) + dspecs(bwd)
    args = [qc, pb, ps] * 2
    ns = 2 * N_HEADS
    return pl.pallas_call(
        _ml2_kernel,
        grid=(b, nblk),
        in_specs=specs,
        out_specs=[pl.BlockSpec((1, TB, 512), lambda b_, i: (b_, i, 0)),
                   pl.BlockSpec((1, TB, 512), lambda b_, i: (b_, bwd(i), 0))],
        out_shape=[jax.ShapeDtypeStruct((b, t, 512), F32)] * 2,
        scratch_shapes=[pltpu.VMEM((ns, 2 * ML_DH, ML_DH), F32),
                        pltpu.VMEM((2, 1, LANES), F32)],
        compiler_params=_cparams(("parallel", "arbitrary")),
        name="mlstm",
    )(*args)


def _ml_dir(d, j, ncb, nblk, qk_ref, hl_ref, hr_ref, v_ref, g_ref, gt_ref, cw_ref, out_ref,
            c_s, n_s, m_s):
    ll = ML_L
    lvalid = jnp.logical_and(j != 0, j != ncb)
    rvalid = jnp.logical_and(j != ncb - 1, j != nblk - 1)
    hl = jnp.where(lvalid, hl_ref[0, 7:8, :], 0.0)
    hr = jnp.where(rvalid, hr_ref[0, 0:1, :], 0.0)
    qk = _silu(_conv3(qk_ref[0], hl, hr, cw_ref[...]))
    dq = N_HEADS * ML_DH
    v = v_ref[0]
    ls = _log_sigmoid(g_ref[0])
    lst = _log_sigmoid(gt_ref[0])
    gi = g_ref[0]
    git = gt_ref[0]
    rows = _iota((ll, ll), 0)
    cols = _iota((ll, ll), 1)
    causal = (cols >= rows) if d else (cols <= rows)
    tri = jnp.where(causal, 1.0, 0.0).astype(BF16)
    trit = jnp.where((rows >= cols) if d else (rows <= cols), 1.0, 0.0).astype(BF16)
    order = range(TB // ll - 1, -1, -1) if d else range(TB // ll)
    for c in order:
        r0 = c * ll
        bc = _dot_exact_l(tri, ls[r0:r0 + ll, :])
        br = _dot_exact_r(lst[:, r0:r0 + ll], trit)
        for h in range(N_HEADS):
            ci = 8 * d + h
            cf = 8 * d + 4 + h
            sidx = d * N_HEADS + h
            li_col = gi[r0:r0 + ll, ci:ci + 1]
            li_row = git[ci:ci + 1, r0:r0 + ll]
            bcol = bc[:, cf:cf + 1]
            brow = br[cf:cf + 1, :]
            bend = bcol[0:1, :] if d else bcol[ll - 1:ll, :]
            m_prev = m_s[sidx][:, 0:1]
            q_h = qk[r0:r0 + ll, h * ML_DH:(h + 1) * ML_DH] * (ML_DH ** -0.5)
            k_h = qk[r0:r0 + ll, dq + h * ML_DH:dq + (h + 1) * ML_DH]
            v_h = v[r0:r0 + ll, h * ML_DH:(h + 1) * ML_DH]
            qb, kb, vb = q_h.astype(BF16), k_h.astype(BF16), v_h.astype(BF16)
            dmat = jnp.where(causal, bcol - brow + li_row, -jnp.inf)
            m_intra = jnp.max(dmat, axis=1, keepdims=True)
            m_inter = bcol + m_prev
            m_t = jnp.maximum(m_inter, m_intra)
            s = _mm(qb, kb, _NT) * jnp.exp(dmat - m_t)
            a = jnp.exp(m_inter - m_t)
            c_prev = c_s[sidx]
            n_prev = n_s[sidx]
            num = _mm(s.astype(BF16), vb) + a * _mm(qb, c_prev.astype(BF16), _NT)
            den = jnp.sum(s, axis=1, keepdims=True) + a * jnp.sum(q_h * n_prev, axis=1, keepdims=True)
            hout = num / jnp.maximum(jnp.abs(den), jnp.exp(-m_t))
            out_ref[0, r0:r0 + ll, h * ML_DH:(h + 1) * ML_DH] = hout
            w_end = bend - bcol + li_col
            m_kv = jnp.max(w_end, axis=0, keepdims=True)
            e = jnp.exp(w_end - m_kv)
            c_kv = _mm((e * v_h).astype(BF16), kb, _TN)
            n_kv = jnp.sum(e * k_h, axis=0, keepdims=True)
            m_new = jnp.maximum(bend + m_prev, m_kv)
            a_old = jnp.exp(bend + m_prev - m_new)
            a_new = jnp.exp(m_kv - m_new)
            c_s[sidx] = a_old * c_prev + a_new * c_kv
            n_s[sidx] = a_old * n_prev + a_new * n_kv
            m_s[sidx] = jnp.broadcast_to(m_new, (1, LANES))


def _ml_kernel(ncb, nblk, qkf, hlf, hrf, vf, gf, gtf, qkb, hlb, hrb, vb, gb, gtb, cw,
               of_ref, ob_ref, c_s, n_s, m_s):
    i = pl.program_id(1)

    @pl.when(i == 0)
    def _():
        c_s[...] = jnp.zeros_like(c_s)
        n_s[...] = jnp.zeros_like(n_s)
        m_s[...] = jnp.zeros_like(m_s)

    _ml_dir(0, i, ncb, nblk, qkf, hlf, hrf, vf, gf, gtf, cw, of_ref, c_s, n_s, m_s)
    _ml_dir(1, _rev_block(i, ncb, nblk), ncb, nblk, qkb, hlb, hrb, vb, gb, gtb, cw, ob_ref, c_s, n_s, m_s)


def _dir_specs(blk, t, qk_w, qk_cb, v_cb):
    r8 = TB // 8
    last8 = t // 8 - 1
    return [
        pl.BlockSpec((1, TB, qk_w), lambda b, i: (b, blk(i), qk_cb)),
        pl.BlockSpec((1, 8, qk_w), lambda b, i: (b, jnp.maximum(blk(i) * r8 - 1, 0), qk_cb)),
        pl.BlockSpec((1, 8, qk_w), lambda b, i: (b, jnp.minimum((blk(i) + 1) * r8, last8), qk_cb)),
        pl.BlockSpec((1, TB, 512), lambda b, i: (b, blk(i), v_cb)),
        pl.BlockSpec((1, TB, LANES), lambda b, i: (b, blk(i), 0)),
        pl.BlockSpec((1, LANES, TB), lambda b, i: (b, 0, blk(i))),
    ]


def _mlstm(pb, ps, pst, cw, ncb):
    b, t, _ = pb.shape
    nblk = t // TB
    fwd = lambda i: i
    bwd = lambda i: _rev_block(i, ncb, nblk)
    specs = _dir_specs(fwd, t, 1024, 0, 2) + _dir_specs(bwd, t, 1024, 0, 2)
    specs.append(pl.BlockSpec((3, 1024), lambda b_, i: (0, 0)))
    args = [pb, pb, pb, pb, ps, pst] * 2 + [cw]
    ns = 2 * N_HEADS
    return pl.pallas_call(
        functools.partial(_ml_kernel, ncb, nblk),
        grid=(b, nblk),
        in_specs=specs,
        out_specs=[pl.BlockSpec((1, TB, 512), lambda b_, i: (b_, i, 0)),
                   pl.BlockSpec((1, TB, 512), lambda b_, i: (b_, bwd(i), 0))],
        out_shape=[jax.ShapeDtypeStruct((b, t, 512), F32)] * 2,
        scratch_shapes=[pltpu.VMEM((ns, ML_DH, ML_DH), F32),
                        pltpu.VMEM((ns, 1, ML_DH), F32),
                        pltpu.VMEM((ns, 1, LANES), F32)],
        compiler_params=_cparams(("parallel", "arbitrary")),
        name="mlstm",
    )(*args)


def _gla_dir(d, qk_ref, v_ref, ps_ref, w2_ref, ba_ref, out_ref, s_s):
    ll = GLA_L
    dkw = N_HEADS * GLA_DK
    qk = qk_ref[0]
    v = v_ref[0]
    la = _log_sigmoid(_dot3(ps_ref[0], w2_ref[d]) + ba_ref[d]) * (1.0 / GLA_GATE_TAU)
    rows = _iota((ll, ll), 0)
    cols = _iota((ll, ll), 1)
    causal = (cols >= rows) if d else (cols <= rows)
    tri = jnp.where(causal, 1.0, 0.0).astype(BF16)
    order = range(TB // ll - 1, -1, -1) if d else range(TB // ll)
    for c in order:
        r0 = c * ll
        bcum = _dot_exact_l(tri, la[r0:r0 + ll, :])
        ref = bcum[ll // 2:ll // 2 + 1, :]
        bend = bcum[0:1, :] if d else bcum[ll - 1:ll, :]
        qt = qk[r0:r0 + ll, 0:dkw] * (GLA_DK ** -0.5) * jnp.exp(bcum - ref)
        kt = qk[r0:r0 + ll, dkw:2 * dkw] * jnp.exp(ref - bcum)
        e_ref = jnp.exp(ref)
        e_end = jnp.exp(bend)
        e_er = jnp.exp(bend - ref)
        for h in range(N_HEADS):
            sl = slice(h * GLA_DK, (h + 1) * GLA_DK)
            sidx = d * N_HEADS + h
            qb = qt[:, sl].astype(BF16)
            kb = kt[:, sl].astype(BF16)
            vb = v[r0:r0 + ll, h * GLA_DV:(h + 1) * GLA_DV].astype(BF16)
            att = jnp.where(causal, _mm(qb, kb, _NT), 0.0)
            st = s_s[sidx]
            o = _mm(att.astype(BF16), vb) + _mm(qb, (st * e_ref[:, sl]).astype(BF16), _NT)
            out_ref[0, r0:r0 + ll, h * GLA_DV:(h + 1) * GLA_DV] = o
            kdec = (kt[:, sl] * e_er[:, sl]).astype(BF16)
            s_s[sidx] = st * e_end[:, sl] + _mm(vb, kdec, _TN)


def _gla_pair(fwd, bwd, w2_ref, ba_ref, s_s):
    ll = GLA_L
    nch = TB // ll
    dkw = N_HEADS * GLA_DK
    rows = _iota((ll, ll), 0)
    cols = _iota((ll, ll), 1)
    causal = [cols <= rows, cols >= rows]
    tri = [jnp.where(m, 1.0, 0.0).astype(BF16) for m in causal]
    qk = [fwd[0][0], bwd[0][0]]
    v = [fwd[1][0], bwd[1][0]]
    outs = [fwd[3], bwd[3]]
    la = [_log_sigmoid(_dot3(r[2][0], w2_ref[d]) + ba_ref[d]) * (1.0 / GLA_GATE_TAU)
          for d, r in enumerate((fwd, bwd))]
    units = [(d, h) for d in range(2) for h in range(N_HEADS)]
    st = {u: s_s[u[0] * N_HEADS + u[1]] for u in units}
    for step in range(nch):
        r0 = [step * ll, (nch - 1 - step) * ll]
        qi, qt, kt, kd, e_end = [], [], [], [], []
        for d in range(2):
            bcum = _dot_exact_l(tri[d], la[d][r0[d]:r0[d] + ll, :])
            ref = bcum[ll // 2:ll // 2 + 1, :]
            bend = bcum[0:1, :] if d else bcum[ll - 1:ll, :]
            q = qk[d][r0[d]:r0[d] + ll, 0:dkw] * (GLA_DK ** -0.5)
            k = qk[d][r0[d]:r0[d] + ll, dkw:2 * dkw]
            qi.append((q * jnp.exp(bcum)).astype(BF16))
            qt.append((q * jnp.exp(bcum - ref)).astype(BF16))
            ktd = k * jnp.exp(ref - bcum)
            kt.append(ktd.astype(BF16))
            kd.append((ktd * jnp.exp(bend - ref)).astype(BF16))
            e_end.append(jnp.exp(bend))
        hs = lambda a, h, w: a[:, h * w:(h + 1) * w]
        vb = {(d, h): hs(v[d][r0[d]:r0[d] + ll, :], h, GLA_DV).astype(BF16) for d, h in units}
        att = {(d, h): _mm(hs(qt[d], h, GLA_DK), hs(kt[d], h, GLA_DK), _NT) for d, h in units}
        inter = {(d, h): _mm(hs(qi[d], h, GLA_DK), st[(d, h)].astype(BF16), _NT) for d, h in units}
        attb = {(d, h): jnp.where(causal[d], att[(d, h)], 0.0).astype(BF16) for d, h in units}
        kv = {(d, h): _mm(vb[(d, h)], hs(kd[d], h, GLA_DK), _TN) for d, h in units}
        for d, h in units:
            o = _mm(attb[(d, h)], vb[(d, h)]) + inter[(d, h)]
            outs[d][0, r0[d]:r0[d] + ll, h * GLA_DV:(h + 1) * GLA_DV] = o
        st = {(d, h): st[(d, h)] * hs(e_end[d], h, GLA_DK) + kv[(d, h)] for d, h in units}
    for d, h in units:
        s_s[d * N_HEADS + h] = st[(d, h)]


def _gla_kernel(ncb, nblk, qkf, vf, psf, qkb, vb, psb, w2, ba, of_ref, ob_ref, s_s):
    i = pl.program_id(1)

    @pl.when(i == 0)
    def _():
        s_s[...] = jnp.zeros_like(s_s)

    _gla_pair((qkf, vf, psf, of_ref), (qkb, vb, psb, ob_ref), w2, ba, s_s)


def _gla(pb, ps, w2e, ba, ncb):
    b, t, _ = pb.shape
    nblk = t // TB
    fwd = lambda i: i
    bwd = lambda i: _rev_block(i, ncb, nblk)

    def dspecs(blk):
        return [pl.BlockSpec((1, TB, 512), lambda b_, i: (b_, blk(i), 4)),
                pl.BlockSpec((1, TB, 512), lambda b_, i: (b_, blk(i), 5)),
                pl.BlockSpec((1, TB, LANES), lambda b_, i: (b_, blk(i), 0))]

    specs = dspecs(fwd) + dspecs(bwd) + [
        pl.BlockSpec((2, LANES, 256), lambda b_, i: (0, 0, 0)),
        pl.BlockSpec((2, 1, 256), lambda b_, i: (0, 0, 0))]
    return pl.pallas_call(
        functools.partial(_gla_kernel, ncb, nblk),
        grid=(b, nblk),
        in_specs=specs,
        out_specs=[pl.BlockSpec((1, TB, 512), lambda b_, i: (b_, i, 0)),
                   pl.BlockSpec((1, TB, 512), lambda b_, i: (b_, bwd(i), 0))],
        out_shape=[jax.ShapeDtypeStruct((b, t, 512), F32)] * 2,
        scratch_shapes=[pltpu.VMEM((2 * N_HEADS, GLA_DV, GLA_DK), F32)],
        compiler_params=_cparams(("parallel", "arbitrary")),
        name="gla",
    )(pb, pb, ps, pb, pb, ps, w2e, ba)


def _mix_kernel(ncb, nblk, qkf, hlf, hrf, vf, gf, gtf, qkb, hlb, hrb, vb, gb, gtb, cw,
                gqf, gvf, gqb, gvb, w2, ba, mf_ref, mb_ref, gf_ref, gb_ref, c_s, n_s, m_s, s_s):
    i = pl.program_id(1)

    @pl.when(i == 0)
    def _():
        c_s[...] = jnp.zeros_like(c_s)
        n_s[...] = jnp.zeros_like(n_s)
        m_s[...] = jnp.zeros_like(m_s)
        s_s[...] = jnp.zeros_like(s_s)

    _ml_dir(0, i, ncb, nblk, qkf, hlf, hrf, vf, gf, gtf, cw, mf_ref, c_s, n_s, m_s)
    _gla_dir(0, gqf, gvf, gf, w2, ba, gf_ref, s_s)
    _ml_dir(1, _rev_block(i, ncb, nblk), ncb, nblk, qkb, hlb, hrb, vb, gb, gtb, cw, mb_ref, c_s, n_s, m_s)
    _gla_dir(1, gqb, gvb, gb, w2, ba, gb_ref, s_s)


def _mixers(pb, ps, pst, cw, w2e, ba, ncb):
    b, t, _ = pb.shape
    nblk = t // TB
    fwd = lambda i: i
    bwd = lambda i: _rev_block(i, ncb, nblk)

    def gspecs(blk):
        return [pl.BlockSpec((1, TB, 512), lambda b_, i: (b_, blk(i), 4)),
                pl.BlockSpec((1, TB, 512), lambda b_, i: (b_, blk(i), 5))]

    specs = (_dir_specs(fwd, t, 1024, 0, 2) + _dir_specs(bwd, t, 1024, 0, 2)
             + [pl.BlockSpec((3, 1024), lambda b_, i: (0, 0))]
             + gspecs(fwd) + gspecs(bwd)
             + [pl.BlockSpec((2, LANES, 256), lambda b_, i: (0, 0, 0)),
                pl.BlockSpec((2, 1, 256), lambda b_, i: (0, 0, 0))])
    args = [pb, pb, pb, pb, ps, pst] * 2 + [cw] + [pb] * 4 + [w2e, ba]
    ns = 2 * N_HEADS
    ofwd = pl.BlockSpec((1, TB, 512), lambda b_, i: (b_, i, 0))
    obwd = pl.BlockSpec((1, TB, 512), lambda b_, i: (b_, bwd(i), 0))
    return pl.pallas_call(
        functools.partial(_mix_kernel, ncb, nblk),
        grid=(b, nblk),
        in_specs=specs,
        out_specs=[ofwd, obwd, ofwd, obwd],
        out_shape=[jax.ShapeDtypeStruct((b, t, 512), F32)] * 4,
        scratch_shapes=[pltpu.VMEM((ns, ML_DH, ML_DH), F32),
                        pltpu.VMEM((ns, 1, ML_DH), F32),
                        pltpu.VMEM((ns, 1, LANES), F32),
                        pltpu.VMEM((ns, GLA_DV, GLA_DK), F32)],
        compiler_params=_cparams(("parallel", "arbitrary")),
        name="mixers",
    )(*args)


def _head_norm(x, g):
    outs = []
    for h in range(N_HEADS):
        seg = x[:, h * 128:(h + 1) * 128]
        outs.append(seg * lax.rsqrt(jnp.mean(seg * seg, axis=-1, keepdims=True) + EPS))
    return jnp.concatenate(outs, axis=-1) * g


def _out_kernel(hf, hb, of, ob, og, rg, x_ref, mod_ref, gml, ggla, wo, gpost, gpre, wrt,
                x1_ref, h2e_ref, aff_ref):
    y_ml = _head_norm(hf[0] + hb[0], gml[...]) * _sigmoid(og[0])
    y_gla = _head_norm(of[0] + ob[0], ggla[...]) * _silu(rg[0])
    y = jnp.concatenate([y_ml, y_gla], axis=-1).astype(BF16)
    y2 = _mm(y, wo[...])
    mod = mod_ref[0, 0]
    x1 = x_ref[0] + mod[2:3, :] * _rms(y2, gpost[...])
    x1_ref[0] = x1
    h2 = _rms(x1, gpre[...]) * (1.0 + mod[4:5, :]) + mod[3:4, :]
    d = h2.shape[1]
    lt = _dot3(wrt[...], h2, _NT)
    ext = jnp.exp(lt - jnp.max(lt, axis=0, keepdims=True))
    aff = ext / jnp.sum(ext, axis=0, keepdims=True)
    aff_ref[0] = aff
    ne = N_EXPERTS
    afft = jnp.concatenate([aff, jnp.zeros((LANES - ne, aff.shape[1]), F32)], axis=0).T
    a_hi, a_mid, a_lo = _split3(afft)
    pieces = a_hi.astype(F32) + pltpu.roll(a_mid.astype(F32), ne, axis=1) + pltpu.roll(a_lo.astype(F32), 2 * ne, axis=1)
    h2e_ref[0, :, 0:d] = h2.astype(BF16)
    h2e_ref[0, :, d:d + LANES] = pieces.astype(BF16)


def _out_proj(hf, hb, of, ob, pb, xa, mod, gml, ggla, wo, gpost, gpre, wrt, ncb):
    b, t, d = xa.shape
    de = d + LANES
    tile = lambda cb: pl.BlockSpec((1, TB, 512), lambda bi, i: (bi, i, cb))
    full = lambda shp: pl.BlockSpec(shp, lambda bi, i: tuple(0 for _ in shp))
    return pl.pallas_call(
        _out_kernel,
        grid=(b, t // TB),
        in_specs=[tile(0), tile(0), tile(0), tile(0), tile(3), tile(6),
                  pl.BlockSpec((1, TB, d), lambda bi, i: (bi, i, 0)),
                  pl.BlockSpec((1, 1, 8, d), lambda bi, i: (bi, jnp.where(i < ncb, 0, 1), 0, 0)),
                  full((1, 512)), full((1, 512)), full((d, d)), full((1, d)), full((1, d)),
                  full((N_EXPERTS, d))],
        out_specs=[pl.BlockSpec((1, TB, d), lambda bi, i: (bi, i, 0)),
                   pl.BlockSpec((1, TB, de), lambda bi, i: (bi, i, 0)),
                   pl.BlockSpec((1, N_EXPERTS, TB), lambda bi, i: (bi, 0, i))],
        out_shape=[jax.ShapeDtypeStruct((b, t, d), F32),
                   jax.ShapeDtypeStruct((b, t, de), BF16),
                   jax.ShapeDtypeStruct((b, N_EXPERTS, t), F32)],
        compiler_params=_cparams(("parallel", "parallel")),
        name="out_proj",
    )(hf, hb, of, ob, pb, pb, xa, mod, gml, ggla, wo, gpost, gpre, wrt)


def _cumsum_blocks(x, r):
    n = x.shape[0]
    xb = x.astype(BF16)
    li = _iota((LANES, LANES), 0)
    lj = _iota((LANES, LANES), 1)
    upper = jnp.where(li <= lj, 1.0, 0.0).astype(BF16)
    ones = jnp.ones((LANES, LANES), BF16)
    inrow = _mm(xb, upper)
    tot = _mm(xb, ones)
    ri = _iota((n, n), 0)
    rj = _iota((n, n), 1)
    same = (ri // r) == (rj // r)
    strict = jnp.where(jnp.logical_and(same, rj < ri), 1.0, 0.0).astype(BF16)
    off = _mm(strict, tot.astype(BF16))
    return inrow + off, off


def _select(aff, r, cap, base_slot):
    ne = N_EXPERTS
    n = ne * r
    aff3 = aff.reshape(ne, r, LANES)
    capf = jnp.float32(cap)

    def body(k, prefix):
        cand = prefix | (jnp.int32(1) << (30 - k))
        candf = lax.bitcast_convert_type(cand, F32)
        cnt = jnp.sum(jnp.where(aff3 >= candf, 1.0, 0.0), axis=(1, 2), keepdims=True)
        return jnp.where(cnt >= capf, cand, prefix)

    thr = lax.bitcast_convert_type(lax.fori_loop(0, 31, body, jnp.zeros((ne, 1, 1), I32)), F32)
    gt = jnp.where(aff3 > thr, 1.0, 0.0)
    eq = jnp.where(aff3 == thr, 1.0, 0.0)
    need = capf - jnp.sum(gt, axis=(1, 2), keepdims=True)
    eq2 = eq.reshape(n, LANES)
    cs_eq, _ = _cumsum_blocks(eq2, r)
    eq_rank = (cs_eq - eq2).reshape(ne, r, LANES)
    sel = (gt + eq * jnp.where(eq_rank < need, 1.0, 0.0)).reshape(n, LANES)
    cs, off = _cumsum_blocks(sel, r)
    slot = jnp.where(sel > 0.5, cs - 1.0 + base_slot, UNSEL)
    return slot, off


def _sel_kernel(rc, rl, cap_c, cap_l, *refs):
    if rc:
        affc, affl, slc, offc, sll, offl = refs
        slc[0], offc[0] = _select(affc[0], rc, cap_c, float(cap_l))
    else:
        affl, sll, offl = refs
    sll[0], offl[0] = _select(affl[0], rl, cap_l, 0.0)


def _route(aff_c, aff_l, cap_c, cap_l):
    b = aff_l.shape[0]
    ne = N_EXPERTS
    rl = aff_l.shape[1] // ne
    rc = aff_c.shape[1] // ne if aff_c is not None else 0
    args = ([aff_c] if rc else []) + [aff_l]
    in_specs, out_shape, out_specs = [], [], []
    for a in args:
        spec = pl.BlockSpec((1,) + a.shape[1:], lambda bi: (bi, 0, 0))
        in_specs.append(spec)
        out_shape += [jax.ShapeDtypeStruct(a.shape, F32)] * 2
        out_specs += [spec, spec]
    return pl.pallas_call(
        functools.partial(_sel_kernel, rc, rl, cap_c, cap_l),
        grid=(b,),
        in_specs=in_specs,
        out_specs=out_specs,
        out_shape=out_shape,
        compiler_params=_cparams(("parallel",)),
        name="route",
    )(*args)


def _window(lo_ref, base, e, m_rows):
    lo_e = lo_ref[base + e]
    hi_e = lo_ref[base + N_EXPERTS + e]
    a_e = jnp.minimum((lo_e // ROW_ALIGN) * ROW_ALIGN, m_rows - WIN)
    return a_e, hi_e


def _disp_kernel(m_rows, ng, lo_ref, h_ref, slot_ref, x_ref):
    ne = N_EXPERTS
    gi = pl.program_id(1)
    i = pl.program_id(2)
    base = (pl.program_id(0) * pl.num_programs(2) + i) * (2 * ne)

    @pl.when(i == 0)
    def _():
        x_ref[...] = jnp.zeros_like(x_ref)

    h = h_ref[0]
    sl = slot_ref[0]
    sub = _iota((WIN, TB), 0).astype(F32)
    wins = [_window(lo_ref, base, gi * ng + k, m_rows) for k in range(ng)]
    rounds = jnp.int32(1)
    for a_e, hi_e in wins:
        rounds = jnp.maximum(rounds, (hi_e - a_e + WIN - 1) // WIN)

    def one_round(rd, carry):
        wts, starts = [], []
        for k, (a_e, _) in enumerate(wins):
            first = a_e + rd * WIN
            a_r = jnp.minimum(first, m_rows - WIN)
            srow = sl[k:k + 1, :]
            hit = jnp.logical_and(srow - a_r.astype(F32) == sub, srow >= first.astype(F32))
            wts.append(jnp.where(hit, 1.0, 0.0).astype(BF16))
            starts.append(a_r)
        g = _mm(jnp.concatenate(wts, axis=0), h).astype(BF16)
        for k, a_r in enumerate(starts):
            rows = pl.ds(pl.multiple_of(a_r, ROW_ALIGN), WIN)
            x_ref[0, k, rows, :] = x_ref[0, k, rows, :] + g[k * WIN:(k + 1) * WIN, :]
        return carry

    lax.fori_loop(0, rounds, one_round, 0)


def _dispatch(lohi, h2e, slots, m_rows, t0, nt):
    b, t, de = h2e.shape
    ne = N_EXPERTS
    ng = 8
    grid_spec = pltpu.PrefetchScalarGridSpec(
        num_scalar_prefetch=1,
        grid=(b, ne // ng, nt),
        in_specs=[pl.BlockSpec((1, TB, de), lambda bi, gi, i, *_: (bi, i + t0, 0)),
                  pl.BlockSpec((1, ng, TB), lambda bi, gi, i, *_: (bi * (ne // ng) + gi, 0, i + t0))],
        out_specs=pl.BlockSpec((1, ng, m_rows, de), lambda bi, gi, i, *_: (bi, gi, 0, 0)))
    return pl.pallas_call(
        functools.partial(_disp_kernel, m_rows, ng),
        grid_spec=grid_spec,
        out_shape=jax.ShapeDtypeStruct((b, ne, m_rows, de), BF16),
        compiler_params=_cparams(("parallel", "parallel", "arbitrary")),
        name="dispatch",
    )(lohi, h2e, slots.reshape(b * (ne // ng), ng, t))


def _row_chunks(m_rows, cap_l):
    step = min(256, cap_l)
    chunks = [(s, step) for s in range(0, cap_l, step)]
    if m_rows > cap_l:
        chunks.append((cap_l, m_rows - cap_l))
    return chunks


def _exp_kernel(m_rows, cap_l, x_ref, wg_ref, wu_ref, wd_ref, y_ref, wg_s, wu_s, wd_s):
    ei = pl.program_id(0)
    d = wg_ref.shape[2]

    @pl.when(pl.program_id(1) == 0)
    def _():
        wg_s[...] = wg_ref[0, 0].astype(BF16)
        wu_s[...] = wu_ref[0, 0].astype(BF16)
        wd_s[...] = wd_ref[0, 0].astype(BF16)

    for r0, mc in _row_chunks(m_rows, cap_l):
        xs = x_ref[0, 0, r0:r0 + mc, 0:d]
        hid = _silu(_mm(xs, wg_s[...])) * _mm(xs, wu_s[...])
        y = _mm(hid.astype(BF16), wd_s[...])
        pieces = x_ref[0, 0, r0:r0 + mc, d:d + LANES].astype(F32)
        lane = _iota((mc, LANES), 1)
        mine = jnp.logical_and(lane % N_EXPERTS == ei, lane < 3 * N_EXPERTS)
        gate = jnp.sum(jnp.where(mine, pieces, 0.0), axis=1, keepdims=True)
        y_ref[0, 0, r0:r0 + mc, :] = (y * gate).astype(BF16)


def _experts(xin, wg, wu, wd, layer, cap_l):
    b, ne, m_rows, de = xin.shape
    _, _, d, f = wg.shape
    return pl.pallas_call(
        functools.partial(_exp_kernel, m_rows, cap_l),
        grid=(ne, b),
        in_specs=[pl.BlockSpec((1, 1, m_rows, de), lambda ei, bi: (bi, ei, 0, 0)),
                  pl.BlockSpec((1, 1, d, f), lambda ei, bi: (layer, ei, 0, 0)),
                  pl.BlockSpec((1, 1, d, f), lambda ei, bi: (layer, ei, 0, 0)),
                  pl.BlockSpec((1, 1, f, d), lambda ei, bi: (layer, ei, 0, 0))],
        out_specs=pl.BlockSpec((1, 1, m_rows, d), lambda ei, bi: (bi, ei, 0, 0)),
        out_shape=jax.ShapeDtypeStruct((b, ne, m_rows, d), BF16),
        scratch_shapes=[pltpu.VMEM((d, f), BF16), pltpu.VMEM((d, f), BF16), pltpu.VMEM((f, d), BF16)],
        compiler_params=_cparams(("arbitrary", "arbitrary")),
        name="experts",
    )(xin, wg, wu, wd)


def _comb_kernel(m_rows, lo_ref, slot_ref, x1_ref, mod_ref, gpost, y_ref, out_ref):
    ne = N_EXPERTS
    i = pl.program_id(1)
    base = (pl.program_id(0) * pl.num_programs(1) + i) * (2 * ne)
    sl = slot_ref[0]
    hi = jnp.floor(sl * (1.0 / 32.0))
    lo = sl - hi * 32.0
    eye = jnp.where(_iota((TB, TB), 0) == _iota((TB, TB), 1), 1.0, 0.0).astype(BF16)
    slt = _mm(eye, hi.astype(BF16), _NT) * 32.0 + _mm(eye, lo.astype(BF16), _NT)
    lane = _iota((TB, WIN), 1).astype(F32)
    wins = [_window(lo_ref, base, e, m_rows) for e in range(ne)]
    rounds = jnp.int32(1)
    for a_e, hi_e in wins:
        rounds = jnp.maximum(rounds, (hi_e - a_e + WIN - 1) // WIN)

    def one_round(rd, acc):
        ws, ys = [], []
        for e, (a_e, _) in enumerate(wins):
            first = a_e + rd * WIN
            a_r = jnp.minimum(first, m_rows - WIN)
            scol = slt[:, e:e + 1]
            hit = jnp.logical_and(scol - a_r.astype(F32) == lane, scol >= first.astype(F32))
            ws.append(jnp.where(hit, 1.0, 0.0).astype(BF16))
            ys.append(y_ref[0, e, pl.ds(pl.multiple_of(a_r, ROW_ALIGN), WIN), :])
        return acc + _mm(jnp.concatenate(ws, axis=1), jnp.concatenate(ys, axis=0))

    ffn = lax.fori_loop(0, rounds, one_round, jnp.zeros(out_ref.shape[1:], F32))
    mod = mod_ref[0, 0]
    out_ref[0] = x1_ref[0] + mod[5:6, :] * _rms(ffn, gpost[...])


def _combine(lohi, slots, x1, mod, gpost, y, t0, nt, ncb):
    b, t, d = x1.shape
    ne = N_EXPERTS
    m_rows = y.shape[2]
    grid_spec = pltpu.PrefetchScalarGridSpec(
        num_scalar_prefetch=1,
        grid=(b, nt),
        in_specs=[pl.BlockSpec((1, ne, TB), lambda bi, i, *_: (bi, 0, i + t0)),
                  pl.BlockSpec((1, TB, d), lambda bi, i, *_: (bi, i + t0, 0)),
                  pl.BlockSpec((1, 1, 8, d), lambda bi, i, *_: (bi, jnp.where(i + t0 < ncb, 0, 1), 0, 0)),
                  pl.BlockSpec((1, d), lambda bi, i, *_: (0, 0)),
                  pl.BlockSpec((1, ne, m_rows, d), lambda bi, i, *_: (bi, 0, 0, 0),
                               pipeline_mode=pl.Buffered(1))],
        out_specs=pl.BlockSpec((1, TB, d), lambda bi, i, *_: (bi, i, 0)))
    return pl.pallas_call(
        functools.partial(_comb_kernel, m_rows),
        grid_spec=grid_spec,
        out_shape=jax.ShapeDtypeStruct((b, nt * TB, d), F32),
        compiler_params=_cparams(("parallel", "arbitrary")),
        name="combine",
    )(lohi, slots, x1, mod, gpost, y)


def _pos_tables(rows, d):
    quarter = d // 4
    freq = jnp.power(POS_BASE, -jnp.arange(quarter, dtype=F32) / quarter)
    ar = jnp.arange(rows, dtype=F32)[:, None] * freq
    ac = jnp.arange(GRID_W, dtype=F32)[:, None] * freq
    return (jnp.concatenate([jnp.sin(ar), jnp.cos(ar)], axis=-1),
            jnp.concatenate([jnp.sin(ac), jnp.cos(ac)], axis=-1))


def _tile_bounds(off, r, ntile, cap, base):
    b = off.shape[0]
    o = off.reshape(b, N_EXPERTS, r, LANES)[:, :, :, 0]
    lo = o[:, :, ::TB // LANES][:, :, :ntile] + base
    hi = jnp.concatenate([lo[:, :, 1:], jnp.full((b, N_EXPERTS, 1), cap + base, F32)], axis=2)
    return lo, hi


def kernel(x, c, ctx, c_ctx, w_ada, b_ada, g_mix_pre, g_mix_post, g_ffn_pre, g_ffn_post,
           w_in, conv_qk, b_ml_gates, w_gla_a2, b_gla_a, g_ml_norm, g_gla_norm, w_out,
           w_router, w_e_gate, w_e_up, w_e_down):
    bsz, n_tok, d = x.shape
    lc = ctx.shape[1]
    depth = w_in.shape[0]
    ne = N_EXPERTS
    t = lc + n_tok
    ncb = lc // TB
    nblk = t // TB
    assert lc % TB == 0 and n_tok % TB == 0 and d == 1024
    cap_l = EC_FACTOR * n_tok // ne
    cap_c = EC_FACTOR * lc // ne

    assert TB % GRID_W == 0
    pos_r, pos_c = _pos_tables(n_tok // GRID_W, d)
    xa = None

    cc = jnp.zeros((8, d), F32).at[:bsz].set(c).at[bsz].set(c_ctx)
    mods = _ada(cc, w_ada, b_ada)

    wide = jnp.concatenate([w_in[:, :, 0:2048], w_in[:, :, 2064:3600]], axis=2).astype(BF16)
    narrow = jnp.concatenate([w_in[:, :, 2048:2064], w_in[:, :, 3600:3632]], axis=2)
    narrow = jnp.pad(narrow, ((0, 0), (0, 0), (0, LANES - narrow.shape[2])))
    narrow = jnp.concatenate(_split2(narrow), axis=2)
    bias_s = jnp.pad(b_ml_gates, ((0, 0), (0, LANES - b_ml_gates.shape[1])))
    w2e = jnp.zeros((depth, 2, LANES, 256), F32)
    w2e = w2e.at[:, 0, 16:32].set(w_gla_a2[:, 0]).at[:, 1, 32:48].set(w_gla_a2[:, 1])

    for l in range(depth):
        last = l == depth - 1
        ml = mods[l, :bsz].reshape(bsz, 6, d)
        mc_ = jnp.broadcast_to(mods[l, bsz].reshape(1, 6, d), (bsz, 6, d))
        mod = jnp.pad(jnp.stack([mc_, ml], axis=1), ((0, 0), (0, 0), (0, 2), (0, 0)))

        if l == 0:
            xa, pb, ps = _in_proj(None, mod, g_mix_pre[l][None], wide[l], narrow[l], bias_s[l][None], ncb,
                                  first=(x, ctx, pos_r, pos_c))
        else:
            pb, ps = _in_proj(xa, mod, g_mix_pre[l][None], wide[l], narrow[l], bias_s[l][None], ncb)
        hf, hb = _mlstm2(_qkconv(pb, conv_qk[l], ncb), pb, ps, ncb)
        of, ob = _gla(pb, ps, w2e[l], b_gla_a[l][:, None, :], ncb)
        x1, h2e, aff = _out_proj(hf, hb, of, ob, pb, xa, mod, g_ml_norm[l][None], g_gla_norm[l][None],
                                 w_out[l].astype(BF16), g_mix_post[l][None], g_ffn_pre[l][None],
                                 w_router[l].T, ncb)

        rl = n_tok // LANES
        aff_l = aff[:, :, lc:].reshape(bsz, ne * rl, LANES)
        if last:
            sll, offl = _route(None, aff_l, 0, cap_l)
            slots = jnp.pad(sll.reshape(bsz, ne, n_tok), ((0, 0), (0, 0), (lc, 0)), constant_values=UNSEL)
            lo, hi = _tile_bounds(offl, rl, nblk - ncb, cap_l, 0)
            t0, nt, m_rows = ncb, nblk - ncb, cap_l
        else:
            rc = max(lc // LANES, 8)
            aff_c = aff[:, :, :lc].reshape(bsz, ne, lc // LANES, LANES)
            aff_c = jnp.pad(aff_c, ((0, 0), (0, 0), (0, rc - lc // LANES), (0, 0)), constant_values=-1.0)
            slc, offc, sll, offl = _route(aff_c.reshape(bsz, ne * rc, LANES), aff_l, cap_c, cap_l)
            slots = jnp.concatenate([slc.reshape(bsz, ne, rc * LANES)[:, :, :lc],
                                     sll.reshape(bsz, ne, n_tok)], axis=2)
            lo_l, hi_l = _tile_bounds(offl, rl, nblk - ncb, cap_l, 0)
            lo_c, hi_c = _tile_bounds(offc, rc, ncb, cap_c, cap_l)
            lo = jnp.concatenate([lo_c, lo_l], axis=2)
            hi = jnp.concatenate([hi_c, hi_l], axis=2)
            t0, nt, m_rows = 0, nblk, cap_l + cap_c
        lohi = jnp.concatenate([lo, hi], axis=1).transpose(0, 2, 1).astype(I32).reshape(-1)
        xin = _dispatch(lohi, h2e, slots, m_rows, t0, nt)
        y = _experts(xin, w_e_gate, w_e_up, w_e_down, l, cap_l)
        xa = _combine(lohi, slots, x1, mod, g_ffn_post[l][None], y, t0, nt, ncb)
    return xa
```

```python
import functools

import jax
import jax.numpy as jnp
from jax import lax
from jax.experimental import pallas as pl
from jax.experimental.pallas import tpu as pltpu

F32 = jnp.float32
BF16 = jnp.bfloat16
I32 = jnp.int32

EPS = 1e-6
GRID_W = 64
POS_BASE = 10000.0
N_HEADS = 4
ML_DH = 128
GLA_DK = 64
GLA_DV = 128
GLA_GATE_TAU = 16.0
N_EXPERTS = 16
EC_FACTOR = 2

LANES = 128
TB = 256
ML_L = 128
GLA_L = 128
WIN = 80
ROW_ALIGN = 16
UNSEL = 2047.0
PB_ML_V, PB_ML_O, PB_GLA_QK, PB_GLA_V, PB_GLA_R = 0, 1, 2, 3, 4
VMEM_LIMIT = 56 * 1024 * 1024


def _cparams(sem):
    return pltpu.CompilerParams(dimension_semantics=sem, vmem_limit_bytes=VMEM_LIMIT)


def _split2(a):
    hi = a.astype(BF16)
    lo = (a - hi.astype(F32)).astype(BF16)
    return hi, lo


def _split3(a):
    hi = a.astype(BF16)
    r = a - hi.astype(F32)
    mid = r.astype(BF16)
    lo = (r - mid.astype(F32)).astype(BF16)
    return hi, mid, lo


_NN = (((1,), (0,)), ((), ()))
_NT = (((1,), (1,)), ((), ()))
_TN = (((0,), (0,)), ((), ()))


def _mm(a, b, dims=_NN):
    return lax.dot_general(a, b, dims, preferred_element_type=F32)


def _dot3(a, b, dims=_NN):
    ah, al = _split2(a)
    bh, bl = _split2(b)
    return _mm(ah, bh, dims) + (_mm(ah, bl, dims) + _mm(al, bh, dims))


def _dot_exact_l(m_bf16, x, dims=_NN):
    hi, mid, lo = _split3(x)
    return _mm(m_bf16, hi, dims) + (_mm(m_bf16, mid, dims) + _mm(m_bf16, lo, dims))


def _dot_exact_r(x, m_bf16, dims=_NN):
    hi, mid, lo = _split3(x)
    return _mm(hi, m_bf16, dims) + (_mm(mid, m_bf16, dims) + _mm(lo, m_bf16, dims))


def _rms(x, g):
    return x * lax.rsqrt(jnp.mean(x * x, axis=-1, keepdims=True) + EPS) * g


def _log_sigmoid(x):
    return jnp.minimum(x, 0.0) - jnp.log1p(jnp.exp(-jnp.abs(x)))


def _sigmoid(x):
    return 1.0 / (1.0 + jnp.exp(-x))


def _silu(x):
    return x * _sigmoid(x)


def _iota(shape, dim):
    return lax.broadcasted_iota(I32, shape, dim)


def _rev_block(i, ncb, nblk):
    return jnp.where(i < ncb, ncb - 1 - i, nblk - 1 - (i - ncb))


def _ada_kernel(c_ref, w_ref, b_ref, o_ref):
    a = _silu(c_ref[...])
    o_ref[0] = _dot3(a, w_ref[0]) + b_ref[0]


def _ada(cc, w_ada, b_ada):
    depth, d, n6 = w_ada.shape
    tn = 1536
    return pl.pallas_call(
        _ada_kernel,
        grid=(depth, n6 // tn),
        in_specs=[pl.BlockSpec((8, d), lambda l, j: (0, 0)),
                  pl.BlockSpec((1, d, tn), lambda l, j: (l, 0, j)),
                  pl.BlockSpec((1, 1, tn), lambda l, j: (l, 0, j))],
        out_specs=pl.BlockSpec((1, 8, tn), lambda l, j: (l, 0, j)),
        out_shape=jax.ShapeDtypeStruct((depth, 8, n6), F32),
        compiler_params=_cparams(("parallel", "parallel")),
        name="ada",
    )(cc, w_ada, b_ada.reshape(depth, 1, n6))


def _project(ncb, x, xp, xn, mod_ref, g_ref, wb_ref, ws_ref, bs_ref, cw_ref, pb_ref, ps_ref, qc_ref):
    i = pl.program_id(1)
    nblk = pl.num_programs(1)
    n = x.shape[0]
    norm = lambda z: _rms(z, g_ref[...]) * (1.0 + mod_ref[0, 0, 1:2, :]) + mod_ref[0, 0, 0:1, :]
    hh, hl = _split2(norm(x))
    wq = cw_ref.shape[1]
    hext = jnp.concatenate([hh, norm(xp).astype(BF16), norm(xn).astype(BF16)], axis=0)
    qk = _mm(hext, wb_ref[:, 0:wq])
    pb_ref[0] = _mm(hh, wb_ref[:, wq:])
    lvalid = jnp.logical_and(i != 0, i != ncb)
    rvalid = jnp.logical_and(i != ncb - 1, i != nblk - 1)
    left = jnp.where(lvalid, qk[n + 7:n + 8, :], 0.0)
    right = jnp.where(rvalid, qk[n + 8:n + 9, :], 0.0)
    y = _silu(_conv3(qk[:n, :], left, right, cw_ref[...]))
    dq = wq // 2
    qc_ref[0, :, 0:dq] = (y[:, 0:dq] * (ML_DH ** -0.5)).astype(BF16)
    qc_ref[0, :, dq:] = y[:, dq:].astype(BF16)
    pr = _mm(jnp.concatenate([hh, hl], axis=0), ws_ref[...])
    ps_ref[0] = (pr[:n, :LANES] + pr[:n, LANES:]) + (pr[n:, :LANES] + pr[n:, LANES:]) + bs_ref[...]


def _in_kernel(ncb, x_ref, xp_ref, xn_ref, mod_ref, g_ref, wb_ref, ws_ref, bs_ref, cw_ref,
               pb_ref, ps_ref, qc_ref):
    _project(ncb, x_ref[0], xp_ref[0], xn_ref[0], mod_ref, g_ref, wb_ref, ws_ref, bs_ref, cw_ref,
             pb_ref, ps_ref, qc_ref)


def _in0_kernel(ncb, x_ref, xp_ref, xn_ref, c_ref, cp_ref, cn_ref, pr_ref, prp_ref, prn_ref, pc_ref,
                mod_ref, g_ref, wb_ref, ws_ref, bs_ref, cw_ref, xa_ref, pb_ref, ps_ref, qc_ref):
    i = pl.program_id(1)
    half = pr_ref.shape[2]
    reps = TB // GRID_W
    prow = jnp.concatenate([jnp.broadcast_to(pr_ref[0, k:k + 1, :], (GRID_W, half)) for k in range(reps)], axis=0)
    pcol = jnp.concatenate([pc_ref[...]] * reps, axis=0)
    lat = x_ref[0] + jnp.concatenate([prow, pcol], axis=1)
    is_ctx = i < ncb
    xa = jnp.where(is_ctx, c_ref[0], lat)
    xa_ref[0] = xa
    pos_p = jnp.concatenate([prp_ref[0, reps - 1:reps, :], pc_ref[GRID_W - 1:GRID_W, :]], axis=1)
    pos_n = jnp.concatenate([prn_ref[0, 0:1, :], pc_ref[0:1, :]], axis=1)
    xp = jnp.where(is_ctx, cp_ref[0], xp_ref[0] + pos_p)
    xn = jnp.where(is_ctx, cn_ref[0], xn_ref[0] + pos_n)
    _project(ncb, xa, xp, xn, mod_ref, g_ref, wb_ref, ws_ref, bs_ref, cw_ref, pb_ref, ps_ref, qc_ref)


def _tile_and_halo_specs(rows, d, tile_of):
    r8 = TB // 8
    ntile, last8 = rows // TB, rows // 8 - 1
    tl = lambda i: jnp.clip(tile_of(i), 0, ntile - 1)
    return [pl.BlockSpec((1, TB, d), lambda bi, i: (bi, tl(i), 0)),
            pl.BlockSpec((1, 8, d), lambda bi, i: (bi, jnp.clip(tl(i) * r8 - 1, 0, last8), 0)),
            pl.BlockSpec((1, 8, d), lambda bi, i: (bi, jnp.clip((tl(i) + 1) * r8, 0, last8), 0))]


def _in_proj(xa, mod, g, wb, ws, bs, cw, ncb, first=None):
    if first is None:
        b, t, d = xa.shape
    else:
        b, t, d = first[0].shape[0], first[0].shape[1] + first[1].shape[1], first[0].shape[2]
    wq = cw.shape[1]
    nb = wb.shape[1] - wq
    const = lambda shp: pl.BlockSpec(shp, lambda bi, i: tuple(0 for _ in shp))
    common = [pl.BlockSpec((1, 1, 8, d), lambda bi, i: (bi, jnp.where(i < ncb, 0, 1), 0, 0)),
              const((1, d)), const((d, nb + wq)), const((d, 2 * LANES)), const((1, LANES)), const((3, wq))]
    out_specs = [pl.BlockSpec((1, TB, nb), lambda bi, i: (bi, i, 0)),
                 pl.BlockSpec((1, TB, LANES), lambda bi, i: (bi, i, 0)),
                 pl.BlockSpec((1, TB, wq), lambda bi, i: (bi, i, 0))]
    out_shape = [jax.ShapeDtypeStruct((b, t, nb), F32),
                 jax.ShapeDtypeStruct((b, t, LANES), F32),
                 jax.ShapeDtypeStruct((b, t, wq), BF16)]
    if first is None:
        body = functools.partial(_in_kernel, ncb)
        in_specs = _tile_and_halo_specs(t, d, lambda i: i) + common
        args = (xa, xa, xa, mod, g, wb, ws, bs, cw)
    else:
        x, ctx, pos_r, pos_c = first
        reps = TB // GRID_W
        ntl = x.shape[1] // TB
        body = functools.partial(_in0_kernel, ncb)
        pr_spec = lambda off: pl.BlockSpec((1, reps, d // 2),
                                           lambda bi, i: (jnp.clip(i - ncb + off, 0, ntl - 1), 0, 0))
        in_specs = (_tile_and_halo_specs(x.shape[1], d, lambda i: i - ncb)
                    + _tile_and_halo_specs(ctx.shape[1], d, lambda i: i)
                    + [pr_spec(0), pr_spec(-1), pr_spec(1), const((GRID_W, d // 2))] + common)
        out_specs = [pl.BlockSpec((1, TB, d), lambda bi, i: (bi, i, 0))] + out_specs
        out_shape = [jax.ShapeDtypeStruct((b, t, d), F32)] + out_shape
        pr3 = pos_r.reshape(-1, reps, d // 2)
        args = (x, x, x, ctx, ctx, ctx, pr3, pr3, pr3, pos_c, mod, g, wb, ws, bs, cw)
    return pl.pallas_call(
        body,
        grid=(b, t // TB),
        in_specs=in_specs,
        out_specs=out_specs,
        out_shape=out_shape,
        compiler_params=_cparams(("parallel", "parallel")),
        name="in_proj",
    )(*args)


def _conv3(x, hl, hr, w):
    rows = _iota(x.shape, 0)
    prev = jnp.where(rows == 0, hl, pltpu.roll(x, 1, axis=0))
    nxt = jnp.where(rows == x.shape[0] - 1, hr, pltpu.roll(x, x.shape[0] - 1, axis=0))
    return prev * w[0:1] + x * w[1:2] + nxt * w[2:3]


def _qkconv_kernel(ncb, nblk, qk_ref, hl_ref, hr_ref, cw_ref, o_ref):
    j = pl.program_id(1)
    lvalid = jnp.logical_and(j != 0, j != ncb)
    rvalid = jnp.logical_and(j != ncb - 1, j != nblk - 1)
    hl = jnp.where(lvalid, hl_ref[0, 7:8, :], 0.0)
    hr = jnp.where(rvalid, hr_ref[0, 0:1, :], 0.0)
    y = _silu(_conv3(qk_ref[0], hl, hr, cw_ref[...]))
    dq = N_HEADS * ML_DH
    o_ref[0, :, 0:dq] = (y[:, 0:dq] * (ML_DH ** -0.5)).astype(BF16)
    o_ref[0, :, dq:] = y[:, dq:].astype(BF16)


def _qkconv(pb, cw, ncb):
    b, t, _ = pb.shape
    nblk = t // TB
    w = 2 * N_HEADS * ML_DH
    specs = _dir_specs(lambda i: i, t, w, 0, 2)[:3] + [pl.BlockSpec((3, w), lambda b_, i: (0, 0))]
    return pl.pallas_call(
        functools.partial(_qkconv_kernel, ncb, nblk),
        grid=(b, nblk),
        in_specs=specs,
        out_specs=pl.BlockSpec((1, TB, w), lambda b_, i: (b_, i, 0)),
        out_shape=jax.ShapeDtypeStruct((b, t, w), BF16),
        compiler_params=_cparams(("parallel", "parallel")),
        name="qkconv",
    )(pb, pb, pb, cw)


def _cummax_rows(x, reverse):
    n = x.shape[0]
    rows = _iota(x.shape, 0)
    s = 1
    while s < n:
        if reverse:
            sh = jnp.where(rows < n - s, pltpu.roll(x, n - s, axis=0), -jnp.inf)
        else:
            sh = jnp.where(rows >= s, pltpu.roll(x, s, axis=0), -jnp.inf)
        x = jnp.maximum(x, sh)
        s *= 2
    return x


def _ml_pair(fwd, bwd, c_s, m_s):
    ll = ML_L
    nch = TB // ll
    dq = N_HEADS * ML_DH
    rows = _iota((ll, ll), 0)
    cols = _iota((ll, ll), 1)
    causal = [cols <= rows, cols >= rows]
    tri = [jnp.where(m, 1.0, 0.0).astype(BF16) for m in causal]
    ones = jnp.ones((ll, ML_DH), BF16)
    qk = [fwd[0][0], bwd[0][0]]
    v = [fwd[1][0], bwd[1][0]]
    g = [fwd[2][0], bwd[2][0]]
    outs = [fwd[3], bwd[3]]
    units = [(d, h) for d in range(2) for h in range(N_HEADS)]
    cx = {u: c_s[u[0] * N_HEADS + u[1]] for u in units}
    m_row = [m_s[0], m_s[1]]
    for step in range(nch):
        r0 = [step * ll, (nch - 1 - step) * ll]
        alpha, a_in, em, e_w, ut, a_old, a_new = [], [], [], [], [], [], []
        for d in range(2):
            gc = g[d][r0[d]:r0[d] + ll, :]
            bc = _dot_exact_l(tri[d], _log_sigmoid(gc))
            u = pltpu.roll(gc, 4, axis=1) - bc
            cm = _cummax_rows(u, bool(d))
            neg_alpha = jnp.maximum(m_row[d], cm)
            alpha.append(-neg_alpha)
            a_in.append(jnp.exp(m_row[d] - neg_alpha))
            em.append(jnp.exp(-neg_alpha - bc))
            last = slice(0, 1) if d else slice(ll - 1, ll)
            cm_end = cm[last, :]
            bend = bc[last, :]
            e_w.append(jnp.exp(u - cm_end))
            ut.append(u.T)
            m_kv = bend + cm_end
            m_new = jnp.maximum(bend + m_row[d], m_kv)
            a_old.append(jnp.exp(bend + m_row[d] - m_new))
            a_new.append(jnp.exp(m_kv - m_new))
            m_row[d] = m_new
        lane = lambda d, h: 8 * d + 4 + h
        qb = {(d, h): qk[d][r0[d]:r0[d] + ll, h * ML_DH:(h + 1) * ML_DH] for d, h in units}
        kb = {(d, h): qk[d][r0[d]:r0[d] + ll, dq + h * ML_DH:dq + (h + 1) * ML_DH] for d, h in units}
        vh = {(d, h): v[d][r0[d]:r0[d] + ll, h * ML_DH:(h + 1) * ML_DH] for d, h in units}
        sc = {u: _mm(qb[u], kb[u], _NT) for u in units}
        inter = {u: _mm(qb[u], cx[u].astype(BF16), _NT) for u in units}
        sb = {}
        for d, h in units:
            c = lane(d, h)
            arg = jnp.where(causal[d], alpha[d][:, c:c + 1] + ut[d][c:c + 1, :], -jnp.inf)
            sb[(d, h)] = (sc[(d, h)] * jnp.exp(arg)).astype(BF16)
        ckv = {}
        for d, h in units:
            c = lane(d, h)
            ew = e_w[d][:, c:c + 1]
            ev = jnp.concatenate([(ew * vh[(d, h)]).astype(BF16), jnp.broadcast_to(ew, (ll, ML_DH)).astype(BF16)], axis=1)
            ckv[(d, h)] = _mm(ev, kb[(d, h)], _TN)
        for d, h in units:
            c = lane(d, h)
            nd = (_mm(sb[(d, h)], jnp.concatenate([vh[(d, h)].astype(BF16), ones], axis=1))
                  + a_in[d][:, c:c + 1] * inter[(d, h)])
            den = jnp.maximum(jnp.abs(nd[:, ML_DH:]), em[d][:, c:c + 1])
            outs[d][0, r0[d]:r0[d] + ll, h * ML_DH:(h + 1) * ML_DH] = nd[:, :ML_DH] / den
        cx = {(d, h): a_old[d][:, lane(d, h):lane(d, h) + 1] * cx[(d, h)]
              + a_new[d][:, lane(d, h):lane(d, h) + 1] * ckv[(d, h)] for d, h in units}
    for d, h in units:
        c_s[d * N_HEADS + h] = cx[(d, h)]
    m_s[0] = m_row[0]
    m_s[1] = m_row[1]


def _ml2_kernel(qkf, vf, gf, qkb, vb, gb, of_ref, ob_ref, c_s, m_s):
    @pl.when(pl.program_id(1) == 0)
    def _():
        c_s[...] = jnp.zeros_like(c_s)
        m_s[...] = jnp.zeros_like(m_s)

    _ml_pair((qkf, vf, gf, of_ref), (qkb, vb, gb, ob_ref), c_s, m_s)


def _mlstm2(qc, pb, ps, ncb):
    b, t, _ = pb.shape
    nblk = t // TB
    fwd = lambda i: i
    bwd = lambda i: _rev_block(i, ncb, nblk)

    def dspecs(blk):
        return [pl.BlockSpec((1, TB, qc.shape[2]), lambda b_, i: (b_, blk(i), 0)),
                pl.BlockSpec((1, TB, 512), lambda b_, i: (b_, blk(i), PB_ML_V)),
                pl.BlockSpec((1, TB, LANES), lambda b_, i: (b_, blk(i), 0))]

    specs = dspecs(fwd) + dspecs(bwd)
    args = [qc, pb, ps] * 2
    ns = 2 * N_HEADS
    return pl.pallas_call(
        _ml2_kernel,
        grid=(b, nblk),
        in_specs=specs,
        out_specs=[pl.BlockSpec((1, TB, 512), lambda b_, i: (b_, i, 0)),
                   pl.BlockSpec((1, TB, 512), lambda b_, i: (b_, bwd(i), 0))],
        out_shape=[jax.ShapeDtypeStruct((b, t, 512), F32)] * 2,
        scratch_shapes=[pltpu.VMEM((ns, 2 * ML_DH, ML_DH), F32),
                        pltpu.VMEM((2, 1, LANES), F32)],
        compiler_params=_cparams(("parallel", "arbitrary")),
        name="mlstm",
    )(*args)


def _ml_dir(d, j, ncb, nblk, qk_ref, hl_ref, hr_ref, v_ref, g_ref, gt_ref, cw_ref, out_ref,
            c_s, n_s, m_s):
    ll = ML_L
    lvalid = jnp.logical_and(j != 0, j != ncb)
    rvalid = jnp.logical_and(j != ncb - 1, j != nblk - 1)
    hl = jnp.where(lvalid, hl_ref[0, 7:8, :], 0.0)
    hr = jnp.where(rvalid, hr_ref[0, 0:1, :], 0.0)
    qk = _silu(_conv3(qk_ref[0], hl, hr, cw_ref[...]))
    dq = N_HEADS * ML_DH
    v = v_ref[0]
    ls = _log_sigmoid(g_ref[0])
    lst = _log_sigmoid(gt_ref[0])
    gi = g_ref[0]
    git = gt_ref[0]
    rows = _iota((ll, ll), 0)
    cols = _iota((ll, ll), 1)
    causal = (cols >= rows) if d else (cols <= rows)
    tri = jnp.where(causal, 1.0, 0.0).astype(BF16)
    trit = jnp.where((rows >= cols) if d else (rows <= cols), 1.0, 0.0).astype(BF16)
    order = range(TB // ll - 1, -1, -1) if d else range(TB // ll)
    for c in order:
        r0 = c * ll
        bc = _dot_exact_l(tri, ls[r0:r0 + ll, :])
        br = _dot_exact_r(lst[:, r0:r0 + ll], trit)
        for h in range(N_HEADS):
            ci = 8 * d + h
            cf = 8 * d + 4 + h
            sidx = d * N_HEADS + h
            li_col = gi[r0:r0 + ll, ci:ci + 1]
            li_row = git[ci:ci + 1, r0:r0 + ll]
            bcol = bc[:, cf:cf + 1]
            brow = br[cf:cf + 1, :]
            bend = bcol[0:1, :] if d else bcol[ll - 1:ll, :]
            m_prev = m_s[sidx][:, 0:1]
            q_h = qk[r0:r0 + ll, h * ML_DH:(h + 1) * ML_DH] * (ML_DH ** -0.5)
            k_h = qk[r0:r0 + ll, dq + h * ML_DH:dq + (h + 1) * ML_DH]
            v_h = v[r0:r0 + ll, h * ML_DH:(h + 1) * ML_DH]
            qb, kb, vb = q_h.astype(BF16), k_h.astype(BF16), v_h.astype(BF16)
            dmat = jnp.where(causal, bcol - brow + li_row, -jnp.inf)
            m_intra = jnp.max(dmat, axis=1, keepdims=True)
            m_inter = bcol + m_prev
            m_t = jnp.maximum(m_inter, m_intra)
            s = _mm(qb, kb, _NT) * jnp.exp(dmat - m_t)
            a = jnp.exp(m_inter - m_t)
            c_prev = c_s[sidx]
            n_prev = n_s[sidx]
            num = _mm(s.astype(BF16), vb) + a * _mm(qb, c_prev.astype(BF16), _NT)
            den = jnp.sum(s, axis=1, keepdims=True) + a * jnp.sum(q_h * n_prev, axis=1, keepdims=True)
            hout = num / jnp.maximum(jnp.abs(den), jnp.exp(-m_t))
            out_ref[0, r0:r0 + ll, h * ML_DH:(h + 1) * ML_DH] = hout
            w_end = bend - bcol + li_col
            m_kv = jnp.max(w_end, axis=0, keepdims=True)
            e = jnp.exp(w_end - m_kv)
            c_kv = _mm((e * v_h).astype(BF16), kb, _TN)
            n_kv = jnp.sum(e * k_h, axis=0, keepdims=True)
            m_new = jnp.maximum(bend + m_prev, m_kv)
            a_old = jnp.exp(bend + m_prev - m_new)
            a_new = jnp.exp(m_kv - m_new)
            c_s[sidx] = a_old * c_prev + a_new * c_kv
            n_s[sidx] = a_old * n_prev + a_new * n_kv
            m_s[sidx] = jnp.broadcast_to(m_new, (1, LANES))


def _ml_kernel(ncb, nblk, qkf, hlf, hrf, vf, gf, gtf, qkb, hlb, hrb, vb, gb, gtb, cw,
               of_ref, ob_ref, c_s, n_s, m_s):
    i = pl.program_id(1)

    @pl.when(i == 0)
    def _():
        c_s[...] = jnp.zeros_like(c_s)
        n_s[...] = jnp.zeros_like(n_s)
        m_s[...] = jnp.zeros_like(m_s)

    _ml_dir(0, i, ncb, nblk, qkf, hlf, hrf, vf, gf, gtf, cw, of_ref, c_s, n_s, m_s)
    _ml_dir(1, _rev_block(i, ncb, nblk), ncb, nblk, qkb, hlb, hrb, vb, gb, gtb, cw, ob_ref, c_s, n_s, m_s)


def _dir_specs(blk, t, qk_w, qk_cb, v_cb):
    r8 = TB // 8
    last8 = t // 8 - 1
    return [
        pl.BlockSpec((1, TB, qk_w), lambda b, i: (b, blk(i), qk_cb)),
        pl.BlockSpec((1, 8, qk_w), lambda b, i: (b, jnp.maximum(blk(i) * r8 - 1, 0), qk_cb)),
        pl.BlockSpec((1, 8, qk_w), lambda b, i: (b, jnp.minimum((blk(i) + 1) * r8, last8), qk_cb)),
        pl.BlockSpec((1, TB, 512), lambda b, i: (b, blk(i), v_cb)),
        pl.BlockSpec((1, TB, LANES), lambda b, i: (b, blk(i), 0)),
        pl.BlockSpec((1, LANES, TB), lambda b, i: (b, 0, blk(i))),
    ]


def _mlstm(pb, ps, pst, cw, ncb):
    b, t, _ = pb.shape
    nblk = t // TB
    fwd = lambda i: i
    bwd = lambda i: _rev_block(i, ncb, nblk)
    specs = _dir_specs(fwd, t, 1024, 0, 2) + _dir_specs(bwd, t, 1024, 0, 2)
    specs.append(pl.BlockSpec((3, 1024), lambda b_, i: (0, 0)))
    args = [pb, pb, pb, pb, ps, pst] * 2 + [cw]
    ns = 2 * N_HEADS
    return pl.pallas_call(
        functools.partial(_ml_kernel, ncb, nblk),
        grid=(b, nblk),
        in_specs=specs,
        out_specs=[pl.BlockSpec((1, TB, 512), lambda b_, i: (b_, i, 0)),
                   pl.BlockSpec((1, TB, 512), lambda b_, i: (b_, bwd(i), 0))],
        out_shape=[jax.ShapeDtypeStruct((b, t, 512), F32)] * 2,
        scratch_shapes=[pltpu.VMEM((ns, ML_DH, ML_DH), F32),
                        pltpu.VMEM((ns, 1, ML_DH), F32),
                        pltpu.VMEM((ns, 1, LANES), F32)],
        compiler_params=_cparams(("parallel", "arbitrary")),
        name="mlstm",
    )(*args)


def _gla_dir(d, qk_ref, v_ref, ps_ref, w2_ref, ba_ref, out_ref, s_s):
    ll = GLA_L
    dkw = N_HEADS * GLA_DK
    qk = qk_ref[0]
    v = v_ref[0]
    la = _log_sigmoid(_dot3(ps_ref[0], w2_ref[d]) + ba_ref[d]) * (1.0 / GLA_GATE_TAU)
    rows = _iota((ll, ll), 0)
    cols = _iota((ll, ll), 1)
    causal = (cols >= rows) if d else (cols <= rows)
    tri = jnp.where(causal, 1.0, 0.0).astype(BF16)
    order = range(TB // ll - 1, -1, -1) if d else range(TB // ll)
    for c in order:
        r0 = c * ll
        bcum = _dot_exact_l(tri, la[r0:r0 + ll, :])
        ref = bcum[ll // 2:ll // 2 + 1, :]
        bend = bcum[0:1, :] if d else bcum[ll - 1:ll, :]
        qt = qk[r0:r0 + ll, 0:dkw] * (GLA_DK ** -0.5) * jnp.exp(bcum - ref)
        kt = qk[r0:r0 + ll, dkw:2 * dkw] * jnp.exp(ref - bcum)
        e_ref = jnp.exp(ref)
        e_end = jnp.exp(bend)
        e_er = jnp.exp(bend - ref)
        for h in range(N_HEADS):
            sl = slice(h * GLA_DK, (h + 1) * GLA_DK)
            sidx = d * N_HEADS + h
            qb = qt[:, sl].astype(BF16)
            kb = kt[:, sl].astype(BF16)
            vb = v[r0:r0 + ll, h * GLA_DV:(h + 1) * GLA_DV].astype(BF16)
            att = jnp.where(causal, _mm(qb, kb, _NT), 0.0)
            st = s_s[sidx]
            o = _mm(att.astype(BF16), vb) + _mm(qb, (st * e_ref[:, sl]).astype(BF16), _NT)
            out_ref[0, r0:r0 + ll, h * GLA_DV:(h + 1) * GLA_DV] = o
            kdec = (kt[:, sl] * e_er[:, sl]).astype(BF16)
            s_s[sidx] = st * e_end[:, sl] + _mm(vb, kdec, _TN)


def _gla_pair(fwd, bwd, w2_ref, ba_ref, s_s):
    ll = GLA_L
    nch = TB // ll
    dkw = N_HEADS * GLA_DK
    rows = _iota((ll, ll), 0)
    cols = _iota((ll, ll), 1)
    causal = [cols <= rows, cols >= rows]
    tri = [jnp.where(m, 1.0, 0.0).astype(BF16) for m in causal]
    qk = [fwd[0][0], bwd[0][0]]
    v = [fwd[1][0], bwd[1][0]]
    outs = [fwd[3], bwd[3]]
    la = [_log_sigmoid(_dot3(r[2][0], w2_ref[d]) + ba_ref[d]) * (1.0 / GLA_GATE_TAU)
          for d, r in enumerate((fwd, bwd))]
    units = [(d, h) for d in range(2) for h in range(N_HEADS)]
    st = {u: s_s[u[0] * N_HEADS + u[1]] for u in units}
    for step in range(nch):
        r0 = [step * ll, (nch - 1 - step) * ll]
        qi, qt, kt, kd, e_end = [], [], [], [], []
        for d in range(2):
            bcum = _dot_exact_l(tri[d], la[d][r0[d]:r0[d] + ll, :])
            ref = bcum[ll // 2:ll // 2 + 1, :]
            bend = bcum[0:1, :] if d else bcum[ll - 1:ll, :]
            q = qk[d][r0[d]:r0[d] + ll, 0:dkw] * (GLA_DK ** -0.5)
            k = qk[d][r0[d]:r0[d] + ll, dkw:2 * dkw]
            qi.append((q * jnp.exp(bcum)).astype(BF16))
            qt.append((q * jnp.exp(bcum - ref)).astype(BF16))
            ktd = k * jnp.exp(ref - bcum)
            kt.append(ktd.astype(BF16))
            kd.append((ktd * jnp.exp(bend - ref)).astype(BF16))
            e_end.append(jnp.exp(bend))
        hs = lambda a, h, w: a[:, h * w:(h + 1) * w]
        vb = {(d, h): hs(v[d][r0[d]:r0[d] + ll, :], h, GLA_DV).astype(BF16) for d, h in units}
        att = {(d, h): _mm(hs(qt[d], h, GLA_DK), hs(kt[d], h, GLA_DK), _NT) for d, h in units}
        inter = {(d, h): _mm(hs(qi[d], h, GLA_DK), st[(d, h)].astype(BF16), _NT) for d, h in units}
        attb = {(d, h): jnp.where(causal[d], att[(d, h)], 0.0).astype(BF16) for d, h in units}
        kv = {(d, h): _mm(vb[(d, h)], hs(kd[d], h, GLA_DK), _TN) for d, h in units}
        for d, h in units:
            o = _mm(attb[(d, h)], vb[(d, h)]) + inter[(d, h)]
            outs[d][0, r0[d]:r0[d] + ll, h * GLA_DV:(h + 1) * GLA_DV] = o
        st = {(d, h): st[(d, h)] * hs(e_end[d], h, GLA_DK) + kv[(d, h)] for d, h in units}
    for d, h in units:
        s_s[d * N_HEADS + h] = st[(d, h)]


def _gla_kernel(ncb, nblk, qkf, vf, psf, qkb, vb, psb, w2, ba, of_ref, ob_ref, s_s):
    i = pl.program_id(1)

    @pl.when(i == 0)
    def _():
        s_s[...] = jnp.zeros_like(s_s)

    _gla_pair((qkf, vf, psf, of_ref), (qkb, vb, psb, ob_ref), w2, ba, s_s)


def _gla(pb, ps, w2e, ba, ncb):
    b, t, _ = pb.shape
    nblk = t // TB
    fwd = lambda i: i
    bwd = lambda i: _rev_block(i, ncb, nblk)

    def dspecs(blk):
        return [pl.BlockSpec((1, TB, 512), lambda b_, i: (b_, blk(i), PB_GLA_QK)),
                pl.BlockSpec((1, TB, 512), lambda b_, i: (b_, blk(i), PB_GLA_V)),
                pl.BlockSpec((1, TB, LANES), lambda b_, i: (b_, blk(i), 0))]

    specs = dspecs(fwd) + dspecs(bwd) + [
        pl.BlockSpec((2, LANES, 256), lambda b_, i: (0, 0, 0)),
        pl.BlockSpec((2, 1, 256), lambda b_, i: (0, 0, 0))]
    return pl.pallas_call(
        functools.partial(_gla_kernel, ncb, nblk),
        grid=(b, nblk),
        in_specs=specs,
        out_specs=[pl.BlockSpec((1, TB, 512), lambda b_, i: (b_, i, 0)),
                   pl.BlockSpec((1, TB, 512), lambda b_, i: (b_, bwd(i), 0))],
        out_shape=[jax.ShapeDtypeStruct((b, t, 512), F32)] * 2,
        scratch_shapes=[pltpu.VMEM((2 * N_HEADS, GLA_DV, GLA_DK), F32)],
        compiler_params=_cparams(("parallel", "arbitrary")),
        name="gla",
    )(pb, pb, ps, pb, pb, ps, w2e, ba)


def _mix_kernel(ncb, nblk, qkf, hlf, hrf, vf, gf, gtf, qkb, hlb, hrb, vb, gb, gtb, cw,
                gqf, gvf, gqb, gvb, w2, ba, mf_ref, mb_ref, gf_ref, gb_ref, c_s, n_s, m_s, s_s):
    i = pl.program_id(1)

    @pl.when(i == 0)
    def _():
        c_s[...] = jnp.zeros_like(c_s)
        n_s[...] = jnp.zeros_like(n_s)
        m_s[...] = jnp.zeros_like(m_s)
        s_s[...] = jnp.zeros_like(s_s)

    _ml_dir(0, i, ncb, nblk, qkf, hlf, hrf, vf, gf, gtf, cw, mf_ref, c_s, n_s, m_s)
    _gla_dir(0, gqf, gvf, gf, w2, ba, gf_ref, s_s)
    _ml_dir(1, _rev_block(i, ncb, nblk), ncb, nblk, qkb, hlb, hrb, vb, gb, gtb, cw, mb_ref, c_s, n_s, m_s)
    _gla_dir(1, gqb, gvb, gb, w2, ba, gb_ref, s_s)


def _mixers(pb, ps, pst, cw, w2e, ba, ncb):
    b, t, _ = pb.shape
    nblk = t // TB
    fwd = lambda i: i
    bwd = lambda i: _rev_block(i, ncb, nblk)

    def gspecs(blk):
        return [pl.BlockSpec((1, TB, 512), lambda b_, i: (b_, blk(i), 4)),
                pl.BlockSpec((1, TB, 512), lambda b_, i: (b_, blk(i), 5))]

    specs = (_dir_specs(fwd, t, 1024, 0, 2) + _dir_specs(bwd, t, 1024, 0, 2)
             + [pl.BlockSpec((3, 1024), lambda b_, i: (0, 0))]
             + gspecs(fwd) + gspecs(bwd)
             + [pl.BlockSpec((2, LANES, 256), lambda b_, i: (0, 0, 0)),
                pl.BlockSpec((2, 1, 256), lambda b_, i: (0, 0, 0))])
    args = [pb, pb, pb, pb, ps, pst] * 2 + [cw] + [pb] * 4 + [w2e, ba]
    ns = 2 * N_HEADS
    ofwd = pl.BlockSpec((1, TB, 512), lambda b_, i: (b_, i, 0))
    obwd = pl.BlockSpec((1, TB, 512), lambda b_, i: (b_, bwd(i), 0))
    return pl.pallas_call(
        functools.partial(_mix_kernel, ncb, nblk),
        grid=(b, nblk),
        in_specs=specs,
        out_specs=[ofwd, obwd, ofwd, obwd],
        out_shape=[jax.ShapeDtypeStruct((b, t, 512), F32)] * 4,
        scratch_shapes=[pltpu.VMEM((ns, ML_DH, ML_DH), F32),
                        pltpu.VMEM((ns, 1, ML_DH), F32),
                        pltpu.VMEM((ns, 1, LANES), F32),
                        pltpu.VMEM((ns, GLA_DV, GLA_DK), F32)],
        compiler_params=_cparams(("parallel", "arbitrary")),
        name="mixers",
    )(*args)


def _head_norm(x, g):
    outs = []
    for h in range(N_HEADS):
        seg = x[:, h * 128:(h + 1) * 128]
        outs.append(seg * lax.rsqrt(jnp.mean(seg * seg, axis=-1, keepdims=True) + EPS))
    return jnp.concatenate(outs, axis=-1) * g


def _out_kernel(hf, hb, of, ob, og, rg, x_ref, mod_ref, gml, ggla, wo, gpost, gpre, wrt,
                x1_ref, h2e_ref, aff_ref):
    ne = N_EXPERTS
    d = x_ref.shape[2]
    rh = LANES
    rs = [slice(k * rh, (k + 1) * rh) for k in range(TB // rh)]
    mod = mod_ref[0, 0]
    y = [jnp.concatenate([_head_norm(hf[0, r, :] + hb[0, r, :], gml[...]) * _sigmoid(og[0, r, :]),
                          _head_norm(of[0, r, :] + ob[0, r, :], ggla[...]) * _silu(rg[0, r, :])],
                         axis=-1).astype(BF16) for r in rs]
    y2 = [_mm(yk, wo[...]) for yk in y]
    x1 = [x_ref[0, r, :] + mod[2:3, :] * _rms(y2k, gpost[...]) for r, y2k in zip(rs, y2)]
    for r, x1k in zip(rs, x1):
        x1_ref[0, r, :] = x1k
    h2 = [_rms(x1k, gpre[...]) * (1.0 + mod[4:5, :]) + mod[3:4, :] for x1k in x1]
    lt = [_dot3(wrt[...], h2k, _NT) for h2k in h2]
    ext = [jnp.exp(ltk - jnp.max(ltk, axis=0, keepdims=True)) for ltk in lt]
    aff = [e / jnp.sum(e, axis=0, keepdims=True) for e in ext]
    for r, affk, h2k in zip(rs, aff, h2):
        aff_ref[0, :, r] = affk
        afft = jnp.concatenate([affk, jnp.zeros((LANES - ne, rh), F32)], axis=0).T
        a_hi, a_mid, a_lo = _split3(afft)
        pieces = (a_hi.astype(F32) + pltpu.roll(a_mid.astype(F32), ne, axis=1)
                  + pltpu.roll(a_lo.astype(F32), 2 * ne, axis=1))
        h2e_ref[0, r, 0:d] = h2k.astype(BF16)
        h2e_ref[0, r, d:d + LANES] = pieces.astype(BF16)


def _out_proj(hf, hb, of, ob, pb, xa, mod, gml, ggla, wo, gpost, gpre, wrt, ncb):
    b, t, d = xa.shape
    de = d + LANES
    tile = lambda cb: pl.BlockSpec((1, TB, 512), lambda bi, i: (bi, i, cb))
    full = lambda shp: pl.BlockSpec(shp, lambda bi, i: tuple(0 for _ in shp))
    return pl.pallas_call(
        _out_kernel,
        grid=(b, t // TB),
        in_specs=[tile(0), tile(0), tile(0), tile(0), tile(PB_ML_O), tile(PB_GLA_R),
                  pl.BlockSpec((1, TB, d), lambda bi, i: (bi, i, 0)),
                  pl.BlockSpec((1, 1, 8, d), lambda bi, i: (bi, jnp.where(i < ncb, 0, 1), 0, 0)),
                  full((1, 512)), full((1, 512)), full((d, d)), full((1, d)), full((1, d)),
                  full((N_EXPERTS, d))],
        out_specs=[pl.BlockSpec((1, TB, d), lambda bi, i: (bi, i, 0)),
                   pl.BlockSpec((1, TB, de), lambda bi, i: (bi, i, 0)),
                   pl.BlockSpec((1, N_EXPERTS, TB), lambda bi, i: (bi, 0, i))],
        out_shape=[jax.ShapeDtypeStruct((b, t, d), F32),
                   jax.ShapeDtypeStruct((b, t, de), BF16),
                   jax.ShapeDtypeStruct((b, N_EXPERTS, t), F32)],
        compiler_params=_cparams(("parallel", "parallel")),
        name="out_proj",
    )(hf, hb, of, ob, pb, pb, xa, mod, gml, ggla, wo, gpost, gpre, wrt)


def _cumsum_blocks(x, r):
    n = x.shape[0]
    xb = x.astype(BF16)
    li = _iota((LANES, LANES), 0)
    lj = _iota((LANES, LANES), 1)
    upper = jnp.where(li <= lj, 1.0, 0.0).astype(BF16)
    ones = jnp.ones((LANES, LANES), BF16)
    inrow = _mm(xb, upper)
    tot = _mm(xb, ones)
    ri = _iota((n, n), 0)
    rj = _iota((n, n), 1)
    same = (ri // r) == (rj // r)
    strict = jnp.where(jnp.logical_and(same, rj < ri), 1.0, 0.0).astype(BF16)
    off = _mm(strict, tot.astype(BF16))
    return inrow + off, off


def _select(aff, r, cap, base_slot):
    ne = N_EXPERTS
    n = ne * r
    aff3 = aff.reshape(ne, r, LANES)
    capf = jnp.float32(cap)

    def body(k, prefix):
        cand = prefix | (jnp.int32(1) << (30 - k))
        candf = lax.bitcast_convert_type(cand, F32)
        cnt = jnp.sum(jnp.where(aff3 >= candf, 1.0, 0.0), axis=(1, 2), keepdims=True)
        return jnp.where(cnt >= capf, cand, prefix)

    thr = lax.bitcast_convert_type(lax.fori_loop(0, 31, body, jnp.zeros((ne, 1, 1), I32)), F32)
    gt = jnp.where(aff3 > thr, 1.0, 0.0)
    eq = jnp.where(aff3 == thr, 1.0, 0.0)
    need = capf - jnp.sum(gt, axis=(1, 2), keepdims=True)
    eq2 = eq.reshape(n, LANES)
    cs_eq, _ = _cumsum_blocks(eq2, r)
    eq_rank = (cs_eq - eq2).reshape(ne, r, LANES)
    sel = (gt + eq * jnp.where(eq_rank < need, 1.0, 0.0)).reshape(n, LANES)
    cs, off = _cumsum_blocks(sel, r)
    slot = jnp.where(sel > 0.5, cs - 1.0 + base_slot, UNSEL)
    return slot, off


def _sel_kernel(rc, rl, cap_c, cap_l, *refs):
    if rc:
        affc, affl, slc, offc, sll, offl = refs
        slc[0], offc[0] = _select(affc[0], rc, cap_c, float(cap_l))
    else:
        affl, sll, offl = refs
    sll[0], offl[0] = _select(affl[0], rl, cap_l, 0.0)


def _route(aff_c, aff_l, cap_c, cap_l):
    b = aff_l.shape[0]
    ne = N_EXPERTS
    rl = aff_l.shape[1] // ne
    rc = aff_c.shape[1] // ne if aff_c is not None else 0
    args = ([aff_c] if rc else []) + [aff_l]
    in_specs, out_shape, out_specs = [], [], []
    for a in args:
        spec = pl.BlockSpec((1,) + a.shape[1:], lambda bi: (bi, 0, 0))
        in_specs.append(spec)
        out_shape += [jax.ShapeDtypeStruct(a.shape, F32)] * 2
        out_specs += [spec, spec]
    return pl.pallas_call(
        functools.partial(_sel_kernel, rc, rl, cap_c, cap_l),
        grid=(b,),
        in_specs=in_specs,
        out_specs=out_specs,
        out_shape=out_shape,
        compiler_params=_cparams(("parallel",)),
        name="route",
    )(*args)


def _window(lo_ref, base, e, m_rows):
    lo_e = lo_ref[base + e]
    hi_e = lo_ref[base + N_EXPERTS + e]
    a_e = jnp.minimum((lo_e // ROW_ALIGN) * ROW_ALIGN, m_rows - WIN)
    return a_e, hi_e


def _disp_kernel(m_rows, ng, lo_ref, h_ref, slot_ref, x_ref):
    ne = N_EXPERTS
    gi = pl.program_id(1)
    i = pl.program_id(2)
    base = (pl.program_id(0) * pl.num_programs(2) + i) * (2 * ne)

    @pl.when(i == 0)
    def _():
        x_ref[...] = jnp.zeros_like(x_ref)

    h = h_ref[0]
    sl = slot_ref[0]
    sub = _iota((WIN, TB), 0).astype(F32)
    wins = [_window(lo_ref, base, gi * ng + k, m_rows) for k in range(ng)]
    rounds = jnp.int32(1)
    for a_e, hi_e in wins:
        rounds = jnp.maximum(rounds, (hi_e - a_e + WIN - 1) // WIN)

    def one_round(rd, carry):
        wts, starts = [], []
        for k, (a_e, _) in enumerate(wins):
            first = a_e + rd * WIN
            a_r = jnp.minimum(first, m_rows - WIN)
            srow = sl[k:k + 1, :]
            hit = jnp.logical_and(srow - a_r.astype(F32) == sub, srow >= first.astype(F32))
            wts.append(jnp.where(hit, 1.0, 0.0).astype(BF16))
            starts.append(a_r)
        g = _mm(jnp.concatenate(wts, axis=0), h).astype(BF16)
        for k, a_r in enumerate(starts):
            rows = pl.ds(pl.multiple_of(a_r, ROW_ALIGN), WIN)
            x_ref[0, k, rows, :] = x_ref[0, k, rows, :] + g[k * WIN:(k + 1) * WIN, :]
        return carry

    lax.fori_loop(0, rounds, one_round, 0)


def _dispatch(lohi, h2e, slots, m_rows, t0, nt):
    b, t, de = h2e.shape
    ne = N_EXPERTS
    ng = 8
    grid_spec = pltpu.PrefetchScalarGridSpec(
        num_scalar_prefetch=1,
        grid=(b, ne // ng, nt),
        in_specs=[pl.BlockSpec((1, TB, de), lambda bi, gi, i, *_: (bi, i + t0, 0)),
                  pl.BlockSpec((1, ng, TB), lambda bi, gi, i, *_: (bi * (ne // ng) + gi, 0, i + t0))],
        out_specs=pl.BlockSpec((1, ng, m_rows, de), lambda bi, gi, i, *_: (bi, gi, 0, 0)))
    return pl.pallas_call(
        functools.partial(_disp_kernel, m_rows, ng),
        grid_spec=grid_spec,
        out_shape=jax.ShapeDtypeStruct((b, ne, m_rows, de), BF16),
        compiler_params=_cparams(("parallel", "parallel", "arbitrary")),
        name="dispatch",
    )(lohi, h2e, slots.reshape(b * (ne // ng), ng, t))


def _row_chunks(m_rows, cap_l):
    step = min(256, cap_l)
    chunks = [(s, step) for s in range(0, cap_l, step)]
    if m_rows > cap_l:
        chunks.append((cap_l, m_rows - cap_l))
    return chunks


def _exp_kernel(m_rows, cap_l, x_ref, wg_ref, wu_ref, wd_ref, y_ref, wg_s, wu_s, wd_s):
    ei = pl.program_id(0)
    d = wg_ref.shape[2]

    @pl.when(pl.program_id(1) == 0)
    def _():
        wg_s[...] = wg_ref[0, 0].astype(BF16)
        wu_s[...] = wu_ref[0, 0].astype(BF16)
        wd_s[...] = wd_ref[0, 0].astype(BF16)

    for r0, mc in _row_chunks(m_rows, cap_l):
        xs = x_ref[0, 0, r0:r0 + mc, 0:d]
        hid = _silu(_mm(xs, wg_s[...])) * _mm(xs, wu_s[...])
        y = _mm(hid.astype(BF16), wd_s[...])
        pieces = x_ref[0, 0, r0:r0 + mc, d:d + LANES].astype(F32)
        lane = _iota((mc, LANES), 1)
        mine = jnp.logical_and(lane % N_EXPERTS == ei, lane < 3 * N_EXPERTS)
        gate = jnp.sum(jnp.where(mine, pieces, 0.0), axis=1, keepdims=True)
        y_ref[0, 0, r0:r0 + mc, :] = (y * gate).astype(BF16)


def _experts(xin, wg, wu, wd, layer, cap_l):
    b, ne, m_rows, de = xin.shape
    _, _, d, f = wg.shape
    return pl.pallas_call(
        functools.partial(_exp_kernel, m_rows, cap_l),
        grid=(ne, b),
        in_specs=[pl.BlockSpec((1, 1, m_rows, de), lambda ei, bi: (bi, ei, 0, 0)),
                  pl.BlockSpec((1, 1, d, f), lambda ei, bi: (layer, ei, 0, 0)),
                  pl.BlockSpec((1, 1, d, f), lambda ei, bi: (layer, ei, 0, 0)),
                  pl.BlockSpec((1, 1, f, d), lambda ei, bi: (layer, ei, 0, 0))],
        out_specs=pl.BlockSpec((1, 1, m_rows, d), lambda ei, bi: (bi, ei, 0, 0)),
        out_shape=jax.ShapeDtypeStruct((b, ne, m_rows, d), BF16),
        scratch_shapes=[pltpu.VMEM((d, f), BF16), pltpu.VMEM((d, f), BF16), pltpu.VMEM((f, d), BF16)],
        compiler_params=_cparams(("arbitrary", "arbitrary")),
        name="experts",
    )(xin, wg, wu, wd)


def _comb_kernel(m_rows, lo_ref, slot_ref, x1_ref, mod_ref, gpost, y_ref, out_ref):
    ne = N_EXPERTS
    i = pl.program_id(1)
    base = (pl.program_id(0) * pl.num_programs(1) + i) * (2 * ne)
    kk = ne * WIN
    sl = slot_ref[0]
    hi = jnp.floor(sl * (1.0 / 32.0))
    lo = sl - hi * 32.0
    col_e = _iota((ne, kk), 1) // WIN
    expand = jnp.where(col_e == _iota((ne, kk), 0), 1.0, 0.0).astype(BF16)
    sx = _mm(hi.astype(BF16), expand, _TN) * 32.0 + _mm(lo.astype(BF16), expand, _TN)
    col = _iota((1, kk), 1)
    jrow = (col % WIN).astype(F32)
    wins = [_window(lo_ref, base, e, m_rows) for e in range(ne)]
    rounds = jnp.int32(1)
    for a_e, hi_e in wins:
        rounds = jnp.maximum(rounds, (hi_e - a_e + WIN - 1) // WIN)

    def one_round(rd, acc):
        ys = []
        arow = jnp.zeros((1, kk), F32)
        frow = jnp.zeros((1, kk), F32)
        for e, (a_e, _) in enumerate(wins):
            first = a_e + rd * WIN
            a_r = jnp.minimum(first, m_rows - WIN)
            mine = col // WIN == e
            arow = jnp.where(mine, a_r.astype(F32), arow)
            frow = jnp.where(mine, first.astype(F32), frow)
            ys.append(y_ref[0, e, pl.ds(pl.multiple_of(a_r, ROW_ALIGN), WIN), :])
        hit = jnp.logical_and(sx - arow == jrow, sx >= frow)
        w = jnp.where(hit, 1.0, 0.0).astype(BF16)
        return acc + _mm(w, jnp.concatenate(ys, axis=0))

    ffn = lax.fori_loop(0, rounds, one_round, jnp.zeros(out_ref.shape[1:], F32))
    mod = mod_ref[0, 0]
    out_ref[0] = x1_ref[0] + mod[5:6, :] * _rms(ffn, gpost[...])


def _combine(lohi, slots, x1, mod, gpost, y, t0, nt, ncb):
    b, t, d = x1.shape
    ne = N_EXPERTS
    m_rows = y.shape[2]
    grid_spec = pltpu.PrefetchScalarGridSpec(
        num_scalar_prefetch=1,
        grid=(b, nt),
        in_specs=[pl.BlockSpec((1, ne, TB), lambda bi, i, *_: (bi, 0, i + t0)),
                  pl.BlockSpec((1, TB, d), lambda bi, i, *_: (bi, i + t0, 0)),
                  pl.BlockSpec((1, 1, 8, d), lambda bi, i, *_: (bi, jnp.where(i + t0 < ncb, 0, 1), 0, 0)),
                  pl.BlockSpec((1, d), lambda bi, i, *_: (0, 0)),
                  pl.BlockSpec((1, ne, m_rows, d), lambda bi, i, *_: (bi, 0, 0, 0),
                               pipeline_mode=pl.Buffered(1))],
        out_specs=pl.BlockSpec((1, TB, d), lambda bi, i, *_: (bi, i, 0)))
    return pl.pallas_call(
        functools.partial(_comb_kernel, m_rows),
        grid_spec=grid_spec,
        out_shape=jax.ShapeDtypeStruct((b, nt * TB, d), F32),
        compiler_params=_cparams(("parallel", "arbitrary")),
        name="combine",
    )(lohi, slots, x1, mod, gpost, y)


def _pos_tables(rows, d):
    quarter = d // 4
    freq = jnp.power(POS_BASE, -jnp.arange(quarter, dtype=F32) / quarter)
    ar = jnp.arange(rows, dtype=F32)[:, None] * freq
    ac = jnp.arange(GRID_W, dtype=F32)[:, None] * freq
    return (jnp.concatenate([jnp.sin(ar), jnp.cos(ar)], axis=-1),
            jnp.concatenate([jnp.sin(ac), jnp.cos(ac)], axis=-1))


def _tile_bounds(off, r, ntile, cap, base):
    b = off.shape[0]
    o = off.reshape(b, N_EXPERTS, r, LANES)[:, :, :, 0]
    lo = o[:, :, ::TB // LANES][:, :, :ntile] + base
    hi = jnp.concatenate([lo[:, :, 1:], jnp.full((b, N_EXPERTS, 1), cap + base, F32)], axis=2)
    return lo, hi


def kernel(x, c, ctx, c_ctx, w_ada, b_ada, g_mix_pre, g_mix_post, g_ffn_pre, g_ffn_post,
           w_in, conv_qk, b_ml_gates, w_gla_a2, b_gla_a, g_ml_norm, g_gla_norm, w_out,
           w_router, w_e_gate, w_e_up, w_e_down):
    bsz, n_tok, d = x.shape
    lc = ctx.shape[1]
    depth = w_in.shape[0]
    ne = N_EXPERTS
    t = lc + n_tok
    ncb = lc // TB
    nblk = t // TB
    assert lc % TB == 0 and n_tok % TB == 0 and d == 1024
    cap_l = EC_FACTOR * n_tok // ne
    cap_c = EC_FACTOR * lc // ne

    assert TB % GRID_W == 0
    pos_r, pos_c = _pos_tables(n_tok // GRID_W, d)
    xa = None

    cc = jnp.zeros((8, d), F32).at[:bsz].set(c).at[bsz].set(c_ctx)
    mods = _ada(cc, w_ada, b_ada)

    wide = jnp.concatenate([w_in[:, :, 0:2048], w_in[:, :, 2064:3600]], axis=2).astype(BF16)
    narrow = jnp.concatenate([w_in[:, :, 2048:2064], w_in[:, :, 3600:3632]], axis=2)
    narrow = jnp.pad(narrow, ((0, 0), (0, 0), (0, LANES - narrow.shape[2])))
    narrow = jnp.concatenate(_split2(narrow), axis=2)
    bias_s = jnp.pad(b_ml_gates, ((0, 0), (0, LANES - b_ml_gates.shape[1])))
    w2e = jnp.zeros((depth, 2, LANES, 256), F32)
    w2e = w2e.at[:, 0, 16:32].set(w_gla_a2[:, 0]).at[:, 1, 32:48].set(w_gla_a2[:, 1])

    for l in range(depth):
        last = l == depth - 1
        ml = mods[l, :bsz].reshape(bsz, 6, d)
        mc_ = jnp.broadcast_to(mods[l, bsz].reshape(1, 6, d), (bsz, 6, d))
        mod = jnp.pad(jnp.stack([mc_, ml], axis=1), ((0, 0), (0, 0), (0, 2), (0, 0)))

        if l == 0:
            xa, pb, ps, qc = _in_proj(None, mod, g_mix_pre[l][None], wide[l], narrow[l], bias_s[l][None],
                                      conv_qk[l], ncb, first=(x, ctx, pos_r, pos_c))
        else:
            pb, ps, qc = _in_proj(xa, mod, g_mix_pre[l][None], wide[l], narrow[l], bias_s[l][None],
                                  conv_qk[l], ncb)
        hf, hb = _mlstm2(qc, pb, ps, ncb)
        of, ob = _gla(pb, ps, w2e[l], b_gla_a[l][:, None, :], ncb)
        x1, h2e, aff = _out_proj(hf, hb, of, ob, pb, xa, mod, g_ml_norm[l][None], g_gla_norm[l][None],
                                 w_out[l].astype(BF16), g_mix_post[l][None], g_ffn_pre[l][None],
                                 w_router[l].T, ncb)

        rl = n_tok // LANES
        aff_l = aff[:, :, lc:].reshape(bsz, ne * rl, LANES)
        if last:
            sll, offl = _route(None, aff_l, 0, cap_l)
            slots = jnp.pad(sll.reshape(bsz, ne, n_tok), ((0, 0), (0, 0), (lc, 0)), constant_values=UNSEL)
            lo, hi = _tile_bounds(offl, rl, nblk - ncb, cap_l, 0)
            t0, nt, m_rows = ncb, nblk - ncb, cap_l
        else:
            rc = max(lc // LANES, 8)
            aff_c = aff[:, :, :lc].reshape(bsz, ne, lc // LANES, LANES)
            aff_c = jnp.pad(aff_c, ((0, 0), (0, 0), (0, rc - lc // LANES), (0, 0)), constant_values=-1.0)
            slc, offc, sll, offl = _route(aff_c.reshape(bsz, ne * rc, LANES), aff_l, cap_c, cap_l)
            slots = jnp.concatenate([slc.reshape(bsz, ne, rc * LANES)[:, :, :lc],
                                     sll.reshape(bsz, ne, n_tok)], axis=2)
            lo_l, hi_l = _tile_bounds(offl, rl, nblk - ncb, cap_l, 0)
            lo_c, hi_c = _tile_bounds(offc, rc, ncb, cap_c, cap_l)
            lo = jnp.concatenate([lo_c, lo_l], axis=2)
            hi = jnp.concatenate([hi_c, hi_l], axis=2)
            t0, nt, m_rows = 0, nblk, cap_l + cap_c
        lohi = jnp.concatenate([lo, hi], axis=1).transpose(0, 2, 1).astype(I32).reshape(-1)
        xin = _dispatch(lohi, h2e, slots, m_rows, t0, nt)
        y = _experts(xin, w_e_gate, w_e_up, w_e_down, l, cap_l)
        xa = _combine(lohi, slots, x1, mod, g_ffn_post[l][None], y, t0, nt, ncb)
    return xa
```

```python
import functools

import jax
import jax.numpy as jnp
from jax import lax
from jax.experimental import pallas as pl
from jax.experimental.pallas import tpu as pltpu

F32 = jnp.float32
BF16 = jnp.bfloat16
I32 = jnp.int32

EPS = 1e-6
GRID_W = 64
POS_BASE = 10000.0
N_HEADS = 4
ML_DH = 128
GLA_DK = 64
GLA_DV = 128
GLA_GATE_TAU = 16.0
N_EXPERTS = 16
EC_FACTOR = 2

LANES = 128
TB = 256
ML_L = 128
GLA_L = 128
EXPERT_ROWS = 256
WIN = 80
ROW_ALIGN = 16
UNSEL = 2047.0
PB_ML_V, PB_ML_O, PB_GLA_QK, PB_GLA_V, PB_GLA_R = 0, 1, 2, 3, 4
VMEM_LIMIT = 56 * 1024 * 1024


def _cparams(sem):
    return pltpu.CompilerParams(dimension_semantics=sem, vmem_limit_bytes=VMEM_LIMIT)


def _split2(a):
    hi = a.astype(BF16)
    lo = (a - hi.astype(F32)).astype(BF16)
    return hi, lo


def _split3(a):
    hi = a.astype(BF16)
    r = a - hi.astype(F32)
    mid = r.astype(BF16)
    lo = (r - mid.astype(F32)).astype(BF16)
    return hi, mid, lo


_NN = (((1,), (0,)), ((), ()))
_NT = (((1,), (1,)), ((), ()))
_TN = (((0,), (0,)), ((), ()))


def _mm(a, b, dims=_NN):
    return lax.dot_general(a, b, dims, preferred_element_type=F32)


def _dot3(a, b, dims=_NN):
    ah, al = _split2(a)
    bh, bl = _split2(b)
    return _mm(ah, bh, dims) + (_mm(ah, bl, dims) + _mm(al, bh, dims))


def _dot_exact_l(m_bf16, x, dims=_NN):
    hi, mid, lo = _split3(x)
    return _mm(m_bf16, hi, dims) + (_mm(m_bf16, mid, dims) + _mm(m_bf16, lo, dims))


def _rms(x, g):
    return x * lax.rsqrt(jnp.mean(x * x, axis=-1, keepdims=True) + EPS) * g


def _log_sigmoid(x):
    return jnp.minimum(x, 0.0) - jnp.log1p(jnp.exp(-jnp.abs(x)))


def _sigmoid(x):
    return 1.0 / (1.0 + jnp.exp(-x))


def _silu(x):
    return x * _sigmoid(x)


def _iota(shape, dim):
    return lax.broadcasted_iota(I32, shape, dim)


def _rev_block(i, ncb, nblk):
    return jnp.where(i < ncb, ncb - 1 - i, nblk - 1 - (i - ncb))


def _ada_kernel(c_ref, w_ref, b_ref, o_ref):
    a = _silu(c_ref[...])
    o_ref[0] = _dot3(a, w_ref[0]) + b_ref[0]


def _ada(cc, w_ada, b_ada):
    depth, d, n6 = w_ada.shape
    tn = 1536
    return pl.pallas_call(
        _ada_kernel,
        grid=(depth, n6 // tn),
        in_specs=[pl.BlockSpec((8, d), lambda l, j: (0, 0)),
                  pl.BlockSpec((1, d, tn), lambda l, j: (l, 0, j)),
                  pl.BlockSpec((1, 1, tn), lambda l, j: (l, 0, j))],
        out_specs=pl.BlockSpec((1, 8, tn), lambda l, j: (l, 0, j)),
        out_shape=jax.ShapeDtypeStruct((depth, 8, n6), F32),
        compiler_params=_cparams(("parallel", "parallel")),
        name="ada",
    )(cc, w_ada, b_ada.reshape(depth, 1, n6))


def _project(ncb, x, xp, xn, mod_ref, g_ref, wb_ref, ws_ref, bs_ref, cw_ref, pb_ref, ps_ref, qc_ref):
    i = pl.program_id(1)
    nblk = pl.num_programs(1)
    n = x.shape[0]
    norm = lambda z: _rms(z, g_ref[...]) * (1.0 + mod_ref[0, 0, 1:2, :]) + mod_ref[0, 0, 0:1, :]
    hh, hl = _split2(norm(x))
    wq = cw_ref.shape[1]
    hext = jnp.concatenate([hh, norm(xp).astype(BF16), norm(xn).astype(BF16)], axis=0)
    qk = _mm(hext, wb_ref[:, 0:wq])
    pb_ref[0] = _mm(hh, wb_ref[:, wq:])
    lvalid = jnp.logical_and(i != 0, i != ncb)
    rvalid = jnp.logical_and(i != ncb - 1, i != nblk - 1)
    left = jnp.where(lvalid, qk[n + 7:n + 8, :], 0.0)
    right = jnp.where(rvalid, qk[n + 8:n + 9, :], 0.0)
    y = _silu(_conv3(qk[:n, :], left, right, cw_ref[...]))
    dq = wq // 2
    qc_ref[0, :, 0:dq] = (y[:, 0:dq] * (ML_DH ** -0.5)).astype(BF16)
    qc_ref[0, :, dq:] = y[:, dq:].astype(BF16)
    pr = _mm(jnp.concatenate([hh, hl], axis=0), ws_ref[...])
    ps_ref[0] = (pr[:n, :LANES] + pr[:n, LANES:]) + (pr[n:, :LANES] + pr[n:, LANES:]) + bs_ref[...]


def _in_kernel(ncb, x_ref, xp_ref, xn_ref, mod_ref, g_ref, wb_ref, ws_ref, bs_ref, cw_ref,
               pb_ref, ps_ref, qc_ref):
    _project(ncb, x_ref[0], xp_ref[0], xn_ref[0], mod_ref, g_ref, wb_ref, ws_ref, bs_ref, cw_ref,
             pb_ref, ps_ref, qc_ref)


def _in0_kernel(ncb, x_ref, xp_ref, xn_ref, c_ref, cp_ref, cn_ref, pr_ref, prp_ref, prn_ref, pc_ref,
                mod_ref, g_ref, wb_ref, ws_ref, bs_ref, cw_ref, xa_ref, pb_ref, ps_ref, qc_ref):
    i = pl.program_id(1)
    half = pr_ref.shape[2]
    reps = TB // GRID_W
    prow = jnp.concatenate([jnp.broadcast_to(pr_ref[0, k:k + 1, :], (GRID_W, half)) for k in range(reps)], axis=0)
    pcol = jnp.concatenate([pc_ref[...]] * reps, axis=0)
    lat = x_ref[0] + jnp.concatenate([prow, pcol], axis=1)
    is_ctx = i < ncb
    xa = jnp.where(is_ctx, c_ref[0], lat)
    xa_ref[0] = xa
    pos_p = jnp.concatenate([prp_ref[0, reps - 1:reps, :], pc_ref[GRID_W - 1:GRID_W, :]], axis=1)
    pos_n = jnp.concatenate([prn_ref[0, 0:1, :], pc_ref[0:1, :]], axis=1)
    xp = jnp.where(is_ctx, cp_ref[0], xp_ref[0] + pos_p)
    xn = jnp.where(is_ctx, cn_ref[0], xn_ref[0] + pos_n)
    _project(ncb, xa, xp, xn, mod_ref, g_ref, wb_ref, ws_ref, bs_ref, cw_ref, pb_ref, ps_ref, qc_ref)


def _tile_and_halo_specs(rows, d, tile_of):
    r8 = TB // 8
    ntile, last8 = rows // TB, rows // 8 - 1
    tl = lambda i: jnp.clip(tile_of(i), 0, ntile - 1)
    return [pl.BlockSpec((1, TB, d), lambda bi, i: (bi, tl(i), 0)),
            pl.BlockSpec((1, 8, d), lambda bi, i: (bi, jnp.clip(tl(i) * r8 - 1, 0, last8), 0)),
            pl.BlockSpec((1, 8, d), lambda bi, i: (bi, jnp.clip((tl(i) + 1) * r8, 0, last8), 0))]


def _in_proj(xa, mod, g, wb, ws, bs, cw, ncb, first=None):
    if first is None:
        b, t, d = xa.shape
    else:
        b, t, d = first[0].shape[0], first[0].shape[1] + first[1].shape[1], first[0].shape[2]
    wq = cw.shape[1]
    nb = wb.shape[1] - wq
    const = lambda shp: pl.BlockSpec(shp, lambda bi, i: tuple(0 for _ in shp))
    common = [pl.BlockSpec((1, 1, 8, d), lambda bi, i: (bi, jnp.where(i < ncb, 0, 1), 0, 0)),
              const((1, d)), const((d, nb + wq)), const((d, 2 * LANES)), const((1, LANES)), const((3, wq))]
    out_specs = [pl.BlockSpec((1, TB, nb), lambda bi, i: (bi, i, 0)),
                 pl.BlockSpec((1, TB, LANES), lambda bi, i: (bi, i, 0)),
                 pl.BlockSpec((1, TB, wq), lambda bi, i: (bi, i, 0))]
    out_shape = [jax.ShapeDtypeStruct((b, t, nb), F32),
                 jax.ShapeDtypeStruct((b, t, LANES), F32),
                 jax.ShapeDtypeStruct((b, t, wq), BF16)]
    if first is None:
        body = functools.partial(_in_kernel, ncb)
        in_specs = _tile_and_halo_specs(t, d, lambda i: i) + common
        args = (xa, xa, xa, mod, g, wb, ws, bs, cw)
    else:
        x, ctx, pos_r, pos_c = first
        reps = TB // GRID_W
        ntl = x.shape[1] // TB
        body = functools.partial(_in0_kernel, ncb)
        pr_spec = lambda off: pl.BlockSpec((1, reps, d // 2),
                                           lambda bi, i: (jnp.clip(i - ncb + off, 0, ntl - 1), 0, 0))
        in_specs = (_tile_and_halo_specs(x.shape[1], d, lambda i: i - ncb)
                    + _tile_and_halo_specs(ctx.shape[1], d, lambda i: i)
                    + [pr_spec(0), pr_spec(-1), pr_spec(1), const((GRID_W, d // 2))] + common)
        out_specs = [pl.BlockSpec((1, TB, d), lambda bi, i: (bi, i, 0))] + out_specs
        out_shape = [jax.ShapeDtypeStruct((b, t, d), F32)] + out_shape
        pr3 = pos_r.reshape(-1, reps, d // 2)
        args = (x, x, x, ctx, ctx, ctx, pr3, pr3, pr3, pos_c, mod, g, wb, ws, bs, cw)
    return pl.pallas_call(
        body,
        grid=(b, t // TB),
        in_specs=in_specs,
        out_specs=out_specs,
        out_shape=out_shape,
        compiler_params=_cparams(("parallel", "parallel")),
        name="in_proj",
    )(*args)


def _conv3(x, hl, hr, w):
    rows = _iota(x.shape, 0)
    prev = jnp.where(rows == 0, hl, pltpu.roll(x, 1, axis=0))
    nxt = jnp.where(rows == x.shape[0] - 1, hr, pltpu.roll(x, x.shape[0] - 1, axis=0))
    return prev * w[0:1] + x * w[1:2] + nxt * w[2:3]


def _cummax_rows(x, reverse):
    n = x.shape[0]
    rows = _iota(x.shape, 0)
    s = 1
    while s < n:
        if reverse:
            sh = jnp.where(rows < n - s, pltpu.roll(x, n - s, axis=0), -jnp.inf)
        else:
            sh = jnp.where(rows >= s, pltpu.roll(x, s, axis=0), -jnp.inf)
        x = jnp.maximum(x, sh)
        s *= 2
    return x


def _ml_pair(fwd, bwd, c_s, m_s):
    ll = ML_L
    nch = TB // ll
    dq = N_HEADS * ML_DH
    rows = _iota((ll, ll), 0)
    cols = _iota((ll, ll), 1)
    causal = [cols <= rows, cols >= rows]
    tri = [jnp.where(m, 1.0, 0.0).astype(BF16) for m in causal]
    ones = jnp.ones((ll, ML_DH), BF16)
    qk = [fwd[0][0], bwd[0][0]]
    v = [fwd[1][0], bwd[1][0]]
    g = [fwd[2][0], bwd[2][0]]
    outs = [fwd[3], bwd[3]]
    units = [(d, h) for d in range(2) for h in range(N_HEADS)]
    cx = {u: c_s[u[0] * N_HEADS + u[1]] for u in units}
    m_row = [m_s[0], m_s[1]]
    for step in range(nch):
        r0 = [step * ll, (nch - 1 - step) * ll]
        alpha, a_in, em, e_w, ut, a_old, a_new = [], [], [], [], [], [], []
        for d in range(2):
            gc = g[d][r0[d]:r0[d] + ll, :]
            bc = _dot_exact_l(tri[d], _log_sigmoid(gc))
            u = pltpu.roll(gc, 4, axis=1) - bc
            cm = _cummax_rows(u, bool(d))
            neg_alpha = jnp.maximum(m_row[d], cm)
            alpha.append(-neg_alpha)
            a_in.append(jnp.exp(m_row[d] - neg_alpha))
            em.append(jnp.exp(-neg_alpha - bc))
            last = slice(0, 1) if d else slice(ll - 1, ll)
            cm_end = cm[last, :]
            bend = bc[last, :]
            e_w.append(jnp.exp(u - cm_end))
            ut.append(u.T)
            m_kv = bend + cm_end
            m_new = jnp.maximum(bend + m_row[d], m_kv)
            a_old.append(jnp.exp(bend + m_row[d] - m_new))
            a_new.append(jnp.exp(m_kv - m_new))
            m_row[d] = m_new
        lane = lambda d, h: 8 * d + 4 + h
        qb = {(d, h): qk[d][r0[d]:r0[d] + ll, h * ML_DH:(h + 1) * ML_DH] for d, h in units}
        kb = {(d, h): qk[d][r0[d]:r0[d] + ll, dq + h * ML_DH:dq + (h + 1) * ML_DH] for d, h in units}
        vh = {(d, h): v[d][r0[d]:r0[d] + ll, h * ML_DH:(h + 1) * ML_DH] for d, h in units}
        sc = {u: _mm(qb[u], kb[u], _NT) for u in units}
        inter = {u: _mm(qb[u], cx[u].astype(BF16), _NT) for u in units}
        sb = {}
        for d, h in units:
            c = lane(d, h)
            arg = jnp.where(causal[d], alpha[d][:, c:c + 1] + ut[d][c:c + 1, :], -jnp.inf)
            sb[(d, h)] = (sc[(d, h)] * jnp.exp(arg)).astype(BF16)
        ckv = {}
        for d, h in units:
            c = lane(d, h)
            ew = e_w[d][:, c:c + 1]
            ev = jnp.concatenate([(ew * vh[(d, h)]).astype(BF16), jnp.broadcast_to(ew, (ll, ML_DH)).astype(BF16)], axis=1)
            ckv[(d, h)] = _mm(ev, kb[(d, h)], _TN)
        for d, h in units:
            c = lane(d, h)
            nd = (_mm(sb[(d, h)], jnp.concatenate([vh[(d, h)].astype(BF16), ones], axis=1))
                  + a_in[d][:, c:c + 1] * inter[(d, h)])
            den = jnp.maximum(jnp.abs(nd[:, ML_DH:]), em[d][:, c:c + 1])
            outs[d][0, r0[d]:r0[d] + ll, h * ML_DH:(h + 1) * ML_DH] = nd[:, :ML_DH] / den
        cx = {(d, h): a_old[d][:, lane(d, h):lane(d, h) + 1] * cx[(d, h)]
              + a_new[d][:, lane(d, h):lane(d, h) + 1] * ckv[(d, h)] for d, h in units}
    for d, h in units:
        c_s[d * N_HEADS + h] = cx[(d, h)]
    m_s[0] = m_row[0]
    m_s[1] = m_row[1]


def _gla_pair(fwd, bwd, w2_ref, ba_ref, s_s):
    ll = GLA_L
    nch = TB // ll
    dkw = N_HEADS * GLA_DK
    rows = _iota((ll, ll), 0)
    cols = _iota((ll, ll), 1)
    causal = [cols <= rows, cols >= rows]
    tri = [jnp.where(m, 1.0, 0.0).astype(BF16) for m in causal]
    qk = [fwd[0][0], bwd[0][0]]
    v = [fwd[1][0], bwd[1][0]]
    outs = [fwd[3], bwd[3]]
    la = [_log_sigmoid(_dot3(r[2][0], w2_ref[d]) + ba_ref[d]) * (1.0 / GLA_GATE_TAU)
          for d, r in enumerate((fwd, bwd))]
    units = [(d, h) for d in range(2) for h in range(N_HEADS)]
    st = {u: s_s[u[0] * N_HEADS + u[1]] for u in units}
    for step in range(nch):
        r0 = [step * ll, (nch - 1 - step) * ll]
        qi, qt, kt, kd, e_end = [], [], [], [], []
        for d in range(2):
            bcum = _dot_exact_l(tri[d], la[d][r0[d]:r0[d] + ll, :])
            ref = bcum[ll // 2:ll // 2 + 1, :]
            bend = bcum[0:1, :] if d else bcum[ll - 1:ll, :]
            q = qk[d][r0[d]:r0[d] + ll, 0:dkw] * (GLA_DK ** -0.5)
            k = qk[d][r0[d]:r0[d] + ll, dkw:2 * dkw]
            qi.append((q * jnp.exp(bcum)).astype(BF16))
            qt.append((q * jnp.exp(bcum - ref)).astype(BF16))
            ktd = k * jnp.exp(ref - bcum)
            kt.append(ktd.astype(BF16))
            kd.append((ktd * jnp.exp(bend - ref)).astype(BF16))
            e_end.append(jnp.exp(bend))
        hs = lambda a, h, w: a[:, h * w:(h + 1) * w]
        vb = {(d, h): hs(v[d][r0[d]:r0[d] + ll, :], h, GLA_DV).astype(BF16) for d, h in units}
        att = {(d, h): _mm(hs(qt[d], h, GLA_DK), hs(kt[d], h, GLA_DK), _NT) for d, h in units}
        inter = {(d, h): _mm(hs(qi[d], h, GLA_DK), st[(d, h)].astype(BF16), _NT) for d, h in units}
        attb = {(d, h): jnp.where(causal[d], att[(d, h)], 0.0).astype(BF16) for d, h in units}
        kv = {(d, h): _mm(vb[(d, h)], hs(kd[d], h, GLA_DK), _TN) for d, h in units}
        for d, h in units:
            o = _mm(attb[(d, h)], vb[(d, h)]) + inter[(d, h)]
            outs[d][0, r0[d]:r0[d] + ll, h * GLA_DV:(h + 1) * GLA_DV] = o
        st = {(d, h): st[(d, h)] * hs(e_end[d], h, GLA_DK) + kv[(d, h)] for d, h in units}
    for d, h in units:
        s_s[d * N_HEADS + h] = st[(d, h)]


def _mix2_kernel(qcf, mvf, psf, gqf, gvf, qcb, mvb, psb, gqb, gvb, w2, ba,
                 mf_ref, mb_ref, gf_ref, gb_ref, c_s, m_s, s_s):
    @pl.when(pl.program_id(1) == 0)
    def _():
        c_s[...] = jnp.zeros_like(c_s)
        m_s[...] = jnp.zeros_like(m_s)
        s_s[...] = jnp.zeros_like(s_s)

    _ml_pair((qcf, mvf, psf, mf_ref), (qcb, mvb, psb, mb_ref), c_s, m_s)
    _gla_pair((gqf, gvf, psf, gf_ref), (gqb, gvb, psb, gb_ref), w2, ba, s_s)


def _mixers2(qc, pb, ps, w2e, ba, ncb):
    b, t, _ = pb.shape
    nblk = t // TB
    fwd = lambda i: i
    bwd = lambda i: _rev_block(i, ncb, nblk)

    def dspecs(blk):
        col = lambda cb: pl.BlockSpec((1, TB, 512), lambda b_, i: (b_, blk(i), cb))
        return [pl.BlockSpec((1, TB, qc.shape[2]), lambda b_, i: (b_, blk(i), 0)), col(PB_ML_V),
                pl.BlockSpec((1, TB, LANES), lambda b_, i: (b_, blk(i), 0)), col(PB_GLA_QK), col(PB_GLA_V)]

    specs = dspecs(fwd) + dspecs(bwd) + [pl.BlockSpec((2, LANES, 256), lambda b_, i: (0, 0, 0)),
                                         pl.BlockSpec((2, 1, 256), lambda b_, i: (0, 0, 0))]
    ns = 2 * N_HEADS
    ofwd = pl.BlockSpec((1, TB, 512), lambda b_, i: (b_, i, 0))
    obwd = pl.BlockSpec((1, TB, 512), lambda b_, i: (b_, bwd(i), 0))
    return pl.pallas_call(
        _mix2_kernel,
        grid=(b, nblk),
        in_specs=specs,
        out_specs=[ofwd, obwd, ofwd, obwd],
        out_shape=[jax.ShapeDtypeStruct((b, t, 512), F32)] * 4,
        scratch_shapes=[pltpu.VMEM((ns, 2 * ML_DH, ML_DH), F32),
                        pltpu.VMEM((2, 1, LANES), F32),
                        pltpu.VMEM((ns, GLA_DV, GLA_DK), F32)],
        compiler_params=_cparams(("parallel", "arbitrary")),
        name="mixers",
    )(*([qc, pb, ps, pb, pb] * 2), w2e, ba)


def _head_norm(x, g):
    outs = []
    for h in range(N_HEADS):
        seg = x[:, h * 128:(h + 1) * 128]
        outs.append(seg * lax.rsqrt(jnp.mean(seg * seg, axis=-1, keepdims=True) + EPS))
    return jnp.concatenate(outs, axis=-1) * g


def _out_kernel(hf, hb, of, ob, og, rg, x_ref, mod_ref, gml, ggla, wo, gpost, gpre, wrt,
                x1_ref, h2e_ref, aff_ref):
    ne = N_EXPERTS
    d = x_ref.shape[2]
    rh = LANES
    rs = [slice(k * rh, (k + 1) * rh) for k in range(TB // rh)]
    mod = mod_ref[0, 0]
    y = [jnp.concatenate([_head_norm(hf[0, r, :] + hb[0, r, :], gml[...]) * _sigmoid(og[0, r, :]),
                          _head_norm(of[0, r, :] + ob[0, r, :], ggla[...]) * _silu(rg[0, r, :])],
                         axis=-1).astype(BF16) for r in rs]
    y2 = [_mm(yk, wo[...]) for yk in y]
    x1 = [x_ref[0, r, :] + mod[2:3, :] * _rms(y2k, gpost[...]) for r, y2k in zip(rs, y2)]
    for r, x1k in zip(rs, x1):
        x1_ref[0, r, :] = x1k
    h2 = [_rms(x1k, gpre[...]) * (1.0 + mod[4:5, :]) + mod[3:4, :] for x1k in x1]
    lt = [_dot3(wrt[...], h2k, _NT) for h2k in h2]
    ext = [jnp.exp(ltk - jnp.max(ltk, axis=0, keepdims=True)) for ltk in lt]
    aff = [e / jnp.sum(e, axis=0, keepdims=True) for e in ext]
    for r, affk, h2k in zip(rs, aff, h2):
        aff_ref[0, :, r] = affk
        afft = jnp.concatenate([affk, jnp.zeros((LANES - ne, rh), F32)], axis=0).T
        a_hi, a_mid, a_lo = _split3(afft)
        pieces = (a_hi.astype(F32) + pltpu.roll(a_mid.astype(F32), ne, axis=1)
                  + pltpu.roll(a_lo.astype(F32), 2 * ne, axis=1))
        h2e_ref[0, r, 0:d] = h2k.astype(BF16)
        h2e_ref[0, r, d:d + LANES] = pieces.astype(BF16)


def _out_proj(hf, hb, of, ob, pb, xa, mod, gml, ggla, wo, gpost, gpre, wrt, ncb):
    b, t, d = xa.shape
    de = d + LANES
    tile = lambda cb: pl.BlockSpec((1, TB, 512), lambda bi, i: (bi, i, cb))
    full = lambda shp: pl.BlockSpec(shp, lambda bi, i: tuple(0 for _ in shp))
    return pl.pallas_call(
        _out_kernel,
        grid=(b, t // TB),
        in_specs=[tile(0), tile(0), tile(0), tile(0), tile(PB_ML_O), tile(PB_GLA_R),
                  pl.BlockSpec((1, TB, d), lambda bi, i: (bi, i, 0)),
                  pl.BlockSpec((1, 1, 8, d), lambda bi, i: (bi, jnp.where(i < ncb, 0, 1), 0, 0)),
                  full((1, 512)), full((1, 512)), full((d, d)), full((1, d)), full((1, d)),
                  full((N_EXPERTS, d))],
        out_specs=[pl.BlockSpec((1, TB, d), lambda bi, i: (bi, i, 0)),
                   pl.BlockSpec((1, TB, de), lambda bi, i: (bi, i, 0)),
                   pl.BlockSpec((1, N_EXPERTS, TB), lambda bi, i: (bi, 0, i))],
        out_shape=[jax.ShapeDtypeStruct((b, t, d), F32),
                   jax.ShapeDtypeStruct((b, t, de), BF16),
                   jax.ShapeDtypeStruct((b, N_EXPERTS, t), F32)],
        compiler_params=_cparams(("parallel", "parallel")),
        name="out_proj",
    )(hf, hb, of, ob, pb, pb, xa, mod, gml, ggla, wo, gpost, gpre, wrt)


def _cumsum_blocks(x, r):
    n = x.shape[0]
    xb = x.astype(BF16)
    li = _iota((LANES, LANES), 0)
    lj = _iota((LANES, LANES), 1)
    upper = jnp.where(li <= lj, 1.0, 0.0).astype(BF16)
    ones = jnp.ones((LANES, LANES), BF16)
    inrow = _mm(xb, upper)
    tot = _mm(xb, ones)
    ri = _iota((n, n), 0)
    rj = _iota((n, n), 1)
    same = (ri // r) == (rj // r)
    strict = jnp.where(jnp.logical_and(same, rj < ri), 1.0, 0.0).astype(BF16)
    off = _mm(strict, tot.astype(BF16))
    return inrow + off, off


def _select(aff, r, cap, base_slot):
    ne = N_EXPERTS
    n = ne * r
    aff3 = aff.reshape(ne, r, LANES)
    capf = jnp.float32(cap)

    def body(k, prefix):
        cand = prefix | (jnp.int32(1) << (30 - k))
        candf = lax.bitcast_convert_type(cand, F32)
        cnt = jnp.sum(jnp.where(aff3 >= candf, 1.0, 0.0), axis=(1, 2), keepdims=True)
        return jnp.where(cnt >= capf, cand, prefix)

    thr = lax.bitcast_convert_type(lax.fori_loop(0, 31, body, jnp.zeros((ne, 1, 1), I32)), F32)
    gt = jnp.where(aff3 > thr, 1.0, 0.0)
    eq = jnp.where(aff3 == thr, 1.0, 0.0)
    need = capf - jnp.sum(gt, axis=(1, 2), keepdims=True)
    eq2 = eq.reshape(n, LANES)
    cs_eq, _ = _cumsum_blocks(eq2, r)
    eq_rank = (cs_eq - eq2).reshape(ne, r, LANES)
    sel = (gt + eq * jnp.where(eq_rank < need, 1.0, 0.0)).reshape(n, LANES)
    cs, off = _cumsum_blocks(sel, r)
    slot = jnp.where(sel > 0.5, cs - 1.0 + base_slot, UNSEL)
    return slot, off


def _sel_kernel(rc, rl, cap_c, cap_l, *refs):
    if rc:
        affc, affl, slc, offc, sll, offl = refs
        slc[0], offc[0] = _select(affc[0], rc, cap_c, float(cap_l))
    else:
        affl, sll, offl = refs
    sll[0], offl[0] = _select(affl[0], rl, cap_l, 0.0)


def _route(aff_c, aff_l, cap_c, cap_l):
    b = aff_l.shape[0]
    ne = N_EXPERTS
    rl = aff_l.shape[1] // ne
    rc = aff_c.shape[1] // ne if aff_c is not None else 0
    args = ([aff_c] if rc else []) + [aff_l]
    in_specs, out_shape, out_specs = [], [], []
    for a in args:
        spec = pl.BlockSpec((1,) + a.shape[1:], lambda bi: (bi, 0, 0))
        in_specs.append(spec)
        out_shape += [jax.ShapeDtypeStruct(a.shape, F32)] * 2
        out_specs += [spec, spec]
    return pl.pallas_call(
        functools.partial(_sel_kernel, rc, rl, cap_c, cap_l),
        grid=(b,),
        in_specs=in_specs,
        out_specs=out_specs,
        out_shape=out_shape,
        compiler_params=_cparams(("parallel",)),
        name="route",
    )(*args)


def _window(lo_ref, base, e, m_rows):
    lo_e = lo_ref[base + e]
    hi_e = lo_ref[base + N_EXPERTS + e]
    a_e = jnp.minimum((lo_e // ROW_ALIGN) * ROW_ALIGN, m_rows - WIN)
    return a_e, hi_e


def _disp_kernel(m_rows, ng, lo_ref, h_ref, slot_ref, x_ref):
    ne = N_EXPERTS
    gi = pl.program_id(1)
    i = pl.program_id(2)
    base = (pl.program_id(0) * pl.num_programs(2) + i) * (2 * ne)

    @pl.when(i == 0)
    def _():
        x_ref[...] = jnp.zeros_like(x_ref)

    h = h_ref[0]
    sl = slot_ref[0]
    sub = _iota((WIN, TB), 0).astype(F32)
    wins = [_window(lo_ref, base, gi * ng + k, m_rows) for k in range(ng)]
    rounds = jnp.int32(1)
    for a_e, hi_e in wins:
        rounds = jnp.maximum(rounds, (hi_e - a_e + WIN - 1) // WIN)

    def one_round(rd, carry):
        wts, starts = [], []
        for k, (a_e, _) in enumerate(wins):
            first = a_e + rd * WIN
            a_r = jnp.minimum(first, m_rows - WIN)
            srow = sl[k:k + 1, :]
            hit = jnp.logical_and(srow - a_r.astype(F32) == sub, srow >= first.astype(F32))
            wts.append(jnp.where(hit, 1.0, 0.0).astype(BF16))
            starts.append(a_r)
        g = _mm(jnp.concatenate(wts, axis=0), h).astype(BF16)
        for k, a_r in enumerate(starts):
            rows = pl.ds(pl.multiple_of(a_r, ROW_ALIGN), WIN)
            x_ref[0, k, rows, :] = x_ref[0, k, rows, :] + g[k * WIN:(k + 1) * WIN, :]
        return carry

    lax.fori_loop(0, rounds, one_round, 0)


def _dispatch(lohi, h2e, slots, m_rows, t0, nt):
    b, t, de = h2e.shape
    ne = N_EXPERTS
    ng = 8
    grid_spec = pltpu.PrefetchScalarGridSpec(
        num_scalar_prefetch=1,
        grid=(b, ne // ng, nt),
        in_specs=[pl.BlockSpec((1, TB, de), lambda bi, gi, i, *_: (bi, i + t0, 0)),
                  pl.BlockSpec((1, ng, TB), lambda bi, gi, i, *_: (bi * (ne // ng) + gi, 0, i + t0))],
        out_specs=pl.BlockSpec((1, ng, m_rows, de), lambda bi, gi, i, *_: (bi, gi, 0, 0)))
    return pl.pallas_call(
        functools.partial(_disp_kernel, m_rows, ng),
        grid_spec=grid_spec,
        out_shape=jax.ShapeDtypeStruct((b, ne, m_rows, de), BF16),
        compiler_params=_cparams(("parallel", "parallel", "arbitrary")),
        name="dispatch",
    )(lohi, h2e, slots.reshape(b * (ne // ng), ng, t))


def _row_chunks(m_rows, cap_l):
    step = min(EXPERT_ROWS, cap_l)
    starts = list(range(0, cap_l, step))
    return [(s, (m_rows - s) if s == starts[-1] else step) for s in starts]


def _exp_kernel(m_rows, cap_l, x_ref, wg_ref, wu_ref, wd_ref, y_ref, wg_s, wu_s, wd_s):
    ei = pl.program_id(0)
    d = wg_ref.shape[2]

    @pl.when(pl.program_id(1) == 0)
    def _():
        wg_s[...] = wg_ref[0, 0].astype(BF16)
        wu_s[...] = wu_ref[0, 0].astype(BF16)
        wd_s[...] = wd_ref[0, 0].astype(BF16)

    for r0, mc in _row_chunks(m_rows, cap_l):
        xs = x_ref[0, 0, r0:r0 + mc, 0:d]
        hid = _silu(_mm(xs, wg_s[...])) * _mm(xs, wu_s[...])
        y = _mm(hid.astype(BF16), wd_s[...])
        pieces = x_ref[0, 0, r0:r0 + mc, d:d + LANES].astype(F32)
        lane = _iota((mc, LANES), 1)
        mine = jnp.logical_and(lane % N_EXPERTS == ei, lane < 3 * N_EXPERTS)
        gate = jnp.sum(jnp.where(mine, pieces, 0.0), axis=1, keepdims=True)
        y_ref[0, 0, r0:r0 + mc, :] = (y * gate).astype(BF16)


def _experts(xin, wg, wu, wd, layer, cap_l):
    b, ne, m_rows, de = xin.shape
    _, _, d, f = wg.shape
    return pl.pallas_call(
        functools.partial(_exp_kernel, m_rows, cap_l),
        grid=(ne, b),
        in_specs=[pl.BlockSpec((1, 1, m_rows, de), lambda ei, bi: (bi, ei, 0, 0)),
                  pl.BlockSpec((1, 1, d, f), lambda ei, bi: (layer, ei, 0, 0)),
                  pl.BlockSpec((1, 1, d, f), lambda ei, bi: (layer, ei, 0, 0)),
                  pl.BlockSpec((1, 1, f, d), lambda ei, bi: (layer, ei, 0, 0))],
        out_specs=pl.BlockSpec((1, 1, m_rows, d), lambda ei, bi: (bi, ei, 0, 0)),
        out_shape=jax.ShapeDtypeStruct((b, ne, m_rows, d), BF16),
        scratch_shapes=[pltpu.VMEM((d, f), BF16), pltpu.VMEM((d, f), BF16), pltpu.VMEM((f, d), BF16)],
        compiler_params=_cparams(("arbitrary", "arbitrary")),
        name="experts",
    )(xin, wg, wu, wd)


def _comb_kernel(m_rows, lo_ref, slot_ref, x1_ref, mod_ref, gpost, y_ref, out_ref):
    ne = N_EXPERTS
    i = pl.program_id(1)
    base = (pl.program_id(0) * pl.num_programs(1) + i) * (2 * ne)
    kk = ne * WIN
    sl = slot_ref[0]
    hi = jnp.floor(sl * (1.0 / 32.0))
    lo = sl - hi * 32.0
    col_e = _iota((ne, kk), 1) // WIN
    expand = jnp.where(col_e == _iota((ne, kk), 0), 1.0, 0.0).astype(BF16)
    sx = _mm(hi.astype(BF16), expand, _TN) * 32.0 + _mm(lo.astype(BF16), expand, _TN)
    col = _iota((1, kk), 1)
    jrow = (col % WIN).astype(F32)
    wins = [_window(lo_ref, base, e, m_rows) for e in range(ne)]
    rounds = jnp.int32(1)
    for a_e, hi_e in wins:
        rounds = jnp.maximum(rounds, (hi_e - a_e + WIN - 1) // WIN)

    def one_round(rd, acc):
        ys = []
        arow = jnp.zeros((1, kk), F32)
        frow = jnp.zeros((1, kk), F32)
        for e, (a_e, _) in enumerate(wins):
            first = a_e + rd * WIN
            a_r = jnp.minimum(first, m_rows - WIN)
            mine = col // WIN == e
            arow = jnp.where(mine, a_r.astype(F32), arow)
            frow = jnp.where(mine, first.astype(F32), frow)
            ys.append(y_ref[0, e, pl.ds(pl.multiple_of(a_r, ROW_ALIGN), WIN), :])
        hit = jnp.logical_and(sx - arow == jrow, sx >= frow)
        w = jnp.where(hit, 1.0, 0.0).astype(BF16)
        return acc + _mm(w, jnp.concatenate(ys, axis=0))

    ffn = lax.fori_loop(0, rounds, one_round, jnp.zeros(out_ref.shape[1:], F32))
    mod = mod_ref[0, 0]
    out_ref[0] = x1_ref[0] + mod[5:6, :] * _rms(ffn, gpost[...])


def _combine(lohi, slots, x1, mod, gpost, y, t0, nt, ncb):
    b, t, d = x1.shape
    ne = N_EXPERTS
    m_rows = y.shape[2]
    grid_spec = pltpu.PrefetchScalarGridSpec(
        num_scalar_prefetch=1,
        grid=(b, nt),
        in_specs=[pl.BlockSpec((1, ne, TB), lambda bi, i, *_: (bi, 0, i + t0)),
                  pl.BlockSpec((1, TB, d), lambda bi, i, *_: (bi, i + t0, 0)),
                  pl.BlockSpec((1, 1, 8, d), lambda bi, i, *_: (bi, jnp.where(i + t0 < ncb, 0, 1), 0, 0)),
                  pl.BlockSpec((1, d), lambda bi, i, *_: (0, 0)),
                  pl.BlockSpec((1, ne, m_rows, d), lambda bi, i, *_: (bi, 0, 0, 0),
                               pipeline_mode=pl.Buffered(1))],
        out_specs=pl.BlockSpec((1, TB, d), lambda bi, i, *_: (bi, i, 0)))
    return pl.pallas_call(
        functools.partial(_comb_kernel, m_rows),
        grid_spec=grid_spec,
        out_shape=jax.ShapeDtypeStruct((b, nt * TB, d), F32),
        compiler_params=_cparams(("parallel", "arbitrary")),
        name="combine",
    )(lohi, slots, x1, mod, gpost, y)


def _pos_tables(rows, d):
    quarter = d // 4
    freq = jnp.power(POS_BASE, -jnp.arange(quarter, dtype=F32) / quarter)
    ar = jnp.arange(rows, dtype=F32)[:, None] * freq
    ac = jnp.arange(GRID_W, dtype=F32)[:, None] * freq
    return (jnp.concatenate([jnp.sin(ar), jnp.cos(ar)], axis=-1),
            jnp.concatenate([jnp.sin(ac), jnp.cos(ac)], axis=-1))


def _tile_bounds(off, r, ntile, cap, base):
    b = off.shape[0]
    o = off.reshape(b, N_EXPERTS, r, LANES)[:, :, :, 0]
    lo = o[:, :, ::TB // LANES][:, :, :ntile] + base
    hi = jnp.concatenate([lo[:, :, 1:], jnp.full((b, N_EXPERTS, 1), cap + base, F32)], axis=2)
    return lo, hi


def kernel(x, c, ctx, c_ctx, w_ada, b_ada, g_mix_pre, g_mix_post, g_ffn_pre, g_ffn_post,
           w_in, conv_qk, b_ml_gates, w_gla_a2, b_gla_a, g_ml_norm, g_gla_norm, w_out,
           w_router, w_e_gate, w_e_up, w_e_down):
    bsz, n_tok, d = x.shape
    lc = ctx.shape[1]
    depth = w_in.shape[0]
    ne = N_EXPERTS
    t = lc + n_tok
    ncb = lc // TB
    nblk = t // TB
    assert lc % TB == 0 and n_tok % TB == 0 and d == 1024
    cap_l = EC_FACTOR * n_tok // ne
    cap_c = EC_FACTOR * lc // ne

    assert TB % GRID_W == 0
    pos_r, pos_c = _pos_tables(n_tok // GRID_W, d)
    xa = None

    cc = jnp.zeros((8, d), F32).at[:bsz].set(c).at[bsz].set(c_ctx)
    mods = _ada(cc, w_ada, b_ada)

    wide = jnp.concatenate([w_in[:, :, 0:2048], w_in[:, :, 2064:3600]], axis=2).astype(BF16)
    narrow = jnp.concatenate([w_in[:, :, 2048:2064], w_in[:, :, 3600:3632]], axis=2)
    narrow = jnp.pad(narrow, ((0, 0), (0, 0), (0, LANES - narrow.shape[2])))
    narrow = jnp.concatenate(_split2(narrow), axis=2)
    bias_s = jnp.pad(b_ml_gates, ((0, 0), (0, LANES - b_ml_gates.shape[1])))
    w2e = jnp.zeros((depth, 2, LANES, 256), F32)
    w2e = w2e.at[:, 0, 16:32].set(w_gla_a2[:, 0]).at[:, 1, 32:48].set(w_gla_a2[:, 1])

    for l in range(depth):
        last = l == depth - 1
        ml = mods[l, :bsz].reshape(bsz, 6, d)
        mc_ = jnp.broadcast_to(mods[l, bsz].reshape(1, 6, d), (bsz, 6, d))
        mod = jnp.pad(jnp.stack([mc_, ml], axis=1), ((0, 0), (0, 0), (0, 2), (0, 0)))

        if l == 0:
            xa, pb, ps, qc = _in_proj(None, mod, g_mix_pre[l][None], wide[l], narrow[l], bias_s[l][None],
                                      conv_qk[l], ncb, first=(x, ctx, pos_r, pos_c))
        else:
            pb, ps, qc = _in_proj(xa, mod, g_mix_pre[l][None], wide[l], narrow[l], bias_s[l][None],
                                  conv_qk[l], ncb)
        hf, hb, of, ob = _mixers2(qc, pb, ps, w2e[l], b_gla_a[l][:, None, :], ncb)
        x1, h2e, aff = _out_proj(hf, hb, of, ob, pb, xa, mod, g_ml_norm[l][None], g_gla_norm[l][None],
                                 w_out[l].astype(BF16), g_mix_post[l][None], g_ffn_pre[l][None],
                                 w_router[l].T, ncb)

        rl = n_tok // LANES
        aff_l = aff[:, :, lc:].reshape(bsz, ne * rl, LANES)
        if last:
            sll, offl = _route(None, aff_l, 0, cap_l)
            slots = jnp.pad(sll.reshape(bsz, ne, n_tok), ((0, 0), (0, 0), (lc, 0)), constant_values=UNSEL)
            lo, hi = _tile_bounds(offl, rl, nblk - ncb, cap_l, 0)
            t0, nt, m_rows = ncb, nblk - ncb, cap_l
        else:
            rc = max(lc // LANES, 8)
            aff_c = aff[:, :, :lc].reshape(bsz, ne, lc // LANES, LANES)
            aff_c = jnp.pad(aff_c, ((0, 0), (0, 0), (0, rc - lc // LANES), (0, 0)), constant_values=-1.0)
            slc, offc, sll, offl = _route(aff_c.reshape(bsz, ne * rc, LANES), aff_l, cap_c, cap_l)
            slots = jnp.concatenate([slc.reshape(bsz, ne, rc * LANES)[:, :, :lc],
                                     sll.reshape(bsz, ne, n_tok)], axis=2)
            lo_l, hi_l = _tile_bounds(offl, rl, nblk - ncb, cap_l, 0)
            lo_c, hi_c = _tile_bounds(offc, rc, ncb, cap_c, cap_l)
            lo = jnp.concatenate([lo_c, lo_l], axis=2)
            hi = jnp.concatenate([hi_c, hi_l], axis=2)
            t0, nt, m_rows = 0, nblk, cap_l + cap_c
        lohi = jnp.concatenate([lo, hi], axis=1).transpose(0, 2, 1).astype(I32).reshape(-1)
        xin = _dispatch(lohi, h2e, slots, m_rows, t0, nt)
        y = _experts(xin, w_e_gate, w_e_up, w_e_down, l, cap_l)
        xa = _combine(lohi, slots, x1, mod, g_ffn_post[l][None], y, t0, nt, ncb)
    return xa
```

```python
import functools

import jax
import jax.numpy as jnp
from jax import lax
from jax.experimental import pallas as pl
from jax.experimental.pallas import tpu as pltpu

F32 = jnp.float32
BF16 = jnp.bfloat16
I32 = jnp.int32

EPS = 1e-6
GRID_W = 64
POS_BASE = 10000.0
N_HEADS = 4
ML_DH = 128
GLA_DK = 64
GLA_DV = 128
GLA_GATE_TAU = 16.0
N_EXPERTS = 16
EC_FACTOR = 2

LANES = 128
TB = 256
ML_L = 128
GLA_L = 128
ML_STAGE_UNITS = 2
EXPERT_ROWS = 256
WIN = 80
ROW_ALIGN = 16
UNSEL = 2047.0
PB_ML_V, PB_ML_O, PB_GLA_QK, PB_GLA_V, PB_GLA_R = 0, 1, 2, 3, 4
VMEM_LIMIT = 56 * 1024 * 1024


def _cparams(sem):
    return pltpu.CompilerParams(dimension_semantics=sem, vmem_limit_bytes=VMEM_LIMIT)


def _split2(a):
    hi = a.astype(BF16)
    lo = (a - hi.astype(F32)).astype(BF16)
    return hi, lo


def _split3(a):
    hi = a.astype(BF16)
    r = a - hi.astype(F32)
    mid = r.astype(BF16)
    lo = (r - mid.astype(F32)).astype(BF16)
    return hi, mid, lo


_NN = (((1,), (0,)), ((), ()))
_NT = (((1,), (1,)), ((), ()))
_TN = (((0,), (0,)), ((), ()))


def _mm(a, b, dims=_NN):
    return lax.dot_general(a, b, dims, preferred_element_type=F32)


def _dot3(a, b, dims=_NN):
    ah, al = _split2(a)
    bh, bl = _split2(b)
    return _mm(ah, bh, dims) + (_mm(ah, bl, dims) + _mm(al, bh, dims))


def _dot_exact_l(m_bf16, x, dims=_NN):
    hi, mid, lo = _split3(x)
    return _mm(m_bf16, hi, dims) + (_mm(m_bf16, mid, dims) + _mm(m_bf16, lo, dims))


def _rms(x, g):
    return x * lax.rsqrt(jnp.mean(x * x, axis=-1, keepdims=True) + EPS) * g


def _log_sigmoid(x):
    return jnp.minimum(x, 0.0) - jnp.log(1.0 + jnp.exp(-jnp.abs(x)))


def _sigmoid(x):
    return 1.0 / (1.0 + jnp.exp(-x))


def _silu(x):
    return x * _sigmoid(x)


def _iota(shape, dim):
    return lax.broadcasted_iota(I32, shape, dim)


def _rev_block(i, ncb, nblk):
    return jnp.where(i < ncb, ncb - 1 - i, nblk - 1 - (i - ncb))


def _ada_kernel(c_ref, w_ref, b_ref, o_ref):
    a = _silu(c_ref[...])
    o_ref[0] = _dot3(a, w_ref[0]) + b_ref[0]


def _ada(cc, w_ada, b_ada):
    depth, d, n6 = w_ada.shape
    tn = 1536
    return pl.pallas_call(
        _ada_kernel,
        grid=(depth, n6 // tn),
        in_specs=[pl.BlockSpec((8, d), lambda l, j: (0, 0)),
                  pl.BlockSpec((1, d, tn), lambda l, j: (l, 0, j)),
                  pl.BlockSpec((1, 1, tn), lambda l, j: (l, 0, j))],
        out_specs=pl.BlockSpec((1, 8, tn), lambda l, j: (l, 0, j)),
        out_shape=jax.ShapeDtypeStruct((depth, 8, n6), F32),
        compiler_params=_cparams(("parallel", "parallel")),
        name="ada",
    )(cc, w_ada, b_ada.reshape(depth, 1, n6))


def _project(ncb, xs, xps, xns, mod_ref, g_ref, wb_ref, ws_ref, bs_ref, cw_ref, pb_ref, ps_ref, qc_ref):
    i = pl.program_id(0)
    nblk = pl.num_programs(0)
    nb, n = len(xs), xs[0].shape[0]
    rows = [slice(b * n, (b + 1) * n) for b in range(nb)]

    def norm(z, b):
        return _rms(z, g_ref[0]) * (1.0 + mod_ref[0, b, 0, 1:2, :]) + mod_ref[0, b, 0, 0:1, :]

    parts = [_split2(norm(xs[b], b)) for b in range(nb)]
    hh = jnp.concatenate([p[0] for p in parts], axis=0)
    hl = jnp.concatenate([p[1] for p in parts], axis=0)
    halo = ([norm(xps[b], b).astype(BF16) for b in range(nb)] + [norm(xns[b], b).astype(BF16) for b in range(nb)])
    wq = cw_ref.shape[2]
    qk = _mm(jnp.concatenate([hh] + halo, axis=0), wb_ref[0, :, 0:wq])
    lvalid = jnp.logical_and(i != 0, i != ncb)
    rvalid = jnp.logical_and(i != ncb - 1, i != nblk - 1)
    dq = wq // 2
    for b in range(nb):
        lrow = nb * n + 8 * b + 7
        rrow = nb * n + 8 * nb + 8 * b
        left = jnp.where(lvalid, qk[lrow:lrow + 1, :], 0.0)
        right = jnp.where(rvalid, qk[rrow:rrow + 1, :], 0.0)
        y = _silu(_conv3(qk[rows[b], :], left, right, cw_ref[0]))
        qc_ref[b, :, 0:dq] = (y[:, 0:dq] * (ML_DH ** -0.5)).astype(BF16)
        qc_ref[b, :, dq:] = y[:, dq:].astype(BF16)
    rest = _mm(hh, wb_ref[0, :, wq:])
    pr = _mm(jnp.concatenate([hh, hl], axis=0), ws_ref[0])
    m = nb * n
    ps = (pr[:m, :LANES] + pr[:m, LANES:]) + (pr[m:, :LANES] + pr[m:, LANES:]) + bs_ref[0]
    lane = _iota(ps.shape, 1)
    forget = jnp.logical_and(lane % 8 >= N_HEADS, lane < 4 * N_HEADS)
    ps = jnp.where(forget, _log_sigmoid(ps), ps)
    for b in range(nb):
        pb_ref[b] = rest[rows[b], :]
        ps_ref[b] = ps[rows[b], :]


def _in_kernel(ncb, x_ref, xp_ref, xn_ref, mod_ref, g_ref, wb_ref, ws_ref, bs_ref, cw_ref,
               pb_ref, ps_ref, qc_ref):
    nb = x_ref.shape[0]
    _project(ncb, [x_ref[b] for b in range(nb)], [xp_ref[b] for b in range(nb)],
             [xn_ref[b] for b in range(nb)], mod_ref, g_ref, wb_ref, ws_ref, bs_ref, cw_ref,
             pb_ref, ps_ref, qc_ref)


def _in0_kernel(ncb, x_ref, xp_ref, xn_ref, c_ref, cp_ref, cn_ref, pr_ref, prp_ref, prn_ref, pc_ref,
                mod_ref, g_ref, wb_ref, ws_ref, bs_ref, cw_ref, xa_ref, pb_ref, ps_ref, qc_ref):
    i = pl.program_id(0)
    nb = x_ref.shape[0]
    half = pr_ref.shape[2]
    reps = TB // GRID_W
    prow = jnp.concatenate([jnp.broadcast_to(pr_ref[0, k:k + 1, :], (GRID_W, half)) for k in range(reps)], axis=0)
    pcol = jnp.concatenate([pc_ref[...]] * reps, axis=0)
    pos = jnp.concatenate([prow, pcol], axis=1)
    pos_p = jnp.concatenate([prp_ref[0, reps - 1:reps, :], pc_ref[GRID_W - 1:GRID_W, :]], axis=1)
    pos_n = jnp.concatenate([prn_ref[0, 0:1, :], pc_ref[0:1, :]], axis=1)
    is_ctx = i < ncb
    xs, xps, xns = [], [], []
    for b in range(nb):
        xa = jnp.where(is_ctx, c_ref[b], x_ref[b] + pos)
        xa_ref[b] = xa
        xs.append(xa)
        xps.append(jnp.where(is_ctx, cp_ref[b], xp_ref[b] + pos_p))
        xns.append(jnp.where(is_ctx, cn_ref[b], xn_ref[b] + pos_n))
    _project(ncb, xs, xps, xns, mod_ref, g_ref, wb_ref, ws_ref, bs_ref, cw_ref, pb_ref, ps_ref, qc_ref)


def _tile_and_halo_specs(b, rows, d, tile_of):
    r8 = TB // 8
    ntile, last8 = rows // TB, rows // 8 - 1
    tl = lambda i: jnp.clip(tile_of(i), 0, ntile - 1)
    return [pl.BlockSpec((b, TB, d), lambda i: (0, tl(i), 0)),
            pl.BlockSpec((b, 8, d), lambda i: (0, jnp.clip(tl(i) * r8 - 1, 0, last8), 0)),
            pl.BlockSpec((b, 8, d), lambda i: (0, jnp.clip((tl(i) + 1) * r8, 0, last8), 0))]


def _layer_spec(layer, arr):
    shp = arr.shape[1:]
    return pl.BlockSpec((1,) + shp, lambda *_: (layer,) + (0,) * len(shp))


def _mod_spec(layer, mod, ncb, tile_of=lambda i: i):
    _, b, _, r, d = mod.shape
    return pl.BlockSpec((1, b, 1, r, d), lambda i, *_: (layer, 0, jnp.where(tile_of(i) < ncb, 0, 1), 0, 0))


def _in_proj(xa, layer, mod, g, wb, ws, bs, cw, ncb, first=None):
    if first is None:
        b, t, d = xa.shape
    else:
        b, t, d = first[0].shape[0], first[0].shape[1] + first[1].shape[1], first[0].shape[2]
    wq = cw.shape[2]
    nb = wb.shape[2] - wq
    const = lambda shp: pl.BlockSpec(shp, lambda i: tuple(0 for _ in shp))
    tile = lambda w: pl.BlockSpec((b, TB, w), lambda i: (0, i, 0))
    common = [_mod_spec(layer, mod, ncb)] + [_layer_spec(layer, a) for a in (g, wb, ws, bs, cw)]
    out_specs = [tile(nb), tile(LANES), tile(wq)]
    out_shape = [jax.ShapeDtypeStruct((b, t, nb), F32),
                 jax.ShapeDtypeStruct((b, t, LANES), F32),
                 jax.ShapeDtypeStruct((b, t, wq), BF16)]
    if first is None:
        body = functools.partial(_in_kernel, ncb)
        in_specs = _tile_and_halo_specs(b, t, d, lambda i: i) + common
        args = (xa, xa, xa, mod, g, wb, ws, bs, cw)
    else:
        x, ctx, pos_r, pos_c = first
        reps = TB // GRID_W
        ntl = x.shape[1] // TB
        body = functools.partial(_in0_kernel, ncb)
        pr_spec = lambda off: pl.BlockSpec((1, reps, d // 2),
                                           lambda i: (jnp.clip(i - ncb + off, 0, ntl - 1), 0, 0))
        in_specs = (_tile_and_halo_specs(b, x.shape[1], d, lambda i: i - ncb)
                    + _tile_and_halo_specs(b, ctx.shape[1], d, lambda i: i)
                    + [pr_spec(0), pr_spec(-1), pr_spec(1), const((GRID_W, d // 2))] + common)
        out_specs = [tile(d)] + out_specs
        out_shape = [jax.ShapeDtypeStruct((b, t, d), F32)] + out_shape
        pr3 = pos_r.reshape(-1, reps, d // 2)
        args = (x, x, x, ctx, ctx, ctx, pr3, pr3, pr3, pos_c, mod, g, wb, ws, bs, cw)
    return pl.pallas_call(
        body,
        grid=(t // TB,),
        in_specs=in_specs,
        out_specs=out_specs,
        out_shape=out_shape,
        compiler_params=_cparams(("parallel",)),
        name="in_proj",
    )(*args)


def _conv3(x, hl, hr, w):
    rows = _iota(x.shape, 0)
    prev = jnp.where(rows == 0, hl, pltpu.roll(x, 1, axis=0))
    nxt = jnp.where(rows == x.shape[0] - 1, hr, pltpu.roll(x, x.shape[0] - 1, axis=0))
    return prev * w[0:1] + x * w[1:2] + nxt * w[2:3]


def _cummax_rows(x, reverse):
    n = x.shape[0]
    rows = _iota(x.shape, 0)
    s = 1
    while s < n:
        if reverse:
            sh = jnp.where(rows < n - s, pltpu.roll(x, n - s, axis=0), -jnp.inf)
        else:
            sh = jnp.where(rows >= s, pltpu.roll(x, s, axis=0), -jnp.inf)
        x = jnp.maximum(x, sh)
        s *= 2
    return x


def _ml_pair(fwd, bwd, c_s, m_s):
    ll = ML_L
    nch = TB // ll
    dq = N_HEADS * ML_DH
    rows = _iota((ll, ll), 0)
    cols = _iota((ll, ll), 1)
    causal = [cols <= rows, cols >= rows]
    tri = [jnp.where(m, 1.0, 0.0).astype(BF16) for m in causal]
    ones = jnp.ones((ll, ML_DH), BF16)
    qk = [fwd[0][0], bwd[0][0]]
    v = [fwd[1][0], bwd[1][0]]
    g = [fwd[2][0], bwd[2][0]]
    outs = [fwd[3], bwd[3]]
    units = [(d, h) for d in range(2) for h in range(N_HEADS)]
    cx = {u: c_s[u[0] * N_HEADS + u[1]] for u in units}
    m_row = [m_s[0], m_s[1]]
    for step in range(nch):
        r0 = [step * ll, (nch - 1 - step) * ll]
        alpha, a_in, em, e_w, ut, a_old, a_new = [], [], [], [], [], [], []
        for d in range(2):
            gc = g[d][r0[d]:r0[d] + ll, :]
            bc = _dot_exact_l(tri[d], gc)
            u = pltpu.roll(gc, 4, axis=1) - bc
            cm = _cummax_rows(u, bool(d))
            neg_alpha = jnp.maximum(m_row[d], cm)
            alpha.append(-neg_alpha)
            a_in.append(jnp.exp(m_row[d] - neg_alpha))
            em.append(jnp.exp(-neg_alpha - bc))
            last = slice(0, 1) if d else slice(ll - 1, ll)
            cm_end = cm[last, :]
            bend = bc[last, :]
            e_w.append(jnp.exp(u - cm_end))
            ut.append(u.T)
            m_kv = bend + cm_end
            m_new = jnp.maximum(bend + m_row[d], m_kv)
            a_old.append(jnp.exp(bend + m_row[d] - m_new))
            a_new.append(jnp.exp(m_kv - m_new))
            m_row[d] = m_new
        lane = lambda d, h: 8 * d + 4 + h
        for g0 in range(0, len(units), ML_STAGE_UNITS):
            grp = units[g0:g0 + ML_STAGE_UNITS]
            qb = {(d, h): qk[d][r0[d]:r0[d] + ll, h * ML_DH:(h + 1) * ML_DH] for d, h in grp}
            kb = {(d, h): qk[d][r0[d]:r0[d] + ll, dq + h * ML_DH:dq + (h + 1) * ML_DH] for d, h in grp}
            vh = {(d, h): v[d][r0[d]:r0[d] + ll, h * ML_DH:(h + 1) * ML_DH] for d, h in grp}
            sc = {u: _mm(qb[u], kb[u], _NT) for u in grp}
            inter = {u: _mm(qb[u], cx[u].astype(BF16), _NT) for u in grp}
            sb = {}
            for d, h in grp:
                c = lane(d, h)
                arg = jnp.where(causal[d], alpha[d][:, c:c + 1] + ut[d][c:c + 1, :], -jnp.inf)
                sb[(d, h)] = (sc[(d, h)] * jnp.exp(arg)).astype(BF16)
            ckv = {}
            for d, h in grp:
                c = lane(d, h)
                ew = e_w[d][:, c:c + 1]
                ev = jnp.concatenate([(ew * vh[(d, h)]).astype(BF16),
                                      jnp.broadcast_to(ew, (ll, ML_DH)).astype(BF16)], axis=1)
                ckv[(d, h)] = _mm(ev, kb[(d, h)], _TN)
            for d, h in grp:
                c = lane(d, h)
                nd = (_mm(sb[(d, h)], jnp.concatenate([vh[(d, h)].astype(BF16), ones], axis=1))
                      + a_in[d][:, c:c + 1] * inter[(d, h)])
                den = jnp.maximum(jnp.abs(nd[:, ML_DH:]), em[d][:, c:c + 1])
                outs[d][0, r0[d]:r0[d] + ll, h * ML_DH:(h + 1) * ML_DH] = (nd[:, :ML_DH] / den).astype(BF16)
            for d, h in grp:
                c = lane(d, h)
                cx[(d, h)] = a_old[d][:, c:c + 1] * cx[(d, h)] + a_new[d][:, c:c + 1] * ckv[(d, h)]
    for d, h in units:
        c_s[d * N_HEADS + h] = cx[(d, h)]
    m_s[0] = m_row[0]
    m_s[1] = m_row[1]


def _gla_pair(fwd, bwd, w2_ref, ba_ref, s_s):
    ll = GLA_L
    nch = TB // ll
    dkw = N_HEADS * GLA_DK
    rows = _iota((ll, ll), 0)
    cols = _iota((ll, ll), 1)
    causal = [cols <= rows, cols >= rows]
    tri = [jnp.where(m, 1.0, 0.0).astype(BF16) for m in causal]
    qk = [fwd[0][0], bwd[0][0]]
    v = [fwd[1][0], bwd[1][0]]
    outs = [fwd[3], bwd[3]]
    la = [_log_sigmoid(_dot3(r[2][0], w2_ref[0, d]) + ba_ref[0, d]) * (1.0 / GLA_GATE_TAU)
          for d, r in enumerate((fwd, bwd))]
    units = [(d, h) for d in range(2) for h in range(N_HEADS)]
    st = {u: s_s[u[0] * N_HEADS + u[1]] for u in units}
    for step in range(nch):
        r0 = [step * ll, (nch - 1 - step) * ll]
        qi, qt, kt, kd, e_end = [], [], [], [], []
        for d in range(2):
            bcum = _dot_exact_l(tri[d], la[d][r0[d]:r0[d] + ll, :])
            ref = bcum[ll // 2:ll // 2 + 1, :]
            bend = bcum[0:1, :] if d else bcum[ll - 1:ll, :]
            q = qk[d][r0[d]:r0[d] + ll, 0:dkw] * (GLA_DK ** -0.5)
            k = qk[d][r0[d]:r0[d] + ll, dkw:2 * dkw]
            qi.append((q * jnp.exp(bcum)).astype(BF16))
            qt.append((q * jnp.exp(bcum - ref)).astype(BF16))
            ktd = k * jnp.exp(ref - bcum)
            kt.append(ktd.astype(BF16))
            kd.append((ktd * jnp.exp(bend - ref)).astype(BF16))
            e_end.append(jnp.exp(bend))
        hs = lambda a, h, w: a[:, h * w:(h + 1) * w]
        vb = {(d, h): hs(v[d][r0[d]:r0[d] + ll, :], h, GLA_DV).astype(BF16) for d, h in units}
        att = {(d, h): _mm(hs(qt[d], h, GLA_DK), hs(kt[d], h, GLA_DK), _NT) for d, h in units}
        inter = {(d, h): _mm(hs(qi[d], h, GLA_DK), st[(d, h)].astype(BF16), _NT) for d, h in units}
        attb = {(d, h): jnp.where(causal[d], att[(d, h)], 0.0).astype(BF16) for d, h in units}
        kv = {(d, h): _mm(vb[(d, h)], hs(kd[d], h, GLA_DK), _TN) for d, h in units}
        for d, h in units:
            o = _mm(attb[(d, h)], vb[(d, h)]) + inter[(d, h)]
            outs[d][0, r0[d]:r0[d] + ll, h * GLA_DV:(h + 1) * GLA_DV] = o.astype(BF16)
        st = {(d, h): st[(d, h)] * hs(e_end[d], h, GLA_DK) + kv[(d, h)] for d, h in units}
    for d, h in units:
        s_s[d * N_HEADS + h] = st[(d, h)]


def _mix2_kernel(qcf, mvf, psf, gqf, gvf, qcb, mvb, psb, gqb, gvb, w2, ba,
                 mf_ref, mb_ref, gf_ref, gb_ref, c_s, m_s, s_s):
    @pl.when(pl.program_id(1) == 0)
    def _():
        c_s[...] = jnp.zeros_like(c_s)
        m_s[...] = jnp.zeros_like(m_s)
        s_s[...] = jnp.zeros_like(s_s)

    _ml_pair((qcf, mvf, psf, mf_ref), (qcb, mvb, psb, mb_ref), c_s, m_s)
    _gla_pair((gqf, gvf, psf, gf_ref), (gqb, gvb, psb, gb_ref), w2, ba, s_s)


def _mixers2(qc, pb, ps, layer, w2e, ba, ncb):
    b, t, _ = pb.shape
    nblk = t // TB
    fwd = lambda i: i
    bwd = lambda i: _rev_block(i, ncb, nblk)

    def dspecs(blk):
        col = lambda cb: pl.BlockSpec((1, TB, 512), lambda b_, i: (b_, blk(i), cb))
        return [pl.BlockSpec((1, TB, qc.shape[2]), lambda b_, i: (b_, blk(i), 0)), col(PB_ML_V),
                pl.BlockSpec((1, TB, LANES), lambda b_, i: (b_, blk(i), 0)), col(PB_GLA_QK), col(PB_GLA_V)]

    specs = dspecs(fwd) + dspecs(bwd) + [_layer_spec(layer, w2e), _layer_spec(layer, ba)]
    ns = 2 * N_HEADS
    ofwd = pl.BlockSpec((1, TB, 512), lambda b_, i: (b_, i, 0))
    obwd = pl.BlockSpec((1, TB, 512), lambda b_, i: (b_, bwd(i), 0))
    return pl.pallas_call(
        _mix2_kernel,
        grid=(b, nblk),
        in_specs=specs,
        out_specs=[ofwd, obwd, ofwd, obwd],
        out_shape=[jax.ShapeDtypeStruct((b, t, 512), BF16)] * 4,
        scratch_shapes=[pltpu.VMEM((ns, 2 * ML_DH, ML_DH), F32),
                        pltpu.VMEM((2, 1, LANES), F32),
                        pltpu.VMEM((ns, GLA_DV, GLA_DK), F32)],
        compiler_params=_cparams(("parallel", "arbitrary")),
        name="mixers",
    )(*([qc, pb, ps, pb, pb] * 2), w2e, ba)


def _head_norm(x, g):
    outs = []
    for h in range(N_HEADS):
        seg = x[:, h * 128:(h + 1) * 128]
        outs.append(seg * lax.rsqrt(jnp.mean(seg * seg, axis=-1, keepdims=True) + EPS))
    return jnp.concatenate(outs, axis=-1) * g


def _out_kernel(hf, hb, of, ob, og, rg, x_ref, mod_ref, gml, ggla, wo, gpost, gpre, wrt,
                x1_ref, h2e_ref, aff_ref):
    ne = N_EXPERTS
    nb, _, d = x_ref.shape
    rh = LANES
    groups = [(b, slice(k * rh, (k + 1) * rh)) for b in range(nb) for k in range(TB // rh)]
    f32 = lambda ref, b, r: ref[b, r, :].astype(F32)
    y = [jnp.concatenate([_head_norm(f32(hf, b, r) + f32(hb, b, r), gml[0]) * _sigmoid(og[b, r, :]),
                          _head_norm(f32(of, b, r) + f32(ob, b, r), ggla[0]) * _silu(rg[b, r, :])],
                         axis=-1).astype(BF16) for b, r in groups]
    y2 = _mm(jnp.concatenate(y, axis=0), wo[0])
    x1 = [x_ref[b, r, :] + mod_ref[0, b, 0, 2:3, :] * _rms(y2[g * rh:(g + 1) * rh, :], gpost[0])
          for g, (b, r) in enumerate(groups)]
    for (b, r), x1k in zip(groups, x1):
        x1_ref[b, r, :] = x1k
    h2 = [_rms(x1k, gpre[0]) * (1.0 + mod_ref[0, b, 0, 4:5, :]) + mod_ref[0, b, 0, 3:4, :]
          for (b, r), x1k in zip(groups, x1)]
    lt = _dot3(wrt[0], jnp.concatenate(h2, axis=0), _NT)
    for g, ((b, r), h2k) in enumerate(zip(groups, h2)):
        ltk = lt[:, g * rh:(g + 1) * rh]
        ext = jnp.exp(ltk - jnp.max(ltk, axis=0, keepdims=True))
        affk = ext / jnp.sum(ext, axis=0, keepdims=True)
        aff_ref[b, :, r] = affk
        afft = jnp.concatenate([affk, jnp.zeros((LANES - ne, rh), F32)], axis=0).T
        a_hi, a_mid, a_lo = _split3(afft)
        pieces = (a_hi.astype(F32) + pltpu.roll(a_mid.astype(F32), ne, axis=1)
                  + pltpu.roll(a_lo.astype(F32), 2 * ne, axis=1))
        h2e_ref[b, r, 0:d] = h2k.astype(BF16)
        h2e_ref[b, r, d:d + LANES] = pieces.astype(BF16)


def _out_proj(hf, hb, of, ob, pb, xa, layer, mod, gml, ggla, wo, gpost, gpre, wrt, ncb):
    b, t, d = xa.shape
    de = d + LANES
    tile = lambda w, cb: pl.BlockSpec((b, TB, w), lambda i: (0, i, cb))
    return pl.pallas_call(
        _out_kernel,
        grid=(t // TB,),
        in_specs=[tile(512, 0), tile(512, 0), tile(512, 0), tile(512, 0), tile(512, PB_ML_O), tile(512, PB_GLA_R),
                  tile(d, 0), _mod_spec(layer, mod, ncb)]
                 + [_layer_spec(layer, a) for a in (gml, ggla, wo, gpost, gpre, wrt)],
        out_specs=[tile(d, 0), tile(de, 0),
                   pl.BlockSpec((b, N_EXPERTS, TB), lambda i: (0, 0, i))],
        out_shape=[jax.ShapeDtypeStruct((b, t, d), F32),
                   jax.ShapeDtypeStruct((b, t, de), BF16),
                   jax.ShapeDtypeStruct((b, N_EXPERTS, t), F32)],
        compiler_params=_cparams(("parallel",)),
        name="out_proj",
    )(hf, hb, of, ob, pb, pb, xa, mod, gml, ggla, wo, gpost, gpre, wrt)


def _cumsum_blocks(x, r):
    n = x.shape[0]
    xb = x.astype(BF16)
    li = _iota((LANES, LANES), 0)
    lj = _iota((LANES, LANES), 1)
    upper = jnp.where(li <= lj, 1.0, 0.0).astype(BF16)
    ones = jnp.ones((LANES, LANES), BF16)
    inrow = _mm(xb, upper)
    tot = _mm(xb, ones)
    ri = _iota((n, n), 0)
    rj = _iota((n, n), 1)
    same = (ri // r) == (rj // r)
    strict = jnp.where(jnp.logical_and(same, rj < ri), 1.0, 0.0).astype(BF16)
    off = _mm(strict, tot.astype(BF16))
    return inrow + off, off


def _select(aff, r, cap, base_slot):
    ne = N_EXPERTS
    n = ne * r
    aff3 = aff.reshape(ne, r, LANES)
    capf = jnp.float32(cap)

    def body(k, prefix):
        cand = prefix | (jnp.int32(1) << (30 - k))
        candf = lax.bitcast_convert_type(cand, F32)
        cnt = jnp.sum(jnp.where(aff3 >= candf, 1.0, 0.0), axis=(1, 2), keepdims=True)
        return jnp.where(cnt >= capf, cand, prefix)

    thr = lax.bitcast_convert_type(lax.fori_loop(0, 31, body, jnp.zeros((ne, 1, 1), I32)), F32)
    gt = jnp.where(aff3 > thr, 1.0, 0.0)
    eq = jnp.where(aff3 == thr, 1.0, 0.0)
    need = capf - jnp.sum(gt, axis=(1, 2), keepdims=True)
    eq2 = eq.reshape(n, LANES)
    cs_eq, _ = _cumsum_blocks(eq2, r)
    eq_rank = (cs_eq - eq2).reshape(ne, r, LANES)
    sel = (gt + eq * jnp.where(eq_rank < need, 1.0, 0.0)).reshape(n, LANES)
    cs, off = _cumsum_blocks(sel, r)
    slot = jnp.where(sel > 0.5, cs - 1.0 + base_slot, UNSEL)
    return slot, off


def _sel_kernel(rc, rl, cap_c, cap_l, *refs):
    if rc:
        affc, affl, slc, offc, sll, offl = refs
        slc[0], offc[0] = _select(affc[0], rc, cap_c, float(cap_l))
    else:
        affl, sll, offl = refs
    sll[0], offl[0] = _select(affl[0], rl, cap_l, 0.0)


def _route(aff_c, aff_l, cap_c, cap_l):
    b = aff_l.shape[0]
    ne = N_EXPERTS
    rl = aff_l.shape[1] // ne
    rc = aff_c.shape[1] // ne if aff_c is not None else 0
    args = ([aff_c] if rc else []) + [aff_l]
    in_specs, out_shape, out_specs = [], [], []
    for a in args:
        spec = pl.BlockSpec((1,) + a.shape[1:], lambda bi: (bi, 0, 0))
        in_specs.append(spec)
        out_shape += [jax.ShapeDtypeStruct(a.shape, F32)] * 2
        out_specs += [spec, spec]
    return pl.pallas_call(
        functools.partial(_sel_kernel, rc, rl, cap_c, cap_l),
        grid=(b,),
        in_specs=in_specs,
        out_specs=out_specs,
        out_shape=out_shape,
        compiler_params=_cparams(("parallel",)),
        name="route",
    )(*args)


def _window(lo_ref, base, e, m_rows):
    lo_e = lo_ref[base + e]
    hi_e = lo_ref[base + N_EXPERTS + e]
    a_e = jnp.minimum((lo_e // ROW_ALIGN) * ROW_ALIGN, m_rows - WIN)
    return a_e, hi_e


def _disp_kernel(m_rows, ng, lo_ref, h_ref, slot_ref, x_ref):
    ne = N_EXPERTS
    gi = pl.program_id(1)
    i = pl.program_id(2)
    base = (pl.program_id(0) * pl.num_programs(2) + i) * (2 * ne)

    @pl.when(i == 0)
    def _():
        x_ref[...] = jnp.zeros_like(x_ref)

    h = h_ref[0]
    sl = slot_ref[0]
    sub = _iota((WIN, TB), 0).astype(F32)
    wins = [_window(lo_ref, base, gi * ng + k, m_rows) for k in range(ng)]
    rounds = jnp.int32(1)
    for a_e, hi_e in wins:
        rounds = jnp.maximum(rounds, (hi_e - a_e + WIN - 1) // WIN)

    def one_round(rd, carry):
        wts, starts = [], []
        for k, (a_e, _) in enumerate(wins):
            first = a_e + rd * WIN
            a_r = jnp.minimum(first, m_rows - WIN)
            srow = sl[k:k + 1, :]
            hit = jnp.logical_and(srow - a_r.astype(F32) == sub, srow >= first.astype(F32))
            wts.append(jnp.where(hit, 1.0, 0.0).astype(BF16))
            starts.append(a_r)
        g = _mm(jnp.concatenate(wts, axis=0), h).astype(BF16)
        for k, a_r in enumerate(starts):
            rows = pl.ds(pl.multiple_of(a_r, ROW_ALIGN), WIN)
            x_ref[0, k, rows, :] = x_ref[0, k, rows, :] + g[k * WIN:(k + 1) * WIN, :]
        return carry

    lax.fori_loop(0, rounds, one_round, 0)


def _dispatch(lohi, h2e, slots, m_rows, t0, nt):
    b, t, de = h2e.shape
    ne = N_EXPERTS
    ng = 8
    grid_spec = pltpu.PrefetchScalarGridSpec(
        num_scalar_prefetch=1,
        grid=(b, ne // ng, nt),
        in_specs=[pl.BlockSpec((1, TB, de), lambda bi, gi, i, *_: (bi, i + t0, 0)),
                  pl.BlockSpec((1, ng, TB), lambda bi, gi, i, *_: (bi * (ne // ng) + gi, 0, i + t0))],
        out_specs=pl.BlockSpec((1, ng, m_rows, de), lambda bi, gi, i, *_: (bi, gi, 0, 0)))
    return pl.pallas_call(
        functools.partial(_disp_kernel, m_rows, ng),
        grid_spec=grid_spec,
        out_shape=jax.ShapeDtypeStruct((b, ne, m_rows, de), BF16),
        compiler_params=_cparams(("parallel", "parallel", "arbitrary")),
        name="dispatch",
    )(lohi, h2e, slots.reshape(b * (ne // ng), ng, t))


def _row_chunks(m_rows, cap_l):
    step = min(EXPERT_ROWS, cap_l)
    starts = list(range(0, cap_l, step))
    return [(s, (m_rows - s) if s == starts[-1] else step) for s in starts]


def _exp_kernel(m_rows, cap_l, x_ref, wg_ref, wu_ref, wd_ref, y_ref, wg_s, wu_s, wd_s):
    ei = pl.program_id(0)
    d = wg_ref.shape[2]

    @pl.when(pl.program_id(1) == 0)
    def _():
        wg_s[...] = wg_ref[0, 0].astype(BF16)
        wu_s[...] = wu_ref[0, 0].astype(BF16)
        wd_s[...] = wd_ref[0, 0].astype(BF16)

    for r0, mc in _row_chunks(m_rows, cap_l):
        xs = x_ref[0, 0, r0:r0 + mc, 0:d]
        hid = _silu(_mm(xs, wg_s[...])) * _mm(xs, wu_s[...])
        y = _mm(hid.astype(BF16), wd_s[...])
        pieces = x_ref[0, 0, r0:r0 + mc, d:d + LANES].astype(F32)
        lane = _iota((mc, LANES), 1)
        mine = jnp.logical_and(lane % N_EXPERTS == ei, lane < 3 * N_EXPERTS)
        gate = jnp.sum(jnp.where(mine, pieces, 0.0), axis=1, keepdims=True)
        y_ref[0, 0, r0:r0 + mc, :] = (y * gate).astype(BF16)


def _experts(xin, wg, wu, wd, layer, cap_l):
    b, ne, m_rows, de = xin.shape
    _, _, d, f = wg.shape
    return pl.pallas_call(
        functools.partial(_exp_kernel, m_rows, cap_l),
        grid=(ne, b),
        in_specs=[pl.BlockSpec((1, 1, m_rows, de), lambda ei, bi: (bi, ei, 0, 0)),
                  pl.BlockSpec((1, 1, d, f), lambda ei, bi: (layer, ei, 0, 0)),
                  pl.BlockSpec((1, 1, d, f), lambda ei, bi: (layer, ei, 0, 0)),
                  pl.BlockSpec((1, 1, f, d), lambda ei, bi: (layer, ei, 0, 0))],
        out_specs=pl.BlockSpec((1, 1, m_rows, d), lambda ei, bi: (bi, ei, 0, 0)),
        out_shape=jax.ShapeDtypeStruct((b, ne, m_rows, d), BF16),
        scratch_shapes=[pltpu.VMEM((d, f), BF16), pltpu.VMEM((d, f), BF16), pltpu.VMEM((f, d), BF16)],
        compiler_params=_cparams(("arbitrary", "arbitrary")),
        name="experts",
    )(xin, wg, wu, wd)


def _comb_kernel(m_rows, lo_ref, slot_ref, x1_ref, mod_ref, gpost, y_ref, out_ref):
    ne = N_EXPERTS
    i = pl.program_id(1)
    base = (pl.program_id(0) * pl.num_programs(1) + i) * (2 * ne)
    kk = ne * WIN
    sl = slot_ref[0]
    hi = jnp.floor(sl * (1.0 / 32.0))
    lo = sl - hi * 32.0
    col_e = _iota((ne, kk), 1) // WIN
    expand = jnp.where(col_e == _iota((ne, kk), 0), 1.0, 0.0).astype(BF16)
    sx = _mm(hi.astype(BF16), expand, _TN) * 32.0 + _mm(lo.astype(BF16), expand, _TN)
    col = _iota((1, kk), 1)
    jrow = (col % WIN).astype(F32)
    wins = [_window(lo_ref, base, e, m_rows) for e in range(ne)]
    rounds = jnp.int32(1)
    for a_e, hi_e in wins:
        rounds = jnp.maximum(rounds, (hi_e - a_e + WIN - 1) // WIN)

    def one_round(rd, acc):
        ys = []
        arow = jnp.zeros((1, kk), F32)
        frow = jnp.zeros((1, kk), F32)
        for e, (a_e, _) in enumerate(wins):
            first = a_e + rd * WIN
            a_r = jnp.minimum(first, m_rows - WIN)
            mine = col // WIN == e
            arow = jnp.where(mine, a_r.astype(F32), arow)
            frow = jnp.where(mine, first.astype(F32), frow)
            ys.append(y_ref[0, e, pl.ds(pl.multiple_of(a_r, ROW_ALIGN), WIN), :])
        hit = jnp.logical_and(sx - arow == jrow, sx >= frow)
        w = jnp.where(hit, 1.0, 0.0).astype(BF16)
        return acc + _mm(w, jnp.concatenate(ys, axis=0))

    ffn = lax.fori_loop(0, rounds, one_round, jnp.zeros(out_ref.shape[1:], F32))
    out_ref[0] = x1_ref[0] + mod_ref[0, 0, 0, 5:6, :] * _rms(ffn, gpost[0])


def _combine(lohi, slots, x1, layer, mod, gpost, y, t0, nt, ncb):
    b, t, d = x1.shape
    ne = N_EXPERTS
    m_rows = y.shape[2]
    r8 = mod.shape[3]
    grid_spec = pltpu.PrefetchScalarGridSpec(
        num_scalar_prefetch=1,
        grid=(b, nt),
        in_specs=[pl.BlockSpec((1, ne, TB), lambda bi, i, *_: (bi, 0, i + t0)),
                  pl.BlockSpec((1, TB, d), lambda bi, i, *_: (bi, i + t0, 0)),
                  pl.BlockSpec((1, 1, 1, r8, d),
                               lambda bi, i, *_: (layer, bi, jnp.where(i + t0 < ncb, 0, 1), 0, 0)),
                  _layer_spec(layer, gpost),
                  pl.BlockSpec((1, ne, m_rows, d), lambda bi, i, *_: (bi, 0, 0, 0),
                               pipeline_mode=pl.Buffered(1))],
        out_specs=pl.BlockSpec((1, TB, d), lambda bi, i, *_: (bi, i, 0)))
    return pl.pallas_call(
        functools.partial(_comb_kernel, m_rows),
        grid_spec=grid_spec,
        out_shape=jax.ShapeDtypeStruct((b, nt * TB, d), F32),
        compiler_params=_cparams(("parallel", "arbitrary")),
        name="combine",
    )(lohi, slots, x1, mod, gpost, y)


def _pos_tables(rows, d):
    quarter = d // 4
    freq = jnp.power(POS_BASE, -jnp.arange(quarter, dtype=F32) / quarter)
    ar = jnp.arange(rows, dtype=F32)[:, None] * freq
    ac = jnp.arange(GRID_W, dtype=F32)[:, None] * freq
    return (jnp.concatenate([jnp.sin(ar), jnp.cos(ar)], axis=-1),
            jnp.concatenate([jnp.sin(ac), jnp.cos(ac)], axis=-1))


def _tile_bounds(off, r, ntile, cap, base):
    b = off.shape[0]
    o = off.reshape(b, N_EXPERTS, r, LANES)[:, :, :, 0]
    lo = o[:, :, ::TB // LANES][:, :, :ntile] + base
    hi = jnp.concatenate([lo[:, :, 1:], jnp.full((b, N_EXPERTS, 1), cap + base, F32)], axis=2)
    return lo, hi


def kernel(x, c, ctx, c_ctx, w_ada, b_ada, g_mix_pre, g_mix_post, g_ffn_pre, g_ffn_post,
           w_in, conv_qk, b_ml_gates, w_gla_a2, b_gla_a, g_ml_norm, g_gla_norm, w_out,
           w_router, w_e_gate, w_e_up, w_e_down):
    bsz, n_tok, d = x.shape
    lc = ctx.shape[1]
    depth = w_in.shape[0]
    ne = N_EXPERTS
    t = lc + n_tok
    ncb = lc // TB
    nblk = t // TB
    assert lc % TB == 0 and n_tok % TB == 0 and d == 1024
    cap_l = EC_FACTOR * n_tok // ne
    cap_c = EC_FACTOR * lc // ne

    assert TB % GRID_W == 0
    pos_r, pos_c = _pos_tables(n_tok // GRID_W, d)
    xa = None

    cc = jnp.zeros((8, d), F32).at[:bsz].set(c).at[bsz].set(c_ctx)
    mods = _ada(cc, w_ada, b_ada)

    wide = jnp.concatenate([w_in[:, :, 0:2048], w_in[:, :, 2064:3600]], axis=2).astype(BF16)
    narrow = jnp.concatenate([w_in[:, :, 2048:2064], w_in[:, :, 3600:3632]], axis=2)
    narrow = jnp.pad(narrow, ((0, 0), (0, 0), (0, LANES - narrow.shape[2])))
    narrow = jnp.concatenate(_split2(narrow), axis=2)
    bias_s = jnp.pad(b_ml_gates, ((0, 0), (0, LANES - b_ml_gates.shape[1])))
    w2e = jnp.zeros((depth, 2, LANES, 256), F32)
    w2e = w2e.at[:, 0, 16:32].set(w_gla_a2[:, 0]).at[:, 1, 32:48].set(w_gla_a2[:, 1])
    m_lat = mods[:, :bsz].reshape(depth, bsz, 1, 6, d)
    m_ctx = jnp.broadcast_to(mods[:, bsz].reshape(depth, 1, 1, 6, d), (depth, bsz, 1, 6, d))
    mod = jnp.pad(jnp.concatenate([m_ctx, m_lat], axis=2), ((0, 0), (0, 0), (0, 0), (0, 2), (0, 0)))
    row = lambda a: a.reshape(depth, 1, -1)
    bias_s, ba = row(bias_s), b_gla_a.reshape(depth, 2, 1, -1)
    g_pre1, g_post1, g_pre2, g_post2 = row(g_mix_pre), row(g_mix_post), row(g_ffn_pre), row(g_ffn_post)
    g_ml, g_gla = row(g_ml_norm), row(g_gla_norm)
    wo = w_out.astype(BF16)
    wrt = w_router.transpose(0, 2, 1)

    for l in range(depth):
        last = l == depth - 1
        if l == 0:
            xa, pb, ps, qc = _in_proj(None, l, mod, g_pre1, wide, narrow, bias_s, conv_qk, ncb,
                                      first=(x, ctx, pos_r, pos_c))
        else:
            pb, ps, qc = _in_proj(xa, l, mod, g_pre1, wide, narrow, bias_s, conv_qk, ncb)
        hf, hb, of, ob = _mixers2(qc, pb, ps, l, w2e, ba, ncb)
        x1, h2e, aff = _out_proj(hf, hb, of, ob, pb, xa, l, mod, g_ml, g_gla, wo, g_post1, g_pre2, wrt, ncb)

        rl = n_tok // LANES
        aff_l = aff[:, :, lc:].reshape(bsz, ne * rl, LANES)
        if last:
            sll, offl = _route(None, aff_l, 0, cap_l)
            slots = jnp.pad(sll.reshape(bsz, ne, n_tok), ((0, 0), (0, 0), (lc, 0)), constant_values=UNSEL)
            lo, hi = _tile_bounds(offl, rl, nblk - ncb, cap_l, 0)
            t0, nt, m_rows = ncb, nblk - ncb, cap_l
        else:
            rc = max(lc // LANES, 8)
            aff_c = aff[:, :, :lc].reshape(bsz, ne, lc // LANES, LANES)
            aff_c = jnp.pad(aff_c, ((0, 0), (0, 0), (0, rc - lc // LANES), (0, 0)), constant_values=-1.0)
            slc, offc, sll, offl = _route(aff_c.reshape(bsz, ne * rc, LANES), aff_l, cap_c, cap_l)
            slots = jnp.concatenate([slc.reshape(bsz, ne, rc * LANES)[:, :, :lc],
                                     sll.reshape(bsz, ne, n_tok)], axis=2)
            lo_l, hi_l = _tile_bounds(offl, rl, nblk - ncb, cap_l, 0)
            lo_c, hi_c = _tile_bounds(offc, rc, ncb, cap_c, cap_l)
            lo = jnp.concatenate([lo_c, lo_l], axis=2)
            hi = jnp.concatenate([hi_c, hi_l], axis=2)
            t0, nt, m_rows = 0, nblk, cap_l + cap_c
        lohi = jnp.concatenate([lo, hi], axis=1).transpose(0, 2, 1).astype(I32).reshape(-1)
        xin = _dispatch(lohi, h2e, slots, m_rows, t0, nt)
        y = _experts(xin, w_e_gate, w_e_up, w_e_down, l, cap_l)
        xa = _combine(lohi, slots, x1, l, mod, g_post2, y, t0, nt, ncb)
    return xa
```

```python
import functools

import jax
import jax.numpy as jnp
from jax import lax
from jax.experimental import pallas as pl
from jax.experimental.pallas import tpu as pltpu

F32 = jnp.float32
BF16 = jnp.bfloat16
I32 = jnp.int32

EPS = 1e-6
GRID_W = 64
POS_BASE = 10000.0
N_HEADS = 4
ML_DH = 128
GLA_DK = 64
GLA_DV = 128
GLA_GATE_TAU = 16.0
N_EXPERTS = 16
EC_FACTOR = 2

LANES = 128
TB = 256
ML_L = 128
GLA_L = 128
ML_STAGE_UNITS = 2
EXPERT_ROWS = 256
WIN = 80
ROW_ALIGN = 16
UNSEL = 2047.0
PB_ML_V, PB_ML_O, PB_GLA_QK, PB_GLA_V, PB_GLA_R = 0, 1, 2, 3, 4
VMEM_LIMIT = 56 * 1024 * 1024


def _cparams(sem):
    return pltpu.CompilerParams(dimension_semantics=sem, vmem_limit_bytes=VMEM_LIMIT)


def _split2(a):
    hi = a.astype(BF16)
    lo = (a - hi.astype(F32)).astype(BF16)
    return hi, lo


def _split3(a):
    hi = a.astype(BF16)
    r = a - hi.astype(F32)
    mid = r.astype(BF16)
    lo = (r - mid.astype(F32)).astype(BF16)
    return hi, mid, lo


_NN = (((1,), (0,)), ((), ()))
_NT = (((1,), (1,)), ((), ()))
_TN = (((0,), (0,)), ((), ()))


def _mm(a, b, dims=_NN):
    return lax.dot_general(a, b, dims, preferred_element_type=F32)


def _dot3(a, b, dims=_NN):
    ah, al = _split2(a)
    bh, bl = _split2(b)
    return _mm(ah, bh, dims) + (_mm(ah, bl, dims) + _mm(al, bh, dims))


def _dot_exact_l(m_bf16, x, dims=_NN):
    hi, mid, lo = _split3(x)
    return _mm(m_bf16, hi, dims) + (_mm(m_bf16, mid, dims) + _mm(m_bf16, lo, dims))


def _rms(x, g):
    return x * lax.rsqrt(jnp.mean(x * x, axis=-1, keepdims=True) + EPS) * g


def _log_sigmoid(x):
    return jnp.minimum(x, 0.0) - jnp.log(1.0 + jnp.exp(-jnp.abs(x)))


def _sigmoid(x):
    return 1.0 / (1.0 + jnp.exp(-x))


def _silu(x):
    return x * _sigmoid(x)


def _iota(shape, dim):
    return lax.broadcasted_iota(I32, shape, dim)


def _rev_block(i, ncb, nblk):
    return jnp.where(i < ncb, ncb - 1 - i, nblk - 1 - (i - ncb))


def _ada_kernel(c_ref, w_ref, b_ref, o_ref):
    a = _silu(c_ref[...])
    o_ref[0] = _dot3(a, w_ref[0]) + b_ref[0]


def _ada(cc, w_ada, b_ada):
    depth, d, n6 = w_ada.shape
    tn = 1536
    return pl.pallas_call(
        _ada_kernel,
        grid=(depth, n6 // tn),
        in_specs=[pl.BlockSpec((8, d), lambda l, j: (0, 0)),
                  pl.BlockSpec((1, d, tn), lambda l, j: (l, 0, j)),
                  pl.BlockSpec((1, 1, tn), lambda l, j: (l, 0, j))],
        out_specs=pl.BlockSpec((1, 8, tn), lambda l, j: (l, 0, j)),
        out_shape=jax.ShapeDtypeStruct((depth, 8, n6), F32),
        compiler_params=_cparams(("parallel", "parallel")),
        name="ada",
    )(cc, w_ada, b_ada.reshape(depth, 1, n6))


def _project(ncb, xs, xps, xns, mod_ref, g_ref, wb_ref, ws_ref, bs_ref, cw_ref, pb_ref, ps_ref, qc_ref):
    i = pl.program_id(0)
    nblk = pl.num_programs(0)
    nb, n = len(xs), xs[0].shape[0]
    rows = [slice(b * n, (b + 1) * n) for b in range(nb)]

    gain = [g_ref[0] * (1.0 + mod_ref[0, b, 0, 1:2, :]) for b in range(nb)]

    def norm(z, b):
        return _rms(z, gain[b]) + mod_ref[0, b, 0, 0:1, :]

    parts = [_split2(norm(xs[b], b)) for b in range(nb)]
    hh = jnp.concatenate([p[0] for p in parts], axis=0)
    hl = jnp.concatenate([p[1] for p in parts], axis=0)
    halo = ([norm(xps[b], b).astype(BF16) for b in range(nb)] + [norm(xns[b], b).astype(BF16) for b in range(nb)])
    wq = cw_ref.shape[2]
    qk = _mm(jnp.concatenate([hh] + halo, axis=0), wb_ref[0, :, 0:wq])
    lvalid = jnp.logical_and(i != 0, i != ncb)
    rvalid = jnp.logical_and(i != ncb - 1, i != nblk - 1)
    dq = wq // 2
    for b in range(nb):
        lrow = nb * n + 8 * b + 7
        rrow = nb * n + 8 * nb + 8 * b
        left = jnp.where(lvalid, qk[lrow:lrow + 1, :], 0.0)
        right = jnp.where(rvalid, qk[rrow:rrow + 1, :], 0.0)
        y = _silu(_conv3(qk[rows[b], :], left, right, cw_ref[0]))
        qc_ref[b, :, 0:dq] = (y[:, 0:dq] * (ML_DH ** -0.5)).astype(BF16)
        qc_ref[b, :, dq:] = y[:, dq:].astype(BF16)
    rest = _mm(hh, wb_ref[0, :, wq:])
    pr = _mm(jnp.concatenate([hh, hl], axis=0), ws_ref[0])
    m = nb * n
    ps = (pr[:m, :LANES] + pr[:m, LANES:]) + (pr[m:, :LANES] + pr[m:, LANES:]) + bs_ref[0]
    lane = _iota(ps.shape, 1)
    forget = jnp.logical_and(lane % 8 >= N_HEADS, lane < 4 * N_HEADS)
    ps = jnp.where(forget, _log_sigmoid(ps), ps)
    for b in range(nb):
        pb_ref[b] = rest[rows[b], :]
        ps_ref[b] = ps[rows[b], :]


def _in_kernel(ncb, x_ref, xp_ref, xn_ref, mod_ref, g_ref, wb_ref, ws_ref, bs_ref, cw_ref,
               pb_ref, ps_ref, qc_ref):
    nb = x_ref.shape[0]
    _project(ncb, [x_ref[b] for b in range(nb)], [xp_ref[b] for b in range(nb)],
             [xn_ref[b] for b in range(nb)], mod_ref, g_ref, wb_ref, ws_ref, bs_ref, cw_ref,
             pb_ref, ps_ref, qc_ref)


def _in0_kernel(ncb, x_ref, xp_ref, xn_ref, c_ref, cp_ref, cn_ref, pr_ref, prp_ref, prn_ref, pc_ref,
                mod_ref, g_ref, wb_ref, ws_ref, bs_ref, cw_ref, xa_ref, pb_ref, ps_ref, qc_ref):
    i = pl.program_id(0)
    nb = x_ref.shape[0]
    half = pr_ref.shape[2]
    reps = TB // GRID_W
    prow = jnp.concatenate([jnp.broadcast_to(pr_ref[0, k:k + 1, :], (GRID_W, half)) for k in range(reps)], axis=0)
    pcol = jnp.concatenate([pc_ref[...]] * reps, axis=0)
    pos = jnp.concatenate([prow, pcol], axis=1)
    pos_p = jnp.concatenate([prp_ref[0, reps - 1:reps, :], pc_ref[GRID_W - 1:GRID_W, :]], axis=1)
    pos_n = jnp.concatenate([prn_ref[0, 0:1, :], pc_ref[0:1, :]], axis=1)
    is_ctx = i < ncb
    xs, xps, xns = [], [], []
    for b in range(nb):
        xa = jnp.where(is_ctx, c_ref[b], x_ref[b] + pos)
        xa_ref[b] = xa
        xs.append(xa)
        xps.append(jnp.where(is_ctx, cp_ref[b], xp_ref[b] + pos_p))
        xns.append(jnp.where(is_ctx, cn_ref[b], xn_ref[b] + pos_n))
    _project(ncb, xs, xps, xns, mod_ref, g_ref, wb_ref, ws_ref, bs_ref, cw_ref, pb_ref, ps_ref, qc_ref)


def _tile_and_halo_specs(b, rows, d, tile_of):
    r8 = TB // 8
    ntile, last8 = rows // TB, rows // 8 - 1
    tl = lambda i: jnp.clip(tile_of(i), 0, ntile - 1)
    return [pl.BlockSpec((b, TB, d), lambda i: (0, tl(i), 0)),
            pl.BlockSpec((b, 8, d), lambda i: (0, jnp.clip(tl(i) * r8 - 1, 0, last8), 0)),
            pl.BlockSpec((b, 8, d), lambda i: (0, jnp.clip((tl(i) + 1) * r8, 0, last8), 0))]


def _layer_spec(layer, arr):
    shp = arr.shape[1:]
    return pl.BlockSpec((1,) + shp, lambda *_: (layer,) + (0,) * len(shp))


def _mod_spec(layer, mod, ncb, tile_of=lambda i: i):
    _, b, _, r, d = mod.shape
    return pl.BlockSpec((1, b, 1, r, d), lambda i, *_: (layer, 0, jnp.where(tile_of(i) < ncb, 0, 1), 0, 0))


def _in_proj(xa, layer, mod, g, wb, ws, bs, cw, ncb, first=None):
    if first is None:
        b, t, d = xa.shape
    else:
        b, t, d = first[0].shape[0], first[0].shape[1] + first[1].shape[1], first[0].shape[2]
    wq = cw.shape[2]
    nb = wb.shape[2] - wq
    const = lambda shp: pl.BlockSpec(shp, lambda i: tuple(0 for _ in shp))
    tile = lambda w: pl.BlockSpec((b, TB, w), lambda i: (0, i, 0))
    common = [_mod_spec(layer, mod, ncb)] + [_layer_spec(layer, a) for a in (g, wb, ws, bs, cw)]
    out_specs = [tile(nb), tile(LANES), tile(wq)]
    out_shape = [jax.ShapeDtypeStruct((b, t, nb), F32),
                 jax.ShapeDtypeStruct((b, t, LANES), F32),
                 jax.ShapeDtypeStruct((b, t, wq), BF16)]
    if first is None:
        body = functools.partial(_in_kernel, ncb)
        in_specs = _tile_and_halo_specs(b, t, d, lambda i: i) + common
        args = (xa, xa, xa, mod, g, wb, ws, bs, cw)
    else:
        x, ctx, pos_r, pos_c = first
        reps = TB // GRID_W
        ntl = x.shape[1] // TB
        body = functools.partial(_in0_kernel, ncb)
        pr_spec = lambda off: pl.BlockSpec((1, reps, d // 2),
                                           lambda i: (jnp.clip(i - ncb + off, 0, ntl - 1), 0, 0))
        in_specs = (_tile_and_halo_specs(b, x.shape[1], d, lambda i: i - ncb)
                    + _tile_and_halo_specs(b, ctx.shape[1], d, lambda i: i)
                    + [pr_spec(0), pr_spec(-1), pr_spec(1), const((GRID_W, d // 2))] + common)
        out_specs = [tile(d)] + out_specs
        out_shape = [jax.ShapeDtypeStruct((b, t, d), F32)] + out_shape
        pr3 = pos_r.reshape(-1, reps, d // 2)
        args = (x, x, x, ctx, ctx, ctx, pr3, pr3, pr3, pos_c, mod, g, wb, ws, bs, cw)
    return pl.pallas_call(
        body,
        grid=(t // TB,),
        in_specs=in_specs,
        out_specs=out_specs,
        out_shape=out_shape,
        compiler_params=_cparams(("parallel",)),
        name="in_proj",
    )(*args)


def _conv3(x, hl, hr, w):
    rows = _iota(x.shape, 0)
    prev = jnp.where(rows == 0, hl, pltpu.roll(x, 1, axis=0))
    nxt = jnp.where(rows == x.shape[0] - 1, hr, pltpu.roll(x, x.shape[0] - 1, axis=0))
    return prev * w[0:1] + x * w[1:2] + nxt * w[2:3]


def _cummax_rows(x, reverse):
    n = x.shape[0]
    rows = _iota(x.shape, 0)
    s = 1
    while s < n:
        if reverse:
            sh = jnp.where(rows < n - s, pltpu.roll(x, n - s, axis=0), -jnp.inf)
        else:
            sh = jnp.where(rows >= s, pltpu.roll(x, s, axis=0), -jnp.inf)
        x = jnp.maximum(x, sh)
        s *= 2
    return x


def _ml_pair(fwd, bwd, c_s, m_s):
    ll = ML_L
    nch = TB // ll
    dq = N_HEADS * ML_DH
    rows = _iota((ll, ll), 0)
    cols = _iota((ll, ll), 1)
    causal = [cols <= rows, cols >= rows]
    tri = [jnp.where(m, 1.0, 0.0).astype(BF16) for m in causal]
    ones = jnp.ones((ll, ML_DH), BF16)
    qk = [fwd[0][0], bwd[0][0]]
    v = [fwd[1][0], bwd[1][0]]
    g = [fwd[2][0], bwd[2][0]]
    outs = [fwd[3], bwd[3]]
    units = [(d, h) for d in range(2) for h in range(N_HEADS)]
    cx = {u: c_s[u[0] * N_HEADS + u[1]] for u in units}
    m_row = [m_s[0], m_s[1]]
    for step in range(nch):
        r0 = [step * ll, (nch - 1 - step) * ll]
        alpha, a_in, em, e_w, ut, a_old, a_new = [], [], [], [], [], [], []
        for d in range(2):
            gc = g[d][r0[d]:r0[d] + ll, :]
            bc = _dot_exact_l(tri[d], gc)
            u = pltpu.roll(gc, 4, axis=1) - bc
            cm = _cummax_rows(u, bool(d))
            neg_alpha = jnp.maximum(m_row[d], cm)
            alpha.append(-neg_alpha)
            a_in.append(jnp.exp(m_row[d] - neg_alpha))
            em.append(jnp.exp(-neg_alpha - bc))
            last = slice(0, 1) if d else slice(ll - 1, ll)
            cm_end = cm[last, :]
            bend = bc[last, :]
            e_w.append(jnp.exp(u - cm_end))
            ut.append(u.T)
            m_kv = bend + cm_end
            m_new = jnp.maximum(bend + m_row[d], m_kv)
            a_old.append(jnp.exp(bend + m_row[d] - m_new))
            a_new.append(jnp.exp(m_kv - m_new))
            m_row[d] = m_new
        lane = lambda d, h: 8 * d + 4 + h
        for g0 in range(0, len(units), ML_STAGE_UNITS):
            grp = units[g0:g0 + ML_STAGE_UNITS]
            qb = {(d, h): qk[d][r0[d]:r0[d] + ll, h * ML_DH:(h + 1) * ML_DH] for d, h in grp}
            kb = {(d, h): qk[d][r0[d]:r0[d] + ll, dq + h * ML_DH:dq + (h + 1) * ML_DH] for d, h in grp}
            vh = {(d, h): v[d][r0[d]:r0[d] + ll, h * ML_DH:(h + 1) * ML_DH] for d, h in grp}
            sc = {u: _mm(qb[u], kb[u], _NT) for u in grp}
            inter = {u: _mm(qb[u], cx[u].astype(BF16), _NT) for u in grp}
            sb = {}
            for d, h in grp:
                c = lane(d, h)
                arg = jnp.where(causal[d], alpha[d][:, c:c + 1] + ut[d][c:c + 1, :], -jnp.inf)
                sb[(d, h)] = (sc[(d, h)] * jnp.exp(arg)).astype(BF16)
            ckv = {}
            for d, h in grp:
                c = lane(d, h)
                ew = e_w[d][:, c:c + 1]
                ev = jnp.concatenate([(ew * vh[(d, h)]).astype(BF16),
                                      jnp.broadcast_to(ew, (ll, ML_DH)).astype(BF16)], axis=1)
                ckv[(d, h)] = _mm(ev, kb[(d, h)], _TN)
            for d, h in grp:
                c = lane(d, h)
                nd = (_mm(sb[(d, h)], jnp.concatenate([vh[(d, h)].astype(BF16), ones], axis=1))
                      + a_in[d][:, c:c + 1] * inter[(d, h)])
                den = jnp.maximum(jnp.abs(nd[:, ML_DH:]), em[d][:, c:c + 1])
                outs[d][0, r0[d]:r0[d] + ll, h * ML_DH:(h + 1) * ML_DH] = (nd[:, :ML_DH] / den).astype(BF16)
            for d, h in grp:
                c = lane(d, h)
                cx[(d, h)] = a_old[d][:, c:c + 1] * cx[(d, h)] + a_new[d][:, c:c + 1] * ckv[(d, h)]
    for d, h in units:
        c_s[d * N_HEADS + h] = cx[(d, h)]
    m_s[0] = m_row[0]
    m_s[1] = m_row[1]


def _gla_pair(fwd, bwd, w2_ref, ba_ref, s_s):
    ll = GLA_L
    nch = TB // ll
    dkw = N_HEADS * GLA_DK
    rows = _iota((ll, ll), 0)
    cols = _iota((ll, ll), 1)
    causal = [cols <= rows, cols >= rows]
    tri = [jnp.where(m, 1.0, 0.0).astype(BF16) for m in causal]
    qk = [fwd[0][0], bwd[0][0]]
    v = [fwd[1][0], bwd[1][0]]
    outs = [fwd[3], bwd[3]]
    la = [_log_sigmoid(_dot3(r[2][0], w2_ref[0, d]) + ba_ref[0, d]) * (1.0 / GLA_GATE_TAU)
          for d, r in enumerate((fwd, bwd))]
    units = [(d, h) for d in range(2) for h in range(N_HEADS)]
    st = {u: s_s[u[0] * N_HEADS + u[1]] for u in units}
    for step in range(nch):
        r0 = [step * ll, (nch - 1 - step) * ll]
        qi, qt, kt, kd, e_end = [], [], [], [], []
        for d in range(2):
            bcum = _dot_exact_l(tri[d], la[d][r0[d]:r0[d] + ll, :])
            ref = bcum[ll // 2:ll // 2 + 1, :]
            bend = bcum[0:1, :] if d else bcum[ll - 1:ll, :]
            q = qk[d][r0[d]:r0[d] + ll, 0:dkw] * (GLA_DK ** -0.5)
            k = qk[d][r0[d]:r0[d] + ll, dkw:2 * dkw]
            qi.append((q * jnp.exp(bcum)).astype(BF16))
            qt.append((q * jnp.exp(bcum - ref)).astype(BF16))
            ktd = k * jnp.exp(ref - bcum)
            kt.append(ktd.astype(BF16))
            kd.append((ktd * jnp.exp(bend - ref)).astype(BF16))
            e_end.append(jnp.exp(bend))
        hs = lambda a, h, w: a[:, h * w:(h + 1) * w]
        vb = {(d, h): hs(v[d][r0[d]:r0[d] + ll, :], h, GLA_DV).astype(BF16) for d, h in units}
        att = {(d, h): _mm(hs(qt[d], h, GLA_DK), hs(kt[d], h, GLA_DK), _NT) for d, h in units}
        inter = {(d, h): _mm(hs(qi[d], h, GLA_DK), st[(d, h)].astype(BF16), _NT) for d, h in units}
        attb = {(d, h): jnp.where(causal[d], att[(d, h)], 0.0).astype(BF16) for d, h in units}
        kv = {(d, h): _mm(vb[(d, h)], hs(kd[d], h, GLA_DK), _TN) for d, h in units}
        for d, h in units:
            o = _mm(attb[(d, h)], vb[(d, h)]) + inter[(d, h)]
            outs[d][0, r0[d]:r0[d] + ll, h * GLA_DV:(h + 1) * GLA_DV] = o.astype(BF16)
        st = {(d, h): st[(d, h)] * hs(e_end[d], h, GLA_DK) + kv[(d, h)] for d, h in units}
    for d, h in units:
        s_s[d * N_HEADS + h] = st[(d, h)]


def _mix2_kernel(qcf, mvf, psf, gqf, gvf, qcb, mvb, psb, gqb, gvb, w2, ba,
                 mf_ref, mb_ref, gf_ref, gb_ref, c_s, m_s, s_s):
    @pl.when(pl.program_id(1) == 0)
    def _():
        c_s[...] = jnp.zeros_like(c_s)
        m_s[...] = jnp.zeros_like(m_s)
        s_s[...] = jnp.zeros_like(s_s)

    _ml_pair((qcf, mvf, psf, mf_ref), (qcb, mvb, psb, mb_ref), c_s, m_s)
    _gla_pair((gqf, gvf, psf, gf_ref), (gqb, gvb, psb, gb_ref), w2, ba, s_s)


def _mixers2(qc, pb, ps, layer, w2e, ba, ncb):
    b, t, _ = pb.shape
    nblk = t // TB
    fwd = lambda i: i
    bwd = lambda i: _rev_block(i, ncb, nblk)

    def dspecs(blk):
        col = lambda cb: pl.BlockSpec((1, TB, 512), lambda b_, i: (b_, blk(i), cb))
        return [pl.BlockSpec((1, TB, qc.shape[2]), lambda b_, i: (b_, blk(i), 0)), col(PB_ML_V),
                pl.BlockSpec((1, TB, LANES), lambda b_, i: (b_, blk(i), 0)), col(PB_GLA_QK), col(PB_GLA_V)]

    specs = dspecs(fwd) + dspecs(bwd) + [_layer_spec(layer, w2e), _layer_spec(layer, ba)]
    ns = 2 * N_HEADS
    ofwd = pl.BlockSpec((1, TB, 512), lambda b_, i: (b_, i, 0))
    obwd = pl.BlockSpec((1, TB, 512), lambda b_, i: (b_, bwd(i), 0))
    return pl.pallas_call(
        _mix2_kernel,
        grid=(b, nblk),
        in_specs=specs,
        out_specs=[ofwd, obwd, ofwd, obwd],
        out_shape=[jax.ShapeDtypeStruct((b, t, 512), BF16)] * 4,
        scratch_shapes=[pltpu.VMEM((ns, 2 * ML_DH, ML_DH), F32),
                        pltpu.VMEM((2, 1, LANES), F32),
                        pltpu.VMEM((ns, GLA_DV, GLA_DK), F32)],
        compiler_params=_cparams(("parallel", "arbitrary")),
        name="mixers",
    )(*([qc, pb, ps, pb, pb] * 2), w2e, ba)


def _head_norm(x, g):
    outs = []
    for h in range(N_HEADS):
        seg = x[:, h * 128:(h + 1) * 128]
        outs.append(seg * lax.rsqrt(jnp.mean(seg * seg, axis=-1, keepdims=True) + EPS))
    return jnp.concatenate(outs, axis=-1) * g


def _out_kernel(hf, hb, of, ob, og, rg, x_ref, mod_ref, gml, ggla, wo, gpost, gpre, wrt,
                x1_ref, h2e_ref, aff_ref):
    ne = N_EXPERTS
    nb, _, d = x_ref.shape
    rh = LANES
    groups = [(b, slice(k * rh, (k + 1) * rh)) for b in range(nb) for k in range(TB // rh)]
    f32 = lambda ref, b, r: ref[b, r, :].astype(F32)
    y = [jnp.concatenate([_head_norm(f32(hf, b, r) + f32(hb, b, r), gml[0]) * _sigmoid(og[b, r, :]),
                          _head_norm(f32(of, b, r) + f32(ob, b, r), ggla[0]) * _silu(rg[b, r, :])],
                         axis=-1).astype(BF16) for b, r in groups]
    y2 = _mm(jnp.concatenate(y, axis=0), wo[0])
    gate1 = [mod_ref[0, b, 0, 2:3, :] * gpost[0] for b in range(nb)]
    gain2 = [gpre[0] * (1.0 + mod_ref[0, b, 0, 4:5, :]) for b in range(nb)]
    x1 = [x_ref[b, r, :] + _rms(y2[g * rh:(g + 1) * rh, :], gate1[b]) for g, (b, r) in enumerate(groups)]
    for (b, r), x1k in zip(groups, x1):
        x1_ref[b, r, :] = x1k
    h2 = [_rms(x1k, gain2[b]) + mod_ref[0, b, 0, 3:4, :] for (b, r), x1k in zip(groups, x1)]
    lt = _dot3(wrt[0], jnp.concatenate(h2, axis=0), _NT)
    for g, ((b, r), h2k) in enumerate(zip(groups, h2)):
        ltk = lt[:, g * rh:(g + 1) * rh]
        ext = jnp.exp(ltk - jnp.max(ltk, axis=0, keepdims=True))
        affk = ext / jnp.sum(ext, axis=0, keepdims=True)
        aff_ref[b, :, r] = affk
        afft = jnp.concatenate([affk, jnp.zeros((LANES - ne, rh), F32)], axis=0).T
        a_hi, a_mid, a_lo = _split3(afft)
        pieces = (a_hi.astype(F32) + pltpu.roll(a_mid.astype(F32), ne, axis=1)
                  + pltpu.roll(a_lo.astype(F32), 2 * ne, axis=1))
        h2e_ref[b, r, 0:d] = h2k.astype(BF16)
        h2e_ref[b, r, d:d + LANES] = pieces.astype(BF16)


def _out_proj(hf, hb, of, ob, pb, xa, layer, mod, gml, ggla, wo, gpost, gpre, wrt, ncb):
    b, t, d = xa.shape
    de = d + LANES
    tile = lambda w, cb: pl.BlockSpec((b, TB, w), lambda i: (0, i, cb))
    return pl.pallas_call(
        _out_kernel,
        grid=(t // TB,),
        in_specs=[tile(512, 0), tile(512, 0), tile(512, 0), tile(512, 0), tile(512, PB_ML_O), tile(512, PB_GLA_R),
                  tile(d, 0), _mod_spec(layer, mod, ncb)]
                 + [_layer_spec(layer, a) for a in (gml, ggla, wo, gpost, gpre, wrt)],
        out_specs=[tile(d, 0), tile(de, 0),
                   pl.BlockSpec((b, N_EXPERTS, TB), lambda i: (0, 0, i))],
        out_shape=[jax.ShapeDtypeStruct((b, t, d), F32),
                   jax.ShapeDtypeStruct((b, t, de), BF16),
                   jax.ShapeDtypeStruct((b, N_EXPERTS, t), F32)],
        compiler_params=_cparams(("parallel",)),
        name="out_proj",
    )(hf, hb, of, ob, pb, pb, xa, mod, gml, ggla, wo, gpost, gpre, wrt)


def _cumsum_blocks(x, r):
    n = x.shape[0]
    xb = x.astype(BF16)
    li = _iota((LANES, LANES), 0)
    lj = _iota((LANES, LANES), 1)
    upper = jnp.where(li <= lj, 1.0, 0.0).astype(BF16)
    ones = jnp.ones((LANES, LANES), BF16)
    inrow = _mm(xb, upper)
    tot = _mm(xb, ones)
    ri = _iota((n, n), 0)
    rj = _iota((n, n), 1)
    same = (ri // r) == (rj // r)
    strict = jnp.where(jnp.logical_and(same, rj < ri), 1.0, 0.0).astype(BF16)
    off = _mm(strict, tot.astype(BF16))
    return inrow + off, off


def _select(aff, r, cap, base_slot):
    ne = N_EXPERTS
    n = ne * r
    aff3 = aff.reshape(ne, r, LANES)
    capf = jnp.float32(cap)

    def body(k, prefix):
        cand = prefix | (jnp.int32(1) << (30 - k))
        candf = lax.bitcast_convert_type(cand, F32)
        cnt = jnp.sum(jnp.where(aff3 >= candf, 1.0, 0.0), axis=(1, 2), keepdims=True)
        return jnp.where(cnt >= capf, cand, prefix)

    thr = lax.bitcast_convert_type(lax.fori_loop(0, 31, body, jnp.zeros((ne, 1, 1), I32)), F32)
    gt = jnp.where(aff3 > thr, 1.0, 0.0)
    eq = jnp.where(aff3 == thr, 1.0, 0.0)
    need = capf - jnp.sum(gt, axis=(1, 2), keepdims=True)
    eq2 = eq.reshape(n, LANES)
    cs_eq, _ = _cumsum_blocks(eq2, r)
    eq_rank = (cs_eq - eq2).reshape(ne, r, LANES)
    sel = (gt + eq * jnp.where(eq_rank < need, 1.0, 0.0)).reshape(n, LANES)
    cs, off = _cumsum_blocks(sel, r)
    slot = jnp.where(sel > 0.5, cs - 1.0 + base_slot, UNSEL)
    return slot, off


def _sel_kernel(rc, rl, cap_c, cap_l, *refs):
    if rc:
        affc, affl, slc, offc, sll, offl = refs
        slc[0], offc[0] = _select(affc[0], rc, cap_c, float(cap_l))
    else:
        affl, sll, offl = refs
    sll[0], offl[0] = _select(affl[0], rl, cap_l, 0.0)


def _route(aff_c, aff_l, cap_c, cap_l):
    b = aff_l.shape[0]
    ne = N_EXPERTS
    rl = aff_l.shape[1] // ne
    rc = aff_c.shape[1] // ne if aff_c is not None else 0
    args = ([aff_c] if rc else []) + [aff_l]
    in_specs, out_shape, out_specs = [], [], []
    for a in args:
        spec = pl.BlockSpec((1,) + a.shape[1:], lambda bi: (bi, 0, 0))
        in_specs.append(spec)
        out_shape += [jax.ShapeDtypeStruct(a.shape, F32)] * 2
        out_specs += [spec, spec]
    return pl.pallas_call(
        functools.partial(_sel_kernel, rc, rl, cap_c, cap_l),
        grid=(b,),
        in_specs=in_specs,
        out_specs=out_specs,
        out_shape=out_shape,
        compiler_params=_cparams(("parallel",)),
        name="route",
    )(*args)


def _window(lo_ref, base, e, m_rows):
    lo_e = lo_ref[base + e]
    hi_e = lo_ref[base + N_EXPERTS + e]
    a_e = jnp.minimum((lo_e // ROW_ALIGN) * ROW_ALIGN, m_rows - WIN)
    return a_e, hi_e


def _disp_kernel(m_rows, ng, lo_ref, h_ref, slot_ref, x_ref):
    ne = N_EXPERTS
    gi = pl.program_id(1)
    i = pl.program_id(2)
    base = (pl.program_id(0) * pl.num_programs(2) + i) * (2 * ne)

    @pl.when(i == 0)
    def _():
        x_ref[...] = jnp.zeros_like(x_ref)

    h = h_ref[0]
    sl = slot_ref[0]
    sub = _iota((WIN, TB), 0).astype(F32)
    wins = [_window(lo_ref, base, gi * ng + k, m_rows) for k in range(ng)]

    def onehot(k, first):
        a_r = jnp.minimum(first, m_rows - WIN)
        srow = sl[k:k + 1, :]
        hit = jnp.logical_and(srow - a_r.astype(F32) == sub, srow >= first.astype(F32))
        return jnp.where(hit, 1.0, 0.0).astype(BF16), a_r

    def add_rows(k, a_r, g):
        rows = pl.ds(pl.multiple_of(a_r, ROW_ALIGN), WIN)
        x_ref[0, k, rows, :] = x_ref[0, k, rows, :] + g

    sel = [onehot(k, a_e) for k, (a_e, _) in enumerate(wins)]
    g = _mm(jnp.concatenate([w for w, _ in sel], axis=0), h).astype(BF16)
    for k, (_, a_r) in enumerate(sel):
        add_rows(k, a_r, g[k * WIN:(k + 1) * WIN, :])

    for k, (a_e, hi_e) in enumerate(wins):
        @pl.when(hi_e - a_e > WIN)
        def _(k=k, a_e=a_e, hi_e=hi_e):
            def more(rd, carry):
                w, a_r = onehot(k, a_e + rd * WIN)
                add_rows(k, a_r, _mm(w, h).astype(BF16))
                return carry

            lax.fori_loop(1, (hi_e - a_e + WIN - 1) // WIN, more, 0)


def _dispatch(lohi, h2e, slots, m_rows, t0, nt):
    b, t, de = h2e.shape
    ne = N_EXPERTS
    ng = 8
    grid_spec = pltpu.PrefetchScalarGridSpec(
        num_scalar_prefetch=1,
        grid=(b, ne // ng, nt),
        in_specs=[pl.BlockSpec((1, TB, de), lambda bi, gi, i, *_: (bi, i + t0, 0)),
                  pl.BlockSpec((1, ng, TB), lambda bi, gi, i, *_: (bi * (ne // ng) + gi, 0, i + t0))],
        out_specs=pl.BlockSpec((1, ng, m_rows, de), lambda bi, gi, i, *_: (bi, gi, 0, 0)))
    return pl.pallas_call(
        functools.partial(_disp_kernel, m_rows, ng),
        grid_spec=grid_spec,
        out_shape=jax.ShapeDtypeStruct((b, ne, m_rows, de), BF16),
        compiler_params=_cparams(("parallel", "parallel", "arbitrary")),
        name="dispatch",
    )(lohi, h2e, slots.reshape(b * (ne // ng), ng, t))


def _row_chunks(m_rows, cap_l):
    step = min(EXPERT_ROWS, cap_l)
    starts = list(range(0, cap_l, step))
    return [(s, (m_rows - s) if s == starts[-1] else step) for s in starts]


def _exp_kernel(m_rows, cap_l, x_ref, wg_ref, wu_ref, wd_ref, y_ref, wg_s, wu_s, wd_s):
    ei = pl.program_id(0)
    d = wg_ref.shape[2]

    @pl.when(pl.program_id(1) == 0)
    def _():
        wg_s[...] = wg_ref[0, 0].astype(BF16)
        wu_s[...] = wu_ref[0, 0].astype(BF16)
        wd_s[...] = wd_ref[0, 0].astype(BF16)

    for r0, mc in _row_chunks(m_rows, cap_l):
        xs = x_ref[0, 0, r0:r0 + mc, 0:d]
        hid = _silu(_mm(xs, wg_s[...])) * _mm(xs, wu_s[...])
        y = _mm(hid.astype(BF16), wd_s[...])
        pieces = x_ref[0, 0, r0:r0 + mc, d:d + LANES].astype(F32)
        lane = _iota((mc, LANES), 1)
        mine = jnp.logical_and(lane % N_EXPERTS == ei, lane < 3 * N_EXPERTS)
        gate = jnp.sum(jnp.where(mine, pieces, 0.0), axis=1, keepdims=True)
        y_ref[0, 0, r0:r0 + mc, :] = (y * gate).astype(BF16)


def _experts(xin, wg, wu, wd, layer, cap_l):
    b, ne, m_rows, de = xin.shape
    _, _, d, f = wg.shape
    return pl.pallas_call(
        functools.partial(_exp_kernel, m_rows, cap_l),
        grid=(ne, b),
        in_specs=[pl.BlockSpec((1, 1, m_rows, de), lambda ei, bi: (bi, ei, 0, 0)),
                  pl.BlockSpec((1, 1, d, f), lambda ei, bi: (layer, ei, 0, 0)),
                  pl.BlockSpec((1, 1, d, f), lambda ei, bi: (layer, ei, 0, 0)),
                  pl.BlockSpec((1, 1, f, d), lambda ei, bi: (layer, ei, 0, 0))],
        out_specs=pl.BlockSpec((1, 1, m_rows, d), lambda ei, bi: (bi, ei, 0, 0)),
        out_shape=jax.ShapeDtypeStruct((b, ne, m_rows, d), BF16),
        scratch_shapes=[pltpu.VMEM((d, f), BF16), pltpu.VMEM((d, f), BF16), pltpu.VMEM((f, d), BF16)],
        compiler_params=_cparams(("arbitrary", "arbitrary")),
        name="experts",
    )(xin, wg, wu, wd)


def _comb_kernel(m_rows, lo_ref, slot_ref, x1_ref, mod_ref, gpost, y_ref, out_ref, acc_s):
    ne = N_EXPERTS
    i = pl.program_id(1)
    base = (pl.program_id(0) * pl.num_programs(1) + i) * (2 * ne)
    kk = ne * WIN
    sl = slot_ref[0]
    hi = jnp.floor(sl * (1.0 / 32.0))
    lo = sl - hi * 32.0
    col_e = _iota((ne, kk), 1) // WIN
    expand = jnp.where(col_e == _iota((ne, kk), 0), 1.0, 0.0).astype(BF16)
    sx = _mm(hi.astype(BF16), expand, _TN) * 32.0 + _mm(lo.astype(BF16), expand, _TN)
    col = _iota((1, kk), 1)
    jrow = (col % WIN).astype(F32)
    wins = [_window(lo_ref, base, e, m_rows) for e in range(ne)]
    ys = []
    arow = jnp.zeros((1, kk), F32)
    for e, (a_e, _) in enumerate(wins):
        arow = jnp.where(col // WIN == e, a_e.astype(F32), arow)
        ys.append(y_ref[0, e, pl.ds(pl.multiple_of(a_e, ROW_ALIGN), WIN), :])
    w = jnp.where(sx - arow == jrow, 1.0, 0.0).astype(BF16)
    acc_s[...] = _mm(w, jnp.concatenate(ys, axis=0))

    lane = _iota((TB, WIN), 1).astype(F32)
    for e, (a_e, hi_e) in enumerate(wins):
        @pl.when(hi_e - a_e > WIN)
        def _(e=e, a_e=a_e, hi_e=hi_e):
            scol = sx[:, e * WIN:e * WIN + 1]

            def more(rd, carry):
                first = a_e + rd * WIN
                a_r = jnp.minimum(first, m_rows - WIN)
                hit = jnp.logical_and(scol - a_r.astype(F32) == lane, scol >= first.astype(F32))
                ye = y_ref[0, e, pl.ds(pl.multiple_of(a_r, ROW_ALIGN), WIN), :]
                acc_s[...] += _mm(jnp.where(hit, 1.0, 0.0).astype(BF16), ye)
                return carry

            lax.fori_loop(1, (hi_e - a_e + WIN - 1) // WIN, more, 0)

    out_ref[0] = x1_ref[0] + _rms(acc_s[...], mod_ref[0, 0, 0, 5:6, :] * gpost[0])


def _combine(lohi, slots, x1, layer, mod, gpost, y, t0, nt, ncb):
    b, t, d = x1.shape
    ne = N_EXPERTS
    m_rows = y.shape[2]
    r8 = mod.shape[3]
    grid_spec = pltpu.PrefetchScalarGridSpec(
        num_scalar_prefetch=1,
        grid=(b, nt),
        in_specs=[pl.BlockSpec((1, ne, TB), lambda bi, i, *_: (bi, 0, i + t0)),
                  pl.BlockSpec((1, TB, d), lambda bi, i, *_: (bi, i + t0, 0)),
                  pl.BlockSpec((1, 1, 1, r8, d),
                               lambda bi, i, *_: (layer, bi, jnp.where(i + t0 < ncb, 0, 1), 0, 0)),
                  _layer_spec(layer, gpost),
                  pl.BlockSpec((1, ne, m_rows, d), lambda bi, i, *_: (bi, 0, 0, 0),
                               pipeline_mode=pl.Buffered(1))],
        out_specs=pl.BlockSpec((1, TB, d), lambda bi, i, *_: (bi, i, 0)),
        scratch_shapes=[pltpu.VMEM((TB, d), F32)])
    return pl.pallas_call(
        functools.partial(_comb_kernel, m_rows),
        grid_spec=grid_spec,
        out_shape=jax.ShapeDtypeStruct((b, nt * TB, d), F32),
        compiler_params=_cparams(("parallel", "arbitrary")),
        name="combine",
    )(lohi, slots, x1, mod, gpost, y)


def _pos_tables(rows, d):
    quarter = d // 4
    freq = jnp.power(POS_BASE, -jnp.arange(quarter, dtype=F32) / quarter)
    ar = jnp.arange(rows, dtype=F32)[:, None] * freq
    ac = jnp.arange(GRID_W, dtype=F32)[:, None] * freq
    return (jnp.concatenate([jnp.sin(ar), jnp.cos(ar)], axis=-1),
            jnp.concatenate([jnp.sin(ac), jnp.cos(ac)], axis=-1))


def _tile_bounds(off, r, ntile, cap, base):
    b = off.shape[0]
    o = off.reshape(b, N_EXPERTS, r, LANES)[:, :, :, 0]
    lo = o[:, :, ::TB // LANES][:, :, :ntile] + base
    hi = jnp.concatenate([lo[:, :, 1:], jnp.full((b, N_EXPERTS, 1), cap + base, F32)], axis=2)
    return lo, hi


def kernel(x, c, ctx, c_ctx, w_ada, b_ada, g_mix_pre, g_mix_post, g_ffn_pre, g_ffn_post,
           w_in, conv_qk, b_ml_gates, w_gla_a2, b_gla_a, g_ml_norm, g_gla_norm, w_out,
           w_router, w_e_gate, w_e_up, w_e_down):
    bsz, n_tok, d = x.shape
    lc = ctx.shape[1]
    depth = w_in.shape[0]
    ne = N_EXPERTS
    t = lc + n_tok
    ncb = lc // TB
    nblk = t // TB
    assert lc % TB == 0 and n_tok % TB == 0 and d == 1024
    cap_l = EC_FACTOR * n_tok // ne
    cap_c = EC_FACTOR * lc // ne

    assert TB % GRID_W == 0
    pos_r, pos_c = _pos_tables(n_tok // GRID_W, d)
    xa = None

    cc = jnp.zeros((8, d), F32).at[:bsz].set(c).at[bsz].set(c_ctx)
    mods = _ada(cc, w_ada, b_ada)

    wide = jnp.concatenate([w_in[:, :, 0:2048], w_in[:, :, 2064:3600]], axis=2).astype(BF16)
    narrow = jnp.concatenate([w_in[:, :, 2048:2064], w_in[:, :, 3600:3632]], axis=2)
    narrow = jnp.pad(narrow, ((0, 0), (0, 0), (0, LANES - narrow.shape[2])))
    narrow = jnp.concatenate(_split2(narrow), axis=2)
    bias_s = jnp.pad(b_ml_gates, ((0, 0), (0, LANES - b_ml_gates.shape[1])))
    w2e = jnp.zeros((depth, 2, LANES, 256), F32)
    w2e = w2e.at[:, 0, 16:32].set(w_gla_a2[:, 0]).at[:, 1, 32:48].set(w_gla_a2[:, 1])
    m_lat = mods[:, :bsz].reshape(depth, bsz, 1, 6, d)
    m_ctx = jnp.broadcast_to(mods[:, bsz].reshape(depth, 1, 1, 6, d), (depth, bsz, 1, 6, d))
    mod = jnp.pad(jnp.concatenate([m_ctx, m_lat], axis=2), ((0, 0), (0, 0), (0, 0), (0, 2), (0, 0)))
    row = lambda a: a.reshape(depth, 1, -1)
    bias_s, ba = row(bias_s), b_gla_a.reshape(depth, 2, 1, -1)
    g_pre1, g_post1, g_pre2, g_post2 = row(g_mix_pre), row(g_mix_post), row(g_ffn_pre), row(g_ffn_post)
    g_ml, g_gla = row(g_ml_norm), row(g_gla_norm)
    wo = w_out.astype(BF16)
    wrt = w_router.transpose(0, 2, 1)

    for l in range(depth):
        last = l == depth - 1
        if l == 0:
            xa, pb, ps, qc = _in_proj(None, l, mod, g_pre1, wide, narrow, bias_s, conv_qk, ncb,
                                      first=(x, ctx, pos_r, pos_c))
        else:
            pb, ps, qc = _in_proj(xa, l, mod, g_pre1, wide, narrow, bias_s, conv_qk, ncb)
        hf, hb, of, ob = _mixers2(qc, pb, ps, l, w2e, ba, ncb)
        x1, h2e, aff = _out_proj(hf, hb, of, ob, pb, xa, l, mod, g_ml, g_gla, wo, g_post1, g_pre2, wrt, ncb)

        rl = n_tok // LANES
        aff_l = aff[:, :, lc:].reshape(bsz, ne * rl, LANES)
        if last:
            sll, offl = _route(None, aff_l, 0, cap_l)
            slots = jnp.pad(sll.reshape(bsz, ne, n_tok), ((0, 0), (0, 0), (lc, 0)), constant_values=UNSEL)
            lo, hi = _tile_bounds(offl, rl, nblk - ncb, cap_l, 0)
            t0, nt, m_rows = ncb, nblk - ncb, cap_l
        else:
            rc = max(lc // LANES, 8)
            aff_c = aff[:, :, :lc].reshape(bsz, ne, lc // LANES, LANES)
            aff_c = jnp.pad(aff_c, ((0, 0), (0, 0), (0, rc - lc // LANES), (0, 0)), constant_values=-1.0)
            slc, offc, sll, offl = _route(aff_c.reshape(bsz, ne * rc, LANES), aff_l, cap_c, cap_l)
            slots = jnp.concatenate([slc.reshape(bsz, ne, rc * LANES)[:, :, :lc],
                                     sll.reshape(bsz, ne, n_tok)], axis=2)
            lo_l, hi_l = _tile_bounds(offl, rl, nblk - ncb, cap_l, 0)
            lo_c, hi_c = _tile_bounds(offc, rc, ncb, cap_c, cap_l)
            lo = jnp.concatenate([lo_c, lo_l], axis=2)
            hi = jnp.concatenate([hi_c, hi_l], axis=2)
            t0, nt, m_rows = 0, nblk, cap_l + cap_c
        lohi = jnp.concatenate([lo, hi], axis=1).transpose(0, 2, 1).astype(I32).reshape(-1)
        xin = _dispatch(lohi, h2e, slots, m_rows, t0, nt)
        y = _experts(xin, w_e_gate, w_e_up, w_e_down, l, cap_l)
        xa = _combine(lohi, slots, x1, l, mod, g_post2, y, t0, nt, ncb)
    return xa
```

```python
import functools

import jax
import jax.numpy as jnp
from jax import lax
from jax.experimental import pallas as pl
from jax.experimental.pallas import tpu as pltpu

F32 = jnp.float32
BF16 = jnp.bfloat16
I32 = jnp.int32

EPS = 1e-6
GRID_W = 64
POS_BASE = 10000.0
N_HEADS = 4
ML_DH = 128
GLA_DK = 64
GLA_DV = 128
GLA_GATE_TAU = 16.0
N_EXPERTS = 16
EC_FACTOR = 2

LANES = 128
TB = 256
ML_L = 128
GLA_L = 128
ML_STAGE_UNITS = 2
EXPERT_ROWS = 256
WIN = 80
ROW_ALIGN = 16
UNSEL = 2047.0
PB_ML_V, PB_ML_O, PB_GLA_QK, PB_GLA_V, PB_GLA_R = 0, 1, 2, 3, 4
VMEM_LIMIT = 56 * 1024 * 1024


def _cparams(sem):
    return pltpu.CompilerParams(dimension_semantics=sem, vmem_limit_bytes=VMEM_LIMIT)


def _split2(a):
    hi = a.astype(BF16)
    lo = (a - hi.astype(F32)).astype(BF16)
    return hi, lo


def _split3(a):
    hi = a.astype(BF16)
    r = a - hi.astype(F32)
    mid = r.astype(BF16)
    lo = (r - mid.astype(F32)).astype(BF16)
    return hi, mid, lo


_NN = (((1,), (0,)), ((), ()))
_NT = (((1,), (1,)), ((), ()))
_TN = (((0,), (0,)), ((), ()))


def _mm(a, b, dims=_NN):
    return lax.dot_general(a, b, dims, preferred_element_type=F32)


def _dot3(a, b, dims=_NN):
    ah, al = _split2(a)
    bh, bl = _split2(b)
    return _mm(ah, bh, dims) + (_mm(ah, bl, dims) + _mm(al, bh, dims))


def _dot_exact_l(m_bf16, x, dims=_NN):
    hi, mid, lo = _split3(x)
    return _mm(m_bf16, hi, dims) + (_mm(m_bf16, mid, dims) + _mm(m_bf16, lo, dims))


def _rms(x, g):
    return x * lax.rsqrt(jnp.mean(x * x, axis=-1, keepdims=True) + EPS) * g


def _log_sigmoid(x):
    return jnp.minimum(x, 0.0) - jnp.log(1.0 + jnp.exp(-jnp.abs(x)))


def _sigmoid(x):
    return 1.0 / (1.0 + jnp.exp(-x))


def _silu(x):
    return x * _sigmoid(x)


def _iota(shape, dim):
    return lax.broadcasted_iota(I32, shape, dim)


def _rev_block(i, ncb, nblk):
    return jnp.where(i < ncb, ncb - 1 - i, nblk - 1 - (i - ncb))


def _ada_kernel(c_ref, w_ref, b_ref, o_ref):
    a = _silu(c_ref[...])
    o_ref[0] = _dot3(a, w_ref[0]) + b_ref[0]


def _ada(cc, w_ada, b_ada):
    depth, d, n6 = w_ada.shape
    tn = 1536
    return pl.pallas_call(
        _ada_kernel,
        grid=(depth, n6 // tn),
        in_specs=[pl.BlockSpec((8, d), lambda l, j: (0, 0)),
                  pl.BlockSpec((1, d, tn), lambda l, j: (l, 0, j)),
                  pl.BlockSpec((1, 1, tn), lambda l, j: (l, 0, j))],
        out_specs=pl.BlockSpec((1, 8, tn), lambda l, j: (l, 0, j)),
        out_shape=jax.ShapeDtypeStruct((depth, 8, n6), F32),
        compiler_params=_cparams(("parallel", "parallel")),
        name="ada",
    )(cc, w_ada, b_ada.reshape(depth, 1, n6))


def _project(ncb, xs, xps, xns, mod_ref, g_ref, wb_ref, ws_ref, bs_ref, cw_ref, pb_ref, ps_ref, qc_ref):
    i = pl.program_id(0)
    nblk = pl.num_programs(0)
    nb, n = len(xs), xs[0].shape[0]
    rows = [slice(b * n, (b + 1) * n) for b in range(nb)]

    gain = [g_ref[0] * (1.0 + mod_ref[0, b, 0, 1:2, :]) for b in range(nb)]

    def norm(z, b):
        return _rms(z, gain[b]) + mod_ref[0, b, 0, 0:1, :]

    hh = jnp.concatenate([norm(xs[b], b).astype(BF16) for b in range(nb)], axis=0)
    halo = ([norm(xps[b], b).astype(BF16) for b in range(nb)] + [norm(xns[b], b).astype(BF16) for b in range(nb)])
    wq = cw_ref.shape[2]
    qk = _mm(jnp.concatenate([hh] + halo, axis=0), wb_ref[0, :, 0:wq])
    lvalid = jnp.logical_and(i != 0, i != ncb)
    rvalid = jnp.logical_and(i != ncb - 1, i != nblk - 1)
    dq = wq // 2
    for b in range(nb):
        lrow = nb * n + 8 * b + 7
        rrow = nb * n + 8 * nb + 8 * b
        left = jnp.where(lvalid, qk[lrow:lrow + 1, :], 0.0)
        right = jnp.where(rvalid, qk[rrow:rrow + 1, :], 0.0)
        y = _silu(_conv3(qk[rows[b], :], left, right, cw_ref[0]))
        qc_ref[b, :, 0:dq] = (y[:, 0:dq] * (ML_DH ** -0.5)).astype(BF16)
        qc_ref[b, :, dq:] = y[:, dq:].astype(BF16)
    rest = _mm(hh, wb_ref[0, :, wq:])
    pr = _mm(hh, ws_ref[0])
    ps = pr[:, :LANES] + pr[:, LANES:] + bs_ref[0]
    lane = _iota(ps.shape, 1)
    forget = jnp.logical_and(lane % 8 >= N_HEADS, lane < 4 * N_HEADS)
    ps = jnp.where(forget, _log_sigmoid(ps), ps)
    for b in range(nb):
        pb_ref[b] = rest[rows[b], :]
        ps_ref[b] = ps[rows[b], :]


def _in_kernel(ncb, x_ref, xp_ref, xn_ref, mod_ref, g_ref, wb_ref, ws_ref, bs_ref, cw_ref,
               pb_ref, ps_ref, qc_ref):
    nb = x_ref.shape[0]
    _project(ncb, [x_ref[b] for b in range(nb)], [xp_ref[b] for b in range(nb)],
             [xn_ref[b] for b in range(nb)], mod_ref, g_ref, wb_ref, ws_ref, bs_ref, cw_ref,
             pb_ref, ps_ref, qc_ref)


def _in0_kernel(ncb, x_ref, xp_ref, xn_ref, c_ref, cp_ref, cn_ref, pr_ref, prp_ref, prn_ref, pc_ref,
                mod_ref, g_ref, wb_ref, ws_ref, bs_ref, cw_ref, xa_ref, pb_ref, ps_ref, qc_ref):
    i = pl.program_id(0)
    nb = x_ref.shape[0]
    half = pr_ref.shape[2]
    reps = TB // GRID_W
    prow = jnp.concatenate([jnp.broadcast_to(pr_ref[0, k:k + 1, :], (GRID_W, half)) for k in range(reps)], axis=0)
    pcol = jnp.concatenate([pc_ref[...]] * reps, axis=0)
    pos = jnp.concatenate([prow, pcol], axis=1)
    pos_p = jnp.concatenate([prp_ref[0, reps - 1:reps, :], pc_ref[GRID_W - 1:GRID_W, :]], axis=1)
    pos_n = jnp.concatenate([prn_ref[0, 0:1, :], pc_ref[0:1, :]], axis=1)
    is_ctx = i < ncb
    xs, xps, xns = [], [], []
    for b in range(nb):
        xa = jnp.where(is_ctx, c_ref[b], x_ref[b] + pos)
        xa_ref[b] = xa
        xs.append(xa)
        xps.append(jnp.where(is_ctx, cp_ref[b], xp_ref[b] + pos_p))
        xns.append(jnp.where(is_ctx, cn_ref[b], xn_ref[b] + pos_n))
    _project(ncb, xs, xps, xns, mod_ref, g_ref, wb_ref, ws_ref, bs_ref, cw_ref, pb_ref, ps_ref, qc_ref)


def _tile_and_halo_specs(b, rows, d, tile_of):
    r8 = TB // 8
    ntile, last8 = rows // TB, rows // 8 - 1
    tl = lambda i: jnp.clip(tile_of(i), 0, ntile - 1)
    return [pl.BlockSpec((b, TB, d), lambda i: (0, tl(i), 0)),
            pl.BlockSpec((b, 8, d), lambda i: (0, jnp.clip(tl(i) * r8 - 1, 0, last8), 0)),
            pl.BlockSpec((b, 8, d), lambda i: (0, jnp.clip((tl(i) + 1) * r8, 0, last8), 0))]


def _layer_spec(layer, arr):
    shp = arr.shape[1:]
    return pl.BlockSpec((1,) + shp, lambda *_: (layer,) + (0,) * len(shp))


def _mod_spec(layer, mod, ncb, tile_of=lambda i: i):
    _, b, _, r, d = mod.shape
    return pl.BlockSpec((1, b, 1, r, d), lambda i, *_: (layer, 0, jnp.where(tile_of(i) < ncb, 0, 1), 0, 0))


def _in_proj(xa, layer, mod, g, wb, ws, bs, cw, ncb, first=None):
    if first is None:
        b, t, d = xa.shape
    else:
        b, t, d = first[0].shape[0], first[0].shape[1] + first[1].shape[1], first[0].shape[2]
    wq = cw.shape[2]
    nb = wb.shape[2] - wq
    const = lambda shp: pl.BlockSpec(shp, lambda i: tuple(0 for _ in shp))
    tile = lambda w: pl.BlockSpec((b, TB, w), lambda i: (0, i, 0))
    common = [_mod_spec(layer, mod, ncb)] + [_layer_spec(layer, a) for a in (g, wb, ws, bs, cw)]
    out_specs = [tile(nb), tile(LANES), tile(wq)]
    out_shape = [jax.ShapeDtypeStruct((b, t, nb), F32),
                 jax.ShapeDtypeStruct((b, t, LANES), F32),
                 jax.ShapeDtypeStruct((b, t, wq), BF16)]
    if first is None:
        body = functools.partial(_in_kernel, ncb)
        in_specs = _tile_and_halo_specs(b, t, d, lambda i: i) + common
        args = (xa, xa, xa, mod, g, wb, ws, bs, cw)
    else:
        x, ctx, pos_r, pos_c = first
        reps = TB // GRID_W
        ntl = x.shape[1] // TB
        body = functools.partial(_in0_kernel, ncb)
        pr_spec = lambda off: pl.BlockSpec((1, reps, d // 2),
                                           lambda i: (jnp.clip(i - ncb + off, 0, ntl - 1), 0, 0))
        in_specs = (_tile_and_halo_specs(b, x.shape[1], d, lambda i: i - ncb)
                    + _tile_and_halo_specs(b, ctx.shape[1], d, lambda i: i)
                    + [pr_spec(0), pr_spec(-1), pr_spec(1), const((GRID_W, d // 2))] + common)
        out_specs = [tile(d)] + out_specs
        out_shape = [jax.ShapeDtypeStruct((b, t, d), F32)] + out_shape
        pr3 = pos_r.reshape(-1, reps, d // 2)
        args = (x, x, x, ctx, ctx, ctx, pr3, pr3, pr3, pos_c, mod, g, wb, ws, bs, cw)
    return pl.pallas_call(
        body,
        grid=(t // TB,),
        in_specs=in_specs,
        out_specs=out_specs,
        out_shape=out_shape,
        compiler_params=_cparams(("parallel",)),
        name="in_proj",
    )(*args)


def _conv3(x, hl, hr, w):
    rows = _iota(x.shape, 0)
    prev = jnp.where(rows == 0, hl, pltpu.roll(x, 1, axis=0))
    nxt = jnp.where(rows == x.shape[0] - 1, hr, pltpu.roll(x, x.shape[0] - 1, axis=0))
    return prev * w[0:1] + x * w[1:2] + nxt * w[2:3]


def _cummax_rows(x, reverse):
    n = x.shape[0]
    rows = _iota(x.shape, 0)
    s = 1
    while s < n:
        if reverse:
            sh = jnp.where(rows < n - s, pltpu.roll(x, n - s, axis=0), -jnp.inf)
        else:
            sh = jnp.where(rows >= s, pltpu.roll(x, s, axis=0), -jnp.inf)
        x = jnp.maximum(x, sh)
        s *= 2
    return x


def _ml_pair(sb, fwd, bwd, c_s, m_s):
    ll = ML_L
    nch = TB // ll
    dq = N_HEADS * ML_DH
    rows = _iota((ll, ll), 0)
    cols = _iota((ll, ll), 1)
    causal = [cols <= rows, cols >= rows]
    tri = [jnp.where(m, 1.0, 0.0).astype(BF16) for m in causal]
    ones = jnp.ones((ll, ML_DH), BF16)
    qk = [fwd[0][sb], bwd[0][sb]]
    v = [fwd[1][sb], bwd[1][sb]]
    g = [fwd[2][sb], bwd[2][sb]]
    outs = [fwd[3], bwd[3]]
    units = [(d, h) for d in range(2) for h in range(N_HEADS)]
    sidx = lambda d, h: (2 * sb + d) * N_HEADS + h
    cx = {(d, h): c_s[sidx(d, h)] for d, h in units}
    m_row = [m_s[2 * sb], m_s[2 * sb + 1]]
    for step in range(nch):
        r0 = [step * ll, (nch - 1 - step) * ll]
        alpha, a_in, em, e_w, ut, a_old, a_new = [], [], [], [], [], [], []
        for d in range(2):
            gc = g[d][r0[d]:r0[d] + ll, :]
            bc = _dot_exact_l(tri[d], gc)
            u = pltpu.roll(gc, 4, axis=1) - bc
            cm = _cummax_rows(u, bool(d))
            neg_alpha = jnp.maximum(m_row[d], cm)
            alpha.append(-neg_alpha)
            a_in.append(jnp.exp(m_row[d] - neg_alpha))
            em.append(jnp.exp(-neg_alpha - bc))
            last = slice(0, 1) if d else slice(ll - 1, ll)
            cm_end = cm[last, :]
            bend = bc[last, :]
            e_w.append(jnp.exp(u - cm_end))
            ut.append(u.T)
            m_kv = bend + cm_end
            m_new = jnp.maximum(bend + m_row[d], m_kv)
            a_old.append(jnp.exp(bend + m_row[d] - m_new))
            a_new.append(jnp.exp(m_kv - m_new))
            m_row[d] = m_new
        lane = lambda d, h: 8 * d + 4 + h
        for g0 in range(0, len(units), ML_STAGE_UNITS):
            grp = units[g0:g0 + ML_STAGE_UNITS]
            qb = {(d, h): qk[d][r0[d]:r0[d] + ll, h * ML_DH:(h + 1) * ML_DH] for d, h in grp}
            kb = {(d, h): qk[d][r0[d]:r0[d] + ll, dq + h * ML_DH:dq + (h + 1) * ML_DH] for d, h in grp}
            vh = {(d, h): v[d][r0[d]:r0[d] + ll, h * ML_DH:(h + 1) * ML_DH] for d, h in grp}
            sc = {u: _mm(qb[u], kb[u], _NT) for u in grp}
            inter = {u: _mm(qb[u], cx[u].astype(BF16), _NT) for u in grp}
            sbf = {}
            for d, h in grp:
                c = lane(d, h)
                arg = jnp.where(causal[d], alpha[d][:, c:c + 1] + ut[d][c:c + 1, :], -jnp.inf)
                sbf[(d, h)] = (sc[(d, h)] * jnp.exp(arg)).astype(BF16)
            ckv = {}
            for d, h in grp:
                c = lane(d, h)
                ew = e_w[d][:, c:c + 1]
                ev = jnp.concatenate([(ew * vh[(d, h)]).astype(BF16),
                                      jnp.broadcast_to(ew, (ll, ML_DH)).astype(BF16)], axis=1)
                ckv[(d, h)] = _mm(ev, kb[(d, h)], _TN)
            for d, h in grp:
                c = lane(d, h)
                nd = (_mm(sbf[(d, h)], jnp.concatenate([vh[(d, h)].astype(BF16), ones], axis=1))
                      + a_in[d][:, c:c + 1] * inter[(d, h)])
                den = jnp.maximum(jnp.abs(nd[:, ML_DH:]), em[d][:, c:c + 1])
                outs[d][sb, r0[d]:r0[d] + ll, h * ML_DH:(h + 1) * ML_DH] = (nd[:, :ML_DH] / den).astype(BF16)
            for d, h in grp:
                c = lane(d, h)
                cx[(d, h)] = a_old[d][:, c:c + 1] * cx[(d, h)] + a_new[d][:, c:c + 1] * ckv[(d, h)]
    for d, h in units:
        c_s[sidx(d, h)] = cx[(d, h)]
    m_s[2 * sb] = m_row[0]
    m_s[2 * sb + 1] = m_row[1]


def _gla_pair(sb, fwd, bwd, w2_ref, ba_ref, s_s):
    ll = GLA_L
    nch = TB // ll
    dkw = N_HEADS * GLA_DK
    rows = _iota((ll, ll), 0)
    cols = _iota((ll, ll), 1)
    causal = [cols <= rows, cols >= rows]
    tri = [jnp.where(m, 1.0, 0.0).astype(BF16) for m in causal]
    qk = [fwd[0][sb], bwd[0][sb]]
    v = [fwd[1][sb], bwd[1][sb]]
    outs = [fwd[3], bwd[3]]
    la = [_log_sigmoid(_dot3(r[2][sb], w2_ref[0, d]) + ba_ref[0, d]) * (1.0 / GLA_GATE_TAU)
          for d, r in enumerate((fwd, bwd))]
    units = [(d, h) for d in range(2) for h in range(N_HEADS)]
    sidx = lambda d, h: (2 * sb + d) * N_HEADS + h
    st = {(d, h): s_s[sidx(d, h)] for d, h in units}
    for step in range(nch):
        r0 = [step * ll, (nch - 1 - step) * ll]
        qi, qt, kt, kd, e_end = [], [], [], [], []
        for d in range(2):
            bcum = _dot_exact_l(tri[d], la[d][r0[d]:r0[d] + ll, :])
            ref = bcum[ll // 2:ll // 2 + 1, :]
            bend = bcum[0:1, :] if d else bcum[ll - 1:ll, :]
            q = qk[d][r0[d]:r0[d] + ll, 0:dkw] * (GLA_DK ** -0.5)
            k = qk[d][r0[d]:r0[d] + ll, dkw:2 * dkw]
            qi.append((q * jnp.exp(bcum)).astype(BF16))
            qt.append((q * jnp.exp(bcum - ref)).astype(BF16))
            ktd = k * jnp.exp(ref - bcum)
            kt.append(ktd.astype(BF16))
            kd.append((ktd * jnp.exp(bend - ref)).astype(BF16))
            e_end.append(jnp.exp(bend))
        hs = lambda a, h, w: a[:, h * w:(h + 1) * w]
        vb = {(d, h): hs(v[d][r0[d]:r0[d] + ll, :], h, GLA_DV).astype(BF16) for d, h in units}
        att = {(d, h): _mm(hs(qt[d], h, GLA_DK), hs(kt[d], h, GLA_DK), _NT) for d, h in units}
        inter = {(d, h): _mm(hs(qi[d], h, GLA_DK), st[(d, h)].astype(BF16), _NT) for d, h in units}
        attb = {(d, h): jnp.where(causal[d], att[(d, h)], 0.0).astype(BF16) for d, h in units}
        kv = {(d, h): _mm(vb[(d, h)], hs(kd[d], h, GLA_DK), _TN) for d, h in units}
        for d, h in units:
            o = _mm(attb[(d, h)], vb[(d, h)]) + inter[(d, h)]
            outs[d][sb, r0[d]:r0[d] + ll, h * GLA_DV:(h + 1) * GLA_DV] = o.astype(BF16)
        st = {(d, h): st[(d, h)] * hs(e_end[d], h, GLA_DK) + kv[(d, h)] for d, h in units}
    for d, h in units:
        s_s[sidx(d, h)] = st[(d, h)]


def _mix2_kernel(qcf, mvf, psf, gqf, gvf, qcb, mvb, psb, gqb, gvb, w2, ba,
                 mf_ref, mb_ref, gf_ref, gb_ref, c_s, m_s, s_s):
    @pl.when(pl.program_id(0) == 0)
    def _():
        c_s[...] = jnp.zeros_like(c_s)
        m_s[...] = jnp.zeros_like(m_s)
        s_s[...] = jnp.zeros_like(s_s)

    for sb in range(qcf.shape[0]):
        _ml_pair(sb, (qcf, mvf, psf, mf_ref), (qcb, mvb, psb, mb_ref), c_s, m_s)
        _gla_pair(sb, (gqf, gvf, psf, gf_ref), (gqb, gvb, psb, gb_ref), w2, ba, s_s)


def _mixers2(qc, pb, ps, layer, w2e, ba, ncb):
    b, t, _ = pb.shape
    nblk = t // TB
    fwd = lambda i: i
    bwd = lambda i: _rev_block(i, ncb, nblk)

    def dspecs(blk):
        col = lambda cb: pl.BlockSpec((b, TB, 512), lambda i: (0, blk(i), cb))
        return [pl.BlockSpec((b, TB, qc.shape[2]), lambda i: (0, blk(i), 0)), col(PB_ML_V),
                pl.BlockSpec((b, TB, LANES), lambda i: (0, blk(i), 0)), col(PB_GLA_QK), col(PB_GLA_V)]

    specs = dspecs(fwd) + dspecs(bwd) + [_layer_spec(layer, w2e), _layer_spec(layer, ba)]
    ns = 2 * N_HEADS * b
    ofwd = pl.BlockSpec((b, TB, 512), lambda i: (0, i, 0))
    obwd = pl.BlockSpec((b, TB, 512), lambda i: (0, bwd(i), 0))
    return pl.pallas_call(
        _mix2_kernel,
        grid=(nblk,),
        in_specs=specs,
        out_specs=[ofwd, obwd, ofwd, obwd],
        out_shape=[jax.ShapeDtypeStruct((b, t, 512), BF16)] * 4,
        scratch_shapes=[pltpu.VMEM((ns, 2 * ML_DH, ML_DH), F32),
                        pltpu.VMEM((2 * b, 1, LANES), F32),
                        pltpu.VMEM((ns, GLA_DV, GLA_DK), F32)],
        compiler_params=_cparams(("arbitrary",)),
        name="mixers",
    )(*([qc, pb, ps, pb, pb] * 2), w2e, ba)


def _head_norm(x, g):
    outs = []
    for h in range(N_HEADS):
        seg = x[:, h * 128:(h + 1) * 128]
        outs.append(seg * lax.rsqrt(jnp.mean(seg * seg, axis=-1, keepdims=True) + EPS))
    return jnp.concatenate(outs, axis=-1) * g


def _out_kernel(hf, hb, of, ob, og, rg, x_ref, mod_ref, gml, ggla, wo, gpost, gpre, wrt,
                x1_ref, h2e_ref, aff_ref):
    ne = N_EXPERTS
    nb, _, d = x_ref.shape
    rh = LANES
    groups = [(b, slice(k * rh, (k + 1) * rh)) for b in range(nb) for k in range(TB // rh)]
    f32 = lambda ref, b, r: ref[b, r, :].astype(F32)
    y = [jnp.concatenate([_head_norm(f32(hf, b, r) + f32(hb, b, r), gml[0]) * _sigmoid(og[b, r, :]),
                          _head_norm(f32(of, b, r) + f32(ob, b, r), ggla[0]) * _silu(rg[b, r, :])],
                         axis=-1).astype(BF16) for b, r in groups]
    y2 = _mm(jnp.concatenate(y, axis=0), wo[0])
    gate1 = [mod_ref[0, b, 0, 2:3, :] * gpost[0] for b in range(nb)]
    gain2 = [gpre[0] * (1.0 + mod_ref[0, b, 0, 4:5, :]) for b in range(nb)]
    x1 = [x_ref[b, r, :] + _rms(y2[g * rh:(g + 1) * rh, :], gate1[b]) for g, (b, r) in enumerate(groups)]
    for (b, r), x1k in zip(groups, x1):
        x1_ref[b, r, :] = x1k
    h2 = [_rms(x1k, gain2[b]) + mod_ref[0, b, 0, 3:4, :] for (b, r), x1k in zip(groups, x1)]
    lt = _dot3(wrt[0], jnp.concatenate(h2, axis=0), _NT)
    for g, ((b, r), h2k) in enumerate(zip(groups, h2)):
        ltk = lt[:, g * rh:(g + 1) * rh]
        ext = jnp.exp(ltk - jnp.max(ltk, axis=0, keepdims=True))
        affk = ext / jnp.sum(ext, axis=0, keepdims=True)
        aff_ref[b, :, r] = affk
        afft = jnp.concatenate([affk, jnp.zeros((LANES - ne, rh), F32)], axis=0).T
        a_hi, a_mid, a_lo = _split3(afft)
        pieces = (a_hi.astype(F32) + pltpu.roll(a_mid.astype(F32), ne, axis=1)
                  + pltpu.roll(a_lo.astype(F32), 2 * ne, axis=1))
        h2e_ref[b, r, 0:d] = h2k.astype(BF16)
        h2e_ref[b, r, d:d + LANES] = pieces.astype(BF16)


def _out_proj(hf, hb, of, ob, pb, xa, layer, mod, gml, ggla, wo, gpost, gpre, wrt, ncb):
    b, t, d = xa.shape
    de = d + LANES
    tile = lambda w, cb: pl.BlockSpec((b, TB, w), lambda i: (0, i, cb))
    return pl.pallas_call(
        _out_kernel,
        grid=(t // TB,),
        in_specs=[tile(512, 0), tile(512, 0), tile(512, 0), tile(512, 0), tile(512, PB_ML_O), tile(512, PB_GLA_R),
                  tile(d, 0), _mod_spec(layer, mod, ncb)]
                 + [_layer_spec(layer, a) for a in (gml, ggla, wo, gpost, gpre, wrt)],
        out_specs=[tile(d, 0), tile(de, 0),
                   pl.BlockSpec((b, N_EXPERTS, TB), lambda i: (0, 0, i))],
        out_shape=[jax.ShapeDtypeStruct((b, t, d), F32),
                   jax.ShapeDtypeStruct((b, t, de), BF16),
                   jax.ShapeDtypeStruct((b, N_EXPERTS, t), F32)],
        compiler_params=_cparams(("parallel",)),
        name="out_proj",
    )(hf, hb, of, ob, pb, pb, xa, mod, gml, ggla, wo, gpost, gpre, wrt)


def _cumsum_blocks(x, r):
    n = x.shape[0]
    xb = x.astype(BF16)
    li = _iota((LANES, LANES), 0)
    lj = _iota((LANES, LANES), 1)
    upper = jnp.where(li <= lj, 1.0, 0.0).astype(BF16)
    ones = jnp.ones((LANES, LANES), BF16)
    inrow = _mm(xb, upper)
    tot = _mm(xb, ones)
    ri = _iota((n, n), 0)
    rj = _iota((n, n), 1)
    same = (ri // r) == (rj // r)
    strict = jnp.where(jnp.logical_and(same, rj < ri), 1.0, 0.0).astype(BF16)
    off = _mm(strict, tot.astype(BF16))
    return inrow + off, off


def _select(aff, r, cap, base_slot):
    ne = N_EXPERTS
    n = ne * r
    aff3 = aff.reshape(ne, r, LANES)
    capf = jnp.float32(cap)

    def body(k, prefix):
        cand = prefix | (jnp.int32(1) << (30 - k))
        candf = lax.bitcast_convert_type(cand, F32)
        cnt = jnp.sum(jnp.where(aff3 >= candf, 1.0, 0.0), axis=(1, 2), keepdims=True)
        return jnp.where(cnt >= capf, cand, prefix)

    thr = lax.bitcast_convert_type(lax.fori_loop(0, 31, body, jnp.zeros((ne, 1, 1), I32)), F32)
    gt = jnp.where(aff3 > thr, 1.0, 0.0)
    eq = jnp.where(aff3 == thr, 1.0, 0.0)
    need = capf - jnp.sum(gt, axis=(1, 2), keepdims=True)
    eq2 = eq.reshape(n, LANES)
    cs_eq, _ = _cumsum_blocks(eq2, r)
    eq_rank = (cs_eq - eq2).reshape(ne, r, LANES)
    sel = (gt + eq * jnp.where(eq_rank < need, 1.0, 0.0)).reshape(n, LANES)
    cs, off = _cumsum_blocks(sel, r)
    slot = jnp.where(sel > 0.5, cs - 1.0 + base_slot, UNSEL)
    return slot, off


def _sel_kernel(rc, rl, cap_c, cap_l, *refs):
    if rc:
        affc, affl, slc, offc, sll, offl = refs
        slc[0], offc[0] = _select(affc[0], rc, cap_c, float(cap_l))
    else:
        affl, sll, offl = refs
    sll[0], offl[0] = _select(affl[0], rl, cap_l, 0.0)


def _route(aff_c, aff_l, cap_c, cap_l):
    b = aff_l.shape[0]
    ne = N_EXPERTS
    rl = aff_l.shape[1] // ne
    rc = aff_c.shape[1] // ne if aff_c is not None else 0
    args = ([aff_c] if rc else []) + [aff_l]
    in_specs, out_shape, out_specs = [], [], []
    for a in args:
        spec = pl.BlockSpec((1,) + a.shape[1:], lambda bi: (bi, 0, 0))
        in_specs.append(spec)
        out_shape += [jax.ShapeDtypeStruct(a.shape, F32)] * 2
        out_specs += [spec, spec]
    return pl.pallas_call(
        functools.partial(_sel_kernel, rc, rl, cap_c, cap_l),
        grid=(b,),
        in_specs=in_specs,
        out_specs=out_specs,
        out_shape=out_shape,
        compiler_params=_cparams(("parallel",)),
        name="route",
    )(*args)


def _window(lo_ref, base, e, m_rows):
    lo_e = lo_ref[base + e]
    hi_e = lo_ref[base + N_EXPERTS + e]
    a_e = jnp.minimum((lo_e // ROW_ALIGN) * ROW_ALIGN, m_rows - WIN)
    return a_e, hi_e


def _disp_kernel(m_rows, ng, lo_ref, h_ref, slot_ref, x_ref):
    ne = N_EXPERTS
    gi = pl.program_id(1)
    i = pl.program_id(2)
    base = (pl.program_id(0) * pl.num_programs(2) + i) * (2 * ne)

    @pl.when(i == 0)
    def _():
        x_ref[...] = jnp.zeros_like(x_ref)

    h = h_ref[0]
    sl = slot_ref[0]
    sub = _iota((WIN, TB), 0).astype(F32)
    wins = [_window(lo_ref, base, gi * ng + k, m_rows) for k in range(ng)]

    def onehot(k, first):
        a_r = jnp.minimum(first, m_rows - WIN)
        srow = sl[k:k + 1, :]
        hit = jnp.logical_and(srow - a_r.astype(F32) == sub, srow >= first.astype(F32))
        return jnp.where(hit, 1.0, 0.0).astype(BF16), a_r

    def add_rows(k, a_r, g):
        rows = pl.ds(pl.multiple_of(a_r, ROW_ALIGN), WIN)
        x_ref[0, k, rows, :] = x_ref[0, k, rows, :] + g

    sel = [onehot(k, a_e) for k, (a_e, _) in enumerate(wins)]
    g = _mm(jnp.concatenate([w for w, _ in sel], axis=0), h).astype(BF16)
    for k, (_, a_r) in enumerate(sel):
        add_rows(k, a_r, g[k * WIN:(k + 1) * WIN, :])

    for k, (a_e, hi_e) in enumerate(wins):
        @pl.when(hi_e - a_e > WIN)
        def _(k=k, a_e=a_e, hi_e=hi_e):
            def more(rd, carry):
                w, a_r = onehot(k, a_e + rd * WIN)
                add_rows(k, a_r, _mm(w, h).astype(BF16))
                return carry

            lax.fori_loop(1, (hi_e - a_e + WIN - 1) // WIN, more, 0)


def _dispatch(lohi, h2e, slots, m_rows, t0, nt):
    b, t, de = h2e.shape
    ne = N_EXPERTS
    ng = 8
    grid_spec = pltpu.PrefetchScalarGridSpec(
        num_scalar_prefetch=1,
        grid=(b, ne // ng, nt),
        in_specs=[pl.BlockSpec((1, TB, de), lambda bi, gi, i, *_: (bi, i + t0, 0)),
                  pl.BlockSpec((1, ng, TB), lambda bi, gi, i, *_: (bi * (ne // ng) + gi, 0, i + t0))],
        out_specs=pl.BlockSpec((1, ng, m_rows, de), lambda bi, gi, i, *_: (bi, gi, 0, 0)))
    return pl.pallas_call(
        functools.partial(_disp_kernel, m_rows, ng),
        grid_spec=grid_spec,
        out_shape=jax.ShapeDtypeStruct((b, ne, m_rows, de), BF16),
        compiler_params=_cparams(("parallel", "parallel", "arbitrary")),
        name="dispatch",
    )(lohi, h2e, slots.reshape(b * (ne // ng), ng, t))


def _row_chunks(m_rows, cap_l):
    step = min(EXPERT_ROWS, cap_l)
    starts = list(range(0, cap_l, step))
    return [(s, (m_rows - s) if s == starts[-1] else step) for s in starts]


def _exp_kernel(m_rows, cap_l, x_ref, wg_ref, wu_ref, wd_ref, y_ref, wg_s, wu_s, wd_s):
    ei = pl.program_id(0)
    d = wg_ref.shape[2]

    @pl.when(pl.program_id(1) == 0)
    def _():
        wg_s[...] = wg_ref[0, 0].astype(BF16)
        wu_s[...] = wu_ref[0, 0].astype(BF16)
        wd_s[...] = wd_ref[0, 0].astype(BF16)

    for r0, mc in _row_chunks(m_rows, cap_l):
        xs = x_ref[0, 0, r0:r0 + mc, 0:d]
        hid = _silu(_mm(xs, wg_s[...])) * _mm(xs, wu_s[...])
        y = _mm(hid.astype(BF16), wd_s[...])
        pieces = x_ref[0, 0, r0:r0 + mc, d:d + LANES].astype(F32)
        lane = _iota((mc, LANES), 1)
        mine = jnp.logical_and(lane % N_EXPERTS == ei, lane < 3 * N_EXPERTS)
        gate = jnp.sum(jnp.where(mine, pieces, 0.0), axis=1, keepdims=True)
        y_ref[0, 0, r0:r0 + mc, :] = (y * gate).astype(BF16)


def _experts(xin, wg, wu, wd, layer, cap_l):
    b, ne, m_rows, de = xin.shape
    _, _, d, f = wg.shape
    return pl.pallas_call(
        functools.partial(_exp_kernel, m_rows, cap_l),
        grid=(ne, b),
        in_specs=[pl.BlockSpec((1, 1, m_rows, de), lambda ei, bi: (bi, ei, 0, 0)),
                  pl.BlockSpec((1, 1, d, f), lambda ei, bi: (layer, ei, 0, 0)),
                  pl.BlockSpec((1, 1, d, f), lambda ei, bi: (layer, ei, 0, 0)),
                  pl.BlockSpec((1, 1, f, d), lambda ei, bi: (layer, ei, 0, 0))],
        out_specs=pl.BlockSpec((1, 1, m_rows, d), lambda ei, bi: (bi, ei, 0, 0)),
        out_shape=jax.ShapeDtypeStruct((b, ne, m_rows, d), BF16),
        scratch_shapes=[pltpu.VMEM((d, f), BF16), pltpu.VMEM((d, f), BF16), pltpu.VMEM((f, d), BF16)],
        compiler_params=_cparams(("arbitrary", "arbitrary")),
        name="experts",
    )(xin, wg, wu, wd)


def _comb_kernel(m_rows, lo_ref, slot_ref, x1_ref, mod_ref, gpost, y_ref, out_ref, acc_s):
    ne = N_EXPERTS
    i = pl.program_id(1)
    base = (pl.program_id(0) * pl.num_programs(1) + i) * (2 * ne)
    kk = ne * WIN
    sl = slot_ref[0]
    hi = jnp.floor(sl * (1.0 / 32.0))
    lo = sl - hi * 32.0
    col_e = _iota((ne, kk), 1) // WIN
    expand = jnp.where(col_e == _iota((ne, kk), 0), 1.0, 0.0).astype(BF16)
    sx = _mm(hi.astype(BF16), expand, _TN) * 32.0 + _mm(lo.astype(BF16), expand, _TN)
    col = _iota((1, kk), 1)
    jrow = (col % WIN).astype(F32)
    wins = [_window(lo_ref, base, e, m_rows) for e in range(ne)]
    ys = []
    arow = jnp.zeros((1, kk), F32)
    for e, (a_e, _) in enumerate(wins):
        arow = jnp.where(col // WIN == e, a_e.astype(F32), arow)
        ys.append(y_ref[0, e, pl.ds(pl.multiple_of(a_e, ROW_ALIGN), WIN), :])
    w = jnp.where(sx - arow == jrow, 1.0, 0.0).astype(BF16)
    acc_s[...] = _mm(w, jnp.concatenate(ys, axis=0))

    lane = _iota((TB, WIN), 1).astype(F32)
    for e, (a_e, hi_e) in enumerate(wins):
        @pl.when(hi_e - a_e > WIN)
        def _(e=e, a_e=a_e, hi_e=hi_e):
            scol = sx[:, e * WIN:e * WIN + 1]

            def more(rd, carry):
                first = a_e + rd * WIN
                a_r = jnp.minimum(first, m_rows - WIN)
                hit = jnp.logical_and(scol - a_r.astype(F32) == lane, scol >= first.astype(F32))
                ye = y_ref[0, e, pl.ds(pl.multiple_of(a_r, ROW_ALIGN), WIN), :]
                acc_s[...] += _mm(jnp.where(hit, 1.0, 0.0).astype(BF16), ye)
                return carry

            lax.fori_loop(1, (hi_e - a_e + WIN - 1) // WIN, more, 0)

    out_ref[0] = x1_ref[0] + _rms(acc_s[...], mod_ref[0, 0, 0, 5:6, :] * gpost[0])


def _combine(lohi, slots, x1, layer, mod, gpost, y, t0, nt, ncb):
    b, t, d = x1.shape
    ne = N_EXPERTS
    m_rows = y.shape[2]
    r8 = mod.shape[3]
    grid_spec = pltpu.PrefetchScalarGridSpec(
        num_scalar_prefetch=1,
        grid=(b, nt),
        in_specs=[pl.BlockSpec((1, ne, TB), lambda bi, i, *_: (bi, 0, i + t0)),
                  pl.BlockSpec((1, TB, d), lambda bi, i, *_: (bi, i + t0, 0)),
                  pl.BlockSpec((1, 1, 1, r8, d),
                               lambda bi, i, *_: (layer, bi, jnp.where(i + t0 < ncb, 0, 1), 0, 0)),
                  _layer_spec(layer, gpost),
                  pl.BlockSpec((1, ne, m_rows, d), lambda bi, i, *_: (bi, 0, 0, 0),
                               pipeline_mode=pl.Buffered(1))],
        out_specs=pl.BlockSpec((1, TB, d), lambda bi, i, *_: (bi, i, 0)),
        scratch_shapes=[pltpu.VMEM((TB, d), F32)])
    return pl.pallas_call(
        functools.partial(_comb_kernel, m_rows),
        grid_spec=grid_spec,
        out_shape=jax.ShapeDtypeStruct((b, nt * TB, d), F32),
        compiler_params=_cparams(("parallel", "arbitrary")),
        name="combine",
    )(lohi, slots, x1, mod, gpost, y)


def _pos_tables(rows, d):
    quarter = d // 4
    freq = jnp.power(POS_BASE, -jnp.arange(quarter, dtype=F32) / quarter)
    ar = jnp.arange(rows, dtype=F32)[:, None] * freq
    ac = jnp.arange(GRID_W, dtype=F32)[:, None] * freq
    return (jnp.concatenate([jnp.sin(ar), jnp.cos(ar)], axis=-1),
            jnp.concatenate([jnp.sin(ac), jnp.cos(ac)], axis=-1))


def _tile_bounds(off, r, ntile, cap, base):
    b = off.shape[0]
    o = off.reshape(b, N_EXPERTS, r, LANES)[:, :, :, 0]
    lo = o[:, :, ::TB // LANES][:, :, :ntile] + base
    hi = jnp.concatenate([lo[:, :, 1:], jnp.full((b, N_EXPERTS, 1), cap + base, F32)], axis=2)
    return lo, hi


def kernel(x, c, ctx, c_ctx, w_ada, b_ada, g_mix_pre, g_mix_post, g_ffn_pre, g_ffn_post,
           w_in, conv_qk, b_ml_gates, w_gla_a2, b_gla_a, g_ml_norm, g_gla_norm, w_out,
           w_router, w_e_gate, w_e_up, w_e_down):
    bsz, n_tok, d = x.shape
    lc = ctx.shape[1]
    depth = w_in.shape[0]
    ne = N_EXPERTS
    t = lc + n_tok
    ncb = lc // TB
    nblk = t // TB
    assert lc % TB == 0 and n_tok % TB == 0 and d == 1024
    cap_l = EC_FACTOR * n_tok // ne
    cap_c = EC_FACTOR * lc // ne

    assert TB % GRID_W == 0
    pos_r, pos_c = _pos_tables(n_tok // GRID_W, d)
    xa = None

    cc = jnp.zeros((8, d), F32).at[:bsz].set(c).at[bsz].set(c_ctx)
    mods = _ada(cc, w_ada, b_ada)

    wide = jnp.concatenate([w_in[:, :, 0:2048], w_in[:, :, 2064:3600]], axis=2).astype(BF16)
    narrow = jnp.concatenate([w_in[:, :, 2048:2064], w_in[:, :, 3600:3632]], axis=2)
    narrow = jnp.pad(narrow, ((0, 0), (0, 0), (0, LANES - narrow.shape[2])))
    narrow = jnp.concatenate(_split2(narrow), axis=2)
    bias_s = jnp.pad(b_ml_gates, ((0, 0), (0, LANES - b_ml_gates.shape[1])))
    w2e = jnp.zeros((depth, 2, LANES, 256), F32)
    w2e = w2e.at[:, 0, 16:32].set(w_gla_a2[:, 0]).at[:, 1, 32:48].set(w_gla_a2[:, 1])
    m_lat = mods[:, :bsz].reshape(depth, bsz, 1, 6, d)
    m_ctx = jnp.broadcast_to(mods[:, bsz].reshape(depth, 1, 1, 6, d), (depth, bsz, 1, 6, d))
    mod = jnp.pad(jnp.concatenate([m_ctx, m_lat], axis=2), ((0, 0), (0, 0), (0, 0), (0, 2), (0, 0)))
    row = lambda a: a.reshape(depth, 1, -1)
    bias_s, ba = row(bias_s), b_gla_a.reshape(depth, 2, 1, -1)
    g_pre1, g_post1, g_pre2, g_post2 = row(g_mix_pre), row(g_mix_post), row(g_ffn_pre), row(g_ffn_post)
    g_ml, g_gla = row(g_ml_norm), row(g_gla_norm)
    wo = w_out.astype(BF16)
    wrt = w_router.transpose(0, 2, 1)

    for l in range(depth):
        last = l == depth - 1
        if l == 0:
            xa, pb, ps, qc = _in_proj(None, l, mod, g_pre1, wide, narrow, bias_s, conv_qk, ncb,
                                      first=(x, ctx, pos_r, pos_c))
        else:
            pb, ps, qc = _in_proj(xa, l, mod, g_pre1, wide, narrow, bias_s, conv_qk, ncb)
        hf, hb, of, ob = _mixers2(qc, pb, ps, l, w2e, ba, ncb)
        x1, h2e, aff = _out_proj(hf, hb, of, ob, pb, xa, l, mod, g_ml, g_gla, wo, g_post1, g_pre2, wrt, ncb)

        rl = n_tok // LANES
        aff_l = aff[:, :, lc:].reshape(bsz, ne * rl, LANES)
        if last:
            sll, offl = _route(None, aff_l, 0, cap_l)
            slots = jnp.pad(sll.reshape(bsz, ne, n_tok), ((0, 0), (0, 0), (lc, 0)), constant_values=UNSEL)
            lo, hi = _tile_bounds(offl, rl, nblk - ncb, cap_l, 0)
            t0, nt, m_rows = ncb, nblk - ncb, cap_l
        else:
            rc = max(lc // LANES, 8)
            aff_c = aff[:, :, :lc].reshape(bsz, ne, lc // LANES, LANES)
            aff_c = jnp.pad(aff_c, ((0, 0), (0, 0), (0, rc - lc // LANES), (0, 0)), constant_values=-1.0)
            slc, offc, sll, offl = _route(aff_c.reshape(bsz, ne * rc, LANES), aff_l, cap_c, cap_l)
            slots = jnp.concatenate([slc.reshape(bsz, ne, rc * LANES)[:, :, :lc],
                                     sll.reshape(bsz, ne, n_tok)], axis=2)
            lo_l, hi_l = _tile_bounds(offl, rl, nblk - ncb, cap_l, 0)
            lo_c, hi_c = _tile_bounds(offc, rc, ncb, cap_c, cap_l)
            lo = jnp.concatenate([lo_c, lo_l], axis=2)
            hi = jnp.concatenate([hi_c, hi_l], axis=2)
            t0, nt, m_rows = 0, nblk, cap_l + cap_c
        lohi = jnp.concatenate([lo, hi], axis=1).transpose(0, 2, 1).astype(I32).reshape(-1)
        xin = _dispatch(lohi, h2e, slots, m_rows, t0, nt)
        y = _experts(xin, w_e_gate, w_e_up, w_e_down, l, cap_l)
        xa = _combine(lohi, slots, x1, l, mod, g_post2, y, t0, nt, ncb)
    return xa
```

```python
import functools

import jax
import jax.numpy as jnp
from jax import lax
from jax.experimental import pallas as pl
from jax.experimental.pallas import tpu as pltpu

F32 = jnp.float32
BF16 = jnp.bfloat16
I32 = jnp.int32

EPS = 1e-6
GRID_W = 64
POS_BASE = 10000.0
N_HEADS = 4
ML_DH = 128
GLA_DK = 64
GLA_DV = 128
GLA_GATE_TAU = 16.0
N_EXPERTS = 16
EC_FACTOR = 2

LANES = 128
TB = 256
ML_L = 128
GLA_L = 128
ML_STAGE_UNITS = 2
EXPERT_ROWS = 256
WIN = 80
ROW_ALIGN = 16
UNSEL = 2047.0
PB_ML_V, PB_ML_O, PB_GLA_QK, PB_GLA_V, PB_GLA_R = 0, 1, 2, 3, 4
VMEM_LIMIT = 56 * 1024 * 1024


def _cparams(sem):
    return pltpu.CompilerParams(dimension_semantics=sem, vmem_limit_bytes=VMEM_LIMIT)


def _split2(a):
    hi = a.astype(BF16)
    lo = (a - hi.astype(F32)).astype(BF16)
    return hi, lo


def _split3(a):
    hi = a.astype(BF16)
    r = a - hi.astype(F32)
    mid = r.astype(BF16)
    lo = (r - mid.astype(F32)).astype(BF16)
    return hi, mid, lo


_NN = (((1,), (0,)), ((), ()))
_NT = (((1,), (1,)), ((), ()))
_TN = (((0,), (0,)), ((), ()))


def _mm(a, b, dims=_NN):
    return lax.dot_general(a, b, dims, preferred_element_type=F32)


def _dot3(a, b, dims=_NN):
    ah, al = _split2(a)
    bh, bl = _split2(b)
    return _mm(ah, bh, dims) + (_mm(ah, bl, dims) + _mm(al, bh, dims))


def _dot_exact_l(m_bf16, x, dims=_NN):
    hi, mid, lo = _split3(x)
    return _mm(m_bf16, hi, dims) + (_mm(m_bf16, mid, dims) + _mm(m_bf16, lo, dims))


def _rms(x, g):
    return x * lax.rsqrt(jnp.mean(x * x, axis=-1, keepdims=True) + EPS) * g


def _log_sigmoid(x):
    return jnp.minimum(x, 0.0) - jnp.log(1.0 + jnp.exp(-jnp.abs(x)))


def _sigmoid(x):
    return 1.0 / (1.0 + jnp.exp(-x))


def _silu(x):
    return x * _sigmoid(x)


def _iota(shape, dim):
    return lax.broadcasted_iota(I32, shape, dim)


def _rev_block(i, ncb, nblk):
    return jnp.where(i < ncb, ncb - 1 - i, nblk - 1 - (i - ncb))


def _ada_kernel(c_ref, w_ref, b_ref, o_ref):
    a = _silu(c_ref[...])
    o_ref[0] = _dot3(a, w_ref[0]) + b_ref[0]


def _ada(cc, w_ada, b_ada):
    depth, d, n6 = w_ada.shape
    tn = 1536
    return pl.pallas_call(
        _ada_kernel,
        grid=(depth, n6 // tn),
        in_specs=[pl.BlockSpec((8, d), lambda l, j: (0, 0)),
                  pl.BlockSpec((1, d, tn), lambda l, j: (l, 0, j)),
                  pl.BlockSpec((1, 1, tn), lambda l, j: (l, 0, j))],
        out_specs=pl.BlockSpec((1, 8, tn), lambda l, j: (l, 0, j)),
        out_shape=jax.ShapeDtypeStruct((depth, 8, n6), F32),
        compiler_params=_cparams(("parallel", "parallel")),
        name="ada",
    )(cc, w_ada, b_ada.reshape(depth, 1, n6))


def _project(ncb, xs, xps, xns, mod_ref, g_ref, wb_ref, ws_ref, bs_ref, cw_ref, pb_ref, ps_ref, qc_ref):
    i = pl.program_id(0)
    nblk = pl.num_programs(0)
    nb, n = len(xs), xs[0].shape[0]
    rows = [slice(b * n, (b + 1) * n) for b in range(nb)]

    gain = [g_ref[0] * (1.0 + mod_ref[0, b, 0, 1:2, :]) for b in range(nb)]

    def norm(z, b):
        return _rms(z, gain[b]) + mod_ref[0, b, 0, 0:1, :]

    hh = jnp.concatenate([norm(xs[b], b).astype(BF16) for b in range(nb)], axis=0)
    halo = ([norm(xps[b], b).astype(BF16) for b in range(nb)] + [norm(xns[b], b).astype(BF16) for b in range(nb)])
    wq = cw_ref.shape[2]
    qk = _mm(jnp.concatenate([hh] + halo, axis=0), wb_ref[0, :, 0:wq])
    lvalid = jnp.logical_and(i != 0, i != ncb)
    rvalid = jnp.logical_and(i != ncb - 1, i != nblk - 1)
    dq = wq // 2
    for b in range(nb):
        lrow = nb * n + 8 * b + 7
        rrow = nb * n + 8 * nb + 8 * b
        left = jnp.where(lvalid, qk[lrow:lrow + 1, :], 0.0)
        right = jnp.where(rvalid, qk[rrow:rrow + 1, :], 0.0)
        y = _silu(_conv3(qk[rows[b], :], left, right, cw_ref[0]))
        qc_ref[b, :, 0:dq] = (y[:, 0:dq] * (ML_DH ** -0.5)).astype(BF16)
        qc_ref[b, :, dq:] = y[:, dq:].astype(BF16)
    rest = _mm(hh, wb_ref[0, :, wq:])
    pr = _mm(hh, ws_ref[0])
    ps = pr[:, :LANES] + pr[:, LANES:] + bs_ref[0]
    lane = _iota(ps.shape, 1)
    forget = jnp.logical_and(lane % 8 >= N_HEADS, lane < 4 * N_HEADS)
    ps = jnp.where(forget, _log_sigmoid(ps), ps)
    for b in range(nb):
        pb_ref[b] = rest[rows[b], :]
        ps_ref[b] = ps[rows[b], :]


def _in_kernel(ncb, x_ref, xp_ref, xn_ref, mod_ref, g_ref, wb_ref, ws_ref, bs_ref, cw_ref,
               pb_ref, ps_ref, qc_ref):
    nb = x_ref.shape[0]
    _project(ncb, [x_ref[b] for b in range(nb)], [xp_ref[b] for b in range(nb)],
             [xn_ref[b] for b in range(nb)], mod_ref, g_ref, wb_ref, ws_ref, bs_ref, cw_ref,
             pb_ref, ps_ref, qc_ref)


def _in0_kernel(ncb, x_ref, xp_ref, xn_ref, c_ref, cp_ref, cn_ref, pr_ref, prp_ref, prn_ref, pc_ref,
                mod_ref, g_ref, wb_ref, ws_ref, bs_ref, cw_ref, xa_ref, pb_ref, ps_ref, qc_ref):
    i = pl.program_id(0)
    nb = x_ref.shape[0]
    half = pr_ref.shape[2]
    reps = TB // GRID_W
    prow = jnp.concatenate([jnp.broadcast_to(pr_ref[0, k:k + 1, :], (GRID_W, half)) for k in range(reps)], axis=0)
    pcol = jnp.concatenate([pc_ref[...]] * reps, axis=0)
    pos = jnp.concatenate([prow, pcol], axis=1)
    pos_p = jnp.concatenate([prp_ref[0, reps - 1:reps, :], pc_ref[GRID_W - 1:GRID_W, :]], axis=1)
    pos_n = jnp.concatenate([prn_ref[0, 0:1, :], pc_ref[0:1, :]], axis=1)
    is_ctx = i < ncb
    xs, xps, xns = [], [], []
    for b in range(nb):
        xa = jnp.where(is_ctx, c_ref[b], x_ref[b] + pos)
        xa_ref[b] = xa
        xs.append(xa)
        xps.append(jnp.where(is_ctx, cp_ref[b], xp_ref[b] + pos_p))
        xns.append(jnp.where(is_ctx, cn_ref[b], xn_ref[b] + pos_n))
    _project(ncb, xs, xps, xns, mod_ref, g_ref, wb_ref, ws_ref, bs_ref, cw_ref, pb_ref, ps_ref, qc_ref)


def _tile_and_halo_specs(b, rows, d, tile_of):
    r8 = TB // 8
    ntile, last8 = rows // TB, rows // 8 - 1
    tl = lambda i: jnp.clip(tile_of(i), 0, ntile - 1)
    return [pl.BlockSpec((b, TB, d), lambda i: (0, tl(i), 0)),
            pl.BlockSpec((b, 8, d), lambda i: (0, jnp.clip(tl(i) * r8 - 1, 0, last8), 0)),
            pl.BlockSpec((b, 8, d), lambda i: (0, jnp.clip((tl(i) + 1) * r8, 0, last8), 0))]


def _layer_spec(layer, arr):
    shp = arr.shape[1:]
    return pl.BlockSpec((1,) + shp, lambda *_: (layer,) + (0,) * len(shp))


def _mod_spec(layer, mod, ncb, tile_of=lambda i: i):
    _, b, _, r, d = mod.shape
    return pl.BlockSpec((1, b, 1, r, d), lambda i, *_: (layer, 0, jnp.where(tile_of(i) < ncb, 0, 1), 0, 0))


def _in_proj(xa, layer, mod, g, wb, ws, bs, cw, ncb, first=None):
    if first is None:
        b, t, d = xa.shape
    else:
        b, t, d = first[0].shape[0], first[0].shape[1] + first[1].shape[1], first[0].shape[2]
    wq = cw.shape[2]
    nb = wb.shape[2] - wq
    const = lambda shp: pl.BlockSpec(shp, lambda i: tuple(0 for _ in shp))
    tile = lambda w: pl.BlockSpec((b, TB, w), lambda i: (0, i, 0))
    common = [_mod_spec(layer, mod, ncb)] + [_layer_spec(layer, a) for a in (g, wb, ws, bs, cw)]
    out_specs = [tile(nb), tile(LANES), tile(wq)]
    out_shape = [jax.ShapeDtypeStruct((b, t, nb), F32),
                 jax.ShapeDtypeStruct((b, t, LANES), F32),
                 jax.ShapeDtypeStruct((b, t, wq), BF16)]
    if first is None:
        body = functools.partial(_in_kernel, ncb)
        in_specs = _tile_and_halo_specs(b, t, d, lambda i: i) + common
        args = (xa, xa, xa, mod, g, wb, ws, bs, cw)
    else:
        x, ctx, pos_r, pos_c = first
        reps = TB // GRID_W
        ntl = x.shape[1] // TB
        body = functools.partial(_in0_kernel, ncb)
        pr_spec = lambda off: pl.BlockSpec((1, reps, d // 2),
                                           lambda i: (jnp.clip(i - ncb + off, 0, ntl - 1), 0, 0))
        in_specs = (_tile_and_halo_specs(b, x.shape[1], d, lambda i: i - ncb)
                    + _tile_and_halo_specs(b, ctx.shape[1], d, lambda i: i)
                    + [pr_spec(0), pr_spec(-1), pr_spec(1), const((GRID_W, d // 2))] + common)
        out_specs = [tile(d)] + out_specs
        out_shape = [jax.ShapeDtypeStruct((b, t, d), F32)] + out_shape
        pr3 = pos_r.reshape(-1, reps, d // 2)
        args = (x, x, x, ctx, ctx, ctx, pr3, pr3, pr3, pos_c, mod, g, wb, ws, bs, cw)
    return pl.pallas_call(
        body,
        grid=(t // TB,),
        in_specs=in_specs,
        out_specs=out_specs,
        out_shape=out_shape,
        compiler_params=_cparams(("parallel",)),
        name="in_proj",
    )(*args)


def _conv3(x, hl, hr, w):
    rows = _iota(x.shape, 0)
    prev = jnp.where(rows == 0, hl, pltpu.roll(x, 1, axis=0))
    nxt = jnp.where(rows == x.shape[0] - 1, hr, pltpu.roll(x, x.shape[0] - 1, axis=0))
    return prev * w[0:1] + x * w[1:2] + nxt * w[2:3]


def _cummax_rows(x, reverse):
    n = x.shape[0]
    rows = _iota(x.shape, 0)
    s = 1
    while s < n:
        if reverse:
            sh = jnp.where(rows < n - s, pltpu.roll(x, n - s, axis=0), -jnp.inf)
        else:
            sh = jnp.where(rows >= s, pltpu.roll(x, s, axis=0), -jnp.inf)
        x = jnp.maximum(x, sh)
        s *= 2
    return x


def _ml_pair(sb, fwd, bwd, c_s, m_s):
    ll = ML_L
    nch = TB // ll
    dq = N_HEADS * ML_DH
    rows = _iota((ll, ll), 0)
    cols = _iota((ll, ll), 1)
    causal = [cols <= rows, cols >= rows]
    tri = [jnp.where(m, 1.0, 0.0).astype(BF16) for m in causal]
    ones = jnp.ones((ll, ML_DH), BF16)
    qk = [fwd[0][sb], bwd[0][sb]]
    v = [fwd[1][sb], bwd[1][sb]]
    g = [fwd[2][sb], bwd[2][sb]]
    outs = [fwd[3], bwd[3]]
    units = [(d, h) for d in range(2) for h in range(N_HEADS)]
    sidx = lambda d, h: (2 * sb + d) * N_HEADS + h
    cx = {(d, h): c_s[sidx(d, h)] for d, h in units}
    m_row = [m_s[2 * sb], m_s[2 * sb + 1]]
    for step in range(nch):
        r0 = [step * ll, (nch - 1 - step) * ll]
        alpha, a_in, em, e_w, ut, a_old, a_new = [], [], [], [], [], [], []
        for d in range(2):
            gc = g[d][r0[d]:r0[d] + ll, :]
            bc = _dot_exact_l(tri[d], gc)
            u = pltpu.roll(gc, 4, axis=1) - bc
            cm = _cummax_rows(u, bool(d))
            neg_alpha = jnp.maximum(m_row[d], cm)
            alpha.append(-neg_alpha)
            a_in.append(jnp.exp(m_row[d] - neg_alpha))
            em.append(jnp.exp(-neg_alpha - bc))
            last = slice(0, 1) if d else slice(ll - 1, ll)
            cm_end = cm[last, :]
            bend = bc[last, :]
            e_w.append(jnp.exp(u - cm_end))
            ut.append(u.T)
            m_kv = bend + cm_end
            m_new = jnp.maximum(bend + m_row[d], m_kv)
            a_old.append(jnp.exp(bend + m_row[d] - m_new))
            a_new.append(jnp.exp(m_kv - m_new))
            m_row[d] = m_new
        lane = lambda d, h: 8 * d + 4 + h
        for g0 in range(0, len(units), ML_STAGE_UNITS):
            grp = units[g0:g0 + ML_STAGE_UNITS]
            qb = {(d, h): qk[d][r0[d]:r0[d] + ll, h * ML_DH:(h + 1) * ML_DH] for d, h in grp}
            kb = {(d, h): qk[d][r0[d]:r0[d] + ll, dq + h * ML_DH:dq + (h + 1) * ML_DH] for d, h in grp}
            vh = {(d, h): v[d][r0[d]:r0[d] + ll, h * ML_DH:(h + 1) * ML_DH] for d, h in grp}
            sc = {u: _mm(qb[u], kb[u], _NT) for u in grp}
            lhs = {}
            for d, h in grp:
                c = lane(d, h)
                arg = jnp.where(causal[d], alpha[d][:, c:c + 1] + ut[d][c:c + 1, :], -jnp.inf)
                sbf = (sc[(d, h)] * jnp.exp(arg)).astype(BF16)
                aq = (a_in[d][:, c:c + 1] * qb[(d, h)].astype(F32)).astype(BF16)
                lhs[(d, h)] = jnp.concatenate([sbf, aq], axis=1)
            ckv = {}
            for d, h in grp:
                c = lane(d, h)
                ew = e_w[d][:, c:c + 1]
                ev = jnp.concatenate([(ew * vh[(d, h)]).astype(BF16),
                                      jnp.broadcast_to(ew, (ll, ML_DH)).astype(BF16)], axis=1)
                ckv[(d, h)] = _mm(kb[(d, h)], ev, _TN)
            for d, h in grp:
                c = lane(d, h)
                rhs = jnp.concatenate([jnp.concatenate([vh[(d, h)].astype(BF16), ones], axis=1),
                                       cx[(d, h)].astype(BF16)], axis=0)
                nd = _mm(lhs[(d, h)], rhs)
                den = jnp.maximum(jnp.abs(nd[:, ML_DH:]), em[d][:, c:c + 1])
                outs[d][sb, r0[d]:r0[d] + ll, h * ML_DH:(h + 1) * ML_DH] = (nd[:, :ML_DH] / den).astype(BF16)
            for d, h in grp:
                c = lane(d, h)
                cx[(d, h)] = a_old[d][:, c:c + 1] * cx[(d, h)] + a_new[d][:, c:c + 1] * ckv[(d, h)]
    for d, h in units:
        c_s[sidx(d, h)] = cx[(d, h)]
    m_s[2 * sb] = m_row[0]
    m_s[2 * sb + 1] = m_row[1]


def _gla_pair(sb, fwd, bwd, w2_ref, ba_ref, s_s):
    ll = GLA_L
    nch = TB // ll
    dkw = N_HEADS * GLA_DK
    rows = _iota((ll, ll), 0)
    cols = _iota((ll, ll), 1)
    causal = [cols <= rows, cols >= rows]
    tri = [jnp.where(m, 1.0, 0.0).astype(BF16) for m in causal]
    qk = [fwd[0][sb], bwd[0][sb]]
    v = [fwd[1][sb], bwd[1][sb]]
    outs = [fwd[3], bwd[3]]
    la = [_log_sigmoid(_mm(r[2][sb].astype(BF16), w2_ref[0, d]) + ba_ref[0, d]) * (1.0 / GLA_GATE_TAU)
          for d, r in enumerate((fwd, bwd))]
    units = [(d, h) for d in range(2) for h in range(N_HEADS)]
    sidx = lambda d, h: (2 * sb + d) * N_HEADS + h
    st = {(d, h): s_s[sidx(d, h)] for d, h in units}
    for step in range(nch):
        r0 = [step * ll, (nch - 1 - step) * ll]
        qi, qt, kt, kd, e_end = [], [], [], [], []
        for d in range(2):
            lah, lal = _split2(la[d][r0[d]:r0[d] + ll, :])
            bcum = _mm(tri[d], lah) + _mm(tri[d], lal)
            ref = bcum[ll // 2:ll // 2 + 1, :]
            bend = bcum[0:1, :] if d else bcum[ll - 1:ll, :]
            q = qk[d][r0[d]:r0[d] + ll, 0:dkw] * (GLA_DK ** -0.5)
            k = qk[d][r0[d]:r0[d] + ll, dkw:2 * dkw]
            qi.append((q * jnp.exp(bcum)).astype(BF16))
            qt.append((q * jnp.exp(bcum - ref)).astype(BF16))
            ktd = k * jnp.exp(ref - bcum)
            kt.append(ktd.astype(BF16))
            kd.append((ktd * jnp.exp(bend - ref)).astype(BF16))
            e_end.append(jnp.exp(bend))
        hs = lambda a, h, w: a[:, h * w:(h + 1) * w]
        vb = {(d, h): hs(v[d][r0[d]:r0[d] + ll, :], h, GLA_DV).astype(BF16) for d, h in units}
        att = {(d, h): _mm(hs(qt[d], h, GLA_DK), hs(kt[d], h, GLA_DK), _NT) for d, h in units}
        inter = {(d, h): _mm(hs(qi[d], h, GLA_DK), st[(d, h)].astype(BF16), _NT) for d, h in units}
        attb = {(d, h): jnp.where(causal[d], att[(d, h)], 0.0).astype(BF16) for d, h in units}
        kv = {(d, h): _mm(vb[(d, h)], hs(kd[d], h, GLA_DK), _TN) for d, h in units}
        for d, h in units:
            o = _mm(attb[(d, h)], vb[(d, h)]) + inter[(d, h)]
            outs[d][sb, r0[d]:r0[d] + ll, h * GLA_DV:(h + 1) * GLA_DV] = o.astype(BF16)
        st = {(d, h): st[(d, h)] * hs(e_end[d], h, GLA_DK) + kv[(d, h)] for d, h in units}
    for d, h in units:
        s_s[sidx(d, h)] = st[(d, h)]


def _mix2_kernel(qcf, mvf, psf, gqf, gvf, qcb, mvb, psb, gqb, gvb, w2, ba,
                 mf_ref, mb_ref, gf_ref, gb_ref, c_s, m_s, s_s):
    @pl.when(pl.program_id(0) == 0)
    def _():
        c_s[...] = jnp.zeros_like(c_s)
        m_s[...] = jnp.zeros_like(m_s)
        s_s[...] = jnp.zeros_like(s_s)

    for sb in range(qcf.shape[0]):
        _ml_pair(sb, (qcf, mvf, psf, mf_ref), (qcb, mvb, psb, mb_ref), c_s, m_s)
        _gla_pair(sb, (gqf, gvf, psf, gf_ref), (gqb, gvb, psb, gb_ref), w2, ba, s_s)


def _mixers2(qc, pb, ps, layer, w2e, ba, ncb):
    b, t, _ = pb.shape
    nblk = t // TB
    fwd = lambda i: i
    bwd = lambda i: _rev_block(i, ncb, nblk)

    def dspecs(blk):
        col = lambda cb: pl.BlockSpec((b, TB, 512), lambda i: (0, blk(i), cb))
        return [pl.BlockSpec((b, TB, qc.shape[2]), lambda i: (0, blk(i), 0)), col(PB_ML_V),
                pl.BlockSpec((b, TB, LANES), lambda i: (0, blk(i), 0)), col(PB_GLA_QK), col(PB_GLA_V)]

    specs = dspecs(fwd) + dspecs(bwd) + [_layer_spec(layer, w2e), _layer_spec(layer, ba)]
    ns = 2 * N_HEADS * b
    ofwd = pl.BlockSpec((b, TB, 512), lambda i: (0, i, 0))
    obwd = pl.BlockSpec((b, TB, 512), lambda i: (0, bwd(i), 0))
    return pl.pallas_call(
        _mix2_kernel,
        grid=(nblk,),
        in_specs=specs,
        out_specs=[ofwd, obwd, ofwd, obwd],
        out_shape=[jax.ShapeDtypeStruct((b, t, 512), BF16)] * 4,
        scratch_shapes=[pltpu.VMEM((ns, ML_DH, 2 * ML_DH), F32),
                        pltpu.VMEM((2 * b, 1, LANES), F32),
                        pltpu.VMEM((ns, GLA_DV, GLA_DK), F32)],
        compiler_params=_cparams(("arbitrary",)),
        name="mixers",
    )(*([qc, pb, ps, pb, pb] * 2), w2e, ba)


def _head_norm(x, g):
    outs = []
    for h in range(N_HEADS):
        seg = x[:, h * 128:(h + 1) * 128]
        outs.append(seg * lax.rsqrt(jnp.mean(seg * seg, axis=-1, keepdims=True) + EPS))
    return jnp.concatenate(outs, axis=-1) * g


def _out_kernel(hf, hb, of, ob, og, rg, x_ref, mod_ref, gml, ggla, wo, gpost, gpre, wrt,
                x1_ref, h2e_ref, aff_ref):
    ne = N_EXPERTS
    nb, _, d = x_ref.shape
    rh = LANES
    groups = [(b, slice(k * rh, (k + 1) * rh)) for b in range(nb) for k in range(TB // rh)]
    f32 = lambda ref, b, r: ref[b, r, :].astype(F32)
    y = [jnp.concatenate([_head_norm(f32(hf, b, r) + f32(hb, b, r), gml[0]) * _sigmoid(og[b, r, :]),
                          _head_norm(f32(of, b, r) + f32(ob, b, r), ggla[0]) * _silu(rg[b, r, :])],
                         axis=-1).astype(BF16) for b, r in groups]
    y2 = _mm(jnp.concatenate(y, axis=0), wo[0])
    gate1 = [mod_ref[0, b, 0, 2:3, :] * gpost[0] for b in range(nb)]
    gain2 = [gpre[0] * (1.0 + mod_ref[0, b, 0, 4:5, :]) for b in range(nb)]
    x1 = [x_ref[b, r, :] + _rms(y2[g * rh:(g + 1) * rh, :], gate1[b]) for g, (b, r) in enumerate(groups)]
    for (b, r), x1k in zip(groups, x1):
        x1_ref[b, r, :] = x1k
    h2 = [_rms(x1k, gain2[b]) + mod_ref[0, b, 0, 3:4, :] for (b, r), x1k in zip(groups, x1)]
    lt = _dot3(wrt[0], jnp.concatenate(h2, axis=0), _NT)
    for g, ((b, r), h2k) in enumerate(zip(groups, h2)):
        ltk = lt[:, g * rh:(g + 1) * rh]
        ext = jnp.exp(ltk - jnp.max(ltk, axis=0, keepdims=True))
        affk = ext / jnp.sum(ext, axis=0, keepdims=True)
        aff_ref[b, :, r] = affk
        afft = jnp.concatenate([affk, jnp.zeros((LANES - ne, rh), F32)], axis=0).T
        a_hi, a_mid, a_lo = _split3(afft)
        pieces = (a_hi.astype(F32) + pltpu.roll(a_mid.astype(F32), ne, axis=1)
                  + pltpu.roll(a_lo.astype(F32), 2 * ne, axis=1))
        h2e_ref[b, r, 0:d] = h2k.astype(BF16)
        h2e_ref[b, r, d:d + LANES] = pieces.astype(BF16)


def _out_proj(hf, hb, of, ob, pb, xa, layer, mod, gml, ggla, wo, gpost, gpre, wrt, ncb):
    b, t, d = xa.shape
    de = d + LANES
    tile = lambda w, cb: pl.BlockSpec((b, TB, w), lambda i: (0, i, cb))
    return pl.pallas_call(
        _out_kernel,
        grid=(t // TB,),
        in_specs=[tile(512, 0), tile(512, 0), tile(512, 0), tile(512, 0), tile(512, PB_ML_O), tile(512, PB_GLA_R),
                  tile(d, 0), _mod_spec(layer, mod, ncb)]
                 + [_layer_spec(layer, a) for a in (gml, ggla, wo, gpost, gpre, wrt)],
        out_specs=[tile(d, 0), tile(de, 0),
                   pl.BlockSpec((b, N_EXPERTS, TB), lambda i: (0, 0, i))],
        out_shape=[jax.ShapeDtypeStruct((b, t, d), F32),
                   jax.ShapeDtypeStruct((b, t, de), BF16),
                   jax.ShapeDtypeStruct((b, N_EXPERTS, t), F32)],
        compiler_params=_cparams(("parallel",)),
        name="out_proj",
    )(hf, hb, of, ob, pb, pb, xa, mod, gml, ggla, wo, gpost, gpre, wrt)


def _cumsum_blocks(x, r):
    n = x.shape[0]
    xb = x.astype(BF16)
    li = _iota((LANES, LANES), 0)
    lj = _iota((LANES, LANES), 1)
    upper = jnp.where(li <= lj, 1.0, 0.0).astype(BF16)
    ones = jnp.ones((LANES, LANES), BF16)
    inrow = _mm(xb, upper)
    tot = _mm(xb, ones)
    ri = _iota((n, n), 0)
    rj = _iota((n, n), 1)
    same = (ri // r) == (rj // r)
    strict = jnp.where(jnp.logical_and(same, rj < ri), 1.0, 0.0).astype(BF16)
    off = _mm(strict, tot.astype(BF16))
    return inrow + off, off


def _select(aff, r, cap, base_slot):
    ne = N_EXPERTS
    n = ne * r
    aff3 = aff.reshape(ne, r, LANES)
    capf = jnp.float32(cap)

    def body(k, prefix):
        cand = prefix | (jnp.int32(1) << (30 - k))
        candf = lax.bitcast_convert_type(cand, F32)
        cnt = jnp.sum(jnp.where(aff3 >= candf, 1.0, 0.0), axis=(1, 2), keepdims=True)
        return jnp.where(cnt >= capf, cand, prefix)

    thr = lax.bitcast_convert_type(lax.fori_loop(0, 31, body, jnp.zeros((ne, 1, 1), I32)), F32)
    gt = jnp.where(aff3 > thr, 1.0, 0.0)
    eq = jnp.where(aff3 == thr, 1.0, 0.0)
    need = capf - jnp.sum(gt, axis=(1, 2), keepdims=True)
    eq2 = eq.reshape(n, LANES)
    cs_eq, _ = _cumsum_blocks(eq2, r)
    eq_rank = (cs_eq - eq2).reshape(ne, r, LANES)
    sel = (gt + eq * jnp.where(eq_rank < need, 1.0, 0.0)).reshape(n, LANES)
    cs, off = _cumsum_blocks(sel, r)
    slot = jnp.where(sel > 0.5, cs - 1.0 + base_slot, UNSEL)
    return slot, off


def _sel_kernel(rc, rl, cap_c, cap_l, *refs):
    if rc:
        affc, affl, slc, offc, sll, offl = refs
        slc[0], offc[0] = _select(affc[0], rc, cap_c, float(cap_l))
    else:
        affl, sll, offl = refs
    sll[0], offl[0] = _select(affl[0], rl, cap_l, 0.0)


def _route(aff_c, aff_l, cap_c, cap_l):
    b = aff_l.shape[0]
    ne = N_EXPERTS
    rl = aff_l.shape[1] // ne
    rc = aff_c.shape[1] // ne if aff_c is not None else 0
    args = ([aff_c] if rc else []) + [aff_l]
    in_specs, out_shape, out_specs = [], [], []
    for a in args:
        spec = pl.BlockSpec((1,) + a.shape[1:], lambda bi: (bi, 0, 0))
        in_specs.append(spec)
        out_shape += [jax.ShapeDtypeStruct(a.shape, F32)] * 2
        out_specs += [spec, spec]
    return pl.pallas_call(
        functools.partial(_sel_kernel, rc, rl, cap_c, cap_l),
        grid=(b,),
        in_specs=in_specs,
        out_specs=out_specs,
        out_shape=out_shape,
        compiler_params=_cparams(("parallel",)),
        name="route",
    )(*args)


def _window(lo_ref, base, e, m_rows):
    lo_e = lo_ref[base + e]
    hi_e = lo_ref[base + N_EXPERTS + e]
    a_e = jnp.minimum((lo_e // ROW_ALIGN) * ROW_ALIGN, m_rows - WIN)
    return a_e, hi_e


def _disp_kernel(m_rows, ng, lo_ref, h_ref, slot_ref, x_ref):
    ne = N_EXPERTS
    gi = pl.program_id(1)
    i = pl.program_id(2)
    base = (pl.program_id(0) * pl.num_programs(2) + i) * (2 * ne)

    @pl.when(i == 0)
    def _():
        x_ref[...] = jnp.zeros_like(x_ref)

    h = h_ref[0]
    sl = slot_ref[0]
    sub = _iota((WIN, TB), 0).astype(F32)
    wins = [_window(lo_ref, base, gi * ng + k, m_rows) for k in range(ng)]

    def onehot(k, first):
        a_r = jnp.minimum(first, m_rows - WIN)
        srow = sl[k:k + 1, :]
        hit = jnp.logical_and(srow - a_r.astype(F32) == sub, srow >= first.astype(F32))
        return jnp.where(hit, 1.0, 0.0).astype(BF16), a_r

    def add_rows(k, a_r, g):
        rows = pl.ds(pl.multiple_of(a_r, ROW_ALIGN), WIN)
        x_ref[0, k, rows, :] = x_ref[0, k, rows, :] + g

    sel = [onehot(k, a_e) for k, (a_e, _) in enumerate(wins)]
    g = _mm(jnp.concatenate([w for w, _ in sel], axis=0), h).astype(BF16)
    for k, (_, a_r) in enumerate(sel):
        add_rows(k, a_r, g[k * WIN:(k + 1) * WIN, :])

    for k, (a_e, hi_e) in enumerate(wins):
        @pl.when(hi_e - a_e > WIN)
        def _(k=k, a_e=a_e, hi_e=hi_e):
            def more(rd, carry):
                w, a_r = onehot(k, a_e + rd * WIN)
                add_rows(k, a_r, _mm(w, h).astype(BF16))
                return carry

            lax.fori_loop(1, (hi_e - a_e + WIN - 1) // WIN, more, 0)


def _dispatch(lohi, h2e, slots, m_rows, t0, nt):
    b, t, de = h2e.shape
    ne = N_EXPERTS
    ng = 8
    grid_spec = pltpu.PrefetchScalarGridSpec(
        num_scalar_prefetch=1,
        grid=(b, ne // ng, nt),
        in_specs=[pl.BlockSpec((1, TB, de), lambda bi, gi, i, *_: (bi, i + t0, 0)),
                  pl.BlockSpec((1, ng, TB), lambda bi, gi, i, *_: (bi * (ne // ng) + gi, 0, i + t0))],
        out_specs=pl.BlockSpec((1, ng, m_rows, de), lambda bi, gi, i, *_: (bi, gi, 0, 0)))
    return pl.pallas_call(
        functools.partial(_disp_kernel, m_rows, ng),
        grid_spec=grid_spec,
        out_shape=jax.ShapeDtypeStruct((b, ne, m_rows, de), BF16),
        compiler_params=_cparams(("parallel", "parallel", "arbitrary")),
        name="dispatch",
    )(lohi, h2e, slots.reshape(b * (ne // ng), ng, t))


def _row_chunks(m_rows, cap_l):
    step = min(EXPERT_ROWS, cap_l)
    starts = list(range(0, cap_l, step))
    return [(s, (m_rows - s) if s == starts[-1] else step) for s in starts]


def _exp_kernel(m_rows, cap_l, x_ref, wg_ref, wu_ref, wd_ref, y_ref, wg_s, wu_s, wd_s):
    ei = pl.program_id(0)
    d = wg_ref.shape[2]

    @pl.when(pl.program_id(1) == 0)
    def _():
        wg_s[...] = wg_ref[0, 0].astype(BF16)
        wu_s[...] = wu_ref[0, 0].astype(BF16)
        wd_s[...] = wd_ref[0, 0].astype(BF16)

    for r0, mc in _row_chunks(m_rows, cap_l):
        xs = x_ref[0, 0, r0:r0 + mc, 0:d]
        hid = _silu(_mm(xs, wg_s[...])) * _mm(xs, wu_s[...])
        y = _mm(hid.astype(BF16), wd_s[...])
        pieces = x_ref[0, 0, r0:r0 + mc, d:d + LANES].astype(F32)
        lane = _iota((mc, LANES), 1)
        mine = jnp.logical_and(lane % N_EXPERTS == ei, lane < 3 * N_EXPERTS)
        gate = jnp.sum(jnp.where(mine, pieces, 0.0), axis=1, keepdims=True)
        y_ref[0, 0, r0:r0 + mc, :] = (y * gate).astype(BF16)


def _experts(xin, wg, wu, wd, layer, cap_l):
    b, ne, m_rows, de = xin.shape
    _, _, d, f = wg.shape
    return pl.pallas_call(
        functools.partial(_exp_kernel, m_rows, cap_l),
        grid=(ne, b),
        in_specs=[pl.BlockSpec((1, 1, m_rows, de), lambda ei, bi: (bi, ei, 0, 0)),
                  pl.BlockSpec((1, 1, d, f), lambda ei, bi: (layer, ei, 0, 0)),
                  pl.BlockSpec((1, 1, d, f), lambda ei, bi: (layer, ei, 0, 0)),
                  pl.BlockSpec((1, 1, f, d), lambda ei, bi: (layer, ei, 0, 0))],
        out_specs=pl.BlockSpec((1, 1, m_rows, d), lambda ei, bi: (bi, ei, 0, 0)),
        out_shape=jax.ShapeDtypeStruct((b, ne, m_rows, d), BF16),
        scratch_shapes=[pltpu.VMEM((d, f), BF16), pltpu.VMEM((d, f), BF16), pltpu.VMEM((f, d), BF16)],
        compiler_params=_cparams(("arbitrary", "arbitrary")),
        name="experts",
    )(xin, wg, wu, wd)


def _comb_kernel(m_rows, lo_ref, slot_ref, x1_ref, mod_ref, gpost, y_ref, out_ref, acc_s):
    ne = N_EXPERTS
    i = pl.program_id(1)
    base = (pl.program_id(0) * pl.num_programs(1) + i) * (2 * ne)
    kk = ne * WIN
    sl = slot_ref[0]
    hi = jnp.floor(sl * (1.0 / 32.0))
    lo = sl - hi * 32.0
    col_e = _iota((ne, kk), 1) // WIN
    expand = jnp.where(col_e == _iota((ne, kk), 0), 1.0, 0.0).astype(BF16)
    sx = _mm(hi.astype(BF16), expand, _TN) * 32.0 + _mm(lo.astype(BF16), expand, _TN)
    col = _iota((1, kk), 1)
    jrow = (col % WIN).astype(F32)
    wins = [_window(lo_ref, base, e, m_rows) for e in range(ne)]
    ys = []
    arow = jnp.zeros((1, kk), F32)
    for e, (a_e, _) in enumerate(wins):
        arow = jnp.where(col // WIN == e, a_e.astype(F32), arow)
        ys.append(y_ref[0, e, pl.ds(pl.multiple_of(a_e, ROW_ALIGN), WIN), :])
    w = jnp.where(sx - arow == jrow, 1.0, 0.0).astype(BF16)
    acc_s[...] = _mm(w, jnp.concatenate(ys, axis=0))

    lane = _iota((TB, WIN), 1).astype(F32)
    for e, (a_e, hi_e) in enumerate(wins):
        @pl.when(hi_e - a_e > WIN)
        def _(e=e, a_e=a_e, hi_e=hi_e):
            scol = sx[:, e * WIN:e * WIN + 1]

            def more(rd, carry):
                first = a_e + rd * WIN
                a_r = jnp.minimum(first, m_rows - WIN)
                hit = jnp.logical_and(scol - a_r.astype(F32) == lane, scol >= first.astype(F32))
                ye = y_ref[0, e, pl.ds(pl.multiple_of(a_r, ROW_ALIGN), WIN), :]
                acc_s[...] += _mm(jnp.where(hit, 1.0, 0.0).astype(BF16), ye)
                return carry

            lax.fori_loop(1, (hi_e - a_e + WIN - 1) // WIN, more, 0)

    out_ref[0] = x1_ref[0] + _rms(acc_s[...], mod_ref[0, 0, 0, 5:6, :] * gpost[0])


def _combine(lohi, slots, x1, layer, mod, gpost, y, t0, nt, ncb):
    b, t, d = x1.shape
    ne = N_EXPERTS
    m_rows = y.shape[2]
    r8 = mod.shape[3]
    grid_spec = pltpu.PrefetchScalarGridSpec(
        num_scalar_prefetch=1,
        grid=(b, nt),
        in_specs=[pl.BlockSpec((1, ne, TB), lambda bi, i, *_: (bi, 0, i + t0)),
                  pl.BlockSpec((1, TB, d), lambda bi, i, *_: (bi, i + t0, 0)),
                  pl.BlockSpec((1, 1, 1, r8, d),
                               lambda bi, i, *_: (layer, bi, jnp.where(i + t0 < ncb, 0, 1), 0, 0)),
                  _layer_spec(layer, gpost),
                  pl.BlockSpec((1, ne, m_rows, d), lambda bi, i, *_: (bi, 0, 0, 0),
                               pipeline_mode=pl.Buffered(1))],
        out_specs=pl.BlockSpec((1, TB, d), lambda bi, i, *_: (bi, i, 0)),
        scratch_shapes=[pltpu.VMEM((TB, d), F32)])
    return pl.pallas_call(
        functools.partial(_comb_kernel, m_rows),
        grid_spec=grid_spec,
        out_shape=jax.ShapeDtypeStruct((b, nt * TB, d), F32),
        compiler_params=_cparams(("parallel", "arbitrary")),
        name="combine",
    )(lohi, slots, x1, mod, gpost, y)


def _pos_tables(rows, d):
    quarter = d // 4
    freq = jnp.power(POS_BASE, -jnp.arange(quarter, dtype=F32) / quarter)
    ar = jnp.arange(rows, dtype=F32)[:, None] * freq
    ac = jnp.arange(GRID_W, dtype=F32)[:, None] * freq
    return (jnp.concatenate([jnp.sin(ar), jnp.cos(ar)], axis=-1),
            jnp.concatenate([jnp.sin(ac), jnp.cos(ac)], axis=-1))


def _tile_bounds(off, r, ntile, cap, base):
    b = off.shape[0]
    o = off.reshape(b, N_EXPERTS, r, LANES)[:, :, :, 0]
    lo = o[:, :, ::TB // LANES][:, :, :ntile] + base
    hi = jnp.concatenate([lo[:, :, 1:], jnp.full((b, N_EXPERTS, 1), cap + base, F32)], axis=2)
    return lo, hi


def kernel(x, c, ctx, c_ctx, w_ada, b_ada, g_mix_pre, g_mix_post, g_ffn_pre, g_ffn_post,
           w_in, conv_qk, b_ml_gates, w_gla_a2, b_gla_a, g_ml_norm, g_gla_norm, w_out,
           w_router, w_e_gate, w_e_up, w_e_down):
    bsz, n_tok, d = x.shape
    lc = ctx.shape[1]
    depth = w_in.shape[0]
    ne = N_EXPERTS
    t = lc + n_tok
    ncb = lc // TB
    nblk = t // TB
    assert lc % TB == 0 and n_tok % TB == 0 and d == 1024
    cap_l = EC_FACTOR * n_tok // ne
    cap_c = EC_FACTOR * lc // ne

    assert TB % GRID_W == 0
    pos_r, pos_c = _pos_tables(n_tok // GRID_W, d)
    xa = None

    cc = jnp.zeros((8, d), F32).at[:bsz].set(c).at[bsz].set(c_ctx)
    mods = _ada(cc, w_ada, b_ada)

    wide = jnp.concatenate([w_in[:, :, 0:2048], w_in[:, :, 2064:3600]], axis=2).astype(BF16)
    narrow = jnp.concatenate([w_in[:, :, 2048:2064], w_in[:, :, 3600:3632]], axis=2)
    narrow = jnp.pad(narrow, ((0, 0), (0, 0), (0, LANES - narrow.shape[2])))
    narrow = jnp.concatenate(_split2(narrow), axis=2)
    bias_s = jnp.pad(b_ml_gates, ((0, 0), (0, LANES - b_ml_gates.shape[1])))
    w2e = jnp.zeros((depth, 2, LANES, 256), F32)
    w2e = w2e.at[:, 0, 16:32].set(w_gla_a2[:, 0]).at[:, 1, 32:48].set(w_gla_a2[:, 1]).astype(BF16)
    m_lat = mods[:, :bsz].reshape(depth, bsz, 1, 6, d)
    m_ctx = jnp.broadcast_to(mods[:, bsz].reshape(depth, 1, 1, 6, d), (depth, bsz, 1, 6, d))
    mod = jnp.pad(jnp.concatenate([m_ctx, m_lat], axis=2), ((0, 0), (0, 0), (0, 0), (0, 2), (0, 0)))
    row = lambda a: a.reshape(depth, 1, -1)
    bias_s, ba = row(bias_s), b_gla_a.reshape(depth, 2, 1, -1)
    g_pre1, g_post1, g_pre2, g_post2 = row(g_mix_pre), row(g_mix_post), row(g_ffn_pre), row(g_ffn_post)
    g_ml, g_gla = row(g_ml_norm), row(g_gla_norm)
    wo = w_out.astype(BF16)
    wrt = w_router.transpose(0, 2, 1)

    for l in range(depth):
        last = l == depth - 1
        if l == 0:
            xa, pb, ps, qc = _in_proj(None, l, mod, g_pre1, wide, narrow, bias_s, conv_qk, ncb,
                                      first=(x, ctx, pos_r, pos_c))
        else:
            pb, ps, qc = _in_proj(xa, l, mod, g_pre1, wide, narrow, bias_s, conv_qk, ncb)
        hf, hb, of, ob = _mixers2(qc, pb, ps, l, w2e, ba, ncb)
        x1, h2e, aff = _out_proj(hf, hb, of, ob, pb, xa, l, mod, g_ml, g_gla, wo, g_post1, g_pre2, wrt, ncb)

        rl = n_tok // LANES
        aff_l = aff[:, :, lc:].reshape(bsz, ne * rl, LANES)
        if last:
            sll, offl = _route(None, aff_l, 0, cap_l)
            slots = jnp.pad(sll.reshape(bsz, ne, n_tok), ((0, 0), (0, 0), (lc, 0)), constant_values=UNSEL)
            lo, hi = _tile_bounds(offl, rl, nblk - ncb, cap_l, 0)
            t0, nt, m_rows = ncb, nblk - ncb, cap_l
        else:
            rc = max(lc // LANES, 8)
            aff_c = aff[:, :, :lc].reshape(bsz, ne, lc // LANES, LANES)
            aff_c = jnp.pad(aff_c, ((0, 0), (0, 0), (0, rc - lc // LANES), (0, 0)), constant_values=-1.0)
            slc, offc, sll, offl = _route(aff_c.reshape(bsz, ne * rc, LANES), aff_l, cap_c, cap_l)
            slots = jnp.concatenate([slc.reshape(bsz, ne, rc * LANES)[:, :, :lc],
                                     sll.reshape(bsz, ne, n_tok)], axis=2)
            lo_l, hi_l = _tile_bounds(offl, rl, nblk - ncb, cap_l, 0)
            lo_c, hi_c = _tile_bounds(offc, rc, ncb, cap_c, cap_l)
            lo = jnp.concatenate([lo_c, lo_l], axis=2)
            hi = jnp.concatenate([hi_c, hi_l], axis=2)
            t0, nt, m_rows = 0, nblk, cap_l + cap_c
        lohi = jnp.concatenate([lo, hi], axis=1).transpose(0, 2, 1).astype(I32).reshape(-1)
        xin = _dispatch(lohi, h2e, slots, m_rows, t0, nt)
        y = _experts(xin, w_e_gate, w_e_up, w_e_down, l, cap_l)
        xa = _combine(lohi, slots, x1, l, mod, g_post2, y, t0, nt, ncb)
    return xa
```

```python
import functools

import jax
import jax.numpy as jnp
from jax import lax
from jax.experimental import pallas as pl
from jax.experimental.pallas import tpu as pltpu

F32 = jnp.float32
BF16 = jnp.bfloat16
I32 = jnp.int32

EPS = 1e-6
GRID_W = 64
POS_BASE = 10000.0
N_HEADS = 4
ML_DH = 128
GLA_DK = 64
GLA_DV = 128
GLA_GATE_TAU = 16.0
N_EXPERTS = 16
EC_FACTOR = 2

LANES = 128
TB = 256
ML_L = 128
GLA_L = 128
ML_STAGE_UNITS = 2
EXPERT_ROWS = 256
WIN = 80
ROW_ALIGN = 16
UNSEL = 2047.0
PB_ML_V, PB_ML_O, PB_GLA_QK, PB_GLA_V, PB_GLA_R = 0, 1, 2, 3, 4
VMEM_LIMIT = 56 * 1024 * 1024


def _cparams(sem):
    return pltpu.CompilerParams(dimension_semantics=sem, vmem_limit_bytes=VMEM_LIMIT)


def _split2(a):
    hi = a.astype(BF16)
    lo = (a - hi.astype(F32)).astype(BF16)
    return hi, lo


def _split3(a):
    hi = a.astype(BF16)
    r = a - hi.astype(F32)
    mid = r.astype(BF16)
    lo = (r - mid.astype(F32)).astype(BF16)
    return hi, mid, lo


_NN = (((1,), (0,)), ((), ()))
_NT = (((1,), (1,)), ((), ()))
_TN = (((0,), (0,)), ((), ()))


def _mm(a, b, dims=_NN):
    return lax.dot_general(a, b, dims, preferred_element_type=F32)


def _dot_exact_l(m_bf16, x, dims=_NN):
    hi, mid, lo = _split3(x)
    return _mm(m_bf16, hi, dims) + (_mm(m_bf16, mid, dims) + _mm(m_bf16, lo, dims))


def _rms(x, g):
    return x * lax.rsqrt(jnp.mean(x * x, axis=-1, keepdims=True) + EPS) * g


def _log_sigmoid(x):
    return jnp.minimum(x, 0.0) - jnp.log(1.0 + jnp.exp(-jnp.abs(x)))


def _sigmoid(x):
    return 1.0 / (1.0 + jnp.exp(-x))


def _silu(x):
    return x * _sigmoid(x)


def _iota(shape, dim):
    return lax.broadcasted_iota(I32, shape, dim)


def _rev_block(i, ncb, nblk):
    return jnp.where(i < ncb, ncb - 1 - i, nblk - 1 - (i - ncb))


def _ada_kernel(c_ref, w_ref, b_ref, o_ref):
    a_hi, a_lo = _split2(_silu(c_ref[...]))
    w = w_ref[0].astype(BF16)
    o_ref[0] = _mm(a_hi, w) + _mm(a_lo, w) + b_ref[0]


def _ada(cc, w_ada, b_ada):
    depth, d, n6 = w_ada.shape
    tn = 1536
    return pl.pallas_call(
        _ada_kernel,
        grid=(depth, n6 // tn),
        in_specs=[pl.BlockSpec((8, d), lambda l, j: (0, 0)),
                  pl.BlockSpec((1, d, tn), lambda l, j: (l, 0, j)),
                  pl.BlockSpec((1, 1, tn), lambda l, j: (l, 0, j))],
        out_specs=pl.BlockSpec((1, 8, tn), lambda l, j: (l, 0, j)),
        out_shape=jax.ShapeDtypeStruct((depth, 8, n6), F32),
        compiler_params=_cparams(("parallel", "parallel")),
        name="ada",
    )(cc, w_ada, b_ada.reshape(depth, 1, n6))


def _project(ncb, xs, xps, xns, mod_ref, g_ref, wb_ref, ws_ref, bs_ref, cw_ref, pb_ref, ps_ref, qc_ref):
    i = pl.program_id(0)
    nblk = pl.num_programs(0)
    nb, n = len(xs), xs[0].shape[0]
    rows = [slice(b * n, (b + 1) * n) for b in range(nb)]

    gain = [g_ref[0] * (1.0 + mod_ref[0, b, 0, 1:2, :]) for b in range(nb)]

    def norm(z, b):
        return _rms(z, gain[b]) + mod_ref[0, b, 0, 0:1, :]

    hh = jnp.concatenate([norm(xs[b], b).astype(BF16) for b in range(nb)], axis=0)
    halo = ([norm(xps[b], b).astype(BF16) for b in range(nb)] + [norm(xns[b], b).astype(BF16) for b in range(nb)])
    wq = cw_ref.shape[2]
    qk = _mm(jnp.concatenate([hh] + halo, axis=0), wb_ref[0, :, 0:wq])
    lvalid = jnp.logical_and(i != 0, i != ncb)
    rvalid = jnp.logical_and(i != ncb - 1, i != nblk - 1)
    dq = wq // 2
    for b in range(nb):
        lrow = nb * n + 8 * b + 7
        rrow = nb * n + 8 * nb + 8 * b
        left = jnp.where(lvalid, qk[lrow:lrow + 1, :], 0.0)
        right = jnp.where(rvalid, qk[rrow:rrow + 1, :], 0.0)
        y = _silu(_conv3(qk[rows[b], :], left, right, cw_ref[0]))
        qc_ref[b, :, 0:dq] = (y[:, 0:dq] * (ML_DH ** -0.5)).astype(BF16)
        qc_ref[b, :, dq:] = y[:, dq:].astype(BF16)
    rest = _mm(hh, wb_ref[0, :, wq:])
    pr = _mm(hh, ws_ref[0])
    ps = pr[:, :LANES] + pr[:, LANES:] + bs_ref[0]
    lane = _iota(ps.shape, 1)
    forget = jnp.logical_and(lane % 8 >= N_HEADS, lane < 4 * N_HEADS)
    ps = jnp.where(forget, _log_sigmoid(ps), ps)
    for b in range(nb):
        pb_ref[b] = rest[rows[b], :]
        ps_ref[b] = ps[rows[b], :]


def _in_kernel(ncb, x_ref, xp_ref, xn_ref, mod_ref, g_ref, wb_ref, ws_ref, bs_ref, cw_ref,
               pb_ref, ps_ref, qc_ref):
    nb = x_ref.shape[0]
    _project(ncb, [x_ref[b] for b in range(nb)], [xp_ref[b] for b in range(nb)],
             [xn_ref[b] for b in range(nb)], mod_ref, g_ref, wb_ref, ws_ref, bs_ref, cw_ref,
             pb_ref, ps_ref, qc_ref)


def _in0_kernel(ncb, x_ref, xp_ref, xn_ref, c_ref, cp_ref, cn_ref, pr_ref, prp_ref, prn_ref, pc_ref,
                mod_ref, g_ref, wb_ref, ws_ref, bs_ref, cw_ref, xa_ref, pb_ref, ps_ref, qc_ref):
    i = pl.program_id(0)
    nb = x_ref.shape[0]
    half = pr_ref.shape[2]
    reps = TB // GRID_W
    prow = jnp.concatenate([jnp.broadcast_to(pr_ref[0, k:k + 1, :], (GRID_W, half)) for k in range(reps)], axis=0)
    pcol = jnp.concatenate([pc_ref[...]] * reps, axis=0)
    pos = jnp.concatenate([prow, pcol], axis=1)
    pos_p = jnp.concatenate([prp_ref[0, reps - 1:reps, :], pc_ref[GRID_W - 1:GRID_W, :]], axis=1)
    pos_n = jnp.concatenate([prn_ref[0, 0:1, :], pc_ref[0:1, :]], axis=1)
    is_ctx = i < ncb
    xs, xps, xns = [], [], []
    for b in range(nb):
        xa = jnp.where(is_ctx, c_ref[b], x_ref[b] + pos)
        xa_ref[b] = xa
        xs.append(xa)
        xps.append(jnp.where(is_ctx, cp_ref[b], xp_ref[b] + pos_p))
        xns.append(jnp.where(is_ctx, cn_ref[b], xn_ref[b] + pos_n))
    _project(ncb, xs, xps, xns, mod_ref, g_ref, wb_ref, ws_ref, bs_ref, cw_ref, pb_ref, ps_ref, qc_ref)


def _tile_and_halo_specs(b, rows, d, tile_of):
    r8 = TB // 8
    ntile, last8 = rows // TB, rows // 8 - 1
    tl = lambda i: jnp.clip(tile_of(i), 0, ntile - 1)
    return [pl.BlockSpec((b, TB, d), lambda i: (0, tl(i), 0)),
            pl.BlockSpec((b, 8, d), lambda i: (0, jnp.clip(tl(i) * r8 - 1, 0, last8), 0)),
            pl.BlockSpec((b, 8, d), lambda i: (0, jnp.clip((tl(i) + 1) * r8, 0, last8), 0))]


def _layer_spec(layer, arr):
    shp = arr.shape[1:]
    return pl.BlockSpec((1,) + shp, lambda *_: (layer,) + (0,) * len(shp))


def _mod_spec(layer, mod, ncb, tile_of=lambda i: i):
    _, b, _, r, d = mod.shape
    return pl.BlockSpec((1, b, 1, r, d), lambda i, *_: (layer, 0, jnp.where(tile_of(i) < ncb, 0, 1), 0, 0))


def _in_proj(xa, layer, mod, g, wb, ws, bs, cw, ncb, first=None):
    if first is None:
        b, t, d = xa.shape
    else:
        b, t, d = first[0].shape[0], first[0].shape[1] + first[1].shape[1], first[0].shape[2]
    wq = cw.shape[2]
    nb = wb.shape[2] - wq
    const = lambda shp: pl.BlockSpec(shp, lambda i: tuple(0 for _ in shp))
    tile = lambda w: pl.BlockSpec((b, TB, w), lambda i: (0, i, 0))
    common = [_mod_spec(layer, mod, ncb)] + [_layer_spec(layer, a) for a in (g, wb, ws, bs, cw)]
    out_specs = [tile(nb), tile(LANES), tile(wq)]
    out_shape = [jax.ShapeDtypeStruct((b, t, nb), F32),
                 jax.ShapeDtypeStruct((b, t, LANES), F32),
                 jax.ShapeDtypeStruct((b, t, wq), BF16)]
    if first is None:
        body = functools.partial(_in_kernel, ncb)
        in_specs = _tile_and_halo_specs(b, t, d, lambda i: i) + common
        args = (xa, xa, xa, mod, g, wb, ws, bs, cw)
    else:
        x, ctx, pos_r, pos_c = first
        reps = TB // GRID_W
        ntl = x.shape[1] // TB
        body = functools.partial(_in0_kernel, ncb)
        pr_spec = lambda off: pl.BlockSpec((1, reps, d // 2),
                                           lambda i: (jnp.clip(i - ncb + off, 0, ntl - 1), 0, 0))
        in_specs = (_tile_and_halo_specs(b, x.shape[1], d, lambda i: i - ncb)
                    + _tile_and_halo_specs(b, ctx.shape[1], d, lambda i: i)
                    + [pr_spec(0), pr_spec(-1), pr_spec(1), const((GRID_W, d // 2))] + common)
        out_specs = [tile(d)] + out_specs
        out_shape = [jax.ShapeDtypeStruct((b, t, d), F32)] + out_shape
        pr3 = pos_r.reshape(-1, reps, d // 2)
        args = (x, x, x, ctx, ctx, ctx, pr3, pr3, pr3, pos_c, mod, g, wb, ws, bs, cw)
    return pl.pallas_call(
        body,
        grid=(t // TB,),
        in_specs=in_specs,
        out_specs=out_specs,
        out_shape=out_shape,
        compiler_params=_cparams(("parallel",)),
        name="in_proj",
    )(*args)


def _conv3(x, hl, hr, w):
    rows = _iota(x.shape, 0)
    prev = jnp.where(rows == 0, hl, pltpu.roll(x, 1, axis=0))
    nxt = jnp.where(rows == x.shape[0] - 1, hr, pltpu.roll(x, x.shape[0] - 1, axis=0))
    return prev * w[0:1] + x * w[1:2] + nxt * w[2:3]


def _cummax_rows(x, reverse):
    n = x.shape[0]
    rows = _iota(x.shape, 0)
    s = 1
    while s < n:
        if reverse:
            sh = jnp.where(rows < n - s, pltpu.roll(x, n - s, axis=0), -jnp.inf)
        else:
            sh = jnp.where(rows >= s, pltpu.roll(x, s, axis=0), -jnp.inf)
        x = jnp.maximum(x, sh)
        s *= 2
    return x


def _ml_pair(sb, fwd, bwd, c_s, m_s):
    ll = ML_L
    nch = TB // ll
    dq = N_HEADS * ML_DH
    rows = _iota((ll, ll), 0)
    cols = _iota((ll, ll), 1)
    causal = [cols <= rows, cols >= rows]
    tri = [jnp.where(m, 1.0, 0.0).astype(BF16) for m in causal]
    ones = jnp.ones((ll, ML_DH), BF16)
    qk = [fwd[0][sb], bwd[0][sb]]
    v = [fwd[1][sb], bwd[1][sb]]
    g = [fwd[2][sb], bwd[2][sb]]
    outs = [fwd[3], bwd[3]]
    units = [(d, h) for d in range(2) for h in range(N_HEADS)]
    sidx = lambda d, h: (2 * sb + d) * N_HEADS + h
    cx = {(d, h): c_s[sidx(d, h)] for d, h in units}
    m_row = [m_s[2 * sb], m_s[2 * sb + 1]]
    for step in range(nch):
        r0 = [step * ll, (nch - 1 - step) * ll]
        alpha, a_in, em, e_w, ut, a_old, a_new = [], [], [], [], [], [], []
        for d in range(2):
            gc = g[d][r0[d]:r0[d] + ll, :]
            bc = _dot_exact_l(tri[d], gc)
            u = pltpu.roll(gc, 4, axis=1) - bc
            cm = _cummax_rows(u, bool(d))
            neg_alpha = jnp.maximum(m_row[d], cm)
            alpha.append(-neg_alpha)
            a_in.append(jnp.exp(m_row[d] - neg_alpha))
            em.append(jnp.exp(-neg_alpha - bc))
            last = slice(0, 1) if d else slice(ll - 1, ll)
            cm_end = cm[last, :]
            bend = bc[last, :]
            e_w.append(jnp.exp(u - cm_end))
            ut.append(u.T)
            m_kv = bend + cm_end
            m_new = jnp.maximum(bend + m_row[d], m_kv)
            a_old.append(jnp.exp(bend + m_row[d] - m_new))
            a_new.append(jnp.exp(m_kv - m_new))
            m_row[d] = m_new
        lane = lambda d, h: 8 * d + 4 + h
        for g0 in range(0, len(units), ML_STAGE_UNITS):
            grp = units[g0:g0 + ML_STAGE_UNITS]
            qb = {(d, h): qk[d][r0[d]:r0[d] + ll, h * ML_DH:(h + 1) * ML_DH] for d, h in grp}
            kb = {(d, h): qk[d][r0[d]:r0[d] + ll, dq + h * ML_DH:dq + (h + 1) * ML_DH] for d, h in grp}
            vh = {(d, h): v[d][r0[d]:r0[d] + ll, h * ML_DH:(h + 1) * ML_DH] for d, h in grp}
            sc = {u: _mm(qb[u], kb[u], _NT) for u in grp}
            lhs = {}
            for d, h in grp:
                c = lane(d, h)
                arg = jnp.where(causal[d], alpha[d][:, c:c + 1] + ut[d][c:c + 1, :], -jnp.inf)
                sbf = (sc[(d, h)] * jnp.exp(arg)).astype(BF16)
                aq = (a_in[d][:, c:c + 1] * qb[(d, h)].astype(F32)).astype(BF16)
                lhs[(d, h)] = jnp.concatenate([sbf, aq], axis=1)
            ckv = {}
            for d, h in grp:
                c = lane(d, h)
                ew = e_w[d][:, c:c + 1]
                ev = jnp.concatenate([(ew * vh[(d, h)]).astype(BF16),
                                      jnp.broadcast_to(ew, (ll, ML_DH)).astype(BF16)], axis=1)
                ckv[(d, h)] = _mm(kb[(d, h)], ev, _TN)
            for d, h in grp:
                c = lane(d, h)
                rhs = jnp.concatenate([jnp.concatenate([vh[(d, h)].astype(BF16), ones], axis=1),
                                       cx[(d, h)].astype(BF16)], axis=0)
                nd = _mm(lhs[(d, h)], rhs)
                den = jnp.maximum(jnp.abs(nd[:, ML_DH:]), em[d][:, c:c + 1])
                outs[d][sb, r0[d]:r0[d] + ll, h * ML_DH:(h + 1) * ML_DH] = (nd[:, :ML_DH] / den).astype(BF16)
            for d, h in grp:
                c = lane(d, h)
                cx[(d, h)] = a_old[d][:, c:c + 1] * cx[(d, h)] + a_new[d][:, c:c + 1] * ckv[(d, h)]
    for d, h in units:
        c_s[sidx(d, h)] = cx[(d, h)]
    m_s[2 * sb] = m_row[0]
    m_s[2 * sb + 1] = m_row[1]


def _gla_pair(sb, fwd, bwd, w2_ref, ba_ref, s_s):
    ll = GLA_L
    nch = TB // ll
    dkw = N_HEADS * GLA_DK
    rows = _iota((ll, ll), 0)
    cols = _iota((ll, ll), 1)
    causal = [cols <= rows, cols >= rows]
    tri = [jnp.where(m, 1.0, 0.0).astype(BF16) for m in causal]
    qk = [fwd[0][sb], bwd[0][sb]]
    v = [fwd[1][sb], bwd[1][sb]]
    outs = [fwd[3], bwd[3]]
    la = [_log_sigmoid(_mm(r[2][sb].astype(BF16), w2_ref[0, d]) + ba_ref[0, d]) * (1.0 / GLA_GATE_TAU)
          for d, r in enumerate((fwd, bwd))]
    units = [(d, h) for d in range(2) for h in range(N_HEADS)]
    sidx = lambda d, h: (2 * sb + d) * N_HEADS + h
    st = {(d, h): s_s[sidx(d, h)] for d, h in units}
    for step in range(nch):
        r0 = [step * ll, (nch - 1 - step) * ll]
        qi, qt, kt, kd, e_end = [], [], [], [], []
        for d in range(2):
            lah, lal = _split2(la[d][r0[d]:r0[d] + ll, :])
            bcum = _mm(tri[d], lah) + _mm(tri[d], lal)
            ref = bcum[ll // 2:ll // 2 + 1, :]
            bend = bcum[0:1, :] if d else bcum[ll - 1:ll, :]
            q = qk[d][r0[d]:r0[d] + ll, 0:dkw] * (GLA_DK ** -0.5)
            k = qk[d][r0[d]:r0[d] + ll, dkw:2 * dkw]
            qi.append((q * jnp.exp(bcum)).astype(BF16))
            qt.append((q * jnp.exp(bcum - ref)).astype(BF16))
            ktd = k * jnp.exp(ref - bcum)
            kt.append(ktd.astype(BF16))
            kd.append((ktd * jnp.exp(bend - ref)).astype(BF16))
            e_end.append(jnp.exp(bend))
        hs = lambda a, h, w: a[:, h * w:(h + 1) * w]
        vb = {(d, h): hs(v[d][r0[d]:r0[d] + ll, :], h, GLA_DV).astype(BF16) for d, h in units}
        att = {(d, h): _mm(hs(qt[d], h, GLA_DK), hs(kt[d], h, GLA_DK), _NT) for d, h in units}
        inter = {(d, h): _mm(hs(qi[d], h, GLA_DK), st[(d, h)].astype(BF16), _NT) for d, h in units}
        attb = {(d, h): jnp.where(causal[d], att[(d, h)], 0.0).astype(BF16) for d, h in units}
        kv = {(d, h): _mm(vb[(d, h)], hs(kd[d], h, GLA_DK), _TN) for d, h in units}
        for d, h in units:
            o = _mm(attb[(d, h)], vb[(d, h)]) + inter[(d, h)]
            outs[d][sb, r0[d]:r0[d] + ll, h * GLA_DV:(h + 1) * GLA_DV] = o.astype(BF16)
        st = {(d, h): st[(d, h)] * hs(e_end[d], h, GLA_DK) + kv[(d, h)] for d, h in units}
    for d, h in units:
        s_s[sidx(d, h)] = st[(d, h)]


def _mix2_kernel(qcf, mvf, psf, gqf, gvf, qcb, mvb, psb, gqb, gvb, w2, ba,
                 mf_ref, mb_ref, gf_ref, gb_ref, c_s, m_s, s_s):
    @pl.when(pl.program_id(0) == 0)
    def _():
        c_s[...] = jnp.zeros_like(c_s)
        m_s[...] = jnp.zeros_like(m_s)
        s_s[...] = jnp.zeros_like(s_s)

    for sb in range(qcf.shape[0]):
        _ml_pair(sb, (qcf, mvf, psf, mf_ref), (qcb, mvb, psb, mb_ref), c_s, m_s)
        _gla_pair(sb, (gqf, gvf, psf, gf_ref), (gqb, gvb, psb, gb_ref), w2, ba, s_s)


def _mixers2(qc, pb, ps, layer, w2e, ba, ncb):
    b, t, _ = pb.shape
    nblk = t // TB
    fwd = lambda i: i
    bwd = lambda i: _rev_block(i, ncb, nblk)

    def dspecs(blk):
        col = lambda cb: pl.BlockSpec((b, TB, 512), lambda i: (0, blk(i), cb))
        return [pl.BlockSpec((b, TB, qc.shape[2]), lambda i: (0, blk(i), 0)), col(PB_ML_V),
                pl.BlockSpec((b, TB, LANES), lambda i: (0, blk(i), 0)), col(PB_GLA_QK), col(PB_GLA_V)]

    specs = dspecs(fwd) + dspecs(bwd) + [_layer_spec(layer, w2e), _layer_spec(layer, ba)]
    ns = 2 * N_HEADS * b
    ofwd = pl.BlockSpec((b, TB, 512), lambda i: (0, i, 0))
    obwd = pl.BlockSpec((b, TB, 512), lambda i: (0, bwd(i), 0))
    return pl.pallas_call(
        _mix2_kernel,
        grid=(nblk,),
        in_specs=specs,
        out_specs=[ofwd, obwd, ofwd, obwd],
        out_shape=[jax.ShapeDtypeStruct((b, t, 512), BF16)] * 4,
        scratch_shapes=[pltpu.VMEM((ns, ML_DH, 2 * ML_DH), F32),
                        pltpu.VMEM((2 * b, 1, LANES), F32),
                        pltpu.VMEM((ns, GLA_DV, GLA_DK), F32)],
        compiler_params=_cparams(("arbitrary",)),
        name="mixers",
    )(*([qc, pb, ps, pb, pb] * 2), w2e, ba)


def _head_norm(x, g):
    outs = []
    for h in range(N_HEADS):
        seg = x[:, h * 128:(h + 1) * 128]
        outs.append(seg * lax.rsqrt(jnp.mean(seg * seg, axis=-1, keepdims=True) + EPS))
    return jnp.concatenate(outs, axis=-1) * g


def _out_kernel(hf, hb, of, ob, og, rg, x_ref, mod_ref, gml, ggla, wo, gpost, gpre, wrt,
                x1_ref, h2e_ref, aff_ref):
    ne = N_EXPERTS
    nb, _, d = x_ref.shape
    rh = LANES
    groups = [(b, slice(k * rh, (k + 1) * rh)) for b in range(nb) for k in range(TB // rh)]
    f32 = lambda ref, b, r: ref[b, r, :].astype(F32)
    y = [jnp.concatenate([_head_norm(f32(hf, b, r) + f32(hb, b, r), gml[0]) * _sigmoid(og[b, r, :]),
                          _head_norm(f32(of, b, r) + f32(ob, b, r), ggla[0]) * _silu(rg[b, r, :])],
                         axis=-1).astype(BF16) for b, r in groups]
    y2 = _mm(jnp.concatenate(y, axis=0), wo[0])
    gate1 = [mod_ref[0, b, 0, 2:3, :] * gpost[0] for b in range(nb)]
    gain2 = [gpre[0] * (1.0 + mod_ref[0, b, 0, 4:5, :]) for b in range(nb)]
    x1 = [x_ref[b, r, :] + _rms(y2[g * rh:(g + 1) * rh, :], gate1[b]) for g, (b, r) in enumerate(groups)]
    for (b, r), x1k in zip(groups, x1):
        x1_ref[b, r, :] = x1k
    h2 = [(_rms(x1k, gain2[b]) + mod_ref[0, b, 0, 3:4, :]).astype(BF16) for (b, r), x1k in zip(groups, x1)]
    hcat = jnp.concatenate(h2, axis=0)
    w_hi, w_lo = _split2(wrt[0])
    lt = _mm(w_hi, hcat, _NT) + _mm(w_lo, hcat, _NT)
    for g, ((b, r), h2k) in enumerate(zip(groups, h2)):
        ltk = lt[:, g * rh:(g + 1) * rh]
        ext = jnp.exp(ltk - jnp.max(ltk, axis=0, keepdims=True))
        affk = ext / jnp.sum(ext, axis=0, keepdims=True)
        aff_ref[b, :, r] = affk
        afft = jnp.concatenate([affk, jnp.zeros((LANES - ne, rh), F32)], axis=0).T
        a_hi, a_mid, a_lo = _split3(afft)
        pieces = (a_hi.astype(F32) + pltpu.roll(a_mid.astype(F32), ne, axis=1)
                  + pltpu.roll(a_lo.astype(F32), 2 * ne, axis=1))
        h2e_ref[b, r, 0:d] = h2k
        h2e_ref[b, r, d:d + LANES] = pieces.astype(BF16)


def _out_proj(hf, hb, of, ob, pb, xa, layer, mod, gml, ggla, wo, gpost, gpre, wrt, ncb):
    b, t, d = xa.shape
    de = d + LANES
    tile = lambda w, cb: pl.BlockSpec((b, TB, w), lambda i: (0, i, cb))
    return pl.pallas_call(
        _out_kernel,
        grid=(t // TB,),
        in_specs=[tile(512, 0), tile(512, 0), tile(512, 0), tile(512, 0), tile(512, PB_ML_O), tile(512, PB_GLA_R),
                  tile(d, 0), _mod_spec(layer, mod, ncb)]
                 + [_layer_spec(layer, a) for a in (gml, ggla, wo, gpost, gpre, wrt)],
        out_specs=[tile(d, 0), tile(de, 0),
                   pl.BlockSpec((b, N_EXPERTS, TB), lambda i: (0, 0, i))],
        out_shape=[jax.ShapeDtypeStruct((b, t, d), F32),
                   jax.ShapeDtypeStruct((b, t, de), BF16),
                   jax.ShapeDtypeStruct((b, N_EXPERTS, t), F32)],
        compiler_params=_cparams(("parallel",)),
        name="out_proj",
    )(hf, hb, of, ob, pb, pb, xa, mod, gml, ggla, wo, gpost, gpre, wrt)


def _cumsum_blocks(x, r):
    n = x.shape[0]
    xb = x.astype(BF16)
    li = _iota((LANES, LANES), 0)
    lj = _iota((LANES, LANES), 1)
    upper = jnp.where(li <= lj, 1.0, 0.0).astype(BF16)
    ones = jnp.ones((LANES, LANES), BF16)
    inrow = _mm(xb, upper)
    tot = _mm(xb, ones)
    ri = _iota((n, n), 0)
    rj = _iota((n, n), 1)
    same = (ri // r) == (rj // r)
    strict = jnp.where(jnp.logical_and(same, rj < ri), 1.0, 0.0).astype(BF16)
    off = _mm(strict, tot.astype(BF16))
    return inrow + off, off


def _select(aff, r, cap, base_slot):
    ne = N_EXPERTS
    n = ne * r
    aff3 = aff.reshape(ne, r, LANES)
    capf = jnp.float32(cap)

    def body(k, prefix):
        cand = prefix | (jnp.int32(1) << (30 - k))
        candf = lax.bitcast_convert_type(cand, F32)
        cnt = jnp.sum(jnp.where(aff3 >= candf, 1.0, 0.0), axis=(1, 2), keepdims=True)
        return jnp.where(cnt >= capf, cand, prefix)

    thr = lax.bitcast_convert_type(lax.fori_loop(0, 31, body, jnp.zeros((ne, 1, 1), I32)), F32)
    gt = jnp.where(aff3 > thr, 1.0, 0.0)
    eq = jnp.where(aff3 == thr, 1.0, 0.0)
    need = capf - jnp.sum(gt, axis=(1, 2), keepdims=True)
    eq2 = eq.reshape(n, LANES)
    cs_eq, _ = _cumsum_blocks(eq2, r)
    eq_rank = (cs_eq - eq2).reshape(ne, r, LANES)
    sel = (gt + eq * jnp.where(eq_rank < need, 1.0, 0.0)).reshape(n, LANES)
    cs, off = _cumsum_blocks(sel, r)
    slot = jnp.where(sel > 0.5, cs - 1.0 + base_slot, UNSEL)
    return slot, off


def _sel_kernel(rc, rl, cap_c, cap_l, *refs):
    if rc:
        affc, affl, slc, offc, sll, offl = refs
        slc[0], offc[0] = _select(affc[0], rc, cap_c, float(cap_l))
    else:
        affl, sll, offl = refs
    sll[0], offl[0] = _select(affl[0], rl, cap_l, 0.0)


def _route(aff_c, aff_l, cap_c, cap_l):
    b = aff_l.shape[0]
    ne = N_EXPERTS
    rl = aff_l.shape[1] // ne
    rc = aff_c.shape[1] // ne if aff_c is not None else 0
    args = ([aff_c] if rc else []) + [aff_l]
    in_specs, out_shape, out_specs = [], [], []
    for a in args:
        spec = pl.BlockSpec((1,) + a.shape[1:], lambda bi: (bi, 0, 0))
        in_specs.append(spec)
        out_shape += [jax.ShapeDtypeStruct(a.shape, F32)] * 2
        out_specs += [spec, spec]
    return pl.pallas_call(
        functools.partial(_sel_kernel, rc, rl, cap_c, cap_l),
        grid=(b,),
        in_specs=in_specs,
        out_specs=out_specs,
        out_shape=out_shape,
        compiler_params=_cparams(("parallel",)),
        name="route",
    )(*args)


def _window(lo_ref, base, e, m_rows):
    lo_e = lo_ref[base + e]
    hi_e = lo_ref[base + N_EXPERTS + e]
    a_e = jnp.minimum((lo_e // ROW_ALIGN) * ROW_ALIGN, m_rows - WIN)
    return a_e, hi_e


def _disp_kernel(m_rows, ng, lo_ref, h_ref, slot_ref, x_ref):
    ne = N_EXPERTS
    gi = pl.program_id(1)
    i = pl.program_id(2)
    base = (pl.program_id(0) * pl.num_programs(2) + i) * (2 * ne)

    @pl.when(i == 0)
    def _():
        x_ref[...] = jnp.zeros_like(x_ref)

    h = h_ref[0]
    sl = slot_ref[0]
    sub = _iota((WIN, TB), 0).astype(F32)
    wins = [_window(lo_ref, base, gi * ng + k, m_rows) for k in range(ng)]

    def onehot(k, first):
        a_r = jnp.minimum(first, m_rows - WIN)
        srow = sl[k:k + 1, :]
        hit = jnp.logical_and(srow - a_r.astype(F32) == sub, srow >= first.astype(F32))
        return jnp.where(hit, 1.0, 0.0).astype(BF16), a_r

    def add_rows(k, a_r, g):
        rows = pl.ds(pl.multiple_of(a_r, ROW_ALIGN), WIN)
        x_ref[0, k, rows, :] = x_ref[0, k, rows, :] + g

    sel = [onehot(k, a_e) for k, (a_e, _) in enumerate(wins)]
    g = _mm(jnp.concatenate([w for w, _ in sel], axis=0), h).astype(BF16)
    for k, (_, a_r) in enumerate(sel):
        add_rows(k, a_r, g[k * WIN:(k + 1) * WIN, :])

    for k, (a_e, hi_e) in enumerate(wins):
        @pl.when(hi_e - a_e > WIN)
        def _(k=k, a_e=a_e, hi_e=hi_e):
            def more(rd, carry):
                w, a_r = onehot(k, a_e + rd * WIN)
                add_rows(k, a_r, _mm(w, h).astype(BF16))
                return carry

            lax.fori_loop(1, (hi_e - a_e + WIN - 1) // WIN, more, 0)


def _dispatch(lohi, h2e, slots, m_rows, t0, nt):
    b, t, de = h2e.shape
    ne = N_EXPERTS
    ng = 8
    grid_spec = pltpu.PrefetchScalarGridSpec(
        num_scalar_prefetch=1,
        grid=(b, ne // ng, nt),
        in_specs=[pl.BlockSpec((1, TB, de), lambda bi, gi, i, *_: (bi, i + t0, 0)),
                  pl.BlockSpec((1, ng, TB), lambda bi, gi, i, *_: (bi * (ne // ng) + gi, 0, i + t0))],
        out_specs=pl.BlockSpec((1, ng, m_rows, de), lambda bi, gi, i, *_: (bi, gi, 0, 0)))
    return pl.pallas_call(
        functools.partial(_disp_kernel, m_rows, ng),
        grid_spec=grid_spec,
        out_shape=jax.ShapeDtypeStruct((b, ne, m_rows, de), BF16),
        compiler_params=_cparams(("parallel", "parallel", "arbitrary")),
        name="dispatch",
    )(lohi, h2e, slots.reshape(b * (ne // ng), ng, t))


def _row_chunks(m_rows, cap_l):
    step = min(EXPERT_ROWS, cap_l)
    starts = list(range(0, cap_l, step))
    return [(s, (m_rows - s) if s == starts[-1] else step) for s in starts]


def _exp_kernel(m_rows, cap_l, x_ref, wg_ref, wu_ref, wd_ref, y_ref, wg_s, wu_s, wd_s):
    ei = pl.program_id(0)
    d = wg_ref.shape[2]

    @pl.when(pl.program_id(1) == 0)
    def _():
        wg_s[...] = wg_ref[0, 0].astype(BF16)
        wu_s[...] = wu_ref[0, 0].astype(BF16)
        wd_s[...] = wd_ref[0, 0].astype(BF16)

    for r0, mc in _row_chunks(m_rows, cap_l):
        xs = x_ref[0, 0, r0:r0 + mc, 0:d]
        hid = _silu(_mm(xs, wg_s[...])) * _mm(xs, wu_s[...])
        y = _mm(hid.astype(BF16), wd_s[...])
        pieces = x_ref[0, 0, r0:r0 + mc, d:d + LANES].astype(F32)
        lane = _iota((mc, LANES), 1)
        mine = jnp.logical_and(lane % N_EXPERTS == ei, lane < 3 * N_EXPERTS)
        gate = jnp.sum(jnp.where(mine, pieces, 0.0), axis=1, keepdims=True)
        y_ref[0, 0, r0:r0 + mc, :] = (y * gate).astype(BF16)


def _experts(xin, wg, wu, wd, layer, cap_l):
    b, ne, m_rows, de = xin.shape
    _, _, d, f = wg.shape
    return pl.pallas_call(
        functools.partial(_exp_kernel, m_rows, cap_l),
        grid=(ne, b),
        in_specs=[pl.BlockSpec((1, 1, m_rows, de), lambda ei, bi: (bi, ei, 0, 0)),
                  pl.BlockSpec((1, 1, d, f), lambda ei, bi: (layer, ei, 0, 0)),
                  pl.BlockSpec((1, 1, d, f), lambda ei, bi: (layer, ei, 0, 0)),
                  pl.BlockSpec((1, 1, f, d), lambda ei, bi: (layer, ei, 0, 0))],
        out_specs=pl.BlockSpec((1, 1, m_rows, d), lambda ei, bi: (bi, ei, 0, 0)),
        out_shape=jax.ShapeDtypeStruct((b, ne, m_rows, d), BF16),
        scratch_shapes=[pltpu.VMEM((d, f), BF16), pltpu.VMEM((d, f), BF16), pltpu.VMEM((f, d), BF16)],
        compiler_params=_cparams(("arbitrary", "arbitrary")),
        name="experts",
    )(xin, wg, wu, wd)


def _comb_kernel(m_rows, lo_ref, slot_ref, x1_ref, mod_ref, gpost, y_ref, out_ref, acc_s):
    ne = N_EXPERTS
    i = pl.program_id(1)
    base = (pl.program_id(0) * pl.num_programs(1) + i) * (2 * ne)
    kk = ne * WIN
    sl = slot_ref[0]
    hi = jnp.floor(sl * (1.0 / 32.0))
    lo = sl - hi * 32.0
    col_e = _iota((ne, kk), 1) // WIN
    expand = jnp.where(col_e == _iota((ne, kk), 0), 1.0, 0.0).astype(BF16)
    sx = _mm(hi.astype(BF16), expand, _TN) * 32.0 + _mm(lo.astype(BF16), expand, _TN)
    col = _iota((1, kk), 1)
    jrow = (col % WIN).astype(F32)
    wins = [_window(lo_ref, base, e, m_rows) for e in range(ne)]
    ys = []
    arow = jnp.zeros((1, kk), F32)
    for e, (a_e, _) in enumerate(wins):
        arow = jnp.where(col // WIN == e, a_e.astype(F32), arow)
        ys.append(y_ref[0, e, pl.ds(pl.multiple_of(a_e, ROW_ALIGN), WIN), :])
    w = jnp.where(sx - arow == jrow, 1.0, 0.0).astype(BF16)
    acc_s[...] = _mm(w, jnp.concatenate(ys, axis=0))

    lane = _iota((TB, WIN), 1).astype(F32)
    for e, (a_e, hi_e) in enumerate(wins):
        @pl.when(hi_e - a_e > WIN)
        def _(e=e, a_e=a_e, hi_e=hi_e):
            scol = sx[:, e * WIN:e * WIN + 1]

            def more(rd, carry):
                first = a_e + rd * WIN
                a_r = jnp.minimum(first, m_rows - WIN)
                hit = jnp.logical_and(scol - a_r.astype(F32) == lane, scol >= first.astype(F32))
                ye = y_ref[0, e, pl.ds(pl.multiple_of(a_r, ROW_ALIGN), WIN), :]
                acc_s[...] += _mm(jnp.where(hit, 1.0, 0.0).astype(BF16), ye)
                return carry

            lax.fori_loop(1, (hi_e - a_e + WIN - 1) // WIN, more, 0)

    out_ref[0] = x1_ref[0] + _rms(acc_s[...], mod_ref[0, 0, 0, 5:6, :] * gpost[0])


def _combine(lohi, slots, x1, layer, mod, gpost, y, t0, nt, ncb):
    b, t, d = x1.shape
    ne = N_EXPERTS
    m_rows = y.shape[2]
    r8 = mod.shape[3]
    grid_spec = pltpu.PrefetchScalarGridSpec(
        num_scalar_prefetch=1,
        grid=(b, nt),
        in_specs=[pl.BlockSpec((1, ne, TB), lambda bi, i, *_: (bi, 0, i + t0)),
                  pl.BlockSpec((1, TB, d), lambda bi, i, *_: (bi, i + t0, 0)),
                  pl.BlockSpec((1, 1, 1, r8, d),
                               lambda bi, i, *_: (layer, bi, jnp.where(i + t0 < ncb, 0, 1), 0, 0)),
                  _layer_spec(layer, gpost),
                  pl.BlockSpec((1, ne, m_rows, d), lambda bi, i, *_: (bi, 0, 0, 0),
                               pipeline_mode=pl.Buffered(1))],
        out_specs=pl.BlockSpec((1, TB, d), lambda bi, i, *_: (bi, i, 0)),
        scratch_shapes=[pltpu.VMEM((TB, d), F32)])
    return pl.pallas_call(
        functools.partial(_comb_kernel, m_rows),
        grid_spec=grid_spec,
        out_shape=jax.ShapeDtypeStruct((b, nt * TB, d), F32),
        compiler_params=_cparams(("parallel", "arbitrary")),
        name="combine",
    )(lohi, slots, x1, mod, gpost, y)


def _pos_tables(rows, d):
    quarter = d // 4
    freq = jnp.power(POS_BASE, -jnp.arange(quarter, dtype=F32) / quarter)
    ar = jnp.arange(rows, dtype=F32)[:, None] * freq
    ac = jnp.arange(GRID_W, dtype=F32)[:, None] * freq
    return (jnp.concatenate([jnp.sin(ar), jnp.cos(ar)], axis=-1),
            jnp.concatenate([jnp.sin(ac), jnp.cos(ac)], axis=-1))


def _tile_bounds(off, r, ntile, cap, base):
    b = off.shape[0]
    o = off.reshape(b, N_EXPERTS, r, LANES)[:, :, :, 0]
    lo = o[:, :, ::TB // LANES][:, :, :ntile] + base
    hi = jnp.concatenate([lo[:, :, 1:], jnp.full((b, N_EXPERTS, 1), cap + base, F32)], axis=2)
    return lo, hi


def kernel(x, c, ctx, c_ctx, w_ada, b_ada, g_mix_pre, g_mix_post, g_ffn_pre, g_ffn_post,
           w_in, conv_qk, b_ml_gates, w_gla_a2, b_gla_a, g_ml_norm, g_gla_norm, w_out,
           w_router, w_e_gate, w_e_up, w_e_down):
    bsz, n_tok, d = x.shape
    lc = ctx.shape[1]
    depth = w_in.shape[0]
    ne = N_EXPERTS
    t = lc + n_tok
    ncb = lc // TB
    nblk = t // TB
    assert lc % TB == 0 and n_tok % TB == 0 and d == 1024
    cap_l = EC_FACTOR * n_tok // ne
    cap_c = EC_FACTOR * lc // ne

    assert TB % GRID_W == 0
    pos_r, pos_c = _pos_tables(n_tok // GRID_W, d)
    xa = None

    cc = jnp.zeros((8, d), F32).at[:bsz].set(c).at[bsz].set(c_ctx)
    mods = _ada(cc, w_ada, b_ada)

    wide = jnp.concatenate([w_in[:, :, 0:2048], w_in[:, :, 2064:3600]], axis=2).astype(BF16)
    narrow = jnp.concatenate([w_in[:, :, 2048:2064], w_in[:, :, 3600:3632]], axis=2)
    narrow = jnp.pad(narrow, ((0, 0), (0, 0), (0, LANES - narrow.shape[2])))
    narrow = jnp.concatenate(_split2(narrow), axis=2)
    bias_s = jnp.pad(b_ml_gates, ((0, 0), (0, LANES - b_ml_gates.shape[1])))
    w2e = jnp.zeros((depth, 2, LANES, 256), F32)
    w2e = w2e.at[:, 0, 16:32].set(w_gla_a2[:, 0]).at[:, 1, 32:48].set(w_gla_a2[:, 1]).astype(BF16)
    m_lat = mods[:, :bsz].reshape(depth, bsz, 1, 6, d)
    m_ctx = jnp.broadcast_to(mods[:, bsz].reshape(depth, 1, 1, 6, d), (depth, bsz, 1, 6, d))
    mod = jnp.pad(jnp.concatenate([m_ctx, m_lat], axis=2), ((0, 0), (0, 0), (0, 0), (0, 2), (0, 0)))
    row = lambda a: a.reshape(depth, 1, -1)
    bias_s, ba = row(bias_s), b_gla_a.reshape(depth, 2, 1, -1)
    g_pre1, g_post1, g_pre2, g_post2 = row(g_mix_pre), row(g_mix_post), row(g_ffn_pre), row(g_ffn_post)
    g_ml, g_gla = row(g_ml_norm), row(g_gla_norm)
    wo = w_out.astype(BF16)
    wrt = w_router.transpose(0, 2, 1)

    for l in range(depth):
        last = l == depth - 1
        if l == 0:
            xa, pb, ps, qc = _in_proj(None, l, mod, g_pre1, wide, narrow, bias_s, conv_qk, ncb,
                                      first=(x, ctx, pos_r, pos_c))
        else:
            pb, ps, qc = _in_proj(xa, l, mod, g_pre1, wide, narrow, bias_s, conv_qk, ncb)
        hf, hb, of, ob = _mixers2(qc, pb, ps, l, w2e, ba, ncb)
        x1, h2e, aff = _out_proj(hf, hb, of, ob, pb, xa, l, mod, g_ml, g_gla, wo, g_post1, g_pre2, wrt, ncb)

        rl = n_tok // LANES
        aff_l = aff[:, :, lc:].reshape(bsz, ne * rl, LANES)
        if last:
            sll, offl = _route(None, aff_l, 0, cap_l)
            slots = jnp.pad(sll.reshape(bsz, ne, n_tok), ((0, 0), (0, 0), (lc, 0)), constant_values=UNSEL)
            lo, hi = _tile_bounds(offl, rl, nblk - ncb, cap_l, 0)
            t0, nt, m_rows = ncb, nblk - ncb, cap_l
        else:
            rc = max(lc // LANES, 8)
            aff_c = aff[:, :, :lc].reshape(bsz, ne, lc // LANES, LANES)
            aff_c = jnp.pad(aff_c, ((0, 0), (0, 0), (0, rc - lc // LANES), (0, 0)), constant_values=-1.0)
            slc, offc, sll, offl = _route(aff_c.reshape(bsz, ne * rc, LANES), aff_l, cap_c, cap_l)
            slots = jnp.concatenate([slc.reshape(bsz, ne, rc * LANES)[:, :, :lc],
                                     sll.reshape(bsz, ne, n_tok)], axis=2)
            lo_l, hi_l = _tile_bounds(offl, rl, nblk - ncb, cap_l, 0)
            lo_c, hi_c = _tile_bounds(offc, rc, ncb, cap_c, cap_l)
            lo = jnp.concatenate([lo_c, lo_l], axis=2)
            hi = jnp.concatenate([hi_c, hi_l], axis=2)
            t0, nt, m_rows = 0, nblk, cap_l + cap_c
        lohi = jnp.concatenate([lo, hi], axis=1).transpose(0, 2, 1).astype(I32).reshape(-1)
        xin = _dispatch(lohi, h2e, slots, m_rows, t0, nt)
        y = _experts(xin, w_e_gate, w_e_up, w_e_down, l, cap_l)
        xa = _combine(lohi, slots, x1, l, mod, g_post2, y, t0, nt, ncb)
    return xa
```

```python
import functools

import jax
import jax.numpy as jnp
from jax import lax
from jax.experimental import pallas as pl
from jax.experimental.pallas import tpu as pltpu

F32 = jnp.float32
BF16 = jnp.bfloat16
I32 = jnp.int32

EPS = 1e-6
GRID_W = 64
POS_BASE = 10000.0
N_HEADS = 4
ML_DH = 128
GLA_DK = 64
GLA_DV = 128
GLA_GATE_TAU = 16.0
N_EXPERTS = 16
EC_FACTOR = 2

LANES = 128
TB = 256
ML_L = 128
GLA_L = 128
ML_STAGE_UNITS = 2
EXPERT_ROWS = 256
WIN = 80
ROW_ALIGN = 16
UNSEL = 2047.0
PB_ML_V, PB_ML_O, PB_GLA_QK, PB_GLA_V, PB_GLA_R = 0, 1, 2, 3, 4
VMEM_LIMIT = 56 * 1024 * 1024


def _cparams(sem):
    return pltpu.CompilerParams(dimension_semantics=sem, vmem_limit_bytes=VMEM_LIMIT)


def _split2(a):
    hi = a.astype(BF16)
    lo = (a - hi.astype(F32)).astype(BF16)
    return hi, lo


def _split3(a):
    hi = a.astype(BF16)
    r = a - hi.astype(F32)
    mid = r.astype(BF16)
    lo = (r - mid.astype(F32)).astype(BF16)
    return hi, mid, lo


_NN = (((1,), (0,)), ((), ()))
_NT = (((1,), (1,)), ((), ()))
_TN = (((0,), (0,)), ((), ()))


def _mm(a, b, dims=_NN):
    return lax.dot_general(a, b, dims, preferred_element_type=F32)


def _dot3(a, b, dims=_NN):
    ah, al = _split2(a)
    bh, bl = _split2(b)
    return _mm(ah, bh, dims) + (_mm(ah, bl, dims) + _mm(al, bh, dims))


def _dot_exact_l(m_bf16, x, dims=_NN):
    hi, mid, lo = _split3(x)
    return _mm(m_bf16, hi, dims) + (_mm(m_bf16, mid, dims) + _mm(m_bf16, lo, dims))


def _rms(x, g):
    return x * lax.rsqrt(jnp.mean(x * x, axis=-1, keepdims=True) + EPS) * g


def _log_sigmoid(x):
    return jnp.minimum(x, 0.0) - jnp.log(1.0 + jnp.exp(-jnp.abs(x)))


def _sigmoid(x):
    return 1.0 / (1.0 + jnp.exp(-x))


def _silu(x):
    return x * _sigmoid(x)


def _iota(shape, dim):
    return lax.broadcasted_iota(I32, shape, dim)


def _rev_block(i, ncb, nblk):
    return jnp.where(i < ncb, ncb - 1 - i, nblk - 1 - (i - ncb))


def _ada_kernel(c_ref, w_ref, b_ref, o_ref):
    a = _silu(c_ref[...])
    o_ref[0] = _dot3(a, w_ref[0]) + b_ref[0]


def _ada(cc, w_ada, b_ada):
    depth, d, n6 = w_ada.shape
    tn = 1536
    return pl.pallas_call(
        _ada_kernel,
        grid=(depth, n6 // tn),
        in_specs=[pl.BlockSpec((8, d), lambda l, j: (0, 0)),
                  pl.BlockSpec((1, d, tn), lambda l, j: (l, 0, j)),
                  pl.BlockSpec((1, 1, tn), lambda l, j: (l, 0, j))],
        out_specs=pl.BlockSpec((1, 8, tn), lambda l, j: (l, 0, j)),
        out_shape=jax.ShapeDtypeStruct((depth, 8, n6), F32),
        compiler_params=_cparams(("parallel", "parallel")),
        name="ada",
    )(cc, w_ada, b_ada.reshape(depth, 1, n6))


def _project(ncb, xs, xps, xns, mod_ref, g_ref, wb_ref, ws_ref, bs_ref, cw_ref, pb_ref, ps_ref, qc_ref):
    i = pl.program_id(0)
    nblk = pl.num_programs(0)
    nb, n = len(xs), xs[0].shape[0]
    rows = [slice(b * n, (b + 1) * n) for b in range(nb)]

    gain = [g_ref[0] * (1.0 + mod_ref[0, b, 0, 1:2, :]) for b in range(nb)]

    def norm(z, b):
        return _rms(z, gain[b]) + mod_ref[0, b, 0, 0:1, :]

    hh = jnp.concatenate([norm(xs[b], b).astype(BF16) for b in range(nb)], axis=0)
    halo = ([norm(xps[b], b).astype(BF16) for b in range(nb)] + [norm(xns[b], b).astype(BF16) for b in range(nb)])
    wq = cw_ref.shape[2]
    qk = _mm(jnp.concatenate([hh] + halo, axis=0), wb_ref[0, :, 0:wq])
    lvalid = jnp.logical_and(i != 0, i != ncb)
    rvalid = jnp.logical_and(i != ncb - 1, i != nblk - 1)
    dq = wq // 2
    for b in range(nb):
        lrow = nb * n + 8 * b + 7
        rrow = nb * n + 8 * nb + 8 * b
        left = jnp.where(lvalid, qk[lrow:lrow + 1, :], 0.0)
        right = jnp.where(rvalid, qk[rrow:rrow + 1, :], 0.0)
        y = _silu(_conv3(qk[rows[b], :], left, right, cw_ref[0]))
        qc_ref[b, :, 0:dq] = (y[:, 0:dq] * (ML_DH ** -0.5)).astype(BF16)
        qc_ref[b, :, dq:] = y[:, dq:].astype(BF16)
    rest = _mm(hh, wb_ref[0, :, wq:])
    pr = _mm(hh, ws_ref[0])
    ps = pr[:, :LANES] + pr[:, LANES:] + bs_ref[0]
    lane = _iota(ps.shape, 1)
    forget = jnp.logical_and(lane % 8 >= N_HEADS, lane < 4 * N_HEADS)
    ps = jnp.where(forget, _log_sigmoid(ps), ps)
    for b in range(nb):
        pb_ref[b] = rest[rows[b], :]
        ps_ref[b] = ps[rows[b], :]


def _in_kernel(ncb, x_ref, xp_ref, xn_ref, mod_ref, g_ref, wb_ref, ws_ref, bs_ref, cw_ref,
               pb_ref, ps_ref, qc_ref):
    nb = x_ref.shape[0]
    _project(ncb, [x_ref[b] for b in range(nb)], [xp_ref[b] for b in range(nb)],
             [xn_ref[b] for b in range(nb)], mod_ref, g_ref, wb_ref, ws_ref, bs_ref, cw_ref,
             pb_ref, ps_ref, qc_ref)


def _in0_kernel(ncb, x_ref, xp_ref, xn_ref, c_ref, cp_ref, cn_ref, pr_ref, prp_ref, prn_ref, pc_ref,
                mod_ref, g_ref, wb_ref, ws_ref, bs_ref, cw_ref, xa_ref, pb_ref, ps_ref, qc_ref):
    i = pl.program_id(0)
    nb = x_ref.shape[0]
    half = pr_ref.shape[2]
    reps = TB // GRID_W
    prow = jnp.concatenate([jnp.broadcast_to(pr_ref[0, k:k + 1, :], (GRID_W, half)) for k in range(reps)], axis=0)
    pcol = jnp.concatenate([pc_ref[...]] * reps, axis=0)
    pos = jnp.concatenate([prow, pcol], axis=1)
    pos_p = jnp.concatenate([prp_ref[0, reps - 1:reps, :], pc_ref[GRID_W - 1:GRID_W, :]], axis=1)
    pos_n = jnp.concatenate([prn_ref[0, 0:1, :], pc_ref[0:1, :]], axis=1)
    is_ctx = i < ncb
    xs, xps, xns = [], [], []
    for b in range(nb):
        xa = jnp.where(is_ctx, c_ref[b], x_ref[b] + pos)
        xa_ref[b] = xa
        xs.append(xa)
        xps.append(jnp.where(is_ctx, cp_ref[b], xp_ref[b] + pos_p))
        xns.append(jnp.where(is_ctx, cn_ref[b], xn_ref[b] + pos_n))
    _project(ncb, xs, xps, xns, mod_ref, g_ref, wb_ref, ws_ref, bs_ref, cw_ref, pb_ref, ps_ref, qc_ref)


def _tile_and_halo_specs(b, rows, d, tile_of):
    r8 = TB // 8
    ntile, last8 = rows // TB, rows // 8 - 1
    tl = lambda i: jnp.clip(tile_of(i), 0, ntile - 1)
    return [pl.BlockSpec((b, TB, d), lambda i: (0, tl(i), 0)),
            pl.BlockSpec((b, 8, d), lambda i: (0, jnp.clip(tl(i) * r8 - 1, 0, last8), 0)),
            pl.BlockSpec((b, 8, d), lambda i: (0, jnp.clip((tl(i) + 1) * r8, 0, last8), 0))]


def _layer_spec(layer, arr):
    shp = arr.shape[1:]
    return pl.BlockSpec((1,) + shp, lambda *_: (layer,) + (0,) * len(shp))


def _mod_spec(layer, mod, ncb, tile_of=lambda i: i):
    _, b, _, r, d = mod.shape
    return pl.BlockSpec((1, b, 1, r, d), lambda i, *_: (layer, 0, jnp.where(tile_of(i) < ncb, 0, 1), 0, 0))


def _in_proj(xa, layer, mod, g, wb, ws, bs, cw, ncb, first=None):
    if first is None:
        b, t, d = xa.shape
    else:
        b, t, d = first[0].shape[0], first[0].shape[1] + first[1].shape[1], first[0].shape[2]
    wq = cw.shape[2]
    nb = wb.shape[2] - wq
    const = lambda shp: pl.BlockSpec(shp, lambda i: tuple(0 for _ in shp))
    tile = lambda w: pl.BlockSpec((b, TB, w), lambda i: (0, i, 0))
    common = [_mod_spec(layer, mod, ncb)] + [_layer_spec(layer, a) for a in (g, wb, ws, bs, cw)]
    out_specs = [tile(nb), tile(LANES), tile(wq)]
    out_shape = [jax.ShapeDtypeStruct((b, t, nb), F32),
                 jax.ShapeDtypeStruct((b, t, LANES), F32),
                 jax.ShapeDtypeStruct((b, t, wq), BF16)]
    if first is None:
        body = functools.partial(_in_kernel, ncb)
        in_specs = _tile_and_halo_specs(b, t, d, lambda i: i) + common
        args = (xa, xa, xa, mod, g, wb, ws, bs, cw)
    else:
        x, ctx, pos_r, pos_c = first
        reps = TB // GRID_W
        ntl = x.shape[1] // TB
        body = functools.partial(_in0_kernel, ncb)
        pr_spec = lambda off: pl.BlockSpec((1, reps, d // 2),
                                           lambda i: (jnp.clip(i - ncb + off, 0, ntl - 1), 0, 0))
        in_specs = (_tile_and_halo_specs(b, x.shape[1], d, lambda i: i - ncb)
                    + _tile_and_halo_specs(b, ctx.shape[1], d, lambda i: i)
                    + [pr_spec(0), pr_spec(-1), pr_spec(1), const((GRID_W, d // 2))] + common)
        out_specs = [tile(d)] + out_specs
        out_shape = [jax.ShapeDtypeStruct((b, t, d), F32)] + out_shape
        pr3 = pos_r.reshape(-1, reps, d // 2)
        args = (x, x, x, ctx, ctx, ctx, pr3, pr3, pr3, pos_c, mod, g, wb, ws, bs, cw)
    return pl.pallas_call(
        body,
        grid=(t // TB,),
        in_specs=in_specs,
        out_specs=out_specs,
        out_shape=out_shape,
        compiler_params=_cparams(("parallel",)),
        name="in_proj",
    )(*args)


def _conv3(x, hl, hr, w):
    rows = _iota(x.shape, 0)
    prev = jnp.where(rows == 0, hl, pltpu.roll(x, 1, axis=0))
    nxt = jnp.where(rows == x.shape[0] - 1, hr, pltpu.roll(x, x.shape[0] - 1, axis=0))
    return prev * w[0:1] + x * w[1:2] + nxt * w[2:3]


def _cummax_rows(x, reverse):
    n = x.shape[0]
    rows = _iota(x.shape, 0)
    s = 1
    while s < n:
        if reverse:
            sh = jnp.where(rows < n - s, pltpu.roll(x, n - s, axis=0), -jnp.inf)
        else:
            sh = jnp.where(rows >= s, pltpu.roll(x, s, axis=0), -jnp.inf)
        x = jnp.maximum(x, sh)
        s *= 2
    return x


def _ml_pair(sb, fwd, bwd, c_s, m_s):
    ll = ML_L
    nch = TB // ll
    dq = N_HEADS * ML_DH
    rows = _iota((ll, ll), 0)
    cols = _iota((ll, ll), 1)
    causal = [cols <= rows, cols >= rows]
    tri = [jnp.where(m, 1.0, 0.0).astype(BF16) for m in causal]
    ones = jnp.ones((ll, ML_DH), BF16)
    qk = [fwd[0][sb], bwd[0][sb]]
    v = [fwd[1][sb], bwd[1][sb]]
    g = [fwd[2][sb], bwd[2][sb]]
    outs = [fwd[3], bwd[3]]
    units = [(d, h) for d in range(2) for h in range(N_HEADS)]
    sidx = lambda d, h: (2 * sb + d) * N_HEADS + h
    cx = {(d, h): c_s[sidx(d, h)] for d, h in units}
    m_row = [m_s[2 * sb], m_s[2 * sb + 1]]
    for step in range(nch):
        r0 = [step * ll, (nch - 1 - step) * ll]
        alpha, a_in, em, e_w, ut, a_old, a_new = [], [], [], [], [], [], []
        for d in range(2):
            gc = g[d][r0[d]:r0[d] + ll, :]
            bc = _dot_exact_l(tri[d], gc)
            u = pltpu.roll(gc, 4, axis=1) - bc
            cm = _cummax_rows(u, bool(d))
            neg_alpha = jnp.maximum(m_row[d], cm)
            alpha.append(-neg_alpha)
            a_in.append(jnp.exp(m_row[d] - neg_alpha))
            em.append(jnp.exp(-neg_alpha - bc))
            last = slice(0, 1) if d else slice(ll - 1, ll)
            cm_end = cm[last, :]
            bend = bc[last, :]
            e_w.append(jnp.exp(u - cm_end))
            ut.append(u.T)
            m_kv = bend + cm_end
            m_new = jnp.maximum(bend + m_row[d], m_kv)
            a_old.append(jnp.exp(bend + m_row[d] - m_new))
            a_new.append(jnp.exp(m_kv - m_new))
            m_row[d] = m_new
        lane = lambda d, h: 8 * d + 4 + h
        for g0 in range(0, len(units), ML_STAGE_UNITS):
            grp = units[g0:g0 + ML_STAGE_UNITS]
            qb = {(d, h): qk[d][r0[d]:r0[d] + ll, h * ML_DH:(h + 1) * ML_DH] for d, h in grp}
            kb = {(d, h): qk[d][r0[d]:r0[d] + ll, dq + h * ML_DH:dq + (h + 1) * ML_DH] for d, h in grp}
            vh = {(d, h): v[d][r0[d]:r0[d] + ll, h * ML_DH:(h + 1) * ML_DH] for d, h in grp}
            sc = {u: _mm(qb[u], kb[u], _NT) for u in grp}
            lhs = {}
            for d, h in grp:
                c = lane(d, h)
                arg = jnp.where(causal[d], alpha[d][:, c:c + 1] + ut[d][c:c + 1, :], -jnp.inf)
                sbf = (sc[(d, h)] * jnp.exp(arg)).astype(BF16)
                aq = (a_in[d][:, c:c + 1] * qb[(d, h)].astype(F32)).astype(BF16)
                lhs[(d, h)] = jnp.concatenate([sbf, aq], axis=1)
            ckv = {}
            for d, h in grp:
                c = lane(d, h)
                ew = e_w[d][:, c:c + 1]
                ev = jnp.concatenate([(ew * vh[(d, h)]).astype(BF16),
                                      jnp.broadcast_to(ew, (ll, ML_DH)).astype(BF16)], axis=1)
                ckv[(d, h)] = _mm(kb[(d, h)], ev, _TN)
            for d, h in grp:
                c = lane(d, h)
                rhs = jnp.concatenate([jnp.concatenate([vh[(d, h)].astype(BF16), ones], axis=1),
                                       cx[(d, h)].astype(BF16)], axis=0)
                nd = _mm(lhs[(d, h)], rhs)
                den = jnp.maximum(jnp.abs(nd[:, ML_DH:]), em[d][:, c:c + 1])
                outs[d][sb, r0[d]:r0[d] + ll, h * ML_DH:(h + 1) * ML_DH] = (nd[:, :ML_DH] / den).astype(BF16)
            for d, h in grp:
                c = lane(d, h)
                cx[(d, h)] = a_old[d][:, c:c + 1] * cx[(d, h)] + a_new[d][:, c:c + 1] * ckv[(d, h)]
    for d, h in units:
        c_s[sidx(d, h)] = cx[(d, h)]
    m_s[2 * sb] = m_row[0]
    m_s[2 * sb + 1] = m_row[1]


def _gla_pair(sb, fwd, bwd, w2_ref, ba_ref, s_s):
    ll = GLA_L
    nch = TB // ll
    dkw = N_HEADS * GLA_DK
    rows = _iota((ll, ll), 0)
    cols = _iota((ll, ll), 1)
    causal = [cols <= rows, cols >= rows]
    tri = [jnp.where(m, 1.0, 0.0).astype(BF16) for m in causal]
    qk = [fwd[0][sb], bwd[0][sb]]
    v = [fwd[1][sb], bwd[1][sb]]
    outs = [fwd[3], bwd[3]]
    la = [_log_sigmoid(_mm(r[2][sb].astype(BF16), w2_ref[0, d]) + ba_ref[0, d]) * (1.0 / GLA_GATE_TAU)
          for d, r in enumerate((fwd, bwd))]
    units = [(d, h) for d in range(2) for h in range(N_HEADS)]
    sidx = lambda d, h: (2 * sb + d) * N_HEADS + h
    st = {(d, h): s_s[sidx(d, h)] for d, h in units}
    for step in range(nch):
        r0 = [step * ll, (nch - 1 - step) * ll]
        qi, qt, kt, kd, e_end = [], [], [], [], []
        for d in range(2):
            lah, lal = _split2(la[d][r0[d]:r0[d] + ll, :])
            bcum = _mm(tri[d], lah) + _mm(tri[d], lal)
            ref = bcum[ll // 2:ll // 2 + 1, :]
            bend = bcum[0:1, :] if d else bcum[ll - 1:ll, :]
            q = qk[d][r0[d]:r0[d] + ll, 0:dkw] * (GLA_DK ** -0.5)
            k = qk[d][r0[d]:r0[d] + ll, dkw:2 * dkw]
            qi.append((q * jnp.exp(bcum)).astype(BF16))
            qt.append((q * jnp.exp(bcum - ref)).astype(BF16))
            ktd = k * jnp.exp(ref - bcum)
            kt.append(ktd.astype(BF16))
            kd.append((ktd * jnp.exp(bend - ref)).astype(BF16))
            e_end.append(jnp.exp(bend))
        hs = lambda a, h, w: a[:, h * w:(h + 1) * w]
        vb = {(d, h): hs(v[d][r0[d]:r0[d] + ll, :], h, GLA_DV).astype(BF16) for d, h in units}
        att = {(d, h): _mm(hs(qt[d], h, GLA_DK), hs(kt[d], h, GLA_DK), _NT) for d, h in units}
        inter = {(d, h): _mm(hs(qi[d], h, GLA_DK), st[(d, h)].astype(BF16), _NT) for d, h in units}
        attb = {(d, h): jnp.where(causal[d], att[(d, h)], 0.0).astype(BF16) for d, h in units}
        kv = {(d, h): _mm(vb[(d, h)], hs(kd[d], h, GLA_DK), _TN) for d, h in units}
        for d, h in units:
            o = _mm(attb[(d, h)], vb[(d, h)]) + inter[(d, h)]
            outs[d][sb, r0[d]:r0[d] + ll, h * GLA_DV:(h + 1) * GLA_DV] = o.astype(BF16)
        st = {(d, h): st[(d, h)] * hs(e_end[d], h, GLA_DK) + kv[(d, h)] for d, h in units}
    for d, h in units:
        s_s[sidx(d, h)] = st[(d, h)]


def _mix2_kernel(qcf, mvf, psf, gqf, gvf, qcb, mvb, psb, gqb, gvb, w2, ba,
                 mf_ref, mb_ref, gf_ref, gb_ref, c_s, m_s, s_s):
    @pl.when(pl.program_id(0) == 0)
    def _():
        c_s[...] = jnp.zeros_like(c_s)
        m_s[...] = jnp.zeros_like(m_s)
        s_s[...] = jnp.zeros_like(s_s)

    for sb in range(qcf.shape[0]):
        _ml_pair(sb, (qcf, mvf, psf, mf_ref), (qcb, mvb, psb, mb_ref), c_s, m_s)
        _gla_pair(sb, (gqf, gvf, psf, gf_ref), (gqb, gvb, psb, gb_ref), w2, ba, s_s)


def _mixers2(qc, pb, ps, layer, w2e, ba, ncb):
    b, t, _ = pb.shape
    nblk = t // TB
    fwd = lambda i: i
    bwd = lambda i: _rev_block(i, ncb, nblk)

    def dspecs(blk):
        col = lambda cb: pl.BlockSpec((b, TB, 512), lambda i: (0, blk(i), cb))
        return [pl.BlockSpec((b, TB, qc.shape[2]), lambda i: (0, blk(i), 0)), col(PB_ML_V),
                pl.BlockSpec((b, TB, LANES), lambda i: (0, blk(i), 0)), col(PB_GLA_QK), col(PB_GLA_V)]

    specs = dspecs(fwd) + dspecs(bwd) + [_layer_spec(layer, w2e), _layer_spec(layer, ba)]
    ns = 2 * N_HEADS * b
    ofwd = pl.BlockSpec((b, TB, 512), lambda i: (0, i, 0))
    obwd = pl.BlockSpec((b, TB, 512), lambda i: (0, bwd(i), 0))
    return pl.pallas_call(
        _mix2_kernel,
        grid=(nblk,),
        in_specs=specs,
        out_specs=[ofwd, obwd, ofwd, obwd],
        out_shape=[jax.ShapeDtypeStruct((b, t, 512), BF16)] * 4,
        scratch_shapes=[pltpu.VMEM((ns, ML_DH, 2 * ML_DH), F32),
                        pltpu.VMEM((2 * b, 1, LANES), F32),
                        pltpu.VMEM((ns, GLA_DV, GLA_DK), F32)],
        compiler_params=_cparams(("arbitrary",)),
        name="mixers",
    )(*([qc, pb, ps, pb, pb] * 2), w2e, ba)


def _head_norm(x, g):
    outs = []
    for h in range(N_HEADS):
        seg = x[:, h * 128:(h + 1) * 128]
        outs.append(seg * lax.rsqrt(jnp.mean(seg * seg, axis=-1, keepdims=True) + EPS))
    return jnp.concatenate(outs, axis=-1) * g


def _out_kernel(hf, hb, of, ob, og, rg, x_ref, mod_ref, gml, ggla, wo, gpost, gpre, wrt,
                x1_ref, h2e_ref, aff_ref):
    nb = x_ref.shape[0]
    rh = LANES
    groups = [(b, slice(k * rh, (k + 1) * rh)) for b in range(nb) for k in range(TB // rh)]
    f32 = lambda ref, b, r: ref[b, r, :].astype(F32)
    y = [jnp.concatenate([_head_norm(f32(hf, b, r) + f32(hb, b, r), gml[0]) * _sigmoid(og[b, r, :]),
                          _head_norm(f32(of, b, r) + f32(ob, b, r), ggla[0]) * _silu(rg[b, r, :])],
                         axis=-1).astype(BF16) for b, r in groups]
    y2 = _mm(jnp.concatenate(y, axis=0), wo[0])
    gate1 = [mod_ref[0, b, 0, 2:3, :] * gpost[0] for b in range(nb)]
    gain2 = [gpre[0] * (1.0 + mod_ref[0, b, 0, 4:5, :]) for b in range(nb)]
    x1 = [x_ref[b, r, :] + _rms(y2[g * rh:(g + 1) * rh, :], gate1[b]) for g, (b, r) in enumerate(groups)]
    for (b, r), x1k in zip(groups, x1):
        x1_ref[b, r, :] = x1k
    h2 = [_rms(x1k, gain2[b]) + mod_ref[0, b, 0, 3:4, :] for (b, r), x1k in zip(groups, x1)]
    lt = _dot3(wrt[0], jnp.concatenate(h2, axis=0), _NT)
    for g, ((b, r), h2k) in enumerate(zip(groups, h2)):
        ltk = lt[:, g * rh:(g + 1) * rh]
        ext = jnp.exp(ltk - jnp.max(ltk, axis=0, keepdims=True))
        affk = ext / jnp.sum(ext, axis=0, keepdims=True)
        aff_ref[b, :, r] = affk
        h2e_ref[b, r, :] = h2k.astype(BF16)


def _out_proj(hf, hb, of, ob, pb, xa, layer, mod, gml, ggla, wo, gpost, gpre, wrt, ncb):
    b, t, d = xa.shape
    tile = lambda w, cb: pl.BlockSpec((b, TB, w), lambda i: (0, i, cb))
    return pl.pallas_call(
        _out_kernel,
        grid=(t // TB,),
        in_specs=[tile(512, 0), tile(512, 0), tile(512, 0), tile(512, 0), tile(512, PB_ML_O), tile(512, PB_GLA_R),
                  tile(d, 0), _mod_spec(layer, mod, ncb)]
                 + [_layer_spec(layer, a) for a in (gml, ggla, wo, gpost, gpre, wrt)],
        out_specs=[tile(d, 0), tile(d, 0),
                   pl.BlockSpec((b, N_EXPERTS, TB), lambda i: (0, 0, i))],
        out_shape=[jax.ShapeDtypeStruct((b, t, d), F32),
                   jax.ShapeDtypeStruct((b, t, d), BF16),
                   jax.ShapeDtypeStruct((b, N_EXPERTS, t), F32)],
        compiler_params=_cparams(("parallel",)),
        name="out_proj",
    )(hf, hb, of, ob, pb, pb, xa, mod, gml, ggla, wo, gpost, gpre, wrt)


def _cumsum_blocks(x, r):
    n = x.shape[0]
    xb = x.astype(BF16)
    li = _iota((LANES, LANES), 0)
    lj = _iota((LANES, LANES), 1)
    upper = jnp.where(li <= lj, 1.0, 0.0).astype(BF16)
    ones = jnp.ones((LANES, LANES), BF16)
    inrow = _mm(xb, upper)
    tot = _mm(xb, ones)
    ri = _iota((n, n), 0)
    rj = _iota((n, n), 1)
    same = (ri // r) == (rj // r)
    strict = jnp.where(jnp.logical_and(same, rj < ri), 1.0, 0.0).astype(BF16)
    off = _mm(strict, tot.astype(BF16))
    return inrow + off, off


def _select(aff, r, cap, base_slot):
    ne = N_EXPERTS
    n = ne * r
    aff3 = aff.reshape(ne, r, LANES)
    capf = jnp.float32(cap)

    def body(k, prefix):
        cand = prefix | (jnp.int32(1) << (30 - k))
        candf = lax.bitcast_convert_type(cand, F32)
        cnt = jnp.sum(jnp.where(aff3 >= candf, 1.0, 0.0), axis=(1, 2), keepdims=True)
        return jnp.where(cnt >= capf, cand, prefix)

    thr = lax.bitcast_convert_type(lax.fori_loop(0, 31, body, jnp.zeros((ne, 1, 1), I32)), F32)
    gt = jnp.where(aff3 > thr, 1.0, 0.0)
    eq = jnp.where(aff3 == thr, 1.0, 0.0)
    need = capf - jnp.sum(gt, axis=(1, 2), keepdims=True)
    eq2 = eq.reshape(n, LANES)
    cs_eq, _ = _cumsum_blocks(eq2, r)
    eq_rank = (cs_eq - eq2).reshape(ne, r, LANES)
    sel = (gt + eq * jnp.where(eq_rank < need, 1.0, 0.0)).reshape(n, LANES)
    cs, off = _cumsum_blocks(sel, r)
    slot = jnp.where(sel > 0.5, cs - 1.0 + base_slot, UNSEL)
    return slot, off


def _sel_kernel(rc, rl, cap_c, cap_l, *refs):
    if rc:
        affc, affl, slc, offc, sll, offl = refs
        slc[0], offc[0] = _select(affc[0], rc, cap_c, float(cap_l))
    else:
        affl, sll, offl = refs
    sll[0], offl[0] = _select(affl[0], rl, cap_l, 0.0)


def _route(aff_c, aff_l, cap_c, cap_l):
    b = aff_l.shape[0]
    ne = N_EXPERTS
    rl = aff_l.shape[1] // ne
    rc = aff_c.shape[1] // ne if aff_c is not None else 0
    args = ([aff_c] if rc else []) + [aff_l]
    in_specs, out_shape, out_specs = [], [], []
    for a in args:
        spec = pl.BlockSpec((1,) + a.shape[1:], lambda bi: (bi, 0, 0))
        in_specs.append(spec)
        out_shape += [jax.ShapeDtypeStruct(a.shape, F32)] * 2
        out_specs += [spec, spec]
    return pl.pallas_call(
        functools.partial(_sel_kernel, rc, rl, cap_c, cap_l),
        grid=(b,),
        in_specs=in_specs,
        out_specs=out_specs,
        out_shape=out_shape,
        compiler_params=_cparams(("parallel",)),
        name="route",
    )(*args)


def _window(lo_ref, base, e, m_rows):
    lo_e = lo_ref[base + e]
    hi_e = lo_ref[base + N_EXPERTS + e]
    a_e = jnp.minimum((lo_e // ROW_ALIGN) * ROW_ALIGN, m_rows - WIN)
    return a_e, hi_e


def _disp_kernel(m_rows, ng, lo_ref, h_ref, slot_ref, x_ref):
    ne = N_EXPERTS
    gi = pl.program_id(1)
    i = pl.program_id(2)
    base = (pl.program_id(0) * pl.num_programs(2) + i) * (2 * ne)

    @pl.when(i == 0)
    def _():
        x_ref[...] = jnp.zeros_like(x_ref)

    h = h_ref[0]
    sl = slot_ref[0]
    sub = _iota((WIN, TB), 0).astype(F32)
    wins = [_window(lo_ref, base, gi * ng + k, m_rows) for k in range(ng)]

    def onehot(k, first):
        a_r = jnp.minimum(first, m_rows - WIN)
        srow = sl[k:k + 1, :]
        hit = jnp.logical_and(srow - a_r.astype(F32) == sub, srow >= first.astype(F32))
        return jnp.where(hit, 1.0, 0.0).astype(BF16), a_r

    def add_rows(k, a_r, g):
        rows = pl.ds(pl.multiple_of(a_r, ROW_ALIGN), WIN)
        x_ref[0, k, rows, :] = x_ref[0, k, rows, :] + g

    sel = [onehot(k, a_e) for k, (a_e, _) in enumerate(wins)]
    g = _mm(jnp.concatenate([w for w, _ in sel], axis=0), h).astype(BF16)
    for k, (_, a_r) in enumerate(sel):
        add_rows(k, a_r, g[k * WIN:(k + 1) * WIN, :])

    for k, (a_e, hi_e) in enumerate(wins):
        @pl.when(hi_e - a_e > WIN)
        def _(k=k, a_e=a_e, hi_e=hi_e):
            def more(rd, carry):
                w, a_r = onehot(k, a_e + rd * WIN)
                add_rows(k, a_r, _mm(w, h).astype(BF16))
                return carry

            lax.fori_loop(1, (hi_e - a_e + WIN - 1) // WIN, more, 0)


def _dispatch(lohi, h2e, slots, m_rows, t0, nt):
    b, t, de = h2e.shape
    ne = N_EXPERTS
    ng = 8
    grid_spec = pltpu.PrefetchScalarGridSpec(
        num_scalar_prefetch=1,
        grid=(b, ne // ng, nt),
        in_specs=[pl.BlockSpec((1, TB, de), lambda bi, gi, i, *_: (bi, i + t0, 0)),
                  pl.BlockSpec((1, ng, TB), lambda bi, gi, i, *_: (bi * (ne // ng) + gi, 0, i + t0))],
        out_specs=pl.BlockSpec((1, ng, m_rows, de), lambda bi, gi, i, *_: (bi, gi, 0, 0)))
    return pl.pallas_call(
        functools.partial(_disp_kernel, m_rows, ng),
        grid_spec=grid_spec,
        out_shape=jax.ShapeDtypeStruct((b, ne, m_rows, de), BF16),
        compiler_params=_cparams(("parallel", "parallel", "arbitrary")),
        name="dispatch",
    )(lohi, h2e, slots.reshape(b * (ne // ng), ng, t))


def _row_chunks(m_rows, cap_l):
    step = min(EXPERT_ROWS, cap_l)
    starts = list(range(0, cap_l, step))
    return [(s, (m_rows - s) if s == starts[-1] else step) for s in starts]


def _exp_kernel(m_rows, cap_l, x_ref, wg_ref, wu_ref, wd_ref, y_ref, wg_s, wu_s, wd_s):
    @pl.when(pl.program_id(1) == 0)
    def _():
        wg_s[...] = wg_ref[0, 0].astype(BF16)
        wu_s[...] = wu_ref[0, 0].astype(BF16)
        wd_s[...] = wd_ref[0, 0].astype(BF16)

    for r0, mc in _row_chunks(m_rows, cap_l):
        xs = x_ref[0, 0, r0:r0 + mc, :]
        hid = _silu(_mm(xs, wg_s[...])) * _mm(xs, wu_s[...])
        y_ref[0, 0, r0:r0 + mc, :] = _mm(hid.astype(BF16), wd_s[...]).astype(BF16)


def _experts(xin, wg, wu, wd, layer, cap_l):
    b, ne, m_rows, de = xin.shape
    _, _, d, f = wg.shape
    return pl.pallas_call(
        functools.partial(_exp_kernel, m_rows, cap_l),
        grid=(ne, b),
        in_specs=[pl.BlockSpec((1, 1, m_rows, de), lambda ei, bi: (bi, ei, 0, 0)),
                  pl.BlockSpec((1, 1, d, f), lambda ei, bi: (layer, ei, 0, 0)),
                  pl.BlockSpec((1, 1, d, f), lambda ei, bi: (layer, ei, 0, 0)),
                  pl.BlockSpec((1, 1, f, d), lambda ei, bi: (layer, ei, 0, 0))],
        out_specs=pl.BlockSpec((1, 1, m_rows, d), lambda ei, bi: (bi, ei, 0, 0)),
        out_shape=jax.ShapeDtypeStruct((b, ne, m_rows, d), BF16),
        scratch_shapes=[pltpu.VMEM((d, f), BF16), pltpu.VMEM((d, f), BF16), pltpu.VMEM((f, d), BF16)],
        compiler_params=_cparams(("arbitrary", "arbitrary")),
        name="experts",
    )(xin, wg, wu, wd)


def _comb_kernel(m_rows, lo_ref, slot_ref, aff_ref, x1_ref, mod_ref, gpost, y_ref, out_ref, acc_s):
    ne = N_EXPERTS
    i = pl.program_id(1)
    base = (pl.program_id(0) * pl.num_programs(1) + i) * (2 * ne)
    kk = ne * WIN
    sl = slot_ref[0]
    hi = jnp.floor(sl * (1.0 / 32.0))
    lo = sl - hi * 32.0
    col_e = _iota((ne, kk), 1) // WIN
    expand = jnp.where(col_e == _iota((ne, kk), 0), 1.0, 0.0).astype(BF16)
    sx = _mm(hi.astype(BF16), expand, _TN) * 32.0 + _mm(lo.astype(BF16), expand, _TN)
    gx = _mm(aff_ref[0].astype(BF16), expand, _TN)
    col = _iota((1, kk), 1)
    jrow = (col % WIN).astype(F32)
    wins = [_window(lo_ref, base, e, m_rows) for e in range(ne)]
    ys = []
    arow = jnp.zeros((1, kk), F32)
    for e, (a_e, _) in enumerate(wins):
        arow = jnp.where(col // WIN == e, a_e.astype(F32), arow)
        ys.append(y_ref[0, e, pl.ds(pl.multiple_of(a_e, ROW_ALIGN), WIN), :])
    w = jnp.where(sx - arow == jrow, gx, 0.0).astype(BF16)
    acc_s[...] = _mm(w, jnp.concatenate(ys, axis=0))

    lane = _iota((TB, WIN), 1).astype(F32)
    for e, (a_e, hi_e) in enumerate(wins):
        @pl.when(hi_e - a_e > WIN)
        def _(e=e, a_e=a_e, hi_e=hi_e):
            scol = sx[:, e * WIN:e * WIN + 1]
            gcol = gx[:, e * WIN:e * WIN + 1]

            def more(rd, carry):
                first = a_e + rd * WIN
                a_r = jnp.minimum(first, m_rows - WIN)
                hit = jnp.logical_and(scol - a_r.astype(F32) == lane, scol >= first.astype(F32))
                ye = y_ref[0, e, pl.ds(pl.multiple_of(a_r, ROW_ALIGN), WIN), :]
                acc_s[...] += _mm(jnp.where(hit, gcol, 0.0).astype(BF16), ye)
                return carry

            lax.fori_loop(1, (hi_e - a_e + WIN - 1) // WIN, more, 0)

    out_ref[0] = x1_ref[0] + _rms(acc_s[...], mod_ref[0, 0, 0, 5:6, :] * gpost[0])


def _combine(lohi, slots, aff, x1, layer, mod, gpost, y, t0, nt, ncb):
    b, t, d = x1.shape
    ne = N_EXPERTS
    m_rows = y.shape[2]
    r8 = mod.shape[3]
    grid_spec = pltpu.PrefetchScalarGridSpec(
        num_scalar_prefetch=1,
        grid=(b, nt),
        in_specs=[pl.BlockSpec((1, ne, TB), lambda bi, i, *_: (bi, 0, i + t0)),
                  pl.BlockSpec((1, ne, TB), lambda bi, i, *_: (bi, 0, i + t0)),
                  pl.BlockSpec((1, TB, d), lambda bi, i, *_: (bi, i + t0, 0)),
                  pl.BlockSpec((1, 1, 1, r8, d),
                               lambda bi, i, *_: (layer, bi, jnp.where(i + t0 < ncb, 0, 1), 0, 0)),
                  _layer_spec(layer, gpost),
                  pl.BlockSpec((1, ne, m_rows, d), lambda bi, i, *_: (bi, 0, 0, 0),
                               pipeline_mode=pl.Buffered(1))],
        out_specs=pl.BlockSpec((1, TB, d), lambda bi, i, *_: (bi, i, 0)),
        scratch_shapes=[pltpu.VMEM((TB, d), F32)])
    return pl.pallas_call(
        functools.partial(_comb_kernel, m_rows),
        grid_spec=grid_spec,
        out_shape=jax.ShapeDtypeStruct((b, nt * TB, d), F32),
        compiler_params=_cparams(("parallel", "arbitrary")),
        name="combine",
    )(lohi, slots, aff, x1, mod, gpost, y)


def _pos_tables(rows, d):
    quarter = d // 4
    freq = jnp.power(POS_BASE, -jnp.arange(quarter, dtype=F32) / quarter)
    ar = jnp.arange(rows, dtype=F32)[:, None] * freq
    ac = jnp.arange(GRID_W, dtype=F32)[:, None] * freq
    return (jnp.concatenate([jnp.sin(ar), jnp.cos(ar)], axis=-1),
            jnp.concatenate([jnp.sin(ac), jnp.cos(ac)], axis=-1))


def _tile_bounds(off, r, ntile, cap, base):
    b = off.shape[0]
    o = off.reshape(b, N_EXPERTS, r, LANES)[:, :, :, 0]
    lo = o[:, :, ::TB // LANES][:, :, :ntile] + base
    hi = jnp.concatenate([lo[:, :, 1:], jnp.full((b, N_EXPERTS, 1), cap + base, F32)], axis=2)
    return lo, hi


def kernel(x, c, ctx, c_ctx, w_ada, b_ada, g_mix_pre, g_mix_post, g_ffn_pre, g_ffn_post,
           w_in, conv_qk, b_ml_gates, w_gla_a2, b_gla_a, g_ml_norm, g_gla_norm, w_out,
           w_router, w_e_gate, w_e_up, w_e_down):
    bsz, n_tok, d = x.shape
    lc = ctx.shape[1]
    depth = w_in.shape[0]
    ne = N_EXPERTS
    t = lc + n_tok
    ncb = lc // TB
    nblk = t // TB
    assert lc % TB == 0 and n_tok % TB == 0 and d == 1024
    cap_l = EC_FACTOR * n_tok // ne
    cap_c = EC_FACTOR * lc // ne

    assert TB % GRID_W == 0
    pos_r, pos_c = _pos_tables(n_tok // GRID_W, d)
    xa = None

    cc = jnp.zeros((8, d), F32).at[:bsz].set(c).at[bsz].set(c_ctx)
    mods = _ada(cc, w_ada, b_ada)

    wide = jnp.concatenate([w_in[:, :, 0:2048], w_in[:, :, 2064:3600]], axis=2).astype(BF16)
    narrow = jnp.concatenate([w_in[:, :, 2048:2064], w_in[:, :, 3600:3632]], axis=2)
    narrow = jnp.pad(narrow, ((0, 0), (0, 0), (0, LANES - narrow.shape[2])))
    narrow = jnp.concatenate(_split2(narrow), axis=2)
    bias_s = jnp.pad(b_ml_gates, ((0, 0), (0, LANES - b_ml_gates.shape[1])))
    w2e = jnp.zeros((depth, 2, LANES, 256), F32)
    w2e = w2e.at[:, 0, 16:32].set(w_gla_a2[:, 0]).at[:, 1, 32:48].set(w_gla_a2[:, 1]).astype(BF16)
    m_lat = mods[:, :bsz].reshape(depth, bsz, 1, 6, d)
    m_ctx = jnp.broadcast_to(mods[:, bsz].reshape(depth, 1, 1, 6, d), (depth, bsz, 1, 6, d))
    mod = jnp.pad(jnp.concatenate([m_ctx, m_lat], axis=2), ((0, 0), (0, 0), (0, 0), (0, 2), (0, 0)))
    row = lambda a: a.reshape(depth, 1, -1)
    bias_s, ba = row(bias_s), b_gla_a.reshape(depth, 2, 1, -1)
    g_pre1, g_post1, g_pre2, g_post2 = row(g_mix_pre), row(g_mix_post), row(g_ffn_pre), row(g_ffn_post)
    g_ml, g_gla = row(g_ml_norm), row(g_gla_norm)
    wo = w_out.astype(BF16)
    wrt = w_router.transpose(0, 2, 1)

    for l in range(depth):
        last = l == depth - 1
        if l == 0:
            xa, pb, ps, qc = _in_proj(None, l, mod, g_pre1, wide, narrow, bias_s, conv_qk, ncb,
                                      first=(x, ctx, pos_r, pos_c))
        else:
            pb, ps, qc = _in_proj(xa, l, mod, g_pre1, wide, narrow, bias_s, conv_qk, ncb)
        hf, hb, of, ob = _mixers2(qc, pb, ps, l, w2e, ba, ncb)
        x1, h2e, aff = _out_proj(hf, hb, of, ob, pb, xa, l, mod, g_ml, g_gla, wo, g_post1, g_pre2, wrt, ncb)

        rl = n_tok // LANES
        aff_l = aff[:, :, lc:].reshape(bsz, ne * rl, LANES)
        if last:
            sll, offl = _route(None, aff_l, 0, cap_l)
            slots = jnp.pad(sll.reshape(bsz, ne, n_tok), ((0, 0), (0, 0), (lc, 0)), constant_values=UNSEL)
            lo, hi = _tile_bounds(offl, rl, nblk - ncb, cap_l, 0)
            t0, nt, m_rows = ncb, nblk - ncb, cap_l
        else:
            rc = max(lc // LANES, 8)
            aff_c = aff[:, :, :lc].reshape(bsz, ne, lc // LANES, LANES)
            aff_c = jnp.pad(aff_c, ((0, 0), (0, 0), (0, rc - lc // LANES), (0, 0)), constant_values=-1.0)
            slc, offc, sll, offl = _route(aff_c.reshape(bsz, ne * rc, LANES), aff_l, cap_c, cap_l)
            slots = jnp.concatenate([slc.reshape(bsz, ne, rc * LANES)[:, :, :lc],
                                     sll.reshape(bsz, ne, n_tok)], axis=2)
            lo_l, hi_l = _tile_bounds(offl, rl, nblk - ncb, cap_l, 0)
            lo_c, hi_c = _tile_bounds(offc, rc, ncb, cap_c, cap_l)
            lo = jnp.concatenate([lo_c, lo_l], axis=2)
            hi = jnp.concatenate([hi_c, hi_l], axis=2)
            t0, nt, m_rows = 0, nblk, cap_l + cap_c
        lohi = jnp.concatenate([lo, hi], axis=1).transpose(0, 2, 1).astype(I32).reshape(-1)
        xin = _dispatch(lohi, h2e, slots, m_rows, t0, nt)
        y = _experts(xin, w_e_gate, w_e_up, w_e_down, l, cap_l)
        xa = _combine(lohi, slots, aff, x1, l, mod, g_post2, y, t0, nt, ncb)
    return xa
```

```python
import functools

import jax
import jax.numpy as jnp
from jax import lax
from jax.experimental import pallas as pl
from jax.experimental.pallas import tpu as pltpu

F32 = jnp.float32
BF16 = jnp.bfloat16
I32 = jnp.int32

EPS = 1e-6
GRID_W = 64
POS_BASE = 10000.0
N_HEADS = 4
ML_DH = 128
GLA_DK = 64
GLA_DV = 128
GLA_GATE_TAU = 16.0
N_EXPERTS = 16
EC_FACTOR = 2

LANES = 128
TB = 256
ML_L = 128
GLA_L = 128
ML_STAGE_UNITS = 2
EXPERT_ROWS = 256
WIN = 80
ROW_ALIGN = 16
UNSEL = 2047.0
GROUP_W = N_HEADS * ML_DH
PB_ML_V, PB_ML_O, PB_GLA_QK, PB_GLA_V, PB_GLA_R = 0, 1, 2, 3, 4
VMEM_LIMIT = 56 * 1024 * 1024


def _cparams(sem):
    return pltpu.CompilerParams(dimension_semantics=sem, vmem_limit_bytes=VMEM_LIMIT)


def _split2(a):
    hi = a.astype(BF16)
    lo = (a - hi.astype(F32)).astype(BF16)
    return hi, lo


def _split3(a):
    hi = a.astype(BF16)
    r = a - hi.astype(F32)
    mid = r.astype(BF16)
    lo = (r - mid.astype(F32)).astype(BF16)
    return hi, mid, lo


_NN = (((1,), (0,)), ((), ()))
_NT = (((1,), (1,)), ((), ()))
_TN = (((0,), (0,)), ((), ()))


def _mm(a, b, dims=_NN):
    return lax.dot_general(a, b, dims, preferred_element_type=F32)


def _dot3(a, b, dims=_NN):
    ah, al = _split2(a)
    bh, bl = _split2(b)
    return _mm(ah, bh, dims) + (_mm(ah, bl, dims) + _mm(al, bh, dims))


def _dot_exact_l(m_bf16, x, dims=_NN):
    hi, mid, lo = _split3(x)
    return _mm(m_bf16, hi, dims) + (_mm(m_bf16, mid, dims) + _mm(m_bf16, lo, dims))


def _rms(x, g):
    return x * lax.rsqrt(jnp.mean(x * x, axis=-1, keepdims=True) + EPS) * g


def _log_sigmoid(x):
    return jnp.minimum(x, 0.0) - jnp.log(1.0 + jnp.exp(-jnp.abs(x)))


def _sigmoid(x):
    return 1.0 / (1.0 + jnp.exp(-x))


def _silu(x):
    return x * _sigmoid(x)


def _iota(shape, dim):
    return lax.broadcasted_iota(I32, shape, dim)


def _rev_block(i, ncb, nblk):
    return jnp.where(i < ncb, ncb - 1 - i, nblk - 1 - (i - ncb))


def _ada_kernel(c_ref, w_ref, b_ref, o_ref):
    a = _silu(c_ref[...])
    o_ref[0] = _dot3(a, w_ref[0]) + b_ref[0]


def _ada(cc, w_ada, b_ada):
    depth, d, n6 = w_ada.shape
    tn = 1536
    return pl.pallas_call(
        _ada_kernel,
        grid=(depth, n6 // tn),
        in_specs=[pl.BlockSpec((8, d), lambda l, j: (0, 0)),
                  pl.BlockSpec((1, d, tn), lambda l, j: (l, 0, j)),
                  pl.BlockSpec((1, 1, tn), lambda l, j: (l, 0, j))],
        out_specs=pl.BlockSpec((1, 8, tn), lambda l, j: (l, 0, j)),
        out_shape=jax.ShapeDtypeStruct((depth, 8, n6), F32),
        compiler_params=_cparams(("parallel", "parallel")),
        name="ada",
    )(cc, w_ada, b_ada.reshape(depth, 1, n6))


def _project(ncb, xs, xps, xns, mod_ref, g_ref, wb_ref, ws_ref, bs_ref, cw_ref, pb_ref, ps_ref, qc_ref):
    i = pl.program_id(0)
    nblk = pl.num_programs(0)
    nb, n = len(xs), xs[0].shape[0]
    rows = [slice(b * n, (b + 1) * n) for b in range(nb)]

    gain = [g_ref[0] * (1.0 + mod_ref[0, b, 0, 1:2, :]) for b in range(nb)]

    def norm(z, b):
        return _rms(z, gain[b]) + mod_ref[0, b, 0, 0:1, :]

    hh = jnp.concatenate([norm(xs[b], b).astype(BF16) for b in range(nb)], axis=0)
    halo = ([norm(xps[b], b).astype(BF16) for b in range(nb)] + [norm(xns[b], b).astype(BF16) for b in range(nb)])
    wq = cw_ref.shape[2]
    qk = _mm(jnp.concatenate([hh] + halo, axis=0), wb_ref[0, :, 0:wq])
    lvalid = jnp.logical_and(i != 0, i != ncb)
    rvalid = jnp.logical_and(i != ncb - 1, i != nblk - 1)
    dq = wq // 2
    for b in range(nb):
        lrow = nb * n + 8 * b + 7
        rrow = nb * n + 8 * nb + 8 * b
        left = jnp.where(lvalid, qk[lrow:lrow + 1, :], 0.0)
        right = jnp.where(rvalid, qk[rrow:rrow + 1, :], 0.0)
        y = _silu(_conv3(qk[rows[b], :], left, right, cw_ref[0]))
        qc_ref[b, :, 0:dq] = (y[:, 0:dq] * (ML_DH ** -0.5)).astype(BF16)
        qc_ref[b, :, dq:] = y[:, dq:].astype(BF16)
    rest = _mm(hh, wb_ref[0, :, wq:])
    pr = _mm(hh, ws_ref[0])
    ps = pr[:, :LANES] + pr[:, LANES:] + bs_ref[0]
    lane = _iota(ps.shape, 1)
    forget = jnp.logical_and(lane % 8 >= N_HEADS, lane < 4 * N_HEADS)
    ps = jnp.where(forget, _log_sigmoid(ps), ps)
    for b in range(nb):
        pb_ref[b] = rest[rows[b], :]
        ps_ref[b] = ps[rows[b], :]


def _in_kernel(ncb, x_ref, xp_ref, xn_ref, mod_ref, g_ref, wb_ref, ws_ref, bs_ref, cw_ref,
               pb_ref, ps_ref, qc_ref):
    nb = x_ref.shape[0]
    _project(ncb, [x_ref[b] for b in range(nb)], [xp_ref[b] for b in range(nb)],
             [xn_ref[b] for b in range(nb)], mod_ref, g_ref, wb_ref, ws_ref, bs_ref, cw_ref,
             pb_ref, ps_ref, qc_ref)


def _in0_kernel(ncb, x_ref, xp_ref, xn_ref, c_ref, cp_ref, cn_ref, pr_ref, prp_ref, prn_ref, pc_ref,
                mod_ref, g_ref, wb_ref, ws_ref, bs_ref, cw_ref, xa_ref, pb_ref, ps_ref, qc_ref):
    i = pl.program_id(0)
    nb = x_ref.shape[0]
    half = pr_ref.shape[2]
    reps = TB // GRID_W
    prow = jnp.concatenate([jnp.broadcast_to(pr_ref[0, k:k + 1, :], (GRID_W, half)) for k in range(reps)], axis=0)
    pcol = jnp.concatenate([pc_ref[...]] * reps, axis=0)
    pos = jnp.concatenate([prow, pcol], axis=1)
    pos_p = jnp.concatenate([prp_ref[0, reps - 1:reps, :], pc_ref[GRID_W - 1:GRID_W, :]], axis=1)
    pos_n = jnp.concatenate([prn_ref[0, 0:1, :], pc_ref[0:1, :]], axis=1)
    is_ctx = i < ncb
    xs, xps, xns = [], [], []
    for b in range(nb):
        xa = jnp.where(is_ctx, c_ref[b], x_ref[b] + pos)
        xa_ref[b] = xa
        xs.append(xa)
        xps.append(jnp.where(is_ctx, cp_ref[b], xp_ref[b] + pos_p))
        xns.append(jnp.where(is_ctx, cn_ref[b], xn_ref[b] + pos_n))
    _project(ncb, xs, xps, xns, mod_ref, g_ref, wb_ref, ws_ref, bs_ref, cw_ref, pb_ref, ps_ref, qc_ref)


def _tile_and_halo_specs(b, rows, d, tile_of):
    r8 = TB // 8
    ntile, last8 = rows // TB, rows // 8 - 1
    tl = lambda i: jnp.clip(tile_of(i), 0, ntile - 1)
    return [pl.BlockSpec((b, TB, d), lambda i: (0, tl(i), 0)),
            pl.BlockSpec((b, 8, d), lambda i: (0, jnp.clip(tl(i) * r8 - 1, 0, last8), 0)),
            pl.BlockSpec((b, 8, d), lambda i: (0, jnp.clip((tl(i) + 1) * r8, 0, last8), 0))]


def _layer_spec(layer, arr):
    shp = arr.shape[1:]
    return pl.BlockSpec((1,) + shp, lambda *_: (layer,) + (0,) * len(shp))


def _mod_spec(layer, mod, ncb, tile_of=lambda i: i):
    _, b, _, r, d = mod.shape
    return pl.BlockSpec((1, b, 1, r, d), lambda i, *_: (layer, 0, jnp.where(tile_of(i) < ncb, 0, 1), 0, 0))


def _in_proj(xa, layer, mod, g, wb, ws, bs, cw, ncb, first=None):
    if first is None:
        b, t, d = xa.shape
    else:
        b, t, d = first[0].shape[0], first[0].shape[1] + first[1].shape[1], first[0].shape[2]
    wq = cw.shape[2]
    nb = wb.shape[2] - wq
    const = lambda shp: pl.BlockSpec(shp, lambda i: tuple(0 for _ in shp))
    tile = lambda w: pl.BlockSpec((b, TB, w), lambda i: (0, i, 0))
    common = [_mod_spec(layer, mod, ncb)] + [_layer_spec(layer, a) for a in (g, wb, ws, bs, cw)]
    out_specs = [tile(nb), tile(LANES), tile(wq)]
    out_shape = [jax.ShapeDtypeStruct((b, t, nb), F32),
                 jax.ShapeDtypeStruct((b, t, LANES), F32),
                 jax.ShapeDtypeStruct((b, t, wq), BF16)]
    if first is None:
        body = functools.partial(_in_kernel, ncb)
        in_specs = _tile_and_halo_specs(b, t, d, lambda i: i) + common
        args = (xa, xa, xa, mod, g, wb, ws, bs, cw)
    else:
        x, ctx, pos_r, pos_c = first
        reps = TB // GRID_W
        ntl = x.shape[1] // TB
        body = functools.partial(_in0_kernel, ncb)
        pr_spec = lambda off: pl.BlockSpec((1, reps, d // 2),
                                           lambda i: (jnp.clip(i - ncb + off, 0, ntl - 1), 0, 0))
        in_specs = (_tile_and_halo_specs(b, x.shape[1], d, lambda i: i - ncb)
                    + _tile_and_halo_specs(b, ctx.shape[1], d, lambda i: i)
                    + [pr_spec(0), pr_spec(-1), pr_spec(1), const((GRID_W, d // 2))] + common)
        out_specs = [tile(d)] + out_specs
        out_shape = [jax.ShapeDtypeStruct((b, t, d), F32)] + out_shape
        pr3 = pos_r.reshape(-1, reps, d // 2)
        args = (x, x, x, ctx, ctx, ctx, pr3, pr3, pr3, pos_c, mod, g, wb, ws, bs, cw)
    return pl.pallas_call(
        body,
        grid=(t // TB,),
        in_specs=in_specs,
        out_specs=out_specs,
        out_shape=out_shape,
        compiler_params=_cparams(("parallel",)),
        name="in_proj",
    )(*args)


def _conv3(x, hl, hr, w):
    rows = _iota(x.shape, 0)
    prev = jnp.where(rows == 0, hl, pltpu.roll(x, 1, axis=0))
    nxt = jnp.where(rows == x.shape[0] - 1, hr, pltpu.roll(x, x.shape[0] - 1, axis=0))
    return prev * w[0:1] + x * w[1:2] + nxt * w[2:3]


def _cummax_rows(x, reverse):
    n = x.shape[0]
    rows = _iota(x.shape, 0)
    s = 1
    while s < n:
        if reverse:
            sh = jnp.where(rows < n - s, pltpu.roll(x, n - s, axis=0), -jnp.inf)
        else:
            sh = jnp.where(rows >= s, pltpu.roll(x, s, axis=0), -jnp.inf)
        x = jnp.maximum(x, sh)
        s *= 2
    return x


def _ml_pair(sb, fwd, bwd, c_s, m_s):
    ll = ML_L
    nch = TB // ll
    dq = N_HEADS * ML_DH
    rows = _iota((ll, ll), 0)
    cols = _iota((ll, ll), 1)
    causal = [cols <= rows, cols >= rows]
    tri = [jnp.where(m, 1.0, 0.0).astype(BF16) for m in causal]
    ones = jnp.ones((ll, ML_DH), BF16)
    qk = [fwd[0][sb], bwd[0][sb]]
    v = [fwd[1][sb], bwd[1][sb]]
    g = [fwd[2][sb], bwd[2][sb]]
    outs = [fwd[3], bwd[3]]
    units = [(d, h) for d in range(2) for h in range(N_HEADS)]
    sidx = lambda d, h: (2 * sb + d) * N_HEADS + h
    cx = {(d, h): c_s[sidx(d, h)] for d, h in units}
    m_row = [m_s[2 * sb], m_s[2 * sb + 1]]
    for step in range(nch):
        r0 = [step * ll, (nch - 1 - step) * ll]
        alpha, a_in, em, e_w, ut, a_old, a_new = [], [], [], [], [], [], []
        for d in range(2):
            gc = g[d][r0[d]:r0[d] + ll, :]
            bc = _dot_exact_l(tri[d], gc)
            u = pltpu.roll(gc, 4, axis=1) - bc
            cm = _cummax_rows(u, bool(d))
            neg_alpha = jnp.maximum(m_row[d], cm)
            alpha.append(-neg_alpha)
            a_in.append(jnp.exp(m_row[d] - neg_alpha))
            em.append(jnp.exp(-neg_alpha - bc))
            last = slice(0, 1) if d else slice(ll - 1, ll)
            cm_end = cm[last, :]
            bend = bc[last, :]
            e_w.append(jnp.exp(u - cm_end))
            ut.append(u.T)
            m_kv = bend + cm_end
            m_new = jnp.maximum(bend + m_row[d], m_kv)
            a_old.append(jnp.exp(bend + m_row[d] - m_new))
            a_new.append(jnp.exp(m_kv - m_new))
            m_row[d] = m_new
        lane = lambda d, h: 8 * d + 4 + h
        for g0 in range(0, len(units), ML_STAGE_UNITS):
            grp = units[g0:g0 + ML_STAGE_UNITS]
            qb = {(d, h): qk[d][r0[d]:r0[d] + ll, h * ML_DH:(h + 1) * ML_DH] for d, h in grp}
            kb = {(d, h): qk[d][r0[d]:r0[d] + ll, dq + h * ML_DH:dq + (h + 1) * ML_DH] for d, h in grp}
            vh = {(d, h): v[d][r0[d]:r0[d] + ll, h * ML_DH:(h + 1) * ML_DH] for d, h in grp}
            sc = {u: _mm(qb[u], kb[u], _NT) for u in grp}
            lhs = {}
            for d, h in grp:
                c = lane(d, h)
                arg = jnp.where(causal[d], alpha[d][:, c:c + 1] + ut[d][c:c + 1, :], -jnp.inf)
                sbf = (sc[(d, h)] * jnp.exp(arg)).astype(BF16)
                aq = (a_in[d][:, c:c + 1] * qb[(d, h)].astype(F32)).astype(BF16)
                lhs[(d, h)] = jnp.concatenate([sbf, aq], axis=1)
            ckv = {}
            for d, h in grp:
                c = lane(d, h)
                ew = e_w[d][:, c:c + 1]
                ev = jnp.concatenate([(ew * vh[(d, h)]).astype(BF16),
                                      jnp.broadcast_to(ew, (ll, ML_DH)).astype(BF16)], axis=1)
                ckv[(d, h)] = _mm(kb[(d, h)], ev, _TN)
            for d, h in grp:
                c = lane(d, h)
                rhs = jnp.concatenate([jnp.concatenate([vh[(d, h)].astype(BF16), ones], axis=1),
                                       cx[(d, h)].astype(BF16)], axis=0)
                nd = _mm(lhs[(d, h)], rhs)
                den = jnp.maximum(jnp.abs(nd[:, ML_DH:]), em[d][:, c:c + 1])
                outs[d][sb, r0[d]:r0[d] + ll, h * ML_DH:(h + 1) * ML_DH] = (nd[:, :ML_DH] / den).astype(BF16)
            for d, h in grp:
                c = lane(d, h)
                cx[(d, h)] = a_old[d][:, c:c + 1] * cx[(d, h)] + a_new[d][:, c:c + 1] * ckv[(d, h)]
    for d, h in units:
        c_s[sidx(d, h)] = cx[(d, h)]
    m_s[2 * sb] = m_row[0]
    m_s[2 * sb + 1] = m_row[1]


def _gla_pair(sb, fwd, bwd, w2_ref, ba_ref, s_s):
    ll = GLA_L
    nch = TB // ll
    dkw = N_HEADS * GLA_DK
    rows = _iota((ll, ll), 0)
    cols = _iota((ll, ll), 1)
    causal = [cols <= rows, cols >= rows]
    tri = [jnp.where(m, 1.0, 0.0).astype(BF16) for m in causal]
    qk = [fwd[0][sb], bwd[0][sb]]
    v = [fwd[1][sb], bwd[1][sb]]
    outs = [fwd[3], bwd[3]]
    la = [_log_sigmoid(_mm(r[2][sb].astype(BF16), w2_ref[0, d]) + ba_ref[0, d]) * (1.0 / GLA_GATE_TAU)
          for d, r in enumerate((fwd, bwd))]
    units = [(d, h) for d in range(2) for h in range(N_HEADS)]
    sidx = lambda d, h: (2 * sb + d) * N_HEADS + h
    st = {(d, h): s_s[sidx(d, h)] for d, h in units}
    for step in range(nch):
        r0 = [step * ll, (nch - 1 - step) * ll]
        qi, qt, kt, kd, e_end = [], [], [], [], []
        for d in range(2):
            lah, lal = _split2(la[d][r0[d]:r0[d] + ll, :])
            bcum = _mm(tri[d], lah) + _mm(tri[d], lal)
            ref = bcum[ll // 2:ll // 2 + 1, :]
            bend = bcum[0:1, :] if d else bcum[ll - 1:ll, :]
            q = qk[d][r0[d]:r0[d] + ll, 0:dkw] * (GLA_DK ** -0.5)
            k = qk[d][r0[d]:r0[d] + ll, dkw:2 * dkw]
            qi.append((q * jnp.exp(bcum)).astype(BF16))
            qt.append((q * jnp.exp(bcum - ref)).astype(BF16))
            ktd = k * jnp.exp(ref - bcum)
            kt.append(ktd.astype(BF16))
            kd.append((ktd * jnp.exp(bend - ref)).astype(BF16))
            e_end.append(jnp.exp(bend))
        hs = lambda a, h, w: a[:, h * w:(h + 1) * w]
        vb = {(d, h): hs(v[d][r0[d]:r0[d] + ll, :], h, GLA_DV).astype(BF16) for d, h in units}
        att = {(d, h): _mm(hs(qt[d], h, GLA_DK), hs(kt[d], h, GLA_DK), _NT) for d, h in units}
        inter = {(d, h): _mm(hs(qi[d], h, GLA_DK), st[(d, h)].astype(BF16), _NT) for d, h in units}
        attb = {(d, h): jnp.where(causal[d], att[(d, h)], 0.0).astype(BF16) for d, h in units}
        kv = {(d, h): _mm(vb[(d, h)], hs(kd[d], h, GLA_DK), _TN) for d, h in units}
        for d, h in units:
            o = _mm(attb[(d, h)], vb[(d, h)]) + inter[(d, h)]
            outs[d][sb, r0[d]:r0[d] + ll, h * GLA_DV:(h + 1) * GLA_DV] = o.astype(BF16)
        st = {(d, h): st[(d, h)] * hs(e_end[d], h, GLA_DK) + kv[(d, h)] for d, h in units}
    for d, h in units:
        s_s[sidx(d, h)] = st[(d, h)]


def _mix2_kernel(qcf, mvf, psf, gqf, gvf, qcb, mvb, psb, gqb, gvb, w2, ba,
                 mf_ref, mb_ref, gf_ref, gb_ref, c_s, m_s, s_s):
    @pl.when(pl.program_id(0) == 0)
    def _():
        c_s[...] = jnp.zeros_like(c_s)
        m_s[...] = jnp.zeros_like(m_s)
        s_s[...] = jnp.zeros_like(s_s)

    for sb in range(qcf.shape[0]):
        _ml_pair(sb, (qcf, mvf, psf, mf_ref), (qcb, mvb, psb, mb_ref), c_s, m_s)
        _gla_pair(sb, (gqf, gvf, psf, gf_ref), (gqb, gvb, psb, gb_ref), w2, ba, s_s)


def _mixers2(qc, pb, ps, layer, w2e, ba, ncb):
    b, t, _ = pb.shape
    nblk = t // TB
    fwd = lambda i: i
    bwd = lambda i: _rev_block(i, ncb, nblk)

    def dspecs(blk):
        col = lambda cb: pl.BlockSpec((b, TB, GROUP_W), lambda i: (0, blk(i), cb))
        return [pl.BlockSpec((b, TB, qc.shape[2]), lambda i: (0, blk(i), 0)), col(PB_ML_V),
                pl.BlockSpec((b, TB, LANES), lambda i: (0, blk(i), 0)), col(PB_GLA_QK), col(PB_GLA_V)]

    specs = dspecs(fwd) + dspecs(bwd) + [_layer_spec(layer, w2e), _layer_spec(layer, ba)]
    ns = 2 * N_HEADS * b
    ofwd = pl.BlockSpec((b, TB, GROUP_W), lambda i: (0, i, 0))
    obwd = pl.BlockSpec((b, TB, GROUP_W), lambda i: (0, bwd(i), 0))
    return pl.pallas_call(
        _mix2_kernel,
        grid=(nblk,),
        in_specs=specs,
        out_specs=[ofwd, obwd, ofwd, obwd],
        out_shape=[jax.ShapeDtypeStruct((b, t, GROUP_W), BF16)] * 4,
        scratch_shapes=[pltpu.VMEM((ns, ML_DH, 2 * ML_DH), F32),
                        pltpu.VMEM((2 * b, 1, LANES), F32),
                        pltpu.VMEM((ns, GLA_DV, GLA_DK), F32)],
        compiler_params=_cparams(("arbitrary",)),
        name="mixers",
    )(*([qc, pb, ps, pb, pb] * 2), w2e, ba)


def _head_norm(x, g):
    outs = []
    for h in range(N_HEADS):
        seg = x[:, h * 128:(h + 1) * 128]
        outs.append(seg * lax.rsqrt(jnp.mean(seg * seg, axis=-1, keepdims=True) + EPS))
    return jnp.concatenate(outs, axis=-1) * g


def _out_kernel(hf, hb, of, ob, og, rg, x_ref, mod_ref, gml, ggla, wo, gpost, gpre, wrt,
                x1_ref, h2e_ref, aff_ref):
    nb = x_ref.shape[0]
    rh = LANES
    groups = [(b, slice(k * rh, (k + 1) * rh)) for b in range(nb) for k in range(TB // rh)]
    f32 = lambda ref, b, r: ref[b, r, :].astype(F32)
    y = [jnp.concatenate([_head_norm(f32(hf, b, r) + f32(hb, b, r), gml[0]) * _sigmoid(og[b, r, :]),
                          _head_norm(f32(of, b, r) + f32(ob, b, r), ggla[0]) * _silu(rg[b, r, :])],
                         axis=-1).astype(BF16) for b, r in groups]
    y2 = _mm(jnp.concatenate(y, axis=0), wo[0])
    gate1 = [mod_ref[0, b, 0, 2:3, :] * gpost[0] for b in range(nb)]
    gain2 = [gpre[0] * (1.0 + mod_ref[0, b, 0, 4:5, :]) for b in range(nb)]
    x1 = [x_ref[b, r, :] + _rms(y2[g * rh:(g + 1) * rh, :], gate1[b]) for g, (b, r) in enumerate(groups)]
    for (b, r), x1k in zip(groups, x1):
        x1_ref[b, r, :] = x1k
    h2 = [_rms(x1k, gain2[b]) + mod_ref[0, b, 0, 3:4, :] for (b, r), x1k in zip(groups, x1)]
    lt = _dot3(wrt[0], jnp.concatenate(h2, axis=0), _NT)
    for g, ((b, r), h2k) in enumerate(zip(groups, h2)):
        ltk = lt[:, g * rh:(g + 1) * rh]
        ext = jnp.exp(ltk - jnp.max(ltk, axis=0, keepdims=True))
        affk = ext / jnp.sum(ext, axis=0, keepdims=True)
        aff_ref[b, :, r] = affk
        h2e_ref[b, r, :] = h2k.astype(BF16)


def _out_proj(hf, hb, of, ob, pb, xa, layer, mod, gml, ggla, wo, gpost, gpre, wrt, ncb):
    b, t, d = xa.shape
    tile = lambda w, cb: pl.BlockSpec((b, TB, w), lambda i: (0, i, cb))
    return pl.pallas_call(
        _out_kernel,
        grid=(t // TB,),
        in_specs=[tile(GROUP_W, 0)] * 4 + [tile(GROUP_W, PB_ML_O), tile(GROUP_W, PB_GLA_R),
                  tile(d, 0), _mod_spec(layer, mod, ncb)]
                 + [_layer_spec(layer, a) for a in (gml, ggla, wo, gpost, gpre, wrt)],
        out_specs=[tile(d, 0), tile(d, 0),
                   pl.BlockSpec((b, N_EXPERTS, TB), lambda i: (0, 0, i))],
        out_shape=[jax.ShapeDtypeStruct((b, t, d), F32),
                   jax.ShapeDtypeStruct((b, t, d), BF16),
                   jax.ShapeDtypeStruct((b, N_EXPERTS, t), F32)],
        compiler_params=_cparams(("parallel",)),
        name="out_proj",
    )(hf, hb, of, ob, pb, pb, xa, mod, gml, ggla, wo, gpost, gpre, wrt)


def _cumsum_blocks(x, r):
    n = x.shape[0]
    xb = x.astype(BF16)
    li = _iota((LANES, LANES), 0)
    lj = _iota((LANES, LANES), 1)
    upper = jnp.where(li <= lj, 1.0, 0.0).astype(BF16)
    ones = jnp.ones((LANES, LANES), BF16)
    inrow = _mm(xb, upper)
    tot = _mm(xb, ones)
    ri = _iota((n, n), 0)
    rj = _iota((n, n), 1)
    same = (ri // r) == (rj // r)
    strict = jnp.where(jnp.logical_and(same, rj < ri), 1.0, 0.0).astype(BF16)
    off = _mm(strict, tot.astype(BF16))
    return inrow + off, off


def _select(aff, r, cap, base_slot):
    ne = N_EXPERTS
    n = ne * r
    aff3 = aff.reshape(ne, r, LANES)
    capf = jnp.float32(cap)

    def body(k, prefix):
        cand = prefix | (jnp.int32(1) << (30 - k))
        candf = lax.bitcast_convert_type(cand, F32)
        cnt = jnp.sum(jnp.where(aff3 >= candf, 1.0, 0.0), axis=(1, 2), keepdims=True)
        return jnp.where(cnt >= capf, cand, prefix)

    thr = lax.bitcast_convert_type(lax.fori_loop(0, 31, body, jnp.zeros((ne, 1, 1), I32)), F32)
    gt = jnp.where(aff3 > thr, 1.0, 0.0)
    eq = jnp.where(aff3 == thr, 1.0, 0.0)
    need = capf - jnp.sum(gt, axis=(1, 2), keepdims=True)
    eq2 = eq.reshape(n, LANES)
    cs_eq, _ = _cumsum_blocks(eq2, r)
    eq_rank = (cs_eq - eq2).reshape(ne, r, LANES)
    sel = (gt + eq * jnp.where(eq_rank < need, 1.0, 0.0)).reshape(n, LANES)
    cs, off = _cumsum_blocks(sel, r)
    slot = jnp.where(sel > 0.5, cs - 1.0 + base_slot, UNSEL)
    return slot, off


def _sel_kernel(rc, rl, cap_c, cap_l, *refs):
    if rc:
        affc, affl, slc, offc, sll, offl = refs
        slc[0], offc[0] = _select(affc[0], rc, cap_c, float(cap_l))
    else:
        affl, sll, offl = refs
    sll[0], offl[0] = _select(affl[0], rl, cap_l, 0.0)


def _route(aff_c, aff_l, cap_c, cap_l):
    b = aff_l.shape[0]
    ne = N_EXPERTS
    rl = aff_l.shape[1] // ne
    rc = aff_c.shape[1] // ne if aff_c is not None else 0
    args = ([aff_c] if rc else []) + [aff_l]
    in_specs, out_shape, out_specs = [], [], []
    for a in args:
        spec = pl.BlockSpec((1,) + a.shape[1:], lambda bi: (bi, 0, 0))
        in_specs.append(spec)
        out_shape += [jax.ShapeDtypeStruct(a.shape, F32)] * 2
        out_specs += [spec, spec]
    return pl.pallas_call(
        functools.partial(_sel_kernel, rc, rl, cap_c, cap_l),
        grid=(b,),
        in_specs=in_specs,
        out_specs=out_specs,
        out_shape=out_shape,
        compiler_params=_cparams(("parallel",)),
        name="route",
    )(*args)


def _window(lo_ref, base, e, m_rows):
    lo_e = lo_ref[base + e]
    hi_e = lo_ref[base + N_EXPERTS + e]
    a_e = jnp.minimum((lo_e // ROW_ALIGN) * ROW_ALIGN, m_rows - WIN)
    return a_e, hi_e


def _disp_kernel(m_rows, ng, lo_ref, h_ref, slot_ref, x_ref):
    ne = N_EXPERTS
    gi = pl.program_id(1)
    i = pl.program_id(2)
    base = (pl.program_id(0) * pl.num_programs(2) + i) * (2 * ne)

    @pl.when(i == 0)
    def _():
        x_ref[...] = jnp.zeros_like(x_ref)

    h = h_ref[0]
    sl = slot_ref[0]
    sub = _iota((WIN, TB), 0).astype(F32)
    wins = [_window(lo_ref, base, gi * ng + k, m_rows) for k in range(ng)]

    def onehot(k, first):
        a_r = jnp.minimum(first, m_rows - WIN)
        srow = sl[k:k + 1, :]
        hit = jnp.logical_and(srow - a_r.astype(F32) == sub, srow >= first.astype(F32))
        return jnp.where(hit, 1.0, 0.0).astype(BF16), a_r

    def add_rows(k, a_r, g):
        rows = pl.ds(pl.multiple_of(a_r, ROW_ALIGN), WIN)
        x_ref[0, k, rows, :] = x_ref[0, k, rows, :] + g

    sel = [onehot(k, a_e) for k, (a_e, _) in enumerate(wins)]
    g = _mm(jnp.concatenate([w for w, _ in sel], axis=0), h).astype(BF16)
    for k, (_, a_r) in enumerate(sel):
        add_rows(k, a_r, g[k * WIN:(k + 1) * WIN, :])

    for k, (a_e, hi_e) in enumerate(wins):
        @pl.when(hi_e - a_e > WIN)
        def _(k=k, a_e=a_e, hi_e=hi_e):
            def more(rd, carry):
                w, a_r = onehot(k, a_e + rd * WIN)
                add_rows(k, a_r, _mm(w, h).astype(BF16))
                return carry

            lax.fori_loop(1, (hi_e - a_e + WIN - 1) // WIN, more, 0)


def _dispatch(lohi, h2e, slots, m_rows, t0, nt):
    b, t, de = h2e.shape
    ne = N_EXPERTS
    ng = 8
    grid_spec = pltpu.PrefetchScalarGridSpec(
        num_scalar_prefetch=1,
        grid=(b, ne // ng, nt),
        in_specs=[pl.BlockSpec((1, TB, de), lambda bi, gi, i, *_: (bi, i + t0, 0)),
                  pl.BlockSpec((1, ng, TB), lambda bi, gi, i, *_: (bi * (ne // ng) + gi, 0, i + t0))],
        out_specs=pl.BlockSpec((1, ng, m_rows, de), lambda bi, gi, i, *_: (bi, gi, 0, 0)))
    return pl.pallas_call(
        functools.partial(_disp_kernel, m_rows, ng),
        grid_spec=grid_spec,
        out_shape=jax.ShapeDtypeStruct((b, ne, m_rows, de), BF16),
        compiler_params=_cparams(("parallel", "parallel", "arbitrary")),
        name="dispatch",
    )(lohi, h2e, slots.reshape(b * (ne // ng), ng, t))


def _row_chunks(m_rows, cap_l):
    step = min(EXPERT_ROWS, cap_l)
    starts = list(range(0, cap_l, step))
    return [(s, (m_rows - s) if s == starts[-1] else step) for s in starts]


def _exp_kernel(m_rows, cap_l, x_ref, wg_ref, wu_ref, wd_ref, y_ref, wg_s, wu_s, wd_s):
    @pl.when(pl.program_id(1) == 0)
    def _():
        wg_s[...] = wg_ref[0, 0].astype(BF16)
        wu_s[...] = wu_ref[0, 0].astype(BF16)
        wd_s[...] = wd_ref[0, 0].astype(BF16)

    for r0, mc in _row_chunks(m_rows, cap_l):
        xs = x_ref[0, 0, r0:r0 + mc, :]
        hid = _silu(_mm(xs, wg_s[...])) * _mm(xs, wu_s[...])
        y_ref[0, 0, r0:r0 + mc, :] = _mm(hid.astype(BF16), wd_s[...]).astype(BF16)


def _experts(xin, wg, wu, wd, layer, cap_l):
    b, ne, m_rows, de = xin.shape
    _, _, d, f = wg.shape
    return pl.pallas_call(
        functools.partial(_exp_kernel, m_rows, cap_l),
        grid=(ne, b),
        in_specs=[pl.BlockSpec((1, 1, m_rows, de), lambda ei, bi: (bi, ei, 0, 0)),
                  pl.BlockSpec((1, 1, d, f), lambda ei, bi: (layer, ei, 0, 0)),
                  pl.BlockSpec((1, 1, d, f), lambda ei, bi: (layer, ei, 0, 0)),
                  pl.BlockSpec((1, 1, f, d), lambda ei, bi: (layer, ei, 0, 0))],
        out_specs=pl.BlockSpec((1, 1, m_rows, d), lambda ei, bi: (bi, ei, 0, 0)),
        out_shape=jax.ShapeDtypeStruct((b, ne, m_rows, d), BF16),
        scratch_shapes=[pltpu.VMEM((d, f), BF16), pltpu.VMEM((d, f), BF16), pltpu.VMEM((f, d), BF16)],
        compiler_params=_cparams(("arbitrary", "arbitrary")),
        name="experts",
    )(xin, wg, wu, wd)


def _comb_kernel(m_rows, lo_ref, slot_ref, aff_ref, x1_ref, mod_ref, gpost, y_ref, out_ref, acc_s):
    ne = N_EXPERTS
    i = pl.program_id(1)
    base = (pl.program_id(0) * pl.num_programs(1) + i) * (2 * ne)
    kk = ne * WIN
    sl = slot_ref[0]
    hi = jnp.floor(sl * (1.0 / 32.0))
    lo = sl - hi * 32.0
    col_e = _iota((ne, kk), 1) // WIN
    expand = jnp.where(col_e == _iota((ne, kk), 0), 1.0, 0.0).astype(BF16)
    sx = _mm(hi.astype(BF16), expand, _TN) * 32.0 + _mm(lo.astype(BF16), expand, _TN)
    gx = _mm(aff_ref[0].astype(BF16), expand, _TN)
    col = _iota((1, kk), 1)
    jrow = (col % WIN).astype(F32)
    wins = [_window(lo_ref, base, e, m_rows) for e in range(ne)]
    ys = []
    arow = jnp.zeros((1, kk), F32)
    for e, (a_e, _) in enumerate(wins):
        arow = jnp.where(col // WIN == e, a_e.astype(F32), arow)
        ys.append(y_ref[0, e, pl.ds(pl.multiple_of(a_e, ROW_ALIGN), WIN), :])
    w = jnp.where(sx - arow == jrow, gx, 0.0).astype(BF16)
    acc_s[...] = _mm(w, jnp.concatenate(ys, axis=0))

    lane = _iota((TB, WIN), 1).astype(F32)
    for e, (a_e, hi_e) in enumerate(wins):
        @pl.when(hi_e - a_e > WIN)
        def _(e=e, a_e=a_e, hi_e=hi_e):
            scol = sx[:, e * WIN:e * WIN + 1]
            gcol = gx[:, e * WIN:e * WIN + 1]

            def more(rd, carry):
                first = a_e + rd * WIN
                a_r = jnp.minimum(first, m_rows - WIN)
                hit = jnp.logical_and(scol - a_r.astype(F32) == lane, scol >= first.astype(F32))
                ye = y_ref[0, e, pl.ds(pl.multiple_of(a_r, ROW_ALIGN), WIN), :]
                acc_s[...] += _mm(jnp.where(hit, gcol, 0.0).astype(BF16), ye)
                return carry

            lax.fori_loop(1, (hi_e - a_e + WIN - 1) // WIN, more, 0)

    out_ref[0] = x1_ref[0] + _rms(acc_s[...], mod_ref[0, 0, 0, 5:6, :] * gpost[0])


def _combine(lohi, slots, aff, x1, layer, mod, gpost, y, t0, nt, ncb):
    b, t, d = x1.shape
    ne = N_EXPERTS
    m_rows = y.shape[2]
    r8 = mod.shape[3]
    grid_spec = pltpu.PrefetchScalarGridSpec(
        num_scalar_prefetch=1,
        grid=(b, nt),
        in_specs=[pl.BlockSpec((1, ne, TB), lambda bi, i, *_: (bi, 0, i + t0)),
                  pl.BlockSpec((1, ne, TB), lambda bi, i, *_: (bi, 0, i + t0)),
                  pl.BlockSpec((1, TB, d), lambda bi, i, *_: (bi, i + t0, 0)),
                  pl.BlockSpec((1, 1, 1, r8, d),
                               lambda bi, i, *_: (layer, bi, jnp.where(i + t0 < ncb, 0, 1), 0, 0)),
                  _layer_spec(layer, gpost),
                  pl.BlockSpec((1, ne, m_rows, d), lambda bi, i, *_: (bi, 0, 0, 0),
                               pipeline_mode=pl.Buffered(1))],
        out_specs=pl.BlockSpec((1, TB, d), lambda bi, i, *_: (bi, i, 0)),
        scratch_shapes=[pltpu.VMEM((TB, d), F32)])
    return pl.pallas_call(
        functools.partial(_comb_kernel, m_rows),
        grid_spec=grid_spec,
        out_shape=jax.ShapeDtypeStruct((b, nt * TB, d), F32),
        compiler_params=_cparams(("parallel", "arbitrary")),
        name="combine",
    )(lohi, slots, aff, x1, mod, gpost, y)


def _pos_tables(rows, d):
    quarter = d // 4
    freq = jnp.power(POS_BASE, -jnp.arange(quarter, dtype=F32) / quarter)
    ar = jnp.arange(rows, dtype=F32)[:, None] * freq
    ac = jnp.arange(GRID_W, dtype=F32)[:, None] * freq
    return (jnp.concatenate([jnp.sin(ar), jnp.cos(ar)], axis=-1),
            jnp.concatenate([jnp.sin(ac), jnp.cos(ac)], axis=-1))


def _tile_bounds(off, r, ntile, cap, base):
    b = off.shape[0]
    o = off.reshape(b, N_EXPERTS, r, LANES)[:, :, :, 0]
    lo = o[:, :, ::TB // LANES][:, :, :ntile] + base
    hi = jnp.concatenate([lo[:, :, 1:], jnp.full((b, N_EXPERTS, 1), cap + base, F32)], axis=2)
    return lo, hi


def kernel(x, c, ctx, c_ctx, w_ada, b_ada, g_mix_pre, g_mix_post, g_ffn_pre, g_ffn_post,
           w_in, conv_qk, b_ml_gates, w_gla_a2, b_gla_a, g_ml_norm, g_gla_norm, w_out,
           w_router, w_e_gate, w_e_up, w_e_down):
    bsz, n_tok, d = x.shape
    lc = ctx.shape[1]
    depth = w_in.shape[0]
    ne = N_EXPERTS
    t = lc + n_tok
    ncb = lc // TB
    nblk = t // TB
    assert lc % TB == 0 and n_tok % TB == 0 and d == 2 * GROUP_W and TB % GRID_W == 0
    cap_l = EC_FACTOR * n_tok // ne
    cap_c = EC_FACTOR * lc // ne

    pos_r, pos_c = _pos_tables(n_tok // GRID_W, d)
    xa = None

    cc = jnp.zeros((8, d), F32).at[:bsz].set(c).at[bsz].set(c_ctx)
    mods = _ada(cc, w_ada, b_ada)

    rank = w_gla_a2.shape[2]
    ml_w, n_gates, gla_w = 4 * GROUP_W, 4 * N_HEADS, 3 * GROUP_W
    o_gla, o_a = ml_w + n_gates, ml_w + n_gates + gla_w
    assert w_in.shape[2] == o_a + 2 * rank and n_gates + 2 * rank <= LANES
    wide = jnp.concatenate([w_in[:, :, :ml_w], w_in[:, :, o_gla:o_a]], axis=2).astype(BF16)
    narrow = jnp.concatenate([w_in[:, :, ml_w:o_gla], w_in[:, :, o_a:]], axis=2)
    narrow = jnp.pad(narrow, ((0, 0), (0, 0), (0, LANES - narrow.shape[2])))
    narrow = jnp.concatenate(_split2(narrow), axis=2)
    bias_s = jnp.pad(b_ml_gates, ((0, 0), (0, LANES - b_ml_gates.shape[1])))
    w2e = jnp.zeros((depth, 2, LANES, N_HEADS * GLA_DK), F32)
    w2e = (w2e.at[:, 0, n_gates:n_gates + rank].set(w_gla_a2[:, 0])
           .at[:, 1, n_gates + rank:n_gates + 2 * rank].set(w_gla_a2[:, 1]).astype(BF16))
    m_lat = mods[:, :bsz].reshape(depth, bsz, 1, 6, d)
    m_ctx = jnp.broadcast_to(mods[:, bsz].reshape(depth, 1, 1, 6, d), (depth, bsz, 1, 6, d))
    mod = jnp.pad(jnp.concatenate([m_ctx, m_lat], axis=2), ((0, 0), (0, 0), (0, 0), (0, 2), (0, 0)))
    row = lambda a: a.reshape(depth, 1, -1)
    bias_s, ba = row(bias_s), b_gla_a.reshape(depth, 2, 1, -1)
    g_pre1, g_post1, g_pre2, g_post2 = row(g_mix_pre), row(g_mix_post), row(g_ffn_pre), row(g_ffn_post)
    g_ml, g_gla = row(g_ml_norm), row(g_gla_norm)
    wo = w_out.astype(BF16)
    wrt = w_router.transpose(0, 2, 1)

    for l in range(depth):
        last = l == depth - 1
        if l == 0:
            xa, pb, ps, qc = _in_proj(None, l, mod, g_pre1, wide, narrow, bias_s, conv_qk, ncb,
                                      first=(x, ctx, pos_r, pos_c))
        else:
            pb, ps, qc = _in_proj(xa, l, mod, g_pre1, wide, narrow, bias_s, conv_qk, ncb)
        hf, hb, of, ob = _mixers2(qc, pb, ps, l, w2e, ba, ncb)
        x1, h2e, aff = _out_proj(hf, hb, of, ob, pb, xa, l, mod, g_ml, g_gla, wo, g_post1, g_pre2, wrt, ncb)

        rl = n_tok // LANES
        aff_l = aff[:, :, lc:].reshape(bsz, ne * rl, LANES)
        if last:
            sll, offl = _route(None, aff_l, 0, cap_l)
            slots = jnp.pad(sll.reshape(bsz, ne, n_tok), ((0, 0), (0, 0), (lc, 0)), constant_values=UNSEL)
            lo, hi = _tile_bounds(offl, rl, nblk - ncb, cap_l, 0)
            t0, nt, m_rows = ncb, nblk - ncb, cap_l
        else:
            rc = max(lc // LANES, 8)
            aff_c = aff[:, :, :lc].reshape(bsz, ne, lc // LANES, LANES)
            aff_c = jnp.pad(aff_c, ((0, 0), (0, 0), (0, rc - lc // LANES), (0, 0)), constant_values=-1.0)
            slc, offc, sll, offl = _route(aff_c.reshape(bsz, ne * rc, LANES), aff_l, cap_c, cap_l)
            slots = jnp.concatenate([slc.reshape(bsz, ne, rc * LANES)[:, :, :lc],
                                     sll.reshape(bsz, ne, n_tok)], axis=2)
            lo_l, hi_l = _tile_bounds(offl, rl, nblk - ncb, cap_l, 0)
            lo_c, hi_c = _tile_bounds(offc, rc, ncb, cap_c, cap_l)
            lo = jnp.concatenate([lo_c, lo_l], axis=2)
            hi = jnp.concatenate([hi_c, hi_l], axis=2)
            t0, nt, m_rows = 0, nblk, cap_l + cap_c
        lohi = jnp.concatenate([lo, hi], axis=1).transpose(0, 2, 1).astype(I32).reshape(-1)
        xin = _dispatch(lohi, h2e, slots, m_rows, t0, nt)
        y = _experts(xin, w_e_gate, w_e_up, w_e_down, l, cap_l)
        xa = _combine(lohi, slots, aff, x1, l, mod, g_post2, y, t0, nt, ncb)
    return xa
```

```python
import functools

import jax
import jax.numpy as jnp
from jax import lax
from jax.experimental import pallas as pl
from jax.experimental.pallas import tpu as pltpu

F32 = jnp.float32
BF16 = jnp.bfloat16
I32 = jnp.int32

EPS = 1e-6
GRID_W = 64
POS_BASE = 10000.0
N_HEADS = 4
ML_DH = 128
GLA_DK = 64
GLA_DV = 128
GLA_GATE_TAU = 16.0
N_EXPERTS = 16
EC_FACTOR = 2

LANES = 128
SUBLANES = 8
TB = 256
ML_L = 128
GLA_L = 128
GLA_SAFE_SPAN = 80.0
ML_STAGE_UNITS = 2
EXPERT_ROWS = 256
WIN = 80
ROW_ALIGN = 16
UNSEL = 2047.0
GROUP_W = N_HEADS * ML_DH
PB_ML_V, PB_ML_O, PB_GLA_QK, PB_GLA_V, PB_GLA_R = 0, 1, 2, 3, 4
VMEM_LIMIT = 56 * 1024 * 1024


def _cparams(sem):
    return pltpu.CompilerParams(dimension_semantics=sem, vmem_limit_bytes=VMEM_LIMIT)


def _split2(a):
    hi = a.astype(BF16)
    lo = (a - hi.astype(F32)).astype(BF16)
    return hi, lo


def _split3(a):
    hi = a.astype(BF16)
    r = a - hi.astype(F32)
    mid = r.astype(BF16)
    lo = (r - mid.astype(F32)).astype(BF16)
    return hi, mid, lo


_NN = (((1,), (0,)), ((), ()))
_NT = (((1,), (1,)), ((), ()))
_TN = (((0,), (0,)), ((), ()))


def _mm(a, b, dims=_NN):
    return lax.dot_general(a, b, dims, preferred_element_type=F32)


def _dot3(a, b, dims=_NN):
    ah, al = _split2(a)
    bh, bl = _split2(b)
    return _mm(ah, bh, dims) + (_mm(ah, bl, dims) + _mm(al, bh, dims))


def _dot_exact_l(m_bf16, x, dims=_NN):
    hi, mid, lo = _split3(x)
    return _mm(m_bf16, hi, dims) + (_mm(m_bf16, mid, dims) + _mm(m_bf16, lo, dims))


def _rms(x, g):
    return x * lax.rsqrt(jnp.mean(x * x, axis=-1, keepdims=True) + EPS) * g


def _log_sigmoid(x):
    return jnp.minimum(x, 0.0) - jnp.log(1.0 + jnp.exp(-jnp.abs(x)))


def _sigmoid(x):
    return 1.0 / (1.0 + jnp.exp(-x))


def _silu(x):
    return x * _sigmoid(x)


def _iota(shape, dim):
    return lax.broadcasted_iota(I32, shape, dim)


def _rev_block(i, ncb, nblk):
    return jnp.where(i < ncb, ncb - 1 - i, nblk - 1 - (i - ncb))


def _ada_kernel(c_ref, w_ref, b_ref, o_ref):
    a = _silu(c_ref[...])
    o_ref[0] = _dot3(a, w_ref[0]) + b_ref[0]


def _ada(cc, w_ada, b_ada):
    depth, d, n6 = w_ada.shape
    tn = 1536
    return pl.pallas_call(
        _ada_kernel,
        grid=(depth, n6 // tn),
        in_specs=[pl.BlockSpec((8, d), lambda l, j: (0, 0)),
                  pl.BlockSpec((1, d, tn), lambda l, j: (l, 0, j)),
                  pl.BlockSpec((1, 1, tn), lambda l, j: (l, 0, j))],
        out_specs=pl.BlockSpec((1, 8, tn), lambda l, j: (l, 0, j)),
        out_shape=jax.ShapeDtypeStruct((depth, 8, n6), F32),
        compiler_params=_cparams(("parallel", "parallel")),
        name="ada",
    )(cc, w_ada, b_ada.reshape(depth, 1, n6))


def _project(ncb, xs, xps, xns, mod_ref, g_ref, wb_ref, ws_ref, bs_ref, cw_ref, pb_ref, ps_ref, qc_ref):
    i = pl.program_id(0)
    nblk = pl.num_programs(0)
    nb, n = len(xs), xs[0].shape[0]
    rows = [slice(b * n, (b + 1) * n) for b in range(nb)]

    gain = [g_ref[0] * (1.0 + mod_ref[0, b, 0, 1:2, :]) for b in range(nb)]

    def norm(z, b):
        return _rms(z, gain[b]) + mod_ref[0, b, 0, 0:1, :]

    hh = jnp.concatenate([norm(xs[b], b).astype(BF16) for b in range(nb)], axis=0)
    halo = ([norm(xps[b], b).astype(BF16) for b in range(nb)] + [norm(xns[b], b).astype(BF16) for b in range(nb)])
    wq = cw_ref.shape[2]
    qk = _mm(jnp.concatenate([hh] + halo, axis=0), wb_ref[0, :, 0:wq])
    lvalid = jnp.logical_and(i != 0, i != ncb)
    rvalid = jnp.logical_and(i != ncb - 1, i != nblk - 1)
    dq = wq // 2
    for b in range(nb):
        lrow = nb * n + 8 * b + 7
        rrow = nb * n + 8 * nb + 8 * b
        left = jnp.where(lvalid, qk[lrow:lrow + 1, :], 0.0)
        right = jnp.where(rvalid, qk[rrow:rrow + 1, :], 0.0)
        y = _silu(_conv3(qk[rows[b], :], left, right, cw_ref[0]))
        qc_ref[b, :, 0:dq] = (y[:, 0:dq] * (ML_DH ** -0.5)).astype(BF16)
        qc_ref[b, :, dq:] = y[:, dq:].astype(BF16)
    rest = _mm(hh, wb_ref[0, :, wq:])
    pr = _mm(hh, ws_ref[0])
    ps = pr[:, :LANES] + pr[:, LANES:] + bs_ref[0]
    lane = _iota(ps.shape, 1)
    forget = jnp.logical_and(lane % 8 >= N_HEADS, lane < 4 * N_HEADS)
    ps = jnp.where(forget, _log_sigmoid(ps), ps)
    for b in range(nb):
        pb_ref[b] = rest[rows[b], :]
        ps_ref[b] = ps[rows[b], :]


def _in_kernel(ncb, x_ref, xp_ref, xn_ref, mod_ref, g_ref, wb_ref, ws_ref, bs_ref, cw_ref,
               pb_ref, ps_ref, qc_ref):
    nb = x_ref.shape[0]
    _project(ncb, [x_ref[b] for b in range(nb)], [xp_ref[b] for b in range(nb)],
             [xn_ref[b] for b in range(nb)], mod_ref, g_ref, wb_ref, ws_ref, bs_ref, cw_ref,
             pb_ref, ps_ref, qc_ref)


def _in0_kernel(ncb, x_ref, xp_ref, xn_ref, c_ref, cp_ref, cn_ref, pr_ref, prp_ref, prn_ref, pc_ref,
                mod_ref, g_ref, wb_ref, ws_ref, bs_ref, cw_ref, xa_ref, pb_ref, ps_ref, qc_ref):
    i = pl.program_id(0)
    nb = x_ref.shape[0]
    half = pr_ref.shape[2]
    reps = TB // GRID_W
    prow = jnp.concatenate([jnp.broadcast_to(pr_ref[0, k:k + 1, :], (GRID_W, half)) for k in range(reps)], axis=0)
    pcol = jnp.concatenate([pc_ref[...]] * reps, axis=0)
    pos = jnp.concatenate([prow, pcol], axis=1)
    pos_p = jnp.concatenate([prp_ref[0, reps - 1:reps, :], pc_ref[GRID_W - 1:GRID_W, :]], axis=1)
    pos_n = jnp.concatenate([prn_ref[0, 0:1, :], pc_ref[0:1, :]], axis=1)
    is_ctx = i < ncb
    xs, xps, xns = [], [], []
    for b in range(nb):
        xa = jnp.where(is_ctx, c_ref[b], x_ref[b] + pos)
        xa_ref[b] = xa
        xs.append(xa)
        xps.append(jnp.where(is_ctx, cp_ref[b], xp_ref[b] + pos_p))
        xns.append(jnp.where(is_ctx, cn_ref[b], xn_ref[b] + pos_n))
    _project(ncb, xs, xps, xns, mod_ref, g_ref, wb_ref, ws_ref, bs_ref, cw_ref, pb_ref, ps_ref, qc_ref)


def _tile_and_halo_specs(b, rows, d, tile_of):
    r8 = TB // 8
    ntile, last8 = rows // TB, rows // 8 - 1
    tl = lambda i: jnp.clip(tile_of(i), 0, ntile - 1)
    return [pl.BlockSpec((b, TB, d), lambda i: (0, tl(i), 0)),
            pl.BlockSpec((b, 8, d), lambda i: (0, jnp.clip(tl(i) * r8 - 1, 0, last8), 0)),
            pl.BlockSpec((b, 8, d), lambda i: (0, jnp.clip((tl(i) + 1) * r8, 0, last8), 0))]


def _layer_spec(layer, arr):
    shp = arr.shape[1:]
    return pl.BlockSpec((1,) + shp, lambda *_: (layer,) + (0,) * len(shp))


def _mod_spec(layer, mod, ncb, tile_of=lambda i: i):
    _, b, _, r, d = mod.shape
    return pl.BlockSpec((1, b, 1, r, d), lambda i, *_: (layer, 0, jnp.where(tile_of(i) < ncb, 0, 1), 0, 0))


def _in_proj(xa, layer, mod, g, wb, ws, bs, cw, ncb, first=None):
    if first is None:
        b, t, d = xa.shape
    else:
        b, t, d = first[0].shape[0], first[0].shape[1] + first[1].shape[1], first[0].shape[2]
    wq = cw.shape[2]
    nb = wb.shape[2] - wq
    const = lambda shp: pl.BlockSpec(shp, lambda i: tuple(0 for _ in shp))
    tile = lambda w: pl.BlockSpec((b, TB, w), lambda i: (0, i, 0))
    common = [_mod_spec(layer, mod, ncb)] + [_layer_spec(layer, a) for a in (g, wb, ws, bs, cw)]
    out_specs = [tile(nb), tile(LANES), tile(wq)]
    out_shape = [jax.ShapeDtypeStruct((b, t, nb), F32),
                 jax.ShapeDtypeStruct((b, t, LANES), F32),
                 jax.ShapeDtypeStruct((b, t, wq), BF16)]
    if first is None:
        body = functools.partial(_in_kernel, ncb)
        in_specs = _tile_and_halo_specs(b, t, d, lambda i: i) + common
        args = (xa, xa, xa, mod, g, wb, ws, bs, cw)
    else:
        x, ctx, pos_r, pos_c = first
        reps = TB // GRID_W
        ntl = x.shape[1] // TB
        body = functools.partial(_in0_kernel, ncb)
        pr_spec = lambda off: pl.BlockSpec((1, reps, d // 2),
                                           lambda i: (jnp.clip(i - ncb + off, 0, ntl - 1), 0, 0))
        in_specs = (_tile_and_halo_specs(b, x.shape[1], d, lambda i: i - ncb)
                    + _tile_and_halo_specs(b, ctx.shape[1], d, lambda i: i)
                    + [pr_spec(0), pr_spec(-1), pr_spec(1), const((GRID_W, d // 2))] + common)
        out_specs = [tile(d)] + out_specs
        out_shape = [jax.ShapeDtypeStruct((b, t, d), F32)] + out_shape
        pr3 = pos_r.reshape(-1, reps, d // 2)
        args = (x, x, x, ctx, ctx, ctx, pr3, pr3, pr3, pos_c, mod, g, wb, ws, bs, cw)
    return pl.pallas_call(
        body,
        grid=(t // TB,),
        in_specs=in_specs,
        out_specs=out_specs,
        out_shape=out_shape,
        compiler_params=_cparams(("parallel",)),
        name="in_proj",
    )(*args)


def _conv3(x, hl, hr, w):
    rows = _iota(x.shape, 0)
    prev = jnp.where(rows == 0, hl, pltpu.roll(x, 1, axis=0))
    nxt = jnp.where(rows == x.shape[0] - 1, hr, pltpu.roll(x, x.shape[0] - 1, axis=0))
    return prev * w[0:1] + x * w[1:2] + nxt * w[2:3]


def _cummax_rows(x, reverse):
    n = x.shape[0]
    rows = _iota(x.shape, 0)
    s = 1
    while s < n:
        if reverse:
            sh = jnp.where(rows < n - s, pltpu.roll(x, n - s, axis=0), -jnp.inf)
        else:
            sh = jnp.where(rows >= s, pltpu.roll(x, s, axis=0), -jnp.inf)
        x = jnp.maximum(x, sh)
        s *= 2
    return x


def _ml_pair(sb, fwd, bwd, c_s, m_s):
    ll = ML_L
    nch = TB // ll
    dq = N_HEADS * ML_DH
    rows = _iota((ll, ll), 0)
    cols = _iota((ll, ll), 1)
    causal = [cols <= rows, cols >= rows]
    tri = [jnp.where(m, 1.0, 0.0).astype(BF16) for m in causal]
    ones = jnp.ones((ll, ML_DH), BF16)
    qk = [fwd[0][sb], bwd[0][sb]]
    v = [fwd[1][sb], bwd[1][sb]]
    g = [fwd[2][sb], bwd[2][sb]]
    outs = [fwd[3], bwd[3]]
    units = [(d, h) for d in range(2) for h in range(N_HEADS)]
    sidx = lambda d, h: (2 * sb + d) * N_HEADS + h
    cx = {(d, h): c_s[sidx(d, h)] for d, h in units}
    m_row = [m_s[2 * sb], m_s[2 * sb + 1]]
    for step in range(nch):
        r0 = [step * ll, (nch - 1 - step) * ll]
        alpha, a_in, em, e_w, ut, a_old, a_new = [], [], [], [], [], [], []
        for d in range(2):
            gc = g[d][r0[d]:r0[d] + ll, :]
            bc = _dot_exact_l(tri[d], gc)
            u = pltpu.roll(gc, 4, axis=1) - bc
            cm = _cummax_rows(u, bool(d))
            neg_alpha = jnp.maximum(m_row[d], cm)
            alpha.append(-neg_alpha)
            a_in.append(jnp.exp(m_row[d] - neg_alpha))
            em.append(jnp.exp(-neg_alpha - bc))
            last = slice(0, 1) if d else slice(ll - 1, ll)
            cm_end = cm[last, :]
            bend = bc[last, :]
            e_w.append(jnp.exp(u - cm_end))
            ut.append(u.T)
            m_kv = bend + cm_end
            m_new = jnp.maximum(bend + m_row[d], m_kv)
            a_old.append(jnp.exp(bend + m_row[d] - m_new))
            a_new.append(jnp.exp(m_kv - m_new))
            m_row[d] = m_new
        lane = lambda d, h: 8 * d + 4 + h
        for g0 in range(0, len(units), ML_STAGE_UNITS):
            grp = units[g0:g0 + ML_STAGE_UNITS]
            qb = {(d, h): qk[d][r0[d]:r0[d] + ll, h * ML_DH:(h + 1) * ML_DH] for d, h in grp}
            kb = {(d, h): qk[d][r0[d]:r0[d] + ll, dq + h * ML_DH:dq + (h + 1) * ML_DH] for d, h in grp}
            vh = {(d, h): v[d][r0[d]:r0[d] + ll, h * ML_DH:(h + 1) * ML_DH] for d, h in grp}
            sc = {u: _mm(qb[u], kb[u], _NT) for u in grp}
            lhs = {}
            for d, h in grp:
                c = lane(d, h)
                arg = jnp.where(causal[d], alpha[d][:, c:c + 1] + ut[d][c:c + 1, :], -jnp.inf)
                sbf = (sc[(d, h)] * jnp.exp(arg)).astype(BF16)
                aq = (a_in[d][:, c:c + 1] * qb[(d, h)].astype(F32)).astype(BF16)
                lhs[(d, h)] = jnp.concatenate([sbf, aq], axis=1)
            ckv = {}
            for d, h in grp:
                c = lane(d, h)
                ew = e_w[d][:, c:c + 1]
                ev = jnp.concatenate([(ew * vh[(d, h)]).astype(BF16),
                                      jnp.broadcast_to(ew, (ll, ML_DH)).astype(BF16)], axis=1)
                ckv[(d, h)] = _mm(kb[(d, h)], ev, _TN)
            for d, h in grp:
                c = lane(d, h)
                rhs = jnp.concatenate([jnp.concatenate([vh[(d, h)].astype(BF16), ones], axis=1),
                                       cx[(d, h)].astype(BF16)], axis=0)
                nd = _mm(lhs[(d, h)], rhs)
                den = jnp.maximum(jnp.abs(nd[:, ML_DH:]), em[d][:, c:c + 1])
                outs[d][sb, r0[d]:r0[d] + ll, h * ML_DH:(h + 1) * ML_DH] = (nd[:, :ML_DH] / den).astype(BF16)
            for d, h in grp:
                c = lane(d, h)
                cx[(d, h)] = a_old[d][:, c:c + 1] * cx[(d, h)] + a_new[d][:, c:c + 1] * ckv[(d, h)]
    for d, h in units:
        c_s[sidx(d, h)] = cx[(d, h)]
    m_s[2 * sb] = m_row[0]
    m_s[2 * sb + 1] = m_row[1]


def _gla_pairwise(d, q, b, x_s):
    ll = q.shape[0]
    trow = _iota((ll, 1), 0)

    def body(g, acc):
        blk = x_s[pl.ds(pl.multiple_of(g * SUBLANES, SUBLANES), SUBLANES), :]
        for j in range(SUBLANES):
            s = g * SUBLANES + j
            ks, bs, vs = blk[j:j + 1, 0:GLA_DK], blk[j:j + 1, GLA_DK:2 * GLA_DK], blk[j:j + 1, 2 * GLA_DK:]
            w = jnp.exp(jnp.minimum(b - bs, 0.0))
            col = jnp.sum(q * ks * w, axis=1, keepdims=True)
            seen = (trow <= s) if d else (trow >= s)
            acc = acc + jnp.where(seen, col, 0.0) * vs
        return acc

    return lax.fori_loop(0, ll // SUBLANES, body, jnp.zeros((ll, GLA_DV), F32))


def _gla_pair(sb, fwd, bwd, w2_ref, ba_ref, s_s, x_s):
    ll = GLA_L
    nch = TB // ll
    dkw = N_HEADS * GLA_DK
    rows = _iota((ll, ll), 0)
    cols = _iota((ll, ll), 1)
    causal = [cols <= rows, cols >= rows]
    tri = [jnp.where(m, 1.0, 0.0).astype(BF16) for m in causal]
    qk = [fwd[0][sb], bwd[0][sb]]
    v = [fwd[1][sb], bwd[1][sb]]
    outs = [fwd[3], bwd[3]]
    la = [_log_sigmoid(_mm(r[2][sb].astype(BF16), w2_ref[0, d]) + ba_ref[0, d]) * (1.0 / GLA_GATE_TAU)
          for d, r in enumerate((fwd, bwd))]
    units = [(d, h) for d in range(2) for h in range(N_HEADS)]
    sidx = lambda d, h: (2 * sb + d) * N_HEADS + h
    st = {(d, h): s_s[sidx(d, h)] for d, h in units}
    hs = lambda a, h, w: a[:, h * w:(h + 1) * w]

    def cum_decay(d, r):
        lah, lal = _split2(la[d][r:r + ll, :])
        return _mm(tri[d], lah) + _mm(tri[d], lal)

    get_q = lambda d, r: qk[d][r:r + ll, 0:dkw] * (GLA_DK ** -0.5)
    get_k = lambda d, r: qk[d][r:r + ll, dkw:2 * dkw]
    chunk_decay = [jnp.sum(la[d][c * ll:(c + 1) * ll, :], axis=0, keepdims=True)
                   for d in range(2) for c in range(nch)]
    steep = jnp.min(functools.reduce(jnp.minimum, chunk_decay)) < -GLA_SAFE_SPAN
    for step in range(nch):
        r0 = [step * ll, (nch - 1 - step) * ll]
        qi, qt, kt, kd, e_end = [], [], [], [], []
        for d in range(2):
            bcum = cum_decay(d, r0[d])
            ref = bcum[ll // 2:ll // 2 + 1, :]
            bend = bcum[0:1, :] if d else bcum[ll - 1:ll, :]
            q, k = get_q(d, r0[d]), get_k(d, r0[d])
            qi.append((q * jnp.exp(bcum)).astype(BF16))
            qt.append((q * jnp.exp(bcum - ref)).astype(BF16))
            kt.append((k * jnp.exp(ref - bcum)).astype(BF16))
            kd.append((k * jnp.exp(bend - bcum)).astype(BF16))
            e_end.append(jnp.exp(bend))
        vb = {(d, h): hs(v[d][r0[d]:r0[d] + ll, :], h, GLA_DV).astype(BF16) for d, h in units}
        att = {(d, h): _mm(hs(qt[d], h, GLA_DK), hs(kt[d], h, GLA_DK), _NT) for d, h in units}
        inter = {(d, h): _mm(hs(qi[d], h, GLA_DK), st[(d, h)].astype(BF16), _NT) for d, h in units}
        attb = {(d, h): jnp.where(causal[d], att[(d, h)], 0.0).astype(BF16) for d, h in units}
        kv = {(d, h): _mm(vb[(d, h)], hs(kd[d], h, GLA_DK), _TN) for d, h in units}
        for d, h in units:
            o = _mm(attb[(d, h)], vb[(d, h)]) + inter[(d, h)]
            outs[d][sb, r0[d]:r0[d] + ll, h * GLA_DV:(h + 1) * GLA_DV] = o.astype(BF16)

        @pl.when(steep)
        def _():
            for d in range(2):
                bcum, q, k = cum_decay(d, r0[d]), get_q(d, r0[d]), get_k(d, r0[d])
                for h in range(N_HEADS):
                    x_s[:, 0:GLA_DK] = hs(k, h, GLA_DK)
                    x_s[:, GLA_DK:2 * GLA_DK] = hs(bcum, h, GLA_DK)
                    x_s[:, 2 * GLA_DK:] = hs(v[d][r0[d]:r0[d] + ll, :], h, GLA_DV)
                    o = _gla_pairwise(d, hs(q, h, GLA_DK), hs(bcum, h, GLA_DK), x_s) + inter[(d, h)]
                    outs[d][sb, r0[d]:r0[d] + ll, h * GLA_DV:(h + 1) * GLA_DV] = o.astype(BF16)

        st = {(d, h): st[(d, h)] * hs(e_end[d], h, GLA_DK) + kv[(d, h)] for d, h in units}
    for d, h in units:
        s_s[sidx(d, h)] = st[(d, h)]


def _mix2_kernel(qcf, mvf, psf, gqf, gvf, qcb, mvb, psb, gqb, gvb, w2, ba,
                 mf_ref, mb_ref, gf_ref, gb_ref, c_s, m_s, s_s, x_s):
    @pl.when(pl.program_id(0) == 0)
    def _():
        c_s[...] = jnp.zeros_like(c_s)
        m_s[...] = jnp.zeros_like(m_s)
        s_s[...] = jnp.zeros_like(s_s)

    for sb in range(qcf.shape[0]):
        _ml_pair(sb, (qcf, mvf, psf, mf_ref), (qcb, mvb, psb, mb_ref), c_s, m_s)
        _gla_pair(sb, (gqf, gvf, psf, gf_ref), (gqb, gvb, psb, gb_ref), w2, ba, s_s, x_s)


def _mixers2(qc, pb, ps, layer, w2e, ba, ncb):
    b, t, _ = pb.shape
    nblk = t // TB
    fwd = lambda i: i
    bwd = lambda i: _rev_block(i, ncb, nblk)

    def dspecs(blk):
        col = lambda cb: pl.BlockSpec((b, TB, GROUP_W), lambda i: (0, blk(i), cb))
        return [pl.BlockSpec((b, TB, qc.shape[2]), lambda i: (0, blk(i), 0)), col(PB_ML_V),
                pl.BlockSpec((b, TB, LANES), lambda i: (0, blk(i), 0)), col(PB_GLA_QK), col(PB_GLA_V)]

    specs = dspecs(fwd) + dspecs(bwd) + [_layer_spec(layer, w2e), _layer_spec(layer, ba)]
    ns = 2 * N_HEADS * b
    ofwd = pl.BlockSpec((b, TB, GROUP_W), lambda i: (0, i, 0))
    obwd = pl.BlockSpec((b, TB, GROUP_W), lambda i: (0, bwd(i), 0))
    return pl.pallas_call(
        _mix2_kernel,
        grid=(nblk,),
        in_specs=specs,
        out_specs=[ofwd, obwd, ofwd, obwd],
        out_shape=[jax.ShapeDtypeStruct((b, t, GROUP_W), BF16)] * 4,
        scratch_shapes=[pltpu.VMEM((ns, ML_DH, 2 * ML_DH), F32),
                        pltpu.VMEM((2 * b, 1, LANES), F32),
                        pltpu.VMEM((ns, GLA_DV, GLA_DK), F32),
                        pltpu.VMEM((GLA_L, 2 * GLA_DK + GLA_DV), F32)],
        compiler_params=_cparams(("arbitrary",)),
        name="mixers",
    )(*([qc, pb, ps, pb, pb] * 2), w2e, ba)


def _head_norm(x, g):
    outs = []
    for h in range(N_HEADS):
        seg = x[:, h * 128:(h + 1) * 128]
        outs.append(seg * lax.rsqrt(jnp.mean(seg * seg, axis=-1, keepdims=True) + EPS))
    return jnp.concatenate(outs, axis=-1) * g


def _out_kernel(hf, hb, of, ob, og, rg, x_ref, mod_ref, gml, ggla, wo, gpost, gpre, wrt,
                x1_ref, h2e_ref, aff_ref):
    nb = x_ref.shape[0]
    rh = LANES
    groups = [(b, slice(k * rh, (k + 1) * rh)) for b in range(nb) for k in range(TB // rh)]
    f32 = lambda ref, b, r: ref[b, r, :].astype(F32)
    y = [jnp.concatenate([_head_norm(f32(hf, b, r) + f32(hb, b, r), gml[0]) * _sigmoid(og[b, r, :]),
                          _head_norm(f32(of, b, r) + f32(ob, b, r), ggla[0]) * _silu(rg[b, r, :])],
                         axis=-1).astype(BF16) for b, r in groups]
    y2 = _mm(jnp.concatenate(y, axis=0), wo[0])
    gate1 = [mod_ref[0, b, 0, 2:3, :] * gpost[0] for b in range(nb)]
    gain2 = [gpre[0] * (1.0 + mod_ref[0, b, 0, 4:5, :]) for b in range(nb)]
    x1 = [x_ref[b, r, :] + _rms(y2[g * rh:(g + 1) * rh, :], gate1[b]) for g, (b, r) in enumerate(groups)]
    for (b, r), x1k in zip(groups, x1):
        x1_ref[b, r, :] = x1k
    h2 = [_rms(x1k, gain2[b]) + mod_ref[0, b, 0, 3:4, :] for (b, r), x1k in zip(groups, x1)]
    lt = _dot3(wrt[0], jnp.concatenate(h2, axis=0), _NT)
    for g, ((b, r), h2k) in enumerate(zip(groups, h2)):
        ltk = lt[:, g * rh:(g + 1) * rh]
        ext = jnp.exp(ltk - jnp.max(ltk, axis=0, keepdims=True))
        affk = ext / jnp.sum(ext, axis=0, keepdims=True)
        aff_ref[b, :, r] = affk
        h2e_ref[b, r, :] = h2k.astype(BF16)


def _out_proj(hf, hb, of, ob, pb, xa, layer, mod, gml, ggla, wo, gpost, gpre, wrt, ncb):
    b, t, d = xa.shape
    tile = lambda w, cb: pl.BlockSpec((b, TB, w), lambda i: (0, i, cb))
    return pl.pallas_call(
        _out_kernel,
        grid=(t // TB,),
        in_specs=[tile(GROUP_W, 0)] * 4 + [tile(GROUP_W, PB_ML_O), tile(GROUP_W, PB_GLA_R),
                  tile(d, 0), _mod_spec(layer, mod, ncb)]
                 + [_layer_spec(layer, a) for a in (gml, ggla, wo, gpost, gpre, wrt)],
        out_specs=[tile(d, 0), tile(d, 0),
                   pl.BlockSpec((b, N_EXPERTS, TB), lambda i: (0, 0, i))],
        out_shape=[jax.ShapeDtypeStruct((b, t, d), F32),
                   jax.ShapeDtypeStruct((b, t, d), BF16),
                   jax.ShapeDtypeStruct((b, N_EXPERTS, t), F32)],
        compiler_params=_cparams(("parallel",)),
        name="out_proj",
    )(hf, hb, of, ob, pb, pb, xa, mod, gml, ggla, wo, gpost, gpre, wrt)


def _cumsum_blocks(x, r):
    n = x.shape[0]
    xb = x.astype(BF16)
    li = _iota((LANES, LANES), 0)
    lj = _iota((LANES, LANES), 1)
    upper = jnp.where(li <= lj, 1.0, 0.0).astype(BF16)
    ones = jnp.ones((LANES, LANES), BF16)
    inrow = _mm(xb, upper)
    tot = _mm(xb, ones)
    ri = _iota((n, n), 0)
    rj = _iota((n, n), 1)
    same = (ri // r) == (rj // r)
    strict = jnp.where(jnp.logical_and(same, rj < ri), 1.0, 0.0).astype(BF16)
    off = _mm(strict, tot.astype(BF16))
    return inrow + off, off


def _select(aff, r, cap, base_slot):
    ne = N_EXPERTS
    n = ne * r
    aff3 = aff.reshape(ne, r, LANES)
    capf = jnp.float32(cap)

    def body(k, prefix):
        cand = prefix | (jnp.int32(1) << (30 - k))
        candf = lax.bitcast_convert_type(cand, F32)
        cnt = jnp.sum(jnp.where(aff3 >= candf, 1.0, 0.0), axis=(1, 2), keepdims=True)
        return jnp.where(cnt >= capf, cand, prefix)

    thr = lax.bitcast_convert_type(lax.fori_loop(0, 31, body, jnp.zeros((ne, 1, 1), I32)), F32)
    gt = jnp.where(aff3 > thr, 1.0, 0.0)
    eq = jnp.where(aff3 == thr, 1.0, 0.0)
    need = capf - jnp.sum(gt, axis=(1, 2), keepdims=True)
    eq2 = eq.reshape(n, LANES)
    cs_eq, _ = _cumsum_blocks(eq2, r)
    eq_rank = (cs_eq - eq2).reshape(ne, r, LANES)
    sel = (gt + eq * jnp.where(eq_rank < need, 1.0, 0.0)).reshape(n, LANES)
    cs, off = _cumsum_blocks(sel, r)
    slot = jnp.where(sel > 0.5, cs - 1.0 + base_slot, UNSEL)
    return slot, off


def _sel_kernel(rc, rl, cap_c, cap_l, *refs):
    if rc:
        affc, affl, slc, offc, sll, offl = refs
        slc[0], offc[0] = _select(affc[0], rc, cap_c, float(cap_l))
    else:
        affl, sll, offl = refs
    sll[0], offl[0] = _select(affl[0], rl, cap_l, 0.0)


def _route(aff_c, aff_l, cap_c, cap_l):
    b = aff_l.shape[0]
    ne = N_EXPERTS
    rl = aff_l.shape[1] // ne
    rc = aff_c.shape[1] // ne if aff_c is not None else 0
    args = ([aff_c] if rc else []) + [aff_l]
    in_specs, out_shape, out_specs = [], [], []
    for a in args:
        spec = pl.BlockSpec((1,) + a.shape[1:], lambda bi: (bi, 0, 0))
        in_specs.append(spec)
        out_shape += [jax.ShapeDtypeStruct(a.shape, F32)] * 2
        out_specs += [spec, spec]
    return pl.pallas_call(
        functools.partial(_sel_kernel, rc, rl, cap_c, cap_l),
        grid=(b,),
        in_specs=in_specs,
        out_specs=out_specs,
        out_shape=out_shape,
        compiler_params=_cparams(("parallel",)),
        name="route",
    )(*args)


def _window(lo_ref, base, e, m_rows):
    lo_e = lo_ref[base + e]
    hi_e = lo_ref[base + N_EXPERTS + e]
    a_e = jnp.minimum((lo_e // ROW_ALIGN) * ROW_ALIGN, m_rows - WIN)
    return a_e, hi_e


def _disp_kernel(m_rows, ng, lo_ref, h_ref, slot_ref, x_ref):
    ne = N_EXPERTS
    gi = pl.program_id(1)
    i = pl.program_id(2)
    base = (pl.program_id(0) * pl.num_programs(2) + i) * (2 * ne)

    @pl.when(i == 0)
    def _():
        x_ref[...] = jnp.zeros_like(x_ref)

    h = h_ref[0]
    sl = slot_ref[0]
    sub = _iota((WIN, TB), 0).astype(F32)
    wins = [_window(lo_ref, base, gi * ng + k, m_rows) for k in range(ng)]

    def onehot(k, first):
        a_r = jnp.minimum(first, m_rows - WIN)
        srow = sl[k:k + 1, :]
        hit = jnp.logical_and(srow - a_r.astype(F32) == sub, srow >= first.astype(F32))
        return jnp.where(hit, 1.0, 0.0).astype(BF16), a_r

    def add_rows(k, a_r, g):
        rows = pl.ds(pl.multiple_of(a_r, ROW_ALIGN), WIN)
        x_ref[0, k, rows, :] = x_ref[0, k, rows, :] + g

    sel = [onehot(k, a_e) for k, (a_e, _) in enumerate(wins)]
    g = _mm(jnp.concatenate([w for w, _ in sel], axis=0), h).astype(BF16)
    for k, (_, a_r) in enumerate(sel):
        add_rows(k, a_r, g[k * WIN:(k + 1) * WIN, :])

    for k, (a_e, hi_e) in enumerate(wins):
        @pl.when(hi_e - a_e > WIN)
        def _(k=k, a_e=a_e, hi_e=hi_e):
            def more(rd, carry):
                w, a_r = onehot(k, a_e + rd * WIN)
                add_rows(k, a_r, _mm(w, h).astype(BF16))
                return carry

            lax.fori_loop(1, (hi_e - a_e + WIN - 1) // WIN, more, 0)


def _dispatch(lohi, h2e, slots, m_rows, t0, nt):
    b, t, de = h2e.shape
    ne = N_EXPERTS
    ng = 8
    grid_spec = pltpu.PrefetchScalarGridSpec(
        num_scalar_prefetch=1,
        grid=(b, ne // ng, nt),
        in_specs=[pl.BlockSpec((1, TB, de), lambda bi, gi, i, *_: (bi, i + t0, 0)),
                  pl.BlockSpec((1, ng, TB), lambda bi, gi, i, *_: (bi * (ne // ng) + gi, 0, i + t0))],
        out_specs=pl.BlockSpec((1, ng, m_rows, de), lambda bi, gi, i, *_: (bi, gi, 0, 0)))
    return pl.pallas_call(
        functools.partial(_disp_kernel, m_rows, ng),
        grid_spec=grid_spec,
        out_shape=jax.ShapeDtypeStruct((b, ne, m_rows, de), BF16),
        compiler_params=_cparams(("parallel", "parallel", "arbitrary")),
        name="dispatch",
    )(lohi, h2e, slots.reshape(b * (ne // ng), ng, t))


def _row_chunks(m_rows, cap_l):
    step = min(EXPERT_ROWS, cap_l)
    starts = list(range(0, cap_l, step))
    return [(s, (m_rows - s) if s == starts[-1] else step) for s in starts]


def _exp_kernel(m_rows, cap_l, x_ref, wg_ref, wu_ref, wd_ref, y_ref, wg_s, wu_s, wd_s):
    @pl.when(pl.program_id(1) == 0)
    def _():
        wg_s[...] = wg_ref[0, 0].astype(BF16)
        wu_s[...] = wu_ref[0, 0].astype(BF16)
        wd_s[...] = wd_ref[0, 0].astype(BF16)

    for r0, mc in _row_chunks(m_rows, cap_l):
        xs = x_ref[0, 0, r0:r0 + mc, :]
        hid = _silu(_mm(xs, wg_s[...])) * _mm(xs, wu_s[...])
        y_ref[0, 0, r0:r0 + mc, :] = _mm(hid.astype(BF16), wd_s[...]).astype(BF16)


def _experts(xin, wg, wu, wd, layer, cap_l):
    b, ne, m_rows, de = xin.shape
    _, _, d, f = wg.shape
    return pl.pallas_call(
        functools.partial(_exp_kernel, m_rows, cap_l),
        grid=(ne, b),
        in_specs=[pl.BlockSpec((1, 1, m_rows, de), lambda ei, bi: (bi, ei, 0, 0)),
                  pl.BlockSpec((1, 1, d, f), lambda ei, bi: (layer, ei, 0, 0)),
                  pl.BlockSpec((1, 1, d, f), lambda ei, bi: (layer, ei, 0, 0)),
                  pl.BlockSpec((1, 1, f, d), lambda ei, bi: (layer, ei, 0, 0))],
        out_specs=pl.BlockSpec((1, 1, m_rows, d), lambda ei, bi: (bi, ei, 0, 0)),
        out_shape=jax.ShapeDtypeStruct((b, ne, m_rows, d), BF16),
        scratch_shapes=[pltpu.VMEM((d, f), BF16), pltpu.VMEM((d, f), BF16), pltpu.VMEM((f, d), BF16)],
        compiler_params=_cparams(("arbitrary", "arbitrary")),
        name="experts",
    )(xin, wg, wu, wd)


def _comb_kernel(m_rows, lo_ref, slot_ref, aff_ref, x1_ref, mod_ref, gpost, y_ref, out_ref, acc_s):
    ne = N_EXPERTS
    i = pl.program_id(1)
    base = (pl.program_id(0) * pl.num_programs(1) + i) * (2 * ne)
    kk = ne * WIN
    sl = slot_ref[0]
    hi = jnp.floor(sl * (1.0 / 32.0))
    lo = sl - hi * 32.0
    col_e = _iota((ne, kk), 1) // WIN
    expand = jnp.where(col_e == _iota((ne, kk), 0), 1.0, 0.0).astype(BF16)
    sx = _mm(hi.astype(BF16), expand, _TN) * 32.0 + _mm(lo.astype(BF16), expand, _TN)
    gx = _mm(aff_ref[0].astype(BF16), expand, _TN)
    col = _iota((1, kk), 1)
    jrow = (col % WIN).astype(F32)
    wins = [_window(lo_ref, base, e, m_rows) for e in range(ne)]
    ys = []
    arow = jnp.zeros((1, kk), F32)
    for e, (a_e, _) in enumerate(wins):
        arow = jnp.where(col // WIN == e, a_e.astype(F32), arow)
        ys.append(y_ref[0, e, pl.ds(pl.multiple_of(a_e, ROW_ALIGN), WIN), :])
    w = jnp.where(sx - arow == jrow, gx, 0.0).astype(BF16)
    acc_s[...] = _mm(w, jnp.concatenate(ys, axis=0))

    lane = _iota((TB, WIN), 1).astype(F32)
    for e, (a_e, hi_e) in enumerate(wins):
        @pl.when(hi_e - a_e > WIN)
        def _(e=e, a_e=a_e, hi_e=hi_e):
            scol = sx[:, e * WIN:e * WIN + 1]
            gcol = gx[:, e * WIN:e * WIN + 1]

            def more(rd, carry):
                first = a_e + rd * WIN
                a_r = jnp.minimum(first, m_rows - WIN)
                hit = jnp.logical_and(scol - a_r.astype(F32) == lane, scol >= first.astype(F32))
                ye = y_ref[0, e, pl.ds(pl.multiple_of(a_r, ROW_ALIGN), WIN), :]
                acc_s[...] += _mm(jnp.where(hit, gcol, 0.0).astype(BF16), ye)
                return carry

            lax.fori_loop(1, (hi_e - a_e + WIN - 1) // WIN, more, 0)

    out_ref[0] = x1_ref[0] + _rms(acc_s[...], mod_ref[0, 0, 0, 5:6, :] * gpost[0])


def _combine(lohi, slots, aff, x1, layer, mod, gpost, y, t0, nt, ncb):
    b, t, d = x1.shape
    ne = N_EXPERTS
    m_rows = y.shape[2]
    r8 = mod.shape[3]
    grid_spec = pltpu.PrefetchScalarGridSpec(
        num_scalar_prefetch=1,
        grid=(b, nt),
        in_specs=[pl.BlockSpec((1, ne, TB), lambda bi, i, *_: (bi, 0, i + t0)),
                  pl.BlockSpec((1, ne, TB), lambda bi, i, *_: (bi, 0, i + t0)),
                  pl.BlockSpec((1, TB, d), lambda bi, i, *_: (bi, i + t0, 0)),
                  pl.BlockSpec((1, 1, 1, r8, d),
                               lambda bi, i, *_: (layer, bi, jnp.where(i + t0 < ncb, 0, 1), 0, 0)),
                  _layer_spec(layer, gpost),
                  pl.BlockSpec((1, ne, m_rows, d), lambda bi, i, *_: (bi, 0, 0, 0),
                               pipeline_mode=pl.Buffered(1))],
        out_specs=pl.BlockSpec((1, TB, d), lambda bi, i, *_: (bi, i, 0)),
        scratch_shapes=[pltpu.VMEM((TB, d), F32)])
    return pl.pallas_call(
        functools.partial(_comb_kernel, m_rows),
        grid_spec=grid_spec,
        out_shape=jax.ShapeDtypeStruct((b, nt * TB, d), F32),
        compiler_params=_cparams(("parallel", "arbitrary")),
        name="combine",
    )(lohi, slots, aff, x1, mod, gpost, y)


def _pos_tables(rows, d):
    quarter = d // 4
    freq = jnp.power(POS_BASE, -jnp.arange(quarter, dtype=F32) / quarter)
    ar = jnp.arange(rows, dtype=F32)[:, None] * freq
    ac = jnp.arange(GRID_W, dtype=F32)[:, None] * freq
    return (jnp.concatenate([jnp.sin(ar), jnp.cos(ar)], axis=-1),
            jnp.concatenate([jnp.sin(ac), jnp.cos(ac)], axis=-1))


def _tile_bounds(off, r, ntile, cap, base):
    b = off.shape[0]
    o = off.reshape(b, N_EXPERTS, r, LANES)[:, :, :, 0]
    lo = o[:, :, ::TB // LANES][:, :, :ntile] + base
    hi = jnp.concatenate([lo[:, :, 1:], jnp.full((b, N_EXPERTS, 1), cap + base, F32)], axis=2)
    return lo, hi


def kernel(x, c, ctx, c_ctx, w_ada, b_ada, g_mix_pre, g_mix_post, g_ffn_pre, g_ffn_post,
           w_in, conv_qk, b_ml_gates, w_gla_a2, b_gla_a, g_ml_norm, g_gla_norm, w_out,
           w_router, w_e_gate, w_e_up, w_e_down):
    bsz, n_tok, d = x.shape
    lc = ctx.shape[1]
    depth = w_in.shape[0]
    ne = N_EXPERTS
    t = lc + n_tok
    ncb = lc // TB
    nblk = t // TB
    assert lc % TB == 0 and n_tok % TB == 0 and d == 2 * GROUP_W and TB % GRID_W == 0
    cap_l = EC_FACTOR * n_tok // ne
    cap_c = EC_FACTOR * lc // ne

    pos_r, pos_c = _pos_tables(n_tok // GRID_W, d)
    xa = None

    cc = jnp.zeros((8, d), F32).at[:bsz].set(c).at[bsz].set(c_ctx)
    mods = _ada(cc, w_ada, b_ada)

    rank = w_gla_a2.shape[2]
    ml_w, n_gates, gla_w = 4 * GROUP_W, 4 * N_HEADS, 3 * GROUP_W
    o_gla, o_a = ml_w + n_gates, ml_w + n_gates + gla_w
    assert w_in.shape[2] == o_a + 2 * rank and n_gates + 2 * rank <= LANES
    wide = jnp.concatenate([w_in[:, :, :ml_w], w_in[:, :, o_gla:o_a]], axis=2).astype(BF16)
    narrow = jnp.concatenate([w_in[:, :, ml_w:o_gla], w_in[:, :, o_a:]], axis=2)
    narrow = jnp.pad(narrow, ((0, 0), (0, 0), (0, LANES - narrow.shape[2])))
    narrow = jnp.concatenate(_split2(narrow), axis=2)
    bias_s = jnp.pad(b_ml_gates, ((0, 0), (0, LANES - b_ml_gates.shape[1])))
    w2e = jnp.zeros((depth, 2, LANES, N_HEADS * GLA_DK), F32)
    w2e = (w2e.at[:, 0, n_gates:n_gates + rank].set(w_gla_a2[:, 0])
           .at[:, 1, n_gates + rank:n_gates + 2 * rank].set(w_gla_a2[:, 1]).astype(BF16))
    m_lat = mods[:, :bsz].reshape(depth, bsz, 1, 6, d)
    m_ctx = jnp.broadcast_to(mods[:, bsz].reshape(depth, 1, 1, 6, d), (depth, bsz, 1, 6, d))
    mod = jnp.pad(jnp.concatenate([m_ctx, m_lat], axis=2), ((0, 0), (0, 0), (0, 0), (0, 2), (0, 0)))
    row = lambda a: a.reshape(depth, 1, -1)
    bias_s, ba = row(bias_s), b_gla_a.reshape(depth, 2, 1, -1)
    g_pre1, g_post1, g_pre2, g_post2 = row(g_mix_pre), row(g_mix_post), row(g_ffn_pre), row(g_ffn_post)
    g_ml, g_gla = row(g_ml_norm), row(g_gla_norm)
    wo = w_out.astype(BF16)
    wrt = w_router.transpose(0, 2, 1)

    for l in range(depth):
        last = l == depth - 1
        if l == 0:
            xa, pb, ps, qc = _in_proj(None, l, mod, g_pre1, wide, narrow, bias_s, conv_qk, ncb,
                                      first=(x, ctx, pos_r, pos_c))
        else:
            pb, ps, qc = _in_proj(xa, l, mod, g_pre1, wide, narrow, bias_s, conv_qk, ncb)
        hf, hb, of, ob = _mixers2(qc, pb, ps, l, w2e, ba, ncb)
        x1, h2e, aff = _out_proj(hf, hb, of, ob, pb, xa, l, mod, g_ml, g_gla, wo, g_post1, g_pre2, wrt, ncb)

        rl = n_tok // LANES
        aff_l = aff[:, :, lc:].reshape(bsz, ne * rl, LANES)
        if last:
            sll, offl = _route(None, aff_l, 0, cap_l)
            slots = jnp.pad(sll.reshape(bsz, ne, n_tok), ((0, 0), (0, 0), (lc, 0)), constant_values=UNSEL)
            lo, hi = _tile_bounds(offl, rl, nblk - ncb, cap_l, 0)
            t0, nt, m_rows = ncb, nblk - ncb, cap_l
        else:
            rc = max(lc // LANES, 8)
            aff_c = aff[:, :, :lc].reshape(bsz, ne, lc // LANES, LANES)
            aff_c = jnp.pad(aff_c, ((0, 0), (0, 0), (0, rc - lc // LANES), (0, 0)), constant_values=-1.0)
            slc, offc, sll, offl = _route(aff_c.reshape(bsz, ne * rc, LANES), aff_l, cap_c, cap_l)
            slots = jnp.concatenate([slc.reshape(bsz, ne, rc * LANES)[:, :, :lc],
                                     sll.reshape(bsz, ne, n_tok)], axis=2)
            lo_l, hi_l = _tile_bounds(offl, rl, nblk - ncb, cap_l, 0)
            lo_c, hi_c = _tile_bounds(offc, rc, ncb, cap_c, cap_l)
            lo = jnp.concatenate([lo_c, lo_l], axis=2)
            hi = jnp.concatenate([hi_c, hi_l], axis=2)
            t0, nt, m_rows = 0, nblk, cap_l + cap_c
        lohi = jnp.concatenate([lo, hi], axis=1).transpose(0, 2, 1).astype(I32).reshape(-1)
        xin = _dispatch(lohi, h2e, slots, m_rows, t0, nt)
        y = _experts(xin, w_e_gate, w_e_up, w_e_down, l, cap_l)
        xa = _combine(lohi, slots, aff, x1, l, mod, g_post2, y, t0, nt, ncb)
    return xa
```

```python
import functools

import jax
import jax.numpy as jnp
from jax import lax
from jax.experimental import pallas as pl
from jax.experimental.pallas import tpu as pltpu

F32 = jnp.float32
BF16 = jnp.bfloat16
I32 = jnp.int32

EPS = 1e-6
GRID_W = 64
POS_BASE = 10000.0
N_HEADS = 4
ML_DH = 128
GLA_DK = 64
GLA_DV = 128
GLA_GATE_TAU = 16.0
N_EXPERTS = 16
EC_FACTOR = 2

LANES = 128
SUBLANES = 8
TB = 256
ML_L = 128
GLA_L = 128
GLA_SAFE_SPAN = 80.0
ML_STAGE_UNITS = 2
EXPERT_ROWS = 256
WIN = 80
ROW_ALIGN = 16
UNSEL = 2047.0
GROUP_W = N_HEADS * ML_DH
PB_ML_V, PB_ML_O, PB_GLA_QK, PB_GLA_V, PB_GLA_R = 0, 1, 2, 3, 4
VMEM_LIMIT = 56 * 1024 * 1024


def _cparams(sem):
    return pltpu.CompilerParams(dimension_semantics=sem, vmem_limit_bytes=VMEM_LIMIT)


def _split2(a):
    hi = a.astype(BF16)
    lo = (a - hi.astype(F32)).astype(BF16)
    return hi, lo


def _split3(a):
    hi = a.astype(BF16)
    r = a - hi.astype(F32)
    mid = r.astype(BF16)
    lo = (r - mid.astype(F32)).astype(BF16)
    return hi, mid, lo


_NN = (((1,), (0,)), ((), ()))
_NT = (((1,), (1,)), ((), ()))
_TN = (((0,), (0,)), ((), ()))


def _mm(a, b, dims=_NN):
    return lax.dot_general(a, b, dims, preferred_element_type=F32)


def _dot3(a, b, dims=_NN):
    ah, al = _split2(a)
    bh, bl = _split2(b)
    return _mm(ah, bh, dims) + (_mm(ah, bl, dims) + _mm(al, bh, dims))


def _dot_exact_l(m_bf16, x, dims=_NN):
    hi, mid, lo = _split3(x)
    return _mm(m_bf16, hi, dims) + (_mm(m_bf16, mid, dims) + _mm(m_bf16, lo, dims))


def _rms(x, g):
    return x * lax.rsqrt(jnp.mean(x * x, axis=-1, keepdims=True) + EPS) * g


def _log_sigmoid(x):
    return jnp.minimum(x, 0.0) - jnp.log(1.0 + jnp.exp(-jnp.abs(x)))


def _sigmoid(x):
    return 1.0 / (1.0 + jnp.exp(-x))


def _silu(x):
    return x * _sigmoid(x)


def _iota(shape, dim):
    return lax.broadcasted_iota(I32, shape, dim)


def _rev_block(i, ncb, nblk):
    return jnp.where(i < ncb, ncb - 1 - i, nblk - 1 - (i - ncb))


def _ada_kernel(c_ref, w_ref, b_ref, o_ref):
    a = _silu(c_ref[...])
    o_ref[0] = _dot3(a, w_ref[0]) + b_ref[0]


def _ada(cc, w_ada, b_ada):
    depth, d, n6 = w_ada.shape
    tn = 1536
    return pl.pallas_call(
        _ada_kernel,
        grid=(depth, n6 // tn),
        in_specs=[pl.BlockSpec((8, d), lambda l, j: (0, 0)),
                  pl.BlockSpec((1, d, tn), lambda l, j: (l, 0, j)),
                  pl.BlockSpec((1, 1, tn), lambda l, j: (l, 0, j))],
        out_specs=pl.BlockSpec((1, 8, tn), lambda l, j: (l, 0, j)),
        out_shape=jax.ShapeDtypeStruct((depth, 8, n6), F32),
        compiler_params=_cparams(("parallel", "parallel")),
        name="ada",
    )(cc, w_ada, b_ada.reshape(depth, 1, n6))


def _project(ncb, xs, xps, xns, mod_ref, g_ref, wb_ref, ws_ref, bs_ref, cw_ref, pb_ref, ps_ref, qc_ref):
    i = pl.program_id(0)
    nblk = pl.num_programs(0)
    nb, n = len(xs), xs[0].shape[0]
    rows = [slice(b * n, (b + 1) * n) for b in range(nb)]

    gain = [g_ref[0] * (1.0 + mod_ref[0, b, 0, 1:2, :]) for b in range(nb)]

    def norm(z, b):
        return _rms(z, gain[b]) + mod_ref[0, b, 0, 0:1, :]

    hh = jnp.concatenate([norm(xs[b], b).astype(BF16) for b in range(nb)], axis=0)
    halo = ([norm(xps[b], b).astype(BF16) for b in range(nb)] + [norm(xns[b], b).astype(BF16) for b in range(nb)])
    wq = cw_ref.shape[2]
    qk = _mm(jnp.concatenate([hh] + halo, axis=0), wb_ref[0, :, 0:wq])
    lvalid = jnp.logical_and(i != 0, i != ncb)
    rvalid = jnp.logical_and(i != ncb - 1, i != nblk - 1)
    dq = wq // 2
    for b in range(nb):
        lrow = nb * n + 8 * b + 7
        rrow = nb * n + 8 * nb + 8 * b
        left = jnp.where(lvalid, qk[lrow:lrow + 1, :], 0.0)
        right = jnp.where(rvalid, qk[rrow:rrow + 1, :], 0.0)
        y = _silu(_conv3(qk[rows[b], :], left, right, cw_ref[0]))
        qc_ref[b, :, 0:dq] = (y[:, 0:dq] * (ML_DH ** -0.5)).astype(BF16)
        qc_ref[b, :, dq:] = y[:, dq:].astype(BF16)
    rest = _mm(hh, wb_ref[0, :, wq:])
    pr = _mm(hh, ws_ref[0])
    ps = pr[:, :LANES] + pr[:, LANES:] + bs_ref[0]
    lane = _iota(ps.shape, 1)
    forget = jnp.logical_and(lane % 8 >= N_HEADS, lane < 4 * N_HEADS)
    ps = jnp.where(forget, _log_sigmoid(ps), ps)
    for b in range(nb):
        pb_ref[b] = rest[rows[b], :]
        ps_ref[b] = ps[rows[b], :]


def _in_kernel(ncb, x_ref, xp_ref, xn_ref, mod_ref, g_ref, wb_ref, ws_ref, bs_ref, cw_ref,
               pb_ref, ps_ref, qc_ref):
    nb = x_ref.shape[0]
    _project(ncb, [x_ref[b] for b in range(nb)], [xp_ref[b] for b in range(nb)],
             [xn_ref[b] for b in range(nb)], mod_ref, g_ref, wb_ref, ws_ref, bs_ref, cw_ref,
             pb_ref, ps_ref, qc_ref)


def _in0_kernel(ncb, x_ref, xp_ref, xn_ref, c_ref, cp_ref, cn_ref, pr_ref, prp_ref, prn_ref, pc_ref,
                mod_ref, g_ref, wb_ref, ws_ref, bs_ref, cw_ref, xa_ref, pb_ref, ps_ref, qc_ref):
    i = pl.program_id(0)
    nb = x_ref.shape[0]
    half = pr_ref.shape[2]
    reps = TB // GRID_W
    prow = jnp.concatenate([jnp.broadcast_to(pr_ref[0, k:k + 1, :], (GRID_W, half)) for k in range(reps)], axis=0)
    pcol = jnp.concatenate([pc_ref[...]] * reps, axis=0)
    pos = jnp.concatenate([prow, pcol], axis=1)
    pos_p = jnp.concatenate([prp_ref[0, reps - 1:reps, :], pc_ref[GRID_W - 1:GRID_W, :]], axis=1)
    pos_n = jnp.concatenate([prn_ref[0, 0:1, :], pc_ref[0:1, :]], axis=1)
    is_ctx = i < ncb
    xs, xps, xns = [], [], []
    for b in range(nb):
        xa = jnp.where(is_ctx, c_ref[b], x_ref[b] + pos)
        xa_ref[b] = xa
        xs.append(xa)
        xps.append(jnp.where(is_ctx, cp_ref[b], xp_ref[b] + pos_p))
        xns.append(jnp.where(is_ctx, cn_ref[b], xn_ref[b] + pos_n))
    _project(ncb, xs, xps, xns, mod_ref, g_ref, wb_ref, ws_ref, bs_ref, cw_ref, pb_ref, ps_ref, qc_ref)


def _tile_and_halo_specs(b, rows, d, tile_of):
    r8 = TB // 8
    ntile, last8 = rows // TB, rows // 8 - 1
    tl = lambda i: jnp.clip(tile_of(i), 0, ntile - 1)
    return [pl.BlockSpec((b, TB, d), lambda i: (0, tl(i), 0)),
            pl.BlockSpec((b, 8, d), lambda i: (0, jnp.clip(tl(i) * r8 - 1, 0, last8), 0)),
            pl.BlockSpec((b, 8, d), lambda i: (0, jnp.clip((tl(i) + 1) * r8, 0, last8), 0))]


def _layer_spec(layer, arr):
    shp = arr.shape[1:]
    return pl.BlockSpec((1,) + shp, lambda *_: (layer,) + (0,) * len(shp))


def _mod_spec(layer, mod, ncb, tile_of=lambda i: i):
    _, b, _, r, d = mod.shape
    return pl.BlockSpec((1, b, 1, r, d), lambda i, *_: (layer, 0, jnp.where(tile_of(i) < ncb, 0, 1), 0, 0))


def _in_proj(xa, layer, mod, g, wb, ws, bs, cw, ncb, first=None):
    if first is None:
        b, t, d = xa.shape
    else:
        b, t, d = first[0].shape[0], first[0].shape[1] + first[1].shape[1], first[0].shape[2]
    wq = cw.shape[2]
    nb = wb.shape[2] - wq
    const = lambda shp: pl.BlockSpec(shp, lambda i: tuple(0 for _ in shp))
    tile = lambda w: pl.BlockSpec((b, TB, w), lambda i: (0, i, 0))
    common = [_mod_spec(layer, mod, ncb)] + [_layer_spec(layer, a) for a in (g, wb, ws, bs, cw)]
    out_specs = [tile(nb), tile(LANES), tile(wq)]
    out_shape = [jax.ShapeDtypeStruct((b, t, nb), F32),
                 jax.ShapeDtypeStruct((b, t, LANES), F32),
                 jax.ShapeDtypeStruct((b, t, wq), BF16)]
    if first is None:
        body = functools.partial(_in_kernel, ncb)
        in_specs = _tile_and_halo_specs(b, t, d, lambda i: i) + common
        args = (xa, xa, xa, mod, g, wb, ws, bs, cw)
    else:
        x, ctx, pos_r, pos_c = first
        reps = TB // GRID_W
        ntl = x.shape[1] // TB
        body = functools.partial(_in0_kernel, ncb)
        pr_spec = lambda off: pl.BlockSpec((1, reps, d // 2),
                                           lambda i: (jnp.clip(i - ncb + off, 0, ntl - 1), 0, 0))
        in_specs = (_tile_and_halo_specs(b, x.shape[1], d, lambda i: i - ncb)
                    + _tile_and_halo_specs(b, ctx.shape[1], d, lambda i: i)
                    + [pr_spec(0), pr_spec(-1), pr_spec(1), const((GRID_W, d // 2))] + common)
        out_specs = [tile(d)] + out_specs
        out_shape = [jax.ShapeDtypeStruct((b, t, d), F32)] + out_shape
        pr3 = pos_r.reshape(-1, reps, d // 2)
        args = (x, x, x, ctx, ctx, ctx, pr3, pr3, pr3, pos_c, mod, g, wb, ws, bs, cw)
    return pl.pallas_call(
        body,
        grid=(t // TB,),
        in_specs=in_specs,
        out_specs=out_specs,
        out_shape=out_shape,
        compiler_params=_cparams(("parallel",)),
        name="in_proj",
    )(*args)


def _conv3(x, hl, hr, w):
    rows = _iota(x.shape, 0)
    prev = jnp.where(rows == 0, hl, pltpu.roll(x, 1, axis=0))
    nxt = jnp.where(rows == x.shape[0] - 1, hr, pltpu.roll(x, x.shape[0] - 1, axis=0))
    return prev * w[0:1] + x * w[1:2] + nxt * w[2:3]


def _cummax_rows(x, reverse):
    n = x.shape[0]
    rows = _iota(x.shape, 0)
    s = 1
    while s < n:
        if reverse:
            sh = jnp.where(rows < n - s, pltpu.roll(x, n - s, axis=0), -jnp.inf)
        else:
            sh = jnp.where(rows >= s, pltpu.roll(x, s, axis=0), -jnp.inf)
        x = jnp.maximum(x, sh)
        s *= 2
    return x


def _ml_pair(sb, fwd, bwd, c_s, m_s):
    ll = ML_L
    nch = TB // ll
    dq = N_HEADS * ML_DH
    rows = _iota((ll, ll), 0)
    cols = _iota((ll, ll), 1)
    causal = [cols <= rows, cols >= rows]
    tri = [jnp.where(m, 1.0, 0.0).astype(BF16) for m in causal]
    ones = jnp.ones((ll, ML_DH), BF16)
    qk = [fwd[0][sb], bwd[0][sb]]
    v = [fwd[1][sb], bwd[1][sb]]
    g = [fwd[2][sb], bwd[2][sb]]
    outs = [fwd[3], bwd[3]]
    units = [(d, h) for d in range(2) for h in range(N_HEADS)]
    sidx = lambda d, h: (2 * sb + d) * N_HEADS + h
    cx = {(d, h): c_s[sidx(d, h)] for d, h in units}
    m_row = [m_s[2 * sb], m_s[2 * sb + 1]]
    for step in range(nch):
        r0 = [step * ll, (nch - 1 - step) * ll]
        alpha, a_in, em, e_w, ut, a_old, a_new = [], [], [], [], [], [], []
        for d in range(2):
            gc = g[d][r0[d]:r0[d] + ll, :]
            bc = _dot_exact_l(tri[d], gc)
            u = pltpu.roll(gc, 4, axis=1) - bc
            cm = _cummax_rows(u, bool(d))
            neg_alpha = jnp.maximum(m_row[d], cm)
            alpha.append(-neg_alpha)
            a_in.append(jnp.exp(m_row[d] - neg_alpha))
            em.append(jnp.exp(-neg_alpha - bc))
            last = slice(0, 1) if d else slice(ll - 1, ll)
            cm_end = cm[last, :]
            bend = bc[last, :]
            e_w.append(jnp.exp(u - cm_end))
            ut.append(u.T)
            m_kv = bend + cm_end
            m_new = jnp.maximum(bend + m_row[d], m_kv)
            a_old.append(jnp.exp(bend + m_row[d] - m_new))
            a_new.append(jnp.exp(m_kv - m_new))
            m_row[d] = m_new
        lane = lambda d, h: 8 * d + 4 + h
        for g0 in range(0, len(units), ML_STAGE_UNITS):
            grp = units[g0:g0 + ML_STAGE_UNITS]
            qb = {(d, h): qk[d][r0[d]:r0[d] + ll, h * ML_DH:(h + 1) * ML_DH] for d, h in grp}
            kb = {(d, h): qk[d][r0[d]:r0[d] + ll, dq + h * ML_DH:dq + (h + 1) * ML_DH] for d, h in grp}
            vh = {(d, h): v[d][r0[d]:r0[d] + ll, h * ML_DH:(h + 1) * ML_DH] for d, h in grp}
            sc = {u: _mm(qb[u], kb[u], _NT) for u in grp}
            lhs = {}
            for d, h in grp:
                c = lane(d, h)
                arg = jnp.where(causal[d], alpha[d][:, c:c + 1] + ut[d][c:c + 1, :], -jnp.inf)
                sbf = (sc[(d, h)] * jnp.exp(arg)).astype(BF16)
                aq = (a_in[d][:, c:c + 1] * qb[(d, h)].astype(F32)).astype(BF16)
                lhs[(d, h)] = jnp.concatenate([sbf, aq], axis=1)
            ckv = {}
            for d, h in grp:
                c = lane(d, h)
                ew = e_w[d][:, c:c + 1]
                ev = jnp.concatenate([(ew * vh[(d, h)]).astype(BF16),
                                      jnp.broadcast_to(ew, (ll, ML_DH)).astype(BF16)], axis=1)
                ckv[(d, h)] = _mm(kb[(d, h)], ev, _TN)
            for d, h in grp:
                c = lane(d, h)
                rhs = jnp.concatenate([jnp.concatenate([vh[(d, h)].astype(BF16), ones], axis=1),
                                       cx[(d, h)].astype(BF16)], axis=0)
                nd = _mm(lhs[(d, h)], rhs)
                den = jnp.maximum(jnp.abs(nd[:, ML_DH:]), em[d][:, c:c + 1])
                outs[d][sb, r0[d]:r0[d] + ll, h * ML_DH:(h + 1) * ML_DH] = (nd[:, :ML_DH] / den).astype(BF16)
            for d, h in grp:
                c = lane(d, h)
                cx[(d, h)] = a_old[d][:, c:c + 1] * cx[(d, h)] + a_new[d][:, c:c + 1] * ckv[(d, h)]
    for d, h in units:
        c_s[sidx(d, h)] = cx[(d, h)]
    m_s[2 * sb] = m_row[0]
    m_s[2 * sb + 1] = m_row[1]


def _gla_pairwise(d, h, q, b, x_s):
    ll = q.shape[0]
    dkw = N_HEADS * GLA_DK
    trow = _iota((ll, 1), 0)

    def body(g, acc):
        r8 = pl.ds(pl.multiple_of(g * SUBLANES, SUBLANES), SUBLANES)
        k8 = x_s[d, r8, h * GLA_DK:(h + 1) * GLA_DK]
        b8 = x_s[d, r8, dkw + h * GLA_DK:dkw + (h + 1) * GLA_DK]
        v8 = x_s[d, r8, 2 * dkw + h * GLA_DV:2 * dkw + (h + 1) * GLA_DV]
        for j in range(SUBLANES):
            s = g * SUBLANES + j
            ks, bs, vs = k8[j:j + 1, :], b8[j:j + 1, :], v8[j:j + 1, :]
            w = jnp.exp(jnp.minimum(b - bs, 0.0))
            col = jnp.sum(q * ks * w, axis=1, keepdims=True)
            seen = (trow <= s) if d else (trow >= s)
            acc = acc + jnp.where(seen, col, 0.0) * vs
        return acc

    return lax.fori_loop(0, ll // SUBLANES, body, jnp.zeros((ll, GLA_DV), F32))


def _gla_log_decay(sb, fwd, bwd, w2_ref, ba_ref):
    return [_log_sigmoid(_mm(r[2][sb].astype(BF16), w2_ref[0, d]) + ba_ref[0, d]) * (1.0 / GLA_GATE_TAU)
            for d, r in enumerate((fwd, bwd))]


def _gla_steepest(la):
    return functools.reduce(jnp.minimum, [jnp.sum(a[c * GLA_L:(c + 1) * GLA_L, :], axis=0, keepdims=True)
                                          for a in la for c in range(TB // GLA_L)])


def _gla_pair(sb, fwd, bwd, la, s_s, x_s, pairwise):
    ll = GLA_L
    nch = TB // ll
    dkw = N_HEADS * GLA_DK
    rows = _iota((ll, ll), 0)
    cols = _iota((ll, ll), 1)
    causal = [cols <= rows, cols >= rows]
    tri = [jnp.where(m, 1.0, 0.0).astype(BF16) for m in causal]
    qk = [fwd[0][sb], bwd[0][sb]]
    v = [fwd[1][sb], bwd[1][sb]]
    outs = [fwd[3], bwd[3]]
    units = [(d, h) for d in range(2) for h in range(N_HEADS)]
    sidx = lambda d, h: (2 * sb + d) * N_HEADS + h
    st = {(d, h): s_s[sidx(d, h)] for d, h in units}
    hs = lambda a, h, w: a[:, h * w:(h + 1) * w]
    for step in range(nch):
        r0 = [step * ll, (nch - 1 - step) * ll]
        qs, bs, qi, qt, kt, kd, e_end = [], [], [], [], [], [], []
        for d in range(2):
            lah, lal = _split2(la[d][r0[d]:r0[d] + ll, :])
            bcum = _mm(tri[d], lah) + _mm(tri[d], lal)
            bend = bcum[0:1, :] if d else bcum[ll - 1:ll, :]
            q = qk[d][r0[d]:r0[d] + ll, 0:dkw] * (GLA_DK ** -0.5)
            k = qk[d][r0[d]:r0[d] + ll, dkw:2 * dkw]
            qi.append((q * jnp.exp(bcum)).astype(BF16))
            e_end.append(jnp.exp(bend))
            if pairwise:
                qs.append(q)
                bs.append(bcum)
                x_s[d, :, 0:dkw] = k
                x_s[d, :, dkw:2 * dkw] = bcum
                x_s[d, :, 2 * dkw:] = v[d][r0[d]:r0[d] + ll, :]
                kd.append((k * jnp.exp(bend - bcum)).astype(BF16))
            else:
                ref = bcum[ll // 2:ll // 2 + 1, :]
                qt.append((q * jnp.exp(bcum - ref)).astype(BF16))
                ktd = k * jnp.exp(ref - bcum)
                kt.append(ktd.astype(BF16))
                kd.append((ktd * jnp.exp(bend - ref)).astype(BF16))
        vb = {(d, h): hs(v[d][r0[d]:r0[d] + ll, :], h, GLA_DV).astype(BF16) for d, h in units}
        inter = {(d, h): _mm(hs(qi[d], h, GLA_DK), st[(d, h)].astype(BF16), _NT) for d, h in units}
        kv = {(d, h): _mm(vb[(d, h)], hs(kd[d], h, GLA_DK), _TN) for d, h in units}
        if pairwise:
            intra = {(d, h): _gla_pairwise(d, h, hs(qs[d], h, GLA_DK), hs(bs[d], h, GLA_DK), x_s) for d, h in units}
        else:
            att = {(d, h): _mm(hs(qt[d], h, GLA_DK), hs(kt[d], h, GLA_DK), _NT) for d, h in units}
            attb = {(d, h): jnp.where(causal[d], att[(d, h)], 0.0).astype(BF16) for d, h in units}
            intra = {(d, h): _mm(attb[(d, h)], vb[(d, h)]) for d, h in units}
        for d, h in units:
            o = intra[(d, h)] + inter[(d, h)]
            outs[d][sb, r0[d]:r0[d] + ll, h * GLA_DV:(h + 1) * GLA_DV] = o.astype(BF16)
        st = {(d, h): st[(d, h)] * hs(e_end[d], h, GLA_DK) + kv[(d, h)] for d, h in units}
    for d, h in units:
        s_s[sidx(d, h)] = st[(d, h)]


def _mix2_kernel(qcf, mvf, psf, gqf, gvf, qcb, mvb, psb, gqb, gvb, w2, ba,
                 mf_ref, mb_ref, gf_ref, gb_ref, c_s, m_s, s_s, x_s):
    @pl.when(pl.program_id(0) == 0)
    def _():
        c_s[...] = jnp.zeros_like(c_s)
        m_s[...] = jnp.zeros_like(m_s)
        s_s[...] = jnp.zeros_like(s_s)

    samples = range(qcf.shape[0])
    gla_f, gla_b = (gqf, gvf, psf, gf_ref), (gqb, gvb, psb, gb_ref)
    for sb in samples:
        _ml_pair(sb, (qcf, mvf, psf, mf_ref), (qcb, mvb, psb, mb_ref), c_s, m_s)
    la = [_gla_log_decay(sb, gla_f, gla_b, w2, ba) for sb in samples]
    steep = jnp.min(functools.reduce(jnp.minimum, [_gla_steepest(a) for a in la])) < -GLA_SAFE_SPAN
    for pairwise in (False, True):
        @pl.when(steep if pairwise else jnp.logical_not(steep))
        def _():
            for sb in samples:
                _gla_pair(sb, gla_f, gla_b, la[sb], s_s, x_s, pairwise)


def _mixers2(qc, pb, ps, layer, w2e, ba, ncb):
    b, t, _ = pb.shape
    nblk = t // TB
    fwd = lambda i: i
    bwd = lambda i: _rev_block(i, ncb, nblk)

    def dspecs(blk):
        col = lambda cb: pl.BlockSpec((b, TB, GROUP_W), lambda i: (0, blk(i), cb))
        return [pl.BlockSpec((b, TB, qc.shape[2]), lambda i: (0, blk(i), 0)), col(PB_ML_V),
                pl.BlockSpec((b, TB, LANES), lambda i: (0, blk(i), 0)), col(PB_GLA_QK), col(PB_GLA_V)]

    specs = dspecs(fwd) + dspecs(bwd) + [_layer_spec(layer, w2e), _layer_spec(layer, ba)]
    ns = 2 * N_HEADS * b
    ofwd = pl.BlockSpec((b, TB, GROUP_W), lambda i: (0, i, 0))
    obwd = pl.BlockSpec((b, TB, GROUP_W), lambda i: (0, bwd(i), 0))
    return pl.pallas_call(
        _mix2_kernel,
        grid=(nblk,),
        in_specs=specs,
        out_specs=[ofwd, obwd, ofwd, obwd],
        out_shape=[jax.ShapeDtypeStruct((b, t, GROUP_W), BF16)] * 4,
        scratch_shapes=[pltpu.VMEM((ns, ML_DH, 2 * ML_DH), F32),
                        pltpu.VMEM((2 * b, 1, LANES), F32),
                        pltpu.VMEM((ns, GLA_DV, GLA_DK), F32),
                        pltpu.VMEM((2, GLA_L, 2 * GROUP_W), F32)],
        compiler_params=_cparams(("arbitrary",)),
        name="mixers",
    )(*([qc, pb, ps, pb, pb] * 2), w2e, ba)


def _head_norm(x, g):
    outs = []
    for h in range(N_HEADS):
        seg = x[:, h * 128:(h + 1) * 128]
        outs.append(seg * lax.rsqrt(jnp.mean(seg * seg, axis=-1, keepdims=True) + EPS))
    return jnp.concatenate(outs, axis=-1) * g


def _out_kernel(hf, hb, of, ob, og, rg, x_ref, mod_ref, gml, ggla, wo, gpost, gpre, wrt,
                x1_ref, h2e_ref, aff_ref):
    nb = x_ref.shape[0]
    rh = LANES
    groups = [(b, slice(k * rh, (k + 1) * rh)) for b in range(nb) for k in range(TB // rh)]
    f32 = lambda ref, b, r: ref[b, r, :].astype(F32)
    y = [jnp.concatenate([_head_norm(f32(hf, b, r) + f32(hb, b, r), gml[0]) * _sigmoid(og[b, r, :]),
                          _head_norm(f32(of, b, r) + f32(ob, b, r), ggla[0]) * _silu(rg[b, r, :])],
                         axis=-1).astype(BF16) for b, r in groups]
    y2 = _mm(jnp.concatenate(y, axis=0), wo[0])
    gate1 = [mod_ref[0, b, 0, 2:3, :] * gpost[0] for b in range(nb)]
    gain2 = [gpre[0] * (1.0 + mod_ref[0, b, 0, 4:5, :]) for b in range(nb)]
    x1 = [x_ref[b, r, :] + _rms(y2[g * rh:(g + 1) * rh, :], gate1[b]) for g, (b, r) in enumerate(groups)]
    for (b, r), x1k in zip(groups, x1):
        x1_ref[b, r, :] = x1k
    h2 = [_rms(x1k, gain2[b]) + mod_ref[0, b, 0, 3:4, :] for (b, r), x1k in zip(groups, x1)]
    lt = _dot3(wrt[0], jnp.concatenate(h2, axis=0), _NT)
    for g, ((b, r), h2k) in enumerate(zip(groups, h2)):
        ltk = lt[:, g * rh:(g + 1) * rh]
        ext = jnp.exp(ltk - jnp.max(ltk, axis=0, keepdims=True))
        affk = ext / jnp.sum(ext, axis=0, keepdims=True)
        aff_ref[b, :, r] = affk
        h2e_ref[b, r, :] = h2k.astype(BF16)


def _out_proj(hf, hb, of, ob, pb, xa, layer, mod, gml, ggla, wo, gpost, gpre, wrt, ncb):
    b, t, d = xa.shape
    tile = lambda w, cb: pl.BlockSpec((b, TB, w), lambda i: (0, i, cb))
    return pl.pallas_call(
        _out_kernel,
        grid=(t // TB,),
        in_specs=[tile(GROUP_W, 0)] * 4 + [tile(GROUP_W, PB_ML_O), tile(GROUP_W, PB_GLA_R),
                  tile(d, 0), _mod_spec(layer, mod, ncb)]
                 + [_layer_spec(layer, a) for a in (gml, ggla, wo, gpost, gpre, wrt)],
        out_specs=[tile(d, 0), tile(d, 0),
                   pl.BlockSpec((b, N_EXPERTS, TB), lambda i: (0, 0, i))],
        out_shape=[jax.ShapeDtypeStruct((b, t, d), F32),
                   jax.ShapeDtypeStruct((b, t, d), BF16),
                   jax.ShapeDtypeStruct((b, N_EXPERTS, t), F32)],
        compiler_params=_cparams(("parallel",)),
        name="out_proj",
    )(hf, hb, of, ob, pb, pb, xa, mod, gml, ggla, wo, gpost, gpre, wrt)


def _cumsum_blocks(x, r):
    n = x.shape[0]
    xb = x.astype(BF16)
    li = _iota((LANES, LANES), 0)
    lj = _iota((LANES, LANES), 1)
    upper = jnp.where(li <= lj, 1.0, 0.0).astype(BF16)
    ones = jnp.ones((LANES, LANES), BF16)
    inrow = _mm(xb, upper)
    tot = _mm(xb, ones)
    ri = _iota((n, n), 0)
    rj = _iota((n, n), 1)
    same = (ri // r) == (rj // r)
    strict = jnp.where(jnp.logical_and(same, rj < ri), 1.0, 0.0).astype(BF16)
    off = _mm(strict, tot.astype(BF16))
    return inrow + off, off


def _select(aff, r, cap, base_slot):
    ne = N_EXPERTS
    n = ne * r
    aff3 = aff.reshape(ne, r, LANES)
    capf = jnp.float32(cap)

    def body(k, prefix):
        cand = prefix | (jnp.int32(1) << (30 - k))
        candf = lax.bitcast_convert_type(cand, F32)
        cnt = jnp.sum(jnp.where(aff3 >= candf, 1.0, 0.0), axis=(1, 2), keepdims=True)
        return jnp.where(cnt >= capf, cand, prefix)

    thr = lax.bitcast_convert_type(lax.fori_loop(0, 31, body, jnp.zeros((ne, 1, 1), I32)), F32)
    gt = jnp.where(aff3 > thr, 1.0, 0.0)
    eq = jnp.where(aff3 == thr, 1.0, 0.0)
    need = capf - jnp.sum(gt, axis=(1, 2), keepdims=True)
    eq2 = eq.reshape(n, LANES)
    cs_eq, _ = _cumsum_blocks(eq2, r)
    eq_rank = (cs_eq - eq2).reshape(ne, r, LANES)
    sel = (gt + eq * jnp.where(eq_rank < need, 1.0, 0.0)).reshape(n, LANES)
    cs, off = _cumsum_blocks(sel, r)
    slot = jnp.where(sel > 0.5, cs - 1.0 + base_slot, UNSEL)
    return slot, off


def _sel_kernel(rc, rl, cap_c, cap_l, *refs):
    if rc:
        affc, affl, slc, offc, sll, offl = refs
        slc[0], offc[0] = _select(affc[0], rc, cap_c, float(cap_l))
    else:
        affl, sll, offl = refs
    sll[0], offl[0] = _select(affl[0], rl, cap_l, 0.0)


def _route(aff_c, aff_l, cap_c, cap_l):
    b = aff_l.shape[0]
    ne = N_EXPERTS
    rl = aff_l.shape[1] // ne
    rc = aff_c.shape[1] // ne if aff_c is not None else 0
    args = ([aff_c] if rc else []) + [aff_l]
    in_specs, out_shape, out_specs = [], [], []
    for a in args:
        spec = pl.BlockSpec((1,) + a.shape[1:], lambda bi: (bi, 0, 0))
        in_specs.append(spec)
        out_shape += [jax.ShapeDtypeStruct(a.shape, F32)] * 2
        out_specs += [spec, spec]
    return pl.pallas_call(
        functools.partial(_sel_kernel, rc, rl, cap_c, cap_l),
        grid=(b,),
        in_specs=in_specs,
        out_specs=out_specs,
        out_shape=out_shape,
        compiler_params=_cparams(("parallel",)),
        name="route",
    )(*args)


def _window(lo_ref, base, e, m_rows):
    lo_e = lo_ref[base + e]
    hi_e = lo_ref[base + N_EXPERTS + e]
    a_e = jnp.minimum((lo_e // ROW_ALIGN) * ROW_ALIGN, m_rows - WIN)
    return a_e, hi_e


def _disp_kernel(m_rows, ng, lo_ref, h_ref, slot_ref, x_ref):
    ne = N_EXPERTS
    gi = pl.program_id(1)
    i = pl.program_id(2)
    base = (pl.program_id(0) * pl.num_programs(2) + i) * (2 * ne)

    @pl.when(i == 0)
    def _():
        x_ref[...] = jnp.zeros_like(x_ref)

    h = h_ref[0]
    sl = slot_ref[0]
    sub = _iota((WIN, TB), 0).astype(F32)
    wins = [_window(lo_ref, base, gi * ng + k, m_rows) for k in range(ng)]

    def onehot(k, first):
        a_r = jnp.minimum(first, m_rows - WIN)
        srow = sl[k:k + 1, :]
        hit = jnp.logical_and(srow - a_r.astype(F32) == sub, srow >= first.astype(F32))
        return jnp.where(hit, 1.0, 0.0).astype(BF16), a_r

    def add_rows(k, a_r, g):
        rows = pl.ds(pl.multiple_of(a_r, ROW_ALIGN), WIN)
        x_ref[0, k, rows, :] = x_ref[0, k, rows, :] + g

    sel = [onehot(k, a_e) for k, (a_e, _) in enumerate(wins)]
    g = _mm(jnp.concatenate([w for w, _ in sel], axis=0), h).astype(BF16)
    for k, (_, a_r) in enumerate(sel):
        add_rows(k, a_r, g[k * WIN:(k + 1) * WIN, :])

    for k, (a_e, hi_e) in enumerate(wins):
        @pl.when(hi_e - a_e > WIN)
        def _(k=k, a_e=a_e, hi_e=hi_e):
            def more(rd, carry):
                w, a_r = onehot(k, a_e + rd * WIN)
                add_rows(k, a_r, _mm(w, h).astype(BF16))
                return carry

            lax.fori_loop(1, (hi_e - a_e + WIN - 1) // WIN, more, 0)


def _dispatch(lohi, h2e, slots, m_rows, t0, nt):
    b, t, de = h2e.shape
    ne = N_EXPERTS
    ng = 8
    grid_spec = pltpu.PrefetchScalarGridSpec(
        num_scalar_prefetch=1,
        grid=(b, ne // ng, nt),
        in_specs=[pl.BlockSpec((1, TB, de), lambda bi, gi, i, *_: (bi, i + t0, 0)),
                  pl.BlockSpec((1, ng, TB), lambda bi, gi, i, *_: (bi * (ne // ng) + gi, 0, i + t0))],
        out_specs=pl.BlockSpec((1, ng, m_rows, de), lambda bi, gi, i, *_: (bi, gi, 0, 0)))
    return pl.pallas_call(
        functools.partial(_disp_kernel, m_rows, ng),
        grid_spec=grid_spec,
        out_shape=jax.ShapeDtypeStruct((b, ne, m_rows, de), BF16),
        compiler_params=_cparams(("parallel", "parallel", "arbitrary")),
        name="dispatch",
    )(lohi, h2e, slots.reshape(b * (ne // ng), ng, t))


def _row_chunks(m_rows, cap_l):
    step = min(EXPERT_ROWS, cap_l)
    starts = list(range(0, cap_l, step))
    return [(s, (m_rows - s) if s == starts[-1] else step) for s in starts]


def _exp_kernel(m_rows, cap_l, x_ref, wg_ref, wu_ref, wd_ref, y_ref, wg_s, wu_s, wd_s):
    @pl.when(pl.program_id(1) == 0)
    def _():
        wg_s[...] = wg_ref[0, 0].astype(BF16)
        wu_s[...] = wu_ref[0, 0].astype(BF16)
        wd_s[...] = wd_ref[0, 0].astype(BF16)

    for r0, mc in _row_chunks(m_rows, cap_l):
        xs = x_ref[0, 0, r0:r0 + mc, :]
        hid = _silu(_mm(xs, wg_s[...])) * _mm(xs, wu_s[...])
        y_ref[0, 0, r0:r0 + mc, :] = _mm(hid.astype(BF16), wd_s[...]).astype(BF16)


def _experts(xin, wg, wu, wd, layer, cap_l):
    b, ne, m_rows, de = xin.shape
    _, _, d, f = wg.shape
    return pl.pallas_call(
        functools.partial(_exp_kernel, m_rows, cap_l),
        grid=(ne, b),
        in_specs=[pl.BlockSpec((1, 1, m_rows, de), lambda ei, bi: (bi, ei, 0, 0)),
                  pl.BlockSpec((1, 1, d, f), lambda ei, bi: (layer, ei, 0, 0)),
                  pl.BlockSpec((1, 1, d, f), lambda ei, bi: (layer, ei, 0, 0)),
                  pl.BlockSpec((1, 1, f, d), lambda ei, bi: (layer, ei, 0, 0))],
        out_specs=pl.BlockSpec((1, 1, m_rows, d), lambda ei, bi: (bi, ei, 0, 0)),
        out_shape=jax.ShapeDtypeStruct((b, ne, m_rows, d), BF16),
        scratch_shapes=[pltpu.VMEM((d, f), BF16), pltpu.VMEM((d, f), BF16), pltpu.VMEM((f, d), BF16)],
        compiler_params=_cparams(("arbitrary", "arbitrary")),
        name="experts",
    )(xin, wg, wu, wd)


def _comb_kernel(m_rows, lo_ref, slot_ref, aff_ref, x1_ref, mod_ref, gpost, y_ref, out_ref, acc_s):
    ne = N_EXPERTS
    i = pl.program_id(1)
    base = (pl.program_id(0) * pl.num_programs(1) + i) * (2 * ne)
    kk = ne * WIN
    sl = slot_ref[0]
    hi = jnp.floor(sl * (1.0 / 32.0))
    lo = sl - hi * 32.0
    col_e = _iota((ne, kk), 1) // WIN
    expand = jnp.where(col_e == _iota((ne, kk), 0), 1.0, 0.0).astype(BF16)
    sx = _mm(hi.astype(BF16), expand, _TN) * 32.0 + _mm(lo.astype(BF16), expand, _TN)
    gx = _mm(aff_ref[0].astype(BF16), expand, _TN)
    col = _iota((1, kk), 1)
    jrow = (col % WIN).astype(F32)
    wins = [_window(lo_ref, base, e, m_rows) for e in range(ne)]
    ys = []
    arow = jnp.zeros((1, kk), F32)
    for e, (a_e, _) in enumerate(wins):
        arow = jnp.where(col // WIN == e, a_e.astype(F32), arow)
        ys.append(y_ref[0, e, pl.ds(pl.multiple_of(a_e, ROW_ALIGN), WIN), :])
    w = jnp.where(sx - arow == jrow, gx, 0.0).astype(BF16)
    acc_s[...] = _mm(w, jnp.concatenate(ys, axis=0))

    lane = _iota((TB, WIN), 1).astype(F32)
    for e, (a_e, hi_e) in enumerate(wins):
        @pl.when(hi_e - a_e > WIN)
        def _(e=e, a_e=a_e, hi_e=hi_e):
            scol = sx[:, e * WIN:e * WIN + 1]
            gcol = gx[:, e * WIN:e * WIN + 1]

            def more(rd, carry):
                first = a_e + rd * WIN
                a_r = jnp.minimum(first, m_rows - WIN)
                hit = jnp.logical_and(scol - a_r.astype(F32) == lane, scol >= first.astype(F32))
                ye = y_ref[0, e, pl.ds(pl.multiple_of(a_r, ROW_ALIGN), WIN), :]
                acc_s[...] += _mm(jnp.where(hit, gcol, 0.0).astype(BF16), ye)
                return carry

            lax.fori_loop(1, (hi_e - a_e + WIN - 1) // WIN, more, 0)

    out_ref[0] = x1_ref[0] + _rms(acc_s[...], mod_ref[0, 0, 0, 5:6, :] * gpost[0])


def _combine(lohi, slots, aff, x1, layer, mod, gpost, y, t0, nt, ncb):
    b, t, d = x1.shape
    ne = N_EXPERTS
    m_rows = y.shape[2]
    r8 = mod.shape[3]
    grid_spec = pltpu.PrefetchScalarGridSpec(
        num_scalar_prefetch=1,
        grid=(b, nt),
        in_specs=[pl.BlockSpec((1, ne, TB), lambda bi, i, *_: (bi, 0, i + t0)),
                  pl.BlockSpec((1, ne, TB), lambda bi, i, *_: (bi, 0, i + t0)),
                  pl.BlockSpec((1, TB, d), lambda bi, i, *_: (bi, i + t0, 0)),
                  pl.BlockSpec((1, 1, 1, r8, d),
                               lambda bi, i, *_: (layer, bi, jnp.where(i + t0 < ncb, 0, 1), 0, 0)),
                  _layer_spec(layer, gpost),
                  pl.BlockSpec((1, ne, m_rows, d), lambda bi, i, *_: (bi, 0, 0, 0),
                               pipeline_mode=pl.Buffered(1))],
        out_specs=pl.BlockSpec((1, TB, d), lambda bi, i, *_: (bi, i, 0)),
        scratch_shapes=[pltpu.VMEM((TB, d), F32)])
    return pl.pallas_call(
        functools.partial(_comb_kernel, m_rows),
        grid_spec=grid_spec,
        out_shape=jax.ShapeDtypeStruct((b, nt * TB, d), F32),
        compiler_params=_cparams(("parallel", "arbitrary")),
        name="combine",
    )(lohi, slots, aff, x1, mod, gpost, y)


def _pos_tables(rows, d):
    quarter = d // 4
    freq = jnp.power(POS_BASE, -jnp.arange(quarter, dtype=F32) / quarter)
    ar = jnp.arange(rows, dtype=F32)[:, None] * freq
    ac = jnp.arange(GRID_W, dtype=F32)[:, None] * freq
    return (jnp.concatenate([jnp.sin(ar), jnp.cos(ar)], axis=-1),
            jnp.concatenate([jnp.sin(ac), jnp.cos(ac)], axis=-1))


def _tile_bounds(off, r, ntile, cap, base):
    b = off.shape[0]
    o = off.reshape(b, N_EXPERTS, r, LANES)[:, :, :, 0]
    lo = o[:, :, ::TB // LANES][:, :, :ntile] + base
    hi = jnp.concatenate([lo[:, :, 1:], jnp.full((b, N_EXPERTS, 1), cap + base, F32)], axis=2)
    return lo, hi


def kernel(x, c, ctx, c_ctx, w_ada, b_ada, g_mix_pre, g_mix_post, g_ffn_pre, g_ffn_post,
           w_in, conv_qk, b_ml_gates, w_gla_a2, b_gla_a, g_ml_norm, g_gla_norm, w_out,
           w_router, w_e_gate, w_e_up, w_e_down):
    bsz, n_tok, d = x.shape
    lc = ctx.shape[1]
    depth = w_in.shape[0]
    ne = N_EXPERTS
    t = lc + n_tok
    ncb = lc // TB
    nblk = t // TB
    assert lc % TB == 0 and n_tok % TB == 0 and d == 2 * GROUP_W and TB % GRID_W == 0
    cap_l = EC_FACTOR * n_tok // ne
    cap_c = EC_FACTOR * lc // ne

    pos_r, pos_c = _pos_tables(n_tok // GRID_W, d)
    xa = None

    cc = jnp.zeros((8, d), F32).at[:bsz].set(c).at[bsz].set(c_ctx)
    mods = _ada(cc, w_ada, b_ada)

    rank = w_gla_a2.shape[2]
    ml_w, n_gates, gla_w = 4 * GROUP_W, 4 * N_HEADS, 3 * GROUP_W
    o_gla, o_a = ml_w + n_gates, ml_w + n_gates + gla_w
    assert w_in.shape[2] == o_a + 2 * rank and n_gates + 2 * rank <= LANES
    wide = jnp.concatenate([w_in[:, :, :ml_w], w_in[:, :, o_gla:o_a]], axis=2).astype(BF16)
    narrow = jnp.concatenate([w_in[:, :, ml_w:o_gla], w_in[:, :, o_a:]], axis=2)
    narrow = jnp.pad(narrow, ((0, 0), (0, 0), (0, LANES - narrow.shape[2])))
    narrow = jnp.concatenate(_split2(narrow), axis=2)
    bias_s = jnp.pad(b_ml_gates, ((0, 0), (0, LANES - b_ml_gates.shape[1])))
    w2e = jnp.zeros((depth, 2, LANES, N_HEADS * GLA_DK), F32)
    w2e = (w2e.at[:, 0, n_gates:n_gates + rank].set(w_gla_a2[:, 0])
           .at[:, 1, n_gates + rank:n_gates + 2 * rank].set(w_gla_a2[:, 1]).astype(BF16))
    m_lat = mods[:, :bsz].reshape(depth, bsz, 1, 6, d)
    m_ctx = jnp.broadcast_to(mods[:, bsz].reshape(depth, 1, 1, 6, d), (depth, bsz, 1, 6, d))
    mod = jnp.pad(jnp.concatenate([m_ctx, m_lat], axis=2), ((0, 0), (0, 0), (0, 0), (0, 2), (0, 0)))
    row = lambda a: a.reshape(depth, 1, -1)
    bias_s, ba = row(bias_s), b_gla_a.reshape(depth, 2, 1, -1)
    g_pre1, g_post1, g_pre2, g_post2 = row(g_mix_pre), row(g_mix_post), row(g_ffn_pre), row(g_ffn_post)
    g_ml, g_gla = row(g_ml_norm), row(g_gla_norm)
    wo = w_out.astype(BF16)
    wrt = w_router.transpose(0, 2, 1)

    for l in range(depth):
        last = l == depth - 1
        if l == 0:
            xa, pb, ps, qc = _in_proj(None, l, mod, g_pre1, wide, narrow, bias_s, conv_qk, ncb,
                                      first=(x, ctx, pos_r, pos_c))
        else:
            pb, ps, qc = _in_proj(xa, l, mod, g_pre1, wide, narrow, bias_s, conv_qk, ncb)
        hf, hb, of, ob = _mixers2(qc, pb, ps, l, w2e, ba, ncb)
        x1, h2e, aff = _out_proj(hf, hb, of, ob, pb, xa, l, mod, g_ml, g_gla, wo, g_post1, g_pre2, wrt, ncb)

        rl = n_tok // LANES
        aff_l = aff[:, :, lc:].reshape(bsz, ne * rl, LANES)
        if last:
            sll, offl = _route(None, aff_l, 0, cap_l)
            slots = jnp.pad(sll.reshape(bsz, ne, n_tok), ((0, 0), (0, 0), (lc, 0)), constant_values=UNSEL)
            lo, hi = _tile_bounds(offl, rl, nblk - ncb, cap_l, 0)
            t0, nt, m_rows = ncb, nblk - ncb, cap_l
        else:
            rc = max(lc // LANES, 8)
            aff_c = aff[:, :, :lc].reshape(bsz, ne, lc // LANES, LANES)
            aff_c = jnp.pad(aff_c, ((0, 0), (0, 0), (0, rc - lc // LANES), (0, 0)), constant_values=-1.0)
            slc, offc, sll, offl = _route(aff_c.reshape(bsz, ne * rc, LANES), aff_l, cap_c, cap_l)
            slots = jnp.concatenate([slc.reshape(bsz, ne, rc * LANES)[:, :, :lc],
                                     sll.reshape(bsz, ne, n_tok)], axis=2)
            lo_l, hi_l = _tile_bounds(offl, rl, nblk - ncb, cap_l, 0)
            lo_c, hi_c = _tile_bounds(offc, rc, ncb, cap_c, cap_l)
            lo = jnp.concatenate([lo_c, lo_l], axis=2)
            hi = jnp.concatenate([hi_c, hi_l], axis=2)
            t0, nt, m_rows = 0, nblk, cap_l + cap_c
        lohi = jnp.concatenate([lo, hi], axis=1).transpose(0, 2, 1).astype(I32).reshape(-1)
        xin = _dispatch(lohi, h2e, slots, m_rows, t0, nt)
        y = _experts(xin, w_e_gate, w_e_up, w_e_down, l, cap_l)
        xa = _combine(lohi, slots, aff, x1, l, mod, g_post2, y, t0, nt, ncb)
    return xa
```

```python
import functools

import jax
import jax.numpy as jnp
from jax import lax
from jax.experimental import pallas as pl
from jax.experimental.pallas import tpu as pltpu

F32 = jnp.float32
BF16 = jnp.bfloat16
I32 = jnp.int32

EPS = 1e-6
GRID_W = 64
POS_BASE = 10000.0
N_HEADS = 4
ML_DH = 128
GLA_DK = 64
GLA_DV = 128
GLA_GATE_TAU = 16.0
N_EXPERTS = 16
EC_FACTOR = 2

LANES = 128
SUBLANES = 8
TB = 256
ML_L = 128
GLA_L = 128
GLA_SAFE_SPAN = 80.0
ML_STAGE_UNITS = 2
EXPERT_ROWS = 256
WIN = 80
ROW_ALIGN = 16
UNSEL = 2047.0
GROUP_W = N_HEADS * ML_DH
PB_ML_V, PB_ML_O, PB_GLA_QK, PB_GLA_V, PB_GLA_R = 0, 1, 2, 3, 4
VMEM_LIMIT = 56 * 1024 * 1024


def _cparams(sem):
    return pltpu.CompilerParams(dimension_semantics=sem, vmem_limit_bytes=VMEM_LIMIT)


def _split2(a):
    hi = a.astype(BF16)
    lo = (a - hi.astype(F32)).astype(BF16)
    return hi, lo


def _split3(a):
    hi = a.astype(BF16)
    r = a - hi.astype(F32)
    mid = r.astype(BF16)
    lo = (r - mid.astype(F32)).astype(BF16)
    return hi, mid, lo


_NN = (((1,), (0,)), ((), ()))
_NT = (((1,), (1,)), ((), ()))
_TN = (((0,), (0,)), ((), ()))


def _mm(a, b, dims=_NN):
    return lax.dot_general(a, b, dims, preferred_element_type=F32)


def _dot3(a, b, dims=_NN):
    ah, al = _split2(a)
    bh, bl = _split2(b)
    return _mm(ah, bh, dims) + (_mm(ah, bl, dims) + _mm(al, bh, dims))


def _dot_exact_l(m_bf16, x, dims=_NN):
    hi, mid, lo = _split3(x)
    return _mm(m_bf16, hi, dims) + (_mm(m_bf16, mid, dims) + _mm(m_bf16, lo, dims))


def _rms(x, g):
    return x * lax.rsqrt(jnp.mean(x * x, axis=-1, keepdims=True) + EPS) * g


def _log_sigmoid(x):
    return jnp.minimum(x, 0.0) - jnp.log(1.0 + jnp.exp(-jnp.abs(x)))


def _sigmoid(x):
    return 1.0 / (1.0 + jnp.exp(-x))


def _silu(x):
    return x * _sigmoid(x)


def _iota(shape, dim):
    return lax.broadcasted_iota(I32, shape, dim)


def _rev_block(i, ncb, nblk):
    return jnp.where(i < ncb, ncb - 1 - i, nblk - 1 - (i - ncb))


def _ada_kernel(c_ref, w_ref, b_ref, o_ref):
    a = _silu(c_ref[...])
    o_ref[0] = _dot3(a, w_ref[0]) + b_ref[0]


def _ada(cc, w_ada, b_ada):
    depth, d, n6 = w_ada.shape
    tn = 1536
    return pl.pallas_call(
        _ada_kernel,
        grid=(depth, n6 // tn),
        in_specs=[pl.BlockSpec((8, d), lambda l, j: (0, 0)),
                  pl.BlockSpec((1, d, tn), lambda l, j: (l, 0, j)),
                  pl.BlockSpec((1, 1, tn), lambda l, j: (l, 0, j))],
        out_specs=pl.BlockSpec((1, 8, tn), lambda l, j: (l, 0, j)),
        out_shape=jax.ShapeDtypeStruct((depth, 8, n6), F32),
        compiler_params=_cparams(("parallel", "parallel")),
        name="ada",
    )(cc, w_ada, b_ada.reshape(depth, 1, n6))


def _project(ncb, xs, xps, xns, mod_ref, g_ref, wb_ref, ws_ref, bs_ref, cw_ref, pb_ref, ps_ref, qc_ref):
    i = pl.program_id(0)
    nblk = pl.num_programs(0)
    nb, n = len(xs), xs[0].shape[0]
    rows = [slice(b * n, (b + 1) * n) for b in range(nb)]

    gain = [g_ref[0] * (1.0 + mod_ref[0, b, 0, 1:2, :]) for b in range(nb)]

    def norm(z, b):
        return _rms(z, gain[b]) + mod_ref[0, b, 0, 0:1, :]

    hh = jnp.concatenate([norm(xs[b], b).astype(BF16) for b in range(nb)], axis=0)
    halo = ([norm(xps[b], b).astype(BF16) for b in range(nb)] + [norm(xns[b], b).astype(BF16) for b in range(nb)])
    wq = cw_ref.shape[2]
    qk = _mm(jnp.concatenate([hh] + halo, axis=0), wb_ref[0, :, 0:wq])
    lvalid = jnp.logical_and(i != 0, i != ncb)
    rvalid = jnp.logical_and(i != ncb - 1, i != nblk - 1)
    dq = wq // 2
    for b in range(nb):
        lrow = nb * n + 8 * b + 7
        rrow = nb * n + 8 * nb + 8 * b
        left = jnp.where(lvalid, qk[lrow:lrow + 1, :], 0.0)
        right = jnp.where(rvalid, qk[rrow:rrow + 1, :], 0.0)
        y = _silu(_conv3(qk[rows[b], :], left, right, cw_ref[0]))
        qc_ref[b, :, 0:dq] = (y[:, 0:dq] * (ML_DH ** -0.5)).astype(BF16)
        qc_ref[b, :, dq:] = y[:, dq:].astype(BF16)
    rest = _mm(hh, wb_ref[0, :, wq:])
    pr = _mm(hh, ws_ref[0])
    ps = pr[:, :LANES] + pr[:, LANES:] + bs_ref[0]
    lane = _iota(ps.shape, 1)
    forget = jnp.logical_and(lane % 8 >= N_HEADS, lane < 4 * N_HEADS)
    ps = jnp.where(forget, _log_sigmoid(ps), ps)
    for b in range(nb):
        pb_ref[b] = rest[rows[b], :]
        ps_ref[b] = ps[rows[b], :]


def _in_kernel(ncb, x_ref, xp_ref, xn_ref, mod_ref, g_ref, wb_ref, ws_ref, bs_ref, cw_ref,
               pb_ref, ps_ref, qc_ref):
    nb = x_ref.shape[0]
    _project(ncb, [x_ref[b] for b in range(nb)], [xp_ref[b] for b in range(nb)],
             [xn_ref[b] for b in range(nb)], mod_ref, g_ref, wb_ref, ws_ref, bs_ref, cw_ref,
             pb_ref, ps_ref, qc_ref)


def _in0_kernel(ncb, x_ref, xp_ref, xn_ref, c_ref, cp_ref, cn_ref, pr_ref, prp_ref, prn_ref, pc_ref,
                mod_ref, g_ref, wb_ref, ws_ref, bs_ref, cw_ref, xa_ref, pb_ref, ps_ref, qc_ref):
    i = pl.program_id(0)
    nb = x_ref.shape[0]
    half = pr_ref.shape[2]
    reps = TB // GRID_W
    prow = jnp.concatenate([jnp.broadcast_to(pr_ref[0, k:k + 1, :], (GRID_W, half)) for k in range(reps)], axis=0)
    pcol = jnp.concatenate([pc_ref[...]] * reps, axis=0)
    pos = jnp.concatenate([prow, pcol], axis=1)
    pos_p = jnp.concatenate([prp_ref[0, reps - 1:reps, :], pc_ref[GRID_W - 1:GRID_W, :]], axis=1)
    pos_n = jnp.concatenate([prn_ref[0, 0:1, :], pc_ref[0:1, :]], axis=1)
    is_ctx = i < ncb
    xs, xps, xns = [], [], []
    for b in range(nb):
        xa = jnp.where(is_ctx, c_ref[b], x_ref[b] + pos)
        xa_ref[b] = xa
        xs.append(xa)
        xps.append(jnp.where(is_ctx, cp_ref[b], xp_ref[b] + pos_p))
        xns.append(jnp.where(is_ctx, cn_ref[b], xn_ref[b] + pos_n))
    _project(ncb, xs, xps, xns, mod_ref, g_ref, wb_ref, ws_ref, bs_ref, cw_ref, pb_ref, ps_ref, qc_ref)


def _tile_and_halo_specs(b, rows, d, tile_of):
    r8 = TB // 8
    ntile, last8 = rows // TB, rows // 8 - 1
    tl = lambda i: jnp.clip(tile_of(i), 0, ntile - 1)
    return [pl.BlockSpec((b, TB, d), lambda i: (0, tl(i), 0)),
            pl.BlockSpec((b, 8, d), lambda i: (0, jnp.clip(tl(i) * r8 - 1, 0, last8), 0)),
            pl.BlockSpec((b, 8, d), lambda i: (0, jnp.clip((tl(i) + 1) * r8, 0, last8), 0))]


def _layer_spec(layer, arr):
    shp = arr.shape[1:]
    return pl.BlockSpec((1,) + shp, lambda *_: (layer,) + (0,) * len(shp))


def _mod_spec(layer, mod, ncb, tile_of=lambda i: i):
    _, b, _, r, d = mod.shape
    return pl.BlockSpec((1, b, 1, r, d), lambda i, *_: (layer, 0, jnp.where(tile_of(i) < ncb, 0, 1), 0, 0))


def _in_proj(xa, layer, mod, g, wb, ws, bs, cw, ncb, first=None):
    if first is None:
        b, t, d = xa.shape
    else:
        b, t, d = first[0].shape[0], first[0].shape[1] + first[1].shape[1], first[0].shape[2]
    wq = cw.shape[2]
    nb = wb.shape[2] - wq
    const = lambda shp: pl.BlockSpec(shp, lambda i: tuple(0 for _ in shp))
    tile = lambda w: pl.BlockSpec((b, TB, w), lambda i: (0, i, 0))
    common = [_mod_spec(layer, mod, ncb)] + [_layer_spec(layer, a) for a in (g, wb, ws, bs, cw)]
    out_specs = [tile(nb), tile(LANES), tile(wq)]
    out_shape = [jax.ShapeDtypeStruct((b, t, nb), F32),
                 jax.ShapeDtypeStruct((b, t, LANES), F32),
                 jax.ShapeDtypeStruct((b, t, wq), BF16)]
    if first is None:
        body = functools.partial(_in_kernel, ncb)
        in_specs = _tile_and_halo_specs(b, t, d, lambda i: i) + common
        args = (xa, xa, xa, mod, g, wb, ws, bs, cw)
    else:
        x, ctx, pos_r, pos_c = first
        reps = TB // GRID_W
        ntl = x.shape[1] // TB
        body = functools.partial(_in0_kernel, ncb)
        pr_spec = lambda off: pl.BlockSpec((1, reps, d // 2),
                                           lambda i: (jnp.clip(i - ncb + off, 0, ntl - 1), 0, 0))
        in_specs = (_tile_and_halo_specs(b, x.shape[1], d, lambda i: i - ncb)
                    + _tile_and_halo_specs(b, ctx.shape[1], d, lambda i: i)
                    + [pr_spec(0), pr_spec(-1), pr_spec(1), const((GRID_W, d // 2))] + common)
        out_specs = [tile(d)] + out_specs
        out_shape = [jax.ShapeDtypeStruct((b, t, d), F32)] + out_shape
        pr3 = pos_r.reshape(-1, reps, d // 2)
        args = (x, x, x, ctx, ctx, ctx, pr3, pr3, pr3, pos_c, mod, g, wb, ws, bs, cw)
    return pl.pallas_call(
        body,
        grid=(t // TB,),
        in_specs=in_specs,
        out_specs=out_specs,
        out_shape=out_shape,
        compiler_params=_cparams(("parallel",)),
        name="in_proj",
    )(*args)


def _conv3(x, hl, hr, w):
    rows = _iota(x.shape, 0)
    prev = jnp.where(rows == 0, hl, pltpu.roll(x, 1, axis=0))
    nxt = jnp.where(rows == x.shape[0] - 1, hr, pltpu.roll(x, x.shape[0] - 1, axis=0))
    return prev * w[0:1] + x * w[1:2] + nxt * w[2:3]


def _cummax_rows(x, reverse):
    n = x.shape[0]
    rows = _iota(x.shape, 0)
    s = 1
    while s < n:
        if reverse:
            sh = jnp.where(rows < n - s, pltpu.roll(x, n - s, axis=0), -jnp.inf)
        else:
            sh = jnp.where(rows >= s, pltpu.roll(x, s, axis=0), -jnp.inf)
        x = jnp.maximum(x, sh)
        s *= 2
    return x


def _ml_pair(sb, fwd, bwd, c_s, m_s):
    ll = ML_L
    nch = TB // ll
    dq = N_HEADS * ML_DH
    rows = _iota((ll, ll), 0)
    cols = _iota((ll, ll), 1)
    causal = [cols <= rows, cols >= rows]
    tri = [jnp.where(m, 1.0, 0.0).astype(BF16) for m in causal]
    ones = jnp.ones((ll, ML_DH), BF16)
    qk = [fwd[0][sb], bwd[0][sb]]
    v = [fwd[1][sb], bwd[1][sb]]
    g = [fwd[2][sb], bwd[2][sb]]
    outs = [fwd[3], bwd[3]]
    units = [(d, h) for d in range(2) for h in range(N_HEADS)]
    sidx = lambda d, h: (2 * sb + d) * N_HEADS + h
    cx = {(d, h): c_s[sidx(d, h)] for d, h in units}
    m_row = [m_s[2 * sb], m_s[2 * sb + 1]]
    for step in range(nch):
        r0 = [step * ll, (nch - 1 - step) * ll]
        alpha, a_in, em, e_w, ut, a_old, a_new = [], [], [], [], [], [], []
        for d in range(2):
            gc = g[d][r0[d]:r0[d] + ll, :]
            bc = _dot_exact_l(tri[d], gc)
            u = pltpu.roll(gc, 4, axis=1) - bc
            cm = _cummax_rows(u, bool(d))
            neg_alpha = jnp.maximum(m_row[d], cm)
            alpha.append(-neg_alpha)
            a_in.append(jnp.exp(m_row[d] - neg_alpha))
            em.append(jnp.exp(-neg_alpha - bc))
            last = slice(0, 1) if d else slice(ll - 1, ll)
            cm_end = cm[last, :]
            bend = bc[last, :]
            e_w.append(jnp.exp(u - cm_end))
            ut.append(u.T)
            m_kv = bend + cm_end
            m_new = jnp.maximum(bend + m_row[d], m_kv)
            a_old.append(jnp.exp(bend + m_row[d] - m_new))
            a_new.append(jnp.exp(m_kv - m_new))
            m_row[d] = m_new
        lane = lambda d, h: 8 * d + 4 + h
        for g0 in range(0, len(units), ML_STAGE_UNITS):
            grp = units[g0:g0 + ML_STAGE_UNITS]
            qb = {(d, h): qk[d][r0[d]:r0[d] + ll, h * ML_DH:(h + 1) * ML_DH] for d, h in grp}
            kb = {(d, h): qk[d][r0[d]:r0[d] + ll, dq + h * ML_DH:dq + (h + 1) * ML_DH] for d, h in grp}
            vh = {(d, h): v[d][r0[d]:r0[d] + ll, h * ML_DH:(h + 1) * ML_DH] for d, h in grp}
            sc = {u: _mm(qb[u], kb[u], _NT) for u in grp}
            lhs = {}
            for d, h in grp:
                c = lane(d, h)
                arg = jnp.where(causal[d], alpha[d][:, c:c + 1] + ut[d][c:c + 1, :], -jnp.inf)
                sbf = (sc[(d, h)] * jnp.exp(arg)).astype(BF16)
                aq = (a_in[d][:, c:c + 1] * qb[(d, h)].astype(F32)).astype(BF16)
                lhs[(d, h)] = jnp.concatenate([sbf, aq], axis=1)
            ckv = {}
            for d, h in grp:
                c = lane(d, h)
                ew = e_w[d][:, c:c + 1]
                ev = jnp.concatenate([(ew * vh[(d, h)]).astype(BF16),
                                      jnp.broadcast_to(ew, (ll, ML_DH)).astype(BF16)], axis=1)
                ckv[(d, h)] = _mm(kb[(d, h)], ev, _TN)
            for d, h in grp:
                c = lane(d, h)
                rhs = jnp.concatenate([jnp.concatenate([vh[(d, h)].astype(BF16), ones], axis=1),
                                       cx[(d, h)].astype(BF16)], axis=0)
                nd = _mm(lhs[(d, h)], rhs)
                den = jnp.maximum(jnp.abs(nd[:, ML_DH:]), em[d][:, c:c + 1])
                outs[d][sb, r0[d]:r0[d] + ll, h * ML_DH:(h + 1) * ML_DH] = (nd[:, :ML_DH] / den).astype(BF16)
            for d, h in grp:
                c = lane(d, h)
                cx[(d, h)] = a_old[d][:, c:c + 1] * cx[(d, h)] + a_new[d][:, c:c + 1] * ckv[(d, h)]
    for d, h in units:
        c_s[sidx(d, h)] = cx[(d, h)]
    m_s[2 * sb] = m_row[0]
    m_s[2 * sb + 1] = m_row[1]


def _gla_pairwise(d, h, q, b, x_s):
    ll = q.shape[0]
    dkw = N_HEADS * GLA_DK
    trow = _iota((ll, 1), 0)

    def body(g, acc):
        r8 = pl.ds(pl.multiple_of(g * SUBLANES, SUBLANES), SUBLANES)
        k8 = x_s[d, r8, h * GLA_DK:(h + 1) * GLA_DK]
        b8 = x_s[d, r8, dkw + h * GLA_DK:dkw + (h + 1) * GLA_DK]
        v8 = x_s[d, r8, 2 * dkw + h * GLA_DV:2 * dkw + (h + 1) * GLA_DV]
        for j in range(SUBLANES):
            s = g * SUBLANES + j
            ks, bs, vs = k8[j:j + 1, :], b8[j:j + 1, :], v8[j:j + 1, :]
            w = jnp.exp(jnp.minimum(b - bs, 0.0))
            col = jnp.sum(q * ks * w, axis=1, keepdims=True)
            seen = (trow <= s) if d else (trow >= s)
            acc = acc + jnp.where(seen, col, 0.0) * vs
        return acc

    return lax.fori_loop(0, ll // SUBLANES, body, jnp.zeros((ll, GLA_DV), F32))


def _gla_log_decay(sb, fwd, bwd, w2_ref, ba_ref):
    return [_log_sigmoid(_mm(r[2][sb].astype(BF16), w2_ref[0, d]) + ba_ref[0, d]) * (1.0 / GLA_GATE_TAU)
            for d, r in enumerate((fwd, bwd))]


def _gla_steepest(la):
    return functools.reduce(jnp.minimum, [jnp.sum(a[c * GLA_L:(c + 1) * GLA_L, :], axis=0, keepdims=True)
                                          for a in la for c in range(TB // GLA_L)])


def _gla_pair(sb, fwd, bwd, la, s_s, x_s, pairwise):
    ll = GLA_L
    nch = TB // ll
    dkw = N_HEADS * GLA_DK
    rows = _iota((ll, ll), 0)
    cols = _iota((ll, ll), 1)
    causal = [cols <= rows, cols >= rows]
    tri = [jnp.where(m, 1.0, 0.0).astype(BF16) for m in causal]
    qk = [fwd[0][sb], bwd[0][sb]]
    v = [fwd[1][sb], bwd[1][sb]]
    outs = [fwd[3], bwd[3]]
    units = [(d, h) for d in range(2) for h in range(N_HEADS)]
    sidx = lambda d, h: (2 * sb + d) * N_HEADS + h
    st = {(d, h): s_s[sidx(d, h)] for d, h in units}
    hs = lambda a, h, w: a[:, h * w:(h + 1) * w]
    for step in range(nch):
        r0 = [step * ll, (nch - 1 - step) * ll]
        qs, bs, qi, qt, kt, kd, e_end = [], [], [], [], [], [], []
        for d in range(2):
            lah, lal = _split2(la[d][r0[d]:r0[d] + ll, :])
            bcum = _mm(tri[d], lah) + _mm(tri[d], lal)
            bend = bcum[0:1, :] if d else bcum[ll - 1:ll, :]
            q = qk[d][r0[d]:r0[d] + ll, 0:dkw] * (GLA_DK ** -0.5)
            k = qk[d][r0[d]:r0[d] + ll, dkw:2 * dkw]
            qi.append((q * jnp.exp(bcum)).astype(BF16))
            e_end.append(jnp.exp(bend))
            if pairwise:
                qs.append(q)
                bs.append(bcum)
                x_s[d, :, 0:dkw] = k
                x_s[d, :, dkw:2 * dkw] = bcum
                x_s[d, :, 2 * dkw:] = v[d][r0[d]:r0[d] + ll, :]
                kd.append((k * jnp.exp(bend - bcum)).astype(BF16))
            else:
                ref = bcum[ll // 2:ll // 2 + 1, :]
                qt.append((q * jnp.exp(bcum - ref)).astype(BF16))
                ktd = k * jnp.exp(ref - bcum)
                kt.append(ktd.astype(BF16))
                kd.append((ktd * jnp.exp(bend - ref)).astype(BF16))
        vb = {(d, h): hs(v[d][r0[d]:r0[d] + ll, :], h, GLA_DV).astype(BF16) for d, h in units}
        inter = {(d, h): _mm(hs(qi[d], h, GLA_DK), st[(d, h)].astype(BF16), _NT) for d, h in units}
        kv = {(d, h): _mm(vb[(d, h)], hs(kd[d], h, GLA_DK), _TN) for d, h in units}
        if pairwise:
            intra = {(d, h): _gla_pairwise(d, h, hs(qs[d], h, GLA_DK), hs(bs[d], h, GLA_DK), x_s) for d, h in units}
        else:
            att = {(d, h): _mm(hs(qt[d], h, GLA_DK), hs(kt[d], h, GLA_DK), _NT) for d, h in units}
            attb = {(d, h): jnp.where(causal[d], att[(d, h)], 0.0).astype(BF16) for d, h in units}
            intra = {(d, h): _mm(attb[(d, h)], vb[(d, h)]) for d, h in units}
        for d, h in units:
            o = intra[(d, h)] + inter[(d, h)]
            outs[d][sb, r0[d]:r0[d] + ll, h * GLA_DV:(h + 1) * GLA_DV] = o.astype(BF16)
        st = {(d, h): st[(d, h)] * hs(e_end[d], h, GLA_DK) + kv[(d, h)] for d, h in units}
    for d, h in units:
        s_s[sidx(d, h)] = st[(d, h)]


def _mix2_kernel(qcf, mvf, psf, gqf, gvf, qcb, mvb, psb, gqb, gvb, w2, ba,
                 mf_ref, mb_ref, gf_ref, gb_ref, c_s, m_s, s_s, x_s):
    @pl.when(pl.program_id(0) == 0)
    def _():
        c_s[...] = jnp.zeros_like(c_s)
        m_s[...] = jnp.zeros_like(m_s)
        s_s[...] = jnp.zeros_like(s_s)

    samples = range(qcf.shape[0])
    gla_f, gla_b = (gqf, gvf, psf, gf_ref), (gqb, gvb, psb, gb_ref)
    la = [_gla_log_decay(sb, gla_f, gla_b, w2, ba) for sb in samples]
    steep = jnp.min(functools.reduce(jnp.minimum, [_gla_steepest(a) for a in la])) < -GLA_SAFE_SPAN
    for sb in samples:
        _ml_pair(sb, (qcf, mvf, psf, mf_ref), (qcb, mvb, psb, mb_ref), c_s, m_s)
    for pairwise in (False, True):
        @pl.when(steep if pairwise else jnp.logical_not(steep))
        def _():
            for sb in samples:
                _gla_pair(sb, gla_f, gla_b, la[sb], s_s, x_s, pairwise)


def _mixers2(qc, pb, ps, layer, w2e, ba, ncb):
    b, t, _ = pb.shape
    nblk = t // TB
    fwd = lambda i: i
    bwd = lambda i: _rev_block(i, ncb, nblk)

    def dspecs(blk):
        col = lambda cb: pl.BlockSpec((b, TB, GROUP_W), lambda i: (0, blk(i), cb))
        return [pl.BlockSpec((b, TB, qc.shape[2]), lambda i: (0, blk(i), 0)), col(PB_ML_V),
                pl.BlockSpec((b, TB, LANES), lambda i: (0, blk(i), 0)), col(PB_GLA_QK), col(PB_GLA_V)]

    specs = dspecs(fwd) + dspecs(bwd) + [_layer_spec(layer, w2e), _layer_spec(layer, ba)]
    ns = 2 * N_HEADS * b
    ofwd = pl.BlockSpec((b, TB, GROUP_W), lambda i: (0, i, 0))
    obwd = pl.BlockSpec((b, TB, GROUP_W), lambda i: (0, bwd(i), 0))
    return pl.pallas_call(
        _mix2_kernel,
        grid=(nblk,),
        in_specs=specs,
        out_specs=[ofwd, obwd, ofwd, obwd],
        out_shape=[jax.ShapeDtypeStruct((b, t, GROUP_W), BF16)] * 4,
        scratch_shapes=[pltpu.VMEM((ns, ML_DH, 2 * ML_DH), F32),
                        pltpu.VMEM((2 * b, 1, LANES), F32),
                        pltpu.VMEM((ns, GLA_DV, GLA_DK), F32),
                        pltpu.VMEM((2, GLA_L, 2 * GROUP_W), F32)],
        compiler_params=_cparams(("arbitrary",)),
        name="mixers",
    )(*([qc, pb, ps, pb, pb] * 2), w2e, ba)


def _head_norm(x, g):
    outs = []
    for h in range(N_HEADS):
        seg = x[:, h * 128:(h + 1) * 128]
        outs.append(seg * lax.rsqrt(jnp.mean(seg * seg, axis=-1, keepdims=True) + EPS))
    return jnp.concatenate(outs, axis=-1) * g


def _out_kernel(hf, hb, of, ob, og, rg, x_ref, mod_ref, gml, ggla, wo, gpost, gpre, wrt,
                x1_ref, h2e_ref, aff_ref):
    nb = x_ref.shape[0]
    rh = LANES
    groups = [(b, slice(k * rh, (k + 1) * rh)) for b in range(nb) for k in range(TB // rh)]
    f32 = lambda ref, b, r: ref[b, r, :].astype(F32)
    y = [jnp.concatenate([_head_norm(f32(hf, b, r) + f32(hb, b, r), gml[0]) * _sigmoid(og[b, r, :]),
                          _head_norm(f32(of, b, r) + f32(ob, b, r), ggla[0]) * _silu(rg[b, r, :])],
                         axis=-1).astype(BF16) for b, r in groups]
    y2 = _mm(jnp.concatenate(y, axis=0), wo[0])
    gate1 = [mod_ref[0, b, 0, 2:3, :] * gpost[0] for b in range(nb)]
    gain2 = [gpre[0] * (1.0 + mod_ref[0, b, 0, 4:5, :]) for b in range(nb)]
    x1 = [x_ref[b, r, :] + _rms(y2[g * rh:(g + 1) * rh, :], gate1[b]) for g, (b, r) in enumerate(groups)]
    for (b, r), x1k in zip(groups, x1):
        x1_ref[b, r, :] = x1k
    h2 = [_rms(x1k, gain2[b]) + mod_ref[0, b, 0, 3:4, :] for (b, r), x1k in zip(groups, x1)]
    lt = _dot3(wrt[0], jnp.concatenate(h2, axis=0), _NT)
    for g, ((b, r), h2k) in enumerate(zip(groups, h2)):
        ltk = lt[:, g * rh:(g + 1) * rh]
        ext = jnp.exp(ltk - jnp.max(ltk, axis=0, keepdims=True))
        affk = ext / jnp.sum(ext, axis=0, keepdims=True)
        aff_ref[b, :, r] = affk
        h2e_ref[b, r, :] = h2k.astype(BF16)


def _out_proj(hf, hb, of, ob, pb, xa, layer, mod, gml, ggla, wo, gpost, gpre, wrt, ncb):
    b, t, d = xa.shape
    tile = lambda w, cb: pl.BlockSpec((b, TB, w), lambda i: (0, i, cb))
    return pl.pallas_call(
        _out_kernel,
        grid=(t // TB,),
        in_specs=[tile(GROUP_W, 0)] * 4 + [tile(GROUP_W, PB_ML_O), tile(GROUP_W, PB_GLA_R),
                  tile(d, 0), _mod_spec(layer, mod, ncb)]
                 + [_layer_spec(layer, a) for a in (gml, ggla, wo, gpost, gpre, wrt)],
        out_specs=[tile(d, 0), tile(d, 0),
                   pl.BlockSpec((b, N_EXPERTS, TB), lambda i: (0, 0, i))],
        out_shape=[jax.ShapeDtypeStruct((b, t, d), F32),
                   jax.ShapeDtypeStruct((b, t, d), BF16),
                   jax.ShapeDtypeStruct((b, N_EXPERTS, t), F32)],
        compiler_params=_cparams(("parallel",)),
        name="out_proj",
    )(hf, hb, of, ob, pb, pb, xa, mod, gml, ggla, wo, gpost, gpre, wrt)


def _cumsum_blocks(x, r):
    n = x.shape[0]
    xb = x.astype(BF16)
    li = _iota((LANES, LANES), 0)
    lj = _iota((LANES, LANES), 1)
    upper = jnp.where(li <= lj, 1.0, 0.0).astype(BF16)
    ones = jnp.ones((LANES, LANES), BF16)
    inrow = _mm(xb, upper)
    tot = _mm(xb, ones)
    ri = _iota((n, n), 0)
    rj = _iota((n, n), 1)
    same = (ri // r) == (rj // r)
    strict = jnp.where(jnp.logical_and(same, rj < ri), 1.0, 0.0).astype(BF16)
    off = _mm(strict, tot.astype(BF16))
    return inrow + off, off


def _select(aff, r, cap, base_slot):
    ne = N_EXPERTS
    n = ne * r
    aff3 = aff.reshape(ne, r, LANES)
    capf = jnp.float32(cap)

    def body(k, prefix):
        cand = prefix | (jnp.int32(1) << (30 - k))
        candf = lax.bitcast_convert_type(cand, F32)
        cnt = jnp.sum(jnp.where(aff3 >= candf, 1.0, 0.0), axis=(1, 2), keepdims=True)
        return jnp.where(cnt >= capf, cand, prefix)

    thr = lax.bitcast_convert_type(lax.fori_loop(0, 31, body, jnp.zeros((ne, 1, 1), I32)), F32)
    gt = jnp.where(aff3 > thr, 1.0, 0.0)
    eq = jnp.where(aff3 == thr, 1.0, 0.0)
    need = capf - jnp.sum(gt, axis=(1, 2), keepdims=True)
    eq2 = eq.reshape(n, LANES)
    cs_eq, _ = _cumsum_blocks(eq2, r)
    eq_rank = (cs_eq - eq2).reshape(ne, r, LANES)
    sel = (gt + eq * jnp.where(eq_rank < need, 1.0, 0.0)).reshape(n, LANES)
    cs, off = _cumsum_blocks(sel, r)
    slot = jnp.where(sel > 0.5, cs - 1.0 + base_slot, UNSEL)
    return slot, off


def _sel_kernel(rc, rl, cap_c, cap_l, *refs):
    if rc:
        affc, affl, slc, offc, sll, offl = refs
        slc[0], offc[0] = _select(affc[0], rc, cap_c, float(cap_l))
    else:
        affl, sll, offl = refs
    sll[0], offl[0] = _select(affl[0], rl, cap_l, 0.0)


def _route(aff_c, aff_l, cap_c, cap_l):
    b = aff_l.shape[0]
    ne = N_EXPERTS
    rl = aff_l.shape[1] // ne
    rc = aff_c.shape[1] // ne if aff_c is not None else 0
    args = ([aff_c] if rc else []) + [aff_l]
    in_specs, out_shape, out_specs = [], [], []
    for a in args:
        spec = pl.BlockSpec((1,) + a.shape[1:], lambda bi: (bi, 0, 0))
        in_specs.append(spec)
        out_shape += [jax.ShapeDtypeStruct(a.shape, F32)] * 2
        out_specs += [spec, spec]
    return pl.pallas_call(
        functools.partial(_sel_kernel, rc, rl, cap_c, cap_l),
        grid=(b,),
        in_specs=in_specs,
        out_specs=out_specs,
        out_shape=out_shape,
        compiler_params=_cparams(("parallel",)),
        name="route",
    )(*args)


def _window(lo_ref, base, e, m_rows):
    lo_e = lo_ref[base + e]
    hi_e = lo_ref[base + N_EXPERTS + e]
    a_e = jnp.minimum((lo_e // ROW_ALIGN) * ROW_ALIGN, m_rows - WIN)
    return a_e, hi_e


def _disp_kernel(m_rows, ng, lo_ref, h_ref, slot_ref, x_ref):
    ne = N_EXPERTS
    gi = pl.program_id(1)
    i = pl.program_id(2)
    base = (pl.program_id(0) * pl.num_programs(2) + i) * (2 * ne)

    @pl.when(i == 0)
    def _():
        x_ref[...] = jnp.zeros_like(x_ref)

    h = h_ref[0]
    sl = slot_ref[0]
    sub = _iota((WIN, TB), 0).astype(F32)
    wins = [_window(lo_ref, base, gi * ng + k, m_rows) for k in range(ng)]

    def onehot(k, first):
        a_r = jnp.minimum(first, m_rows - WIN)
        srow = sl[k:k + 1, :]
        hit = jnp.logical_and(srow - a_r.astype(F32) == sub, srow >= first.astype(F32))
        return jnp.where(hit, 1.0, 0.0).astype(BF16), a_r

    def add_rows(k, a_r, g):
        rows = pl.ds(pl.multiple_of(a_r, ROW_ALIGN), WIN)
        x_ref[0, k, rows, :] = x_ref[0, k, rows, :] + g

    sel = [onehot(k, a_e) for k, (a_e, _) in enumerate(wins)]
    g = _mm(jnp.concatenate([w for w, _ in sel], axis=0), h).astype(BF16)
    for k, (_, a_r) in enumerate(sel):
        add_rows(k, a_r, g[k * WIN:(k + 1) * WIN, :])

    for k, (a_e, hi_e) in enumerate(wins):
        @pl.when(hi_e - a_e > WIN)
        def _(k=k, a_e=a_e, hi_e=hi_e):
            def more(rd, carry):
                w, a_r = onehot(k, a_e + rd * WIN)
                add_rows(k, a_r, _mm(w, h).astype(BF16))
                return carry

            lax.fori_loop(1, (hi_e - a_e + WIN - 1) // WIN, more, 0)


def _dispatch(lohi, h2e, slots, m_rows, t0, nt):
    b, t, de = h2e.shape
    ne = N_EXPERTS
    ng = 8
    grid_spec = pltpu.PrefetchScalarGridSpec(
        num_scalar_prefetch=1,
        grid=(b, ne // ng, nt),
        in_specs=[pl.BlockSpec((1, TB, de), lambda bi, gi, i, *_: (bi, i + t0, 0)),
                  pl.BlockSpec((1, ng, TB), lambda bi, gi, i, *_: (bi * (ne // ng) + gi, 0, i + t0))],
        out_specs=pl.BlockSpec((1, ng, m_rows, de), lambda bi, gi, i, *_: (bi, gi, 0, 0)))
    return pl.pallas_call(
        functools.partial(_disp_kernel, m_rows, ng),
        grid_spec=grid_spec,
        out_shape=jax.ShapeDtypeStruct((b, ne, m_rows, de), BF16),
        compiler_params=_cparams(("parallel", "parallel", "arbitrary")),
        name="dispatch",
    )(lohi, h2e, slots.reshape(b * (ne // ng), ng, t))


def _row_chunks(m_rows, cap_l):
    step = min(EXPERT_ROWS, cap_l)
    starts = list(range(0, cap_l, step))
    return [(s, (m_rows - s) if s == starts[-1] else step) for s in starts]


def _exp_kernel(m_rows, cap_l, x_ref, wg_ref, wu_ref, wd_ref, y_ref, wg_s, wu_s, wd_s):
    @pl.when(pl.program_id(1) == 0)
    def _():
        wg_s[...] = wg_ref[0, 0].astype(BF16)
        wu_s[...] = wu_ref[0, 0].astype(BF16)
        wd_s[...] = wd_ref[0, 0].astype(BF16)

    for r0, mc in _row_chunks(m_rows, cap_l):
        xs = x_ref[0, 0, r0:r0 + mc, :]
        hid = _silu(_mm(xs, wg_s[...])) * _mm(xs, wu_s[...])
        y_ref[0, 0, r0:r0 + mc, :] = _mm(hid.astype(BF16), wd_s[...]).astype(BF16)


def _experts(xin, wg, wu, wd, layer, cap_l):
    b, ne, m_rows, de = xin.shape
    _, _, d, f = wg.shape
    return pl.pallas_call(
        functools.partial(_exp_kernel, m_rows, cap_l),
        grid=(ne, b),
        in_specs=[pl.BlockSpec((1, 1, m_rows, de), lambda ei, bi: (bi, ei, 0, 0)),
                  pl.BlockSpec((1, 1, d, f), lambda ei, bi: (layer, ei, 0, 0)),
                  pl.BlockSpec((1, 1, d, f), lambda ei, bi: (layer, ei, 0, 0)),
                  pl.BlockSpec((1, 1, f, d), lambda ei, bi: (layer, ei, 0, 0))],
        out_specs=pl.BlockSpec((1, 1, m_rows, d), lambda ei, bi: (bi, ei, 0, 0)),
        out_shape=jax.ShapeDtypeStruct((b, ne, m_rows, d), BF16),
        scratch_shapes=[pltpu.VMEM((d, f), BF16), pltpu.VMEM((d, f), BF16), pltpu.VMEM((f, d), BF16)],
        compiler_params=_cparams(("arbitrary", "arbitrary")),
        name="experts",
    )(xin, wg, wu, wd)


def _comb_kernel(m_rows, lo_ref, slot_ref, aff_ref, x1_ref, mod_ref, gpost, y_ref, out_ref, acc_s):
    ne = N_EXPERTS
    i = pl.program_id(1)
    base = (pl.program_id(0) * pl.num_programs(1) + i) * (2 * ne)
    kk = ne * WIN
    sl = slot_ref[0]
    hi = jnp.floor(sl * (1.0 / 32.0))
    lo = sl - hi * 32.0
    col_e = _iota((ne, kk), 1) // WIN
    expand = jnp.where(col_e == _iota((ne, kk), 0), 1.0, 0.0).astype(BF16)
    sx = _mm(hi.astype(BF16), expand, _TN) * 32.0 + _mm(lo.astype(BF16), expand, _TN)
    gx = _mm(aff_ref[0].astype(BF16), expand, _TN)
    col = _iota((1, kk), 1)
    jrow = (col % WIN).astype(F32)
    wins = [_window(lo_ref, base, e, m_rows) for e in range(ne)]
    ys = []
    arow = jnp.zeros((1, kk), F32)
    for e, (a_e, _) in enumerate(wins):
        arow = jnp.where(col // WIN == e, a_e.astype(F32), arow)
        ys.append(y_ref[0, e, pl.ds(pl.multiple_of(a_e, ROW_ALIGN), WIN), :])
    w = jnp.where(sx - arow == jrow, gx, 0.0).astype(BF16)
    acc_s[...] = _mm(w, jnp.concatenate(ys, axis=0))

    lane = _iota((TB, WIN), 1).astype(F32)
    for e, (a_e, hi_e) in enumerate(wins):
        @pl.when(hi_e - a_e > WIN)
        def _(e=e, a_e=a_e, hi_e=hi_e):
            scol = sx[:, e * WIN:e * WIN + 1]
            gcol = gx[:, e * WIN:e * WIN + 1]

            def more(rd, carry):
                first = a_e + rd * WIN
                a_r = jnp.minimum(first, m_rows - WIN)
                hit = jnp.logical_and(scol - a_r.astype(F32) == lane, scol >= first.astype(F32))
                ye = y_ref[0, e, pl.ds(pl.multiple_of(a_r, ROW_ALIGN), WIN), :]
                acc_s[...] += _mm(jnp.where(hit, gcol, 0.0).astype(BF16), ye)
                return carry

            lax.fori_loop(1, (hi_e - a_e + WIN - 1) // WIN, more, 0)

    out_ref[0] = x1_ref[0] + _rms(acc_s[...], mod_ref[0, 0, 0, 5:6, :] * gpost[0])


def _combine(lohi, slots, aff, x1, layer, mod, gpost, y, t0, nt, ncb):
    b, t, d = x1.shape
    ne = N_EXPERTS
    m_rows = y.shape[2]
    r8 = mod.shape[3]
    grid_spec = pltpu.PrefetchScalarGridSpec(
        num_scalar_prefetch=1,
        grid=(b, nt),
        in_specs=[pl.BlockSpec((1, ne, TB), lambda bi, i, *_: (bi, 0, i + t0)),
                  pl.BlockSpec((1, ne, TB), lambda bi, i, *_: (bi, 0, i + t0)),
                  pl.BlockSpec((1, TB, d), lambda bi, i, *_: (bi, i + t0, 0)),
                  pl.BlockSpec((1, 1, 1, r8, d),
                               lambda bi, i, *_: (layer, bi, jnp.where(i + t0 < ncb, 0, 1), 0, 0)),
                  _layer_spec(layer, gpost),
                  pl.BlockSpec((1, ne, m_rows, d), lambda bi, i, *_: (bi, 0, 0, 0),
                               pipeline_mode=pl.Buffered(1))],
        out_specs=pl.BlockSpec((1, TB, d), lambda bi, i, *_: (bi, i, 0)),
        scratch_shapes=[pltpu.VMEM((TB, d), F32)])
    return pl.pallas_call(
        functools.partial(_comb_kernel, m_rows),
        grid_spec=grid_spec,
        out_shape=jax.ShapeDtypeStruct((b, nt * TB, d), F32),
        compiler_params=_cparams(("parallel", "arbitrary")),
        name="combine",
    )(lohi, slots, aff, x1, mod, gpost, y)


def _pos_tables(rows, d):
    quarter = d // 4
    freq = jnp.power(POS_BASE, -jnp.arange(quarter, dtype=F32) / quarter)
    ar = jnp.arange(rows, dtype=F32)[:, None] * freq
    ac = jnp.arange(GRID_W, dtype=F32)[:, None] * freq
    return (jnp.concatenate([jnp.sin(ar), jnp.cos(ar)], axis=-1),
            jnp.concatenate([jnp.sin(ac), jnp.cos(ac)], axis=-1))


def _tile_bounds(off, r, ntile, cap, base):
    b = off.shape[0]
    o = off.reshape(b, N_EXPERTS, r, LANES)[:, :, :, 0]
    lo = o[:, :, ::TB // LANES][:, :, :ntile] + base
    hi = jnp.concatenate([lo[:, :, 1:], jnp.full((b, N_EXPERTS, 1), cap + base, F32)], axis=2)
    return lo, hi


def kernel(x, c, ctx, c_ctx, w_ada, b_ada, g_mix_pre, g_mix_post, g_ffn_pre, g_ffn_post,
           w_in, conv_qk, b_ml_gates, w_gla_a2, b_gla_a, g_ml_norm, g_gla_norm, w_out,
           w_router, w_e_gate, w_e_up, w_e_down):
    bsz, n_tok, d = x.shape
    lc = ctx.shape[1]
    depth = w_in.shape[0]
    ne = N_EXPERTS
    t = lc + n_tok
    ncb = lc // TB
    nblk = t // TB
    assert lc % TB == 0 and n_tok % TB == 0 and d == 2 * GROUP_W and TB % GRID_W == 0
    cap_l = EC_FACTOR * n_tok // ne
    cap_c = EC_FACTOR * lc // ne

    pos_r, pos_c = _pos_tables(n_tok // GRID_W, d)
    xa = None

    cc = jnp.zeros((8, d), F32).at[:bsz].set(c).at[bsz].set(c_ctx)
    mods = _ada(cc, w_ada, b_ada)

    rank = w_gla_a2.shape[2]
    ml_w, n_gates, gla_w = 4 * GROUP_W, 4 * N_HEADS, 3 * GROUP_W
    o_gla, o_a = ml_w + n_gates, ml_w + n_gates + gla_w
    assert w_in.shape[2] == o_a + 2 * rank and n_gates + 2 * rank <= LANES
    wide = jnp.concatenate([w_in[:, :, :ml_w], w_in[:, :, o_gla:o_a]], axis=2).astype(BF16)
    narrow = jnp.concatenate([w_in[:, :, ml_w:o_gla], w_in[:, :, o_a:]], axis=2)
    narrow = jnp.pad(narrow, ((0, 0), (0, 0), (0, LANES - narrow.shape[2])))
    narrow = jnp.concatenate(_split2(narrow), axis=2)
    bias_s = jnp.pad(b_ml_gates, ((0, 0), (0, LANES - b_ml_gates.shape[1])))
    w2e = jnp.zeros((depth, 2, LANES, N_HEADS * GLA_DK), F32)
    w2e = (w2e.at[:, 0, n_gates:n_gates + rank].set(w_gla_a2[:, 0])
           .at[:, 1, n_gates + rank:n_gates + 2 * rank].set(w_gla_a2[:, 1]).astype(BF16))
    m_lat = mods[:, :bsz].reshape(depth, bsz, 1, 6, d)
    m_ctx = jnp.broadcast_to(mods[:, bsz].reshape(depth, 1, 1, 6, d), (depth, bsz, 1, 6, d))
    mod = jnp.pad(jnp.concatenate([m_ctx, m_lat], axis=2), ((0, 0), (0, 0), (0, 0), (0, 2), (0, 0)))
    row = lambda a: a.reshape(depth, 1, -1)
    bias_s, ba = row(bias_s), b_gla_a.reshape(depth, 2, 1, -1)
    g_pre1, g_post1, g_pre2, g_post2 = row(g_mix_pre), row(g_mix_post), row(g_ffn_pre), row(g_ffn_post)
    g_ml, g_gla = row(g_ml_norm), row(g_gla_norm)
    wo = w_out.astype(BF16)
    wrt = w_router.transpose(0, 2, 1)

    for l in range(depth):
        last = l == depth - 1
        if l == 0:
            xa, pb, ps, qc = _in_proj(None, l, mod, g_pre1, wide, narrow, bias_s, conv_qk, ncb,
                                      first=(x, ctx, pos_r, pos_c))
        else:
            pb, ps, qc = _in_proj(xa, l, mod, g_pre1, wide, narrow, bias_s, conv_qk, ncb)
        hf, hb, of, ob = _mixers2(qc, pb, ps, l, w2e, ba, ncb)
        x1, h2e, aff = _out_proj(hf, hb, of, ob, pb, xa, l, mod, g_ml, g_gla, wo, g_post1, g_pre2, wrt, ncb)

        rl = n_tok // LANES
        aff_l = aff[:, :, lc:].reshape(bsz, ne * rl, LANES)
        if last:
            sll, offl = _route(None, aff_l, 0, cap_l)
            slots = jnp.pad(sll.reshape(bsz, ne, n_tok), ((0, 0), (0, 0), (lc, 0)), constant_values=UNSEL)
            lo, hi = _tile_bounds(offl, rl, nblk - ncb, cap_l, 0)
            t0, nt, m_rows = ncb, nblk - ncb, cap_l
        else:
            rc = max(lc // LANES, 8)
            aff_c = aff[:, :, :lc].reshape(bsz, ne, lc // LANES, LANES)
            aff_c = jnp.pad(aff_c, ((0, 0), (0, 0), (0, rc - lc // LANES), (0, 0)), constant_values=-1.0)
            slc, offc, sll, offl = _route(aff_c.reshape(bsz, ne * rc, LANES), aff_l, cap_c, cap_l)
            slots = jnp.concatenate([slc.reshape(bsz, ne, rc * LANES)[:, :, :lc],
                                     sll.reshape(bsz, ne, n_tok)], axis=2)
            lo_l, hi_l = _tile_bounds(offl, rl, nblk - ncb, cap_l, 0)
            lo_c, hi_c = _tile_bounds(offc, rc, ncb, cap_c, cap_l)
            lo = jnp.concatenate([lo_c, lo_l], axis=2)
            hi = jnp.concatenate([hi_c, hi_l], axis=2)
            t0, nt, m_rows = 0, nblk, cap_l + cap_c
        lohi = jnp.concatenate([lo, hi], axis=1).transpose(0, 2, 1).astype(I32).reshape(-1)
        xin = _dispatch(lohi, h2e, slots, m_rows, t0, nt)
        y = _experts(xin, w_e_gate, w_e_up, w_e_down, l, cap_l)
        xa = _combine(lohi, slots, aff, x1, l, mod, g_post2, y, t0, nt, ncb)
    return xa
```
